```python
import jax, jax.numpy as jnp
from jax import lax
import numpy as np

D_MODEL = 2048
BATCH = 1
SEQ = 8192
DEPTH = 1

CHUNK = 64
EPS = 1e-6
A_HEADS = 8
A_DK = 128
A_DV = 128
A_CONV = 4
A_QK = A_HEADS * A_DK
A_V = A_HEADS * A_DV
B_HEADS = 4
B_DK = 128
B_DV = 256
B_GATE_RANK = 16
B_GATE_NORM = 16.0
B_QK = B_HEADS * B_DK
B_V = B_HEADS * B_DV
N_GROUPS = 4
EXP_PER_GROUP = 8
N_EXPERTS = N_GROUPS * EXP_PER_GROUP
TOPK_IN_GROUP = 2
D_FF_EXPERT = 512
ROW_BLOCK = 128
IN_SPLITS = (A_QK, A_QK, A_V, A_V, A_HEADS, A_HEADS,
             B_QK, B_QK, B_V, B_V, B_GATE_RANK, D_MODEL, D_MODEL)
IN_COLS = 2 * A_QK + 2 * A_V + 2 * A_HEADS + 2 * B_QK + 2 * B_V + B_GATE_RANK + 2 * D_MODEL

kernel_name = "hybrid_gdn_gla_hiermoe_block"


def _rmsnorm(x, w):
    xf = x.astype(jnp.float32)
    y = xf * lax.rsqrt(jnp.mean(xf * xf, axis=-1, keepdims=True) + EPS)
    return (y * w.astype(jnp.float32)).astype(x.dtype)


def _l2norm(x):
    xf = x.astype(jnp.float32)
    return xf * lax.rsqrt(jnp.sum(xf * xf, axis=-1, keepdims=True) + EPS)


def _split_cols(z):
    idx, acc = [], 0
    for s in IN_SPLITS[:-1]:
        acc += s
        idx.append(acc)
    return jnp.split(z, idx, axis=-1)


def _heads(t, n):
    b, s, _ = t.shape
    return t.reshape(b, s, n, -1).transpose(0, 2, 1, 3)


def _causal_depthwise_conv(x, w):
    k, c = w.shape
    return lax.conv_general_dilated(
        x, w[:, None, :].astype(x.dtype), window_strides=(1,), padding=[(k - 1, 0)],
        dimension_numbers=("NWC", "WIO", "NWC"), feature_group_count=c)


def _gated_delta_rule(q, k, v, beta, g):
    bsz, nh, s, dk = q.shape
    dv = v.shape[-1]
    n, c = s // CHUNK, CHUNK
    f32 = jnp.float32
    q = q.astype(f32).reshape(bsz, nh, n, c, dk) * (dk ** -0.5)
    k = k.astype(f32).reshape(bsz, nh, n, c, dk)
    v = v.astype(f32).reshape(bsz, nh, n, c, dv)
    beta = beta.astype(f32).reshape(bsz, nh, n, c)
    gc = jnp.cumsum(g.astype(f32).reshape(bsz, nh, n, c), axis=-1)
    causal = jnp.tril(jnp.ones((c, c), bool))
    strict = jnp.tril(jnp.ones((c, c), bool), -1)
    decay = jnp.exp(jnp.where(causal, gc[..., :, None] - gc[..., None, :], -jnp.inf))
    kb = k * beta[..., None]
    a_strict = jnp.where(strict, jnp.einsum("bhnid,bhnjd->bhnij", kb, k) * decay, 0.0)
    eye = jnp.eye(c, dtype=f32)
    t_mat = lax.linalg.triangular_solve(eye + a_strict, jnp.broadcast_to(eye, a_strict.shape),
                                        left_side=True, lower=True)
    u = jnp.einsum("bhnij,bhnjd->bhnid", t_mat, v * beta[..., None])
    w = jnp.einsum("bhnij,bhnjd->bhnid", t_mat, kb * jnp.exp(gc)[..., None])
    qk = jnp.einsum("bhnid,bhnjd->bhnij", q, k) * decay
    q_dec = q * jnp.exp(gc)[..., None]
    g_last = gc[..., -1]
    k_dec = k * jnp.exp(g_last[..., None] - gc)[..., None]

    def step(state, xs):
        u_n, w_n, qk_n, qd_n, kd_n, gl_n = xs
        v_new = u_n - jnp.einsum("bhid,bhde->bhie", w_n, state)
        o = jnp.einsum("bhid,bhde->bhie", qd_n, state) + jnp.einsum("bhij,bhje->bhie", qk_n, v_new)
        state = state * jnp.exp(gl_n)[..., None, None] + jnp.einsum("bhid,bhie->bhde", kd_n, v_new)
        return state, o

    xs = tuple(jnp.moveaxis(t, 2, 0) for t in (u, w, qk, q_dec, k_dec, g_last))
    _, o = lax.scan(step, jnp.zeros((bsz, nh, dk, dv), f32), xs)
    return jnp.moveaxis(o, 0, 2).reshape(bsz, nh, s, dv)


def _gla_chunked(q, k, v, gk):
    bsz, nh, s, dk = q.shape
    dv = v.shape[-1]
    n, c = s // CHUNK, CHUNK
    f32 = jnp.float32
    def chunks(t, d):
        return jnp.moveaxis(t.astype(f32).reshape(bsz, nh, n, c, d), 2, 0)
    qs = chunks(q, dk) * (dk ** -0.5)
    ks = chunks(k, dk)
    vs = chunks(v, dv)
    bs = jnp.cumsum(chunks(gk, dk), axis=-2)
    causal = jnp.tril(jnp.ones((c, c), bool))[:, :, None]

    def step(state, xs):
        q_n, k_n, v_n, b_n = xs
        dmat = jnp.exp(jnp.where(causal, b_n[..., :, None, :] - b_n[..., None, :, :], -jnp.inf))
        attn = jnp.sum(q_n[..., :, None, :] * k_n[..., None, :, :] * dmat, axis=-1)
        o = jnp.einsum("bhij,bhje->bhie", attn, v_n) + jnp.einsum("bhid,bhde->bhie", q_n * jnp.exp(b_n), state)
        b_last = b_n[..., -1, :]
        state = state * jnp.exp(b_last)[..., None] + jnp.einsum(
            "bhid,bhie->bhde", k_n * jnp.exp(b_last[..., None, :] - b_n), v_n)
        return state, o

    _, o = lax.scan(step, jnp.zeros((bsz, nh, dk, dv), f32), (qs, ks, vs, bs))
    return jnp.moveaxis(o, 0, 2).reshape(bsz, nh, s, dv)


def _hier_moe(h, w_rg, b_rg, w_re, b_re, w1, w3, w2):
    bsz, s, d = h.shape
    t = bsz * s
    ht = h.reshape(t, d)
    lg = (ht @ w_rg).astype(jnp.float32) + b_rg.astype(jnp.float32)
    pg = jax.nn.softmax(lg, axis=-1)
    p_top, g_idx = lax.top_k(pg, 1)
    le = jnp.einsum("td,dge->tge", ht, w_re).astype(jnp.float32) + b_re.astype(jnp.float32)
    le_sel = jnp.take_along_axis(le, jnp.broadcast_to(g_idx[:, :, None], (t, 1, EXP_PER_GROUP)), axis=1)[:, 0]
    l2, e_idx = lax.top_k(le_sel, TOPK_IN_GROUP)
    gate = jax.nn.softmax(l2, axis=-1) * p_top
    expert = g_idx * EXP_PER_GROUP + e_idx

    n_assign = t * TOPK_IN_GROUP
    e_flat = expert.reshape(-1)
    tok_flat = jnp.repeat(jnp.arange(t, dtype=jnp.int32), TOPK_IN_GROUP)
    w_flat = gate.reshape(-1).astype(h.dtype)
    order = jnp.argsort(e_flat)
    e_s, tok_s, w_s = e_flat[order], tok_flat[order], w_flat[order]
    counts = jnp.bincount(e_flat, length=N_EXPERTS)
    start = jnp.cumsum(counts) - counts
    padded = (counts + ROW_BLOCK - 1) // ROW_BLOCK * ROW_BLOCK
    pad_end = jnp.cumsum(padded)
    pad_start = pad_end - padded
    dest = pad_start[e_s] + jnp.arange(n_assign) - start[e_s]
    n_rows = ((n_assign + ROW_BLOCK - 1) // ROW_BLOCK + N_EXPERTS) * ROW_BLOCK
    row_tok = jnp.zeros((n_rows,), jnp.int32).at[dest].set(tok_s)
    row_w = jnp.zeros((n_rows,), h.dtype).at[dest].set(w_s)
    n_blk = n_rows // ROW_BLOCK
    blk_exp = jnp.minimum(jnp.searchsorted(pad_end, jnp.arange(n_blk) * ROW_BLOCK, side="right"),
                          N_EXPERTS - 1)
    xs = ht[row_tok].reshape(n_blk, ROW_BLOCK, d)

    def expert_rows(args):
        xb, e = args
        return (jax.nn.silu(xb @ w1[e]) * (xb @ w3[e])) @ w2[e]

    yb = lax.map(expert_rows, (xs, blk_exp)).reshape(n_rows, d)
    out = jnp.zeros((t, d), h.dtype).at[row_tok].add(yb * row_w[:, None])
    return out.reshape(bsz, s, d)


def _layer(x, norm_mix, w_in, conv_a, a_log, dt_bias, a_norm, w_gk2, b_gk, b_norm,
           w_oa, w_ob, w_out, norm_ffn, w_rg, b_rg, w_re, b_re, w1, w3, w2):
    bsz, s, _ = x.shape
    h = _rmsnorm(x, norm_mix)
    z = h @ w_in
    aq, ak, av, az, abeta, aalpha, bq, bk, bv, bg, blr, mix_a, mix_b = _split_cols(z)

    qkv = jax.nn.silu(_causal_depthwise_conv(jnp.concatenate([aq, ak, av], axis=-1), conv_a))
    aq, ak, av = jnp.split(qkv, [A_QK, 2 * A_QK], axis=-1)
    beta = jax.nn.sigmoid(abeta.astype(jnp.float32)).transpose(0, 2, 1)
    g = (-jnp.exp(a_log.astype(jnp.float32))
         * jax.nn.softplus(aalpha.astype(jnp.float32) + dt_bias.astype(jnp.float32))).transpose(0, 2, 1)
    oa = _gated_delta_rule(_l2norm(_heads(aq, A_HEADS)), _l2norm(_heads(ak, A_HEADS)),
                           _heads(av, A_HEADS), beta, g).astype(x.dtype)
    oa = _rmsnorm(oa.transpose(0, 2, 1, 3), a_norm) * jax.nn.silu(az.reshape(bsz, s, A_HEADS, A_DV))
    ya = oa.reshape(bsz, s, A_V) @ w_oa

    gk = jax.nn.log_sigmoid((blr @ w_gk2 + b_gk).astype(jnp.float32)) / B_GATE_NORM
    ob = _gla_chunked(_heads(bq, B_HEADS), _heads(bk, B_HEADS), _heads(bv, B_HEADS),
                      _heads(gk, B_HEADS)).astype(x.dtype)
    ob = _rmsnorm(ob.transpose(0, 2, 1, 3), b_norm) * jax.nn.silu(bg.reshape(bsz, s, B_HEADS, B_DV))
    yb = ob.reshape(bsz, s, B_V) @ w_ob

    x = x + (jax.nn.sigmoid(mix_a) * ya + jax.nn.sigmoid(mix_b) * yb) @ w_out
    x = x + _hier_moe(_rmsnorm(x, norm_ffn), w_rg, b_rg, w_re, b_re, w1, w3, w2)
    return x


def setup_inputs(seed: int = 0) -> dict:
    key = jax.random.key(seed)
    ks = jax.random.split(key, 24)
    L, D = DEPTH, D_MODEL
    nrm = jax.random.normal
    dt = jnp.exp(jax.random.uniform(ks[4], (L, A_HEADS)) * (jnp.log(0.1) - jnp.log(0.001)) + jnp.log(0.001))
    return {
        "x": nrm(ks[0], (BATCH, SEQ, D), jnp.float32),
        "norm_mix": 1.0 + 0.01 * nrm(ks[1], (L, D)),
        "w_in": nrm(ks[2], (L, D, IN_COLS)) * D ** -0.5,
        "conv_a": nrm(ks[3], (L, A_CONV, 2 * A_QK + A_V)) * A_CONV ** -0.5,
        "a_log": jnp.log(jax.random.uniform(ks[5], (L, A_HEADS), minval=1.0, maxval=16.0)),
        "dt_bias": dt + jnp.log(-jnp.expm1(-dt)),
        "a_norm": 1.0 + 0.01 * nrm(ks[6], (L, A_DV)),
        "w_gk2": nrm(ks[7], (L, B_GATE_RANK, B_QK)) * B_GATE_RANK ** -0.5,
        "b_gk": 0.1 * nrm(ks[8], (L, B_QK)),
        "b_norm": 1.0 + 0.01 * nrm(ks[9], (L, B_DV)),
        "w_oa": nrm(ks[10], (L, A_V, D)) * A_V ** -0.5,
        "w_ob": nrm(ks[11], (L, B_V, D)) * B_V ** -0.5,
        "w_out": nrm(ks[12], (L, D, D)) * D ** -0.5,
        "norm_ffn": 1.0 + 0.01 * nrm(ks[13], (L, D)),
        "w_rg": nrm(ks[14], (L, D, N_GROUPS)) * D ** -0.5,
        "b_rg": 0.01 * nrm(ks[15], (L, N_GROUPS)),
        "w_re": nrm(ks[16], (L, D, N_GROUPS, EXP_PER_GROUP)) * D ** -0.5,
        "b_re": 0.01 * nrm(ks[17], (L, N_GROUPS, EXP_PER_GROUP)),
        "w1": nrm(ks[18], (L, N_EXPERTS, D, D_FF_EXPERT)) * D ** -0.5,
        "w3": nrm(ks[19], (L, N_EXPERTS, D, D_FF_EXPERT)) * D ** -0.5,
        "w2": nrm(ks[20], (L, N_EXPERTS, D_FF_EXPERT, D)) * D_FF_EXPERT ** -0.5,
        "norm_final": 1.0 + 0.01 * nrm(ks[21], (D,)),
    }


def reference(x, norm_mix, w_in, conv_a, a_log, dt_bias, a_norm, w_gk2, b_gk, b_norm,
              w_oa, w_ob, w_out, norm_ffn, w_rg, b_rg, w_re, b_re, w1, w3, w2, norm_final):
    for l in range(DEPTH):
        x = _layer(x, norm_mix[l], w_in[l], conv_a[l], a_log[l], dt_bias[l], a_norm[l], w_gk2[l],
                   b_gk[l], b_norm[l], w_oa[l], w_ob[l], w_out[l], norm_ffn[l], w_rg[l], b_rg[l],
                   w_re[l], b_re[l], w1[l], w3[l], w2[l])
    return _rmsnorm(x, norm_final)
```

```python
import functools

import jax
import jax.numpy as jnp
from jax import lax
from jax.experimental import pallas as pl
from jax.experimental.pallas import tpu as pltpu

D_MODEL = 2048
CHUNK = 64
EPS = 1e-6
A_HEADS, A_DK, A_DV, A_CONV = 8, 128, 128, 4
A_QK, A_V = A_HEADS * A_DK, A_HEADS * A_DV
B_HEADS, B_DK, B_DV, B_GATE_RANK, B_GATE_NORM = 4, 128, 256, 16, 16.0
B_QK, B_V = B_HEADS * B_DK, B_HEADS * B_DV
N_GROUPS, EXP_PER_GROUP, D_FF = 4, 8, 512
N_EXPERTS = N_GROUPS * EXP_PER_GROUP
ROW_BLOCK = 128
LANE = 128
SUB = 16
TB = 256
NEG = -1e30

F32 = jnp.float32
BF16 = jnp.bfloat16
HI = lax.Precision.HIGHEST

COL_A = 0
COL_MIX = 4 * A_QK
COL_B = COL_MIX + 2 * D_MODEL
N_MAIN = COL_B + 2 * B_QK + 2 * B_V


def _cparams(sem, vmem_mib):
    return pltpu.CompilerParams(dimension_semantics=sem, vmem_limit_bytes=vmem_mib * 2 ** 20)


def _bdot(a, b):
    return jnp.dot(a.astype(BF16), b.astype(BF16), preferred_element_type=F32)


def _bdot_nt(a, b):
    return lax.dot_general(a.astype(BF16), b.astype(BF16), (((1,), (1,)), ((), ())),
                           preferred_element_type=F32)


def _bdot_tn(a, b):
    return lax.dot_general(a.astype(BF16), b.astype(BF16), (((0,), (0,)), ((), ())),
                           preferred_element_type=F32)


def _fdot(a, b):
    return jnp.dot(a, b, preferred_element_type=F32, precision=HI)


def _sigmoid(x):
    return 1.0 / (1.0 + jnp.exp(-x))


def _silu(x):
    return x * _sigmoid(x)


def _softplus(x):
    return jnp.maximum(x, 0.0) + jnp.log(1.0 + jnp.exp(-jnp.abs(x)))


def _rmsnorm_kernel(x_ref, w_ref, o_ref):
    x = x_ref[...]
    ms = jnp.mean(x * x, axis=-1, keepdims=True)
    o_ref[...] = (x * lax.rsqrt(ms + EPS) * w_ref[...]).astype(o_ref.dtype)


def _rmsnorm(x, w, tm):
    t, d = x.shape
    return pl.pallas_call(
        _rmsnorm_kernel,
        grid=(t // tm,),
        in_specs=[pl.BlockSpec((tm, d), lambda i: (i, 0)), pl.BlockSpec((1, d), lambda i: (0, 0))],
        out_specs=pl.BlockSpec((tm, d), lambda i: (i, 0)),
        out_shape=jax.ShapeDtypeStruct((t, d), BF16),
        compiler_params=_cparams(("parallel",), 32),
    )(x, w.reshape(1, d))


def _mm_kernel(a_ref, b_ref, o_ref):
    o_ref[...] = jnp.dot(a_ref[...], b_ref[...], preferred_element_type=F32).astype(o_ref.dtype)


def _matmul(a, b, tm, tn, out_dtype):
    m, k = a.shape
    n = b.shape[1]
    return pl.pallas_call(
        _mm_kernel,
        grid=(n // tn, m // tm),
        in_specs=[pl.BlockSpec((tm, k), lambda j, i: (i, 0)), pl.BlockSpec((k, tn), lambda j, i: (0, j))],
        out_specs=pl.BlockSpec((tm, tn), lambda j, i: (i, j)),
        out_shape=jax.ShapeDtypeStruct((m, n), out_dtype),
        compiler_params=_cparams(("parallel", "parallel"), 48),
    )(a, b)


def _chunk_masks(tb):
    row = lax.broadcasted_iota(jnp.int32, (tb, tb), 0)
    col = lax.broadcasted_iota(jnp.int32, (tb, tb), 1)
    same = (row // CHUNK) == (col // CHUNK)
    return row, col, same


def _lane_pick(x, idx):
    lane = lax.broadcasted_iota(jnp.int32, x.shape, 1)
    return jnp.sum(jnp.where(lane == idx, x, 0.0), axis=-1, keepdims=True)


def _causal_conv_silu(x_ref, w_ref, halo_ref, cbuf_ref, idx, tb):
    cbuf_ref[idx, 0:8, :] = halo_ref[idx]
    cbuf_ref[idx, 8:8 + tb, :] = x_ref[...]
    halo_ref[idx] = x_ref[tb - 8:tb, :]
    w = w_ref[...]
    acc = w[A_CONV - 1:A_CONV, :] * x_ref[...]
    for j in range(A_CONV - 1):
        off = 8 - (A_CONV - 1) + j
        acc = acc + w[j:j + 1, :] * cbuf_ref[idx, off:off + tb, :]
    return _silu(acc)


def _mixer_a_kernel(xq_ref, xk_ref, xv_ref, z_ref, gates_ref, wq_ref, wk_ref, wv_ref,
                    alog_ref, dtb_ref, anorm_ref, o_ref, halo_ref, cbuf_ref, state_ref):
    tb = xq_ref.shape[0]
    h = pl.program_id(0)

    @pl.when(pl.program_id(1) == 0)
    def _():
        halo_ref[...] = jnp.zeros_like(halo_ref)
        state_ref[...] = jnp.zeros_like(state_ref)

    q = _causal_conv_silu(xq_ref, wq_ref, halo_ref, cbuf_ref, 0, tb)
    k = _causal_conv_silu(xk_ref, wk_ref, halo_ref, cbuf_ref, 1, tb)
    v = _causal_conv_silu(xv_ref, wv_ref, halo_ref, cbuf_ref, 2, tb)
    q = q * lax.rsqrt(jnp.sum(q * q, axis=-1, keepdims=True) + EPS) * (A_DK ** -0.5)
    k = k * lax.rsqrt(jnp.sum(k * k, axis=-1, keepdims=True) + EPS)

    gl = gates_ref[...]
    beta = _sigmoid(_lane_pick(gl, h))
    a_log = _lane_pick(alog_ref[...], h)
    dt_b = _lane_pick(dtb_ref[...], h)
    g = -jnp.exp(a_log) * _softplus(_lane_pick(gl, h + A_HEADS) + dt_b)

    row, col, same = _chunk_masks(tb)
    causal = same & (col <= row)
    strict = same & (col < row)
    gb = jnp.broadcast_to(g, (tb, LANE))
    gcb = _fdot(jnp.where(causal, 1.0, 0.0), gb)
    glast = _fdot(jnp.where(same, 1.0, 0.0), gb)
    gc_row = gcb.T[0:1, :]
    gc_col = jnp.concatenate([gcb] * (tb // LANE), axis=1)
    diff = gc_col - gc_row
    decay = jnp.exp(jnp.where(causal, diff, NEG))

    kb = k * beta
    a_mat = jnp.where(strict, _bdot_nt(kb, k) * decay, 0.0)
    n_pow = -a_mat
    t_mat = jnp.where(row == col, 1.0, 0.0) + n_pow
    lvl = 2
    while lvl < CHUNK:
        n_pow = _fdot(n_pow, n_pow)
        t_mat = t_mat + _fdot(t_mat, n_pow)
        lvl *= 2

    egc = jnp.exp(gcb)
    uw = _bdot(t_mat, jnp.concatenate([v * beta, kb * egc], axis=1))
    u, w = uw[:, :A_DV], uw[:, A_DV:]
    qk = _bdot_nt(q, k) * decay
    q_dec = q * egc
    k_dec = k * jnp.exp(glast - gcb)
    eg_last = jnp.exp(glast)

    s = state_ref[...]
    outs = []
    for c in range(tb // CHUNK):
        lo, hi = c * CHUNK, (c + 1) * CHUNK
        ws = _bdot(jnp.concatenate([w[lo:hi], q_dec[lo:hi]], axis=0), s)
        v_new = u[lo:hi] - ws[:CHUNK]
        outs.append(ws[CHUNK:] + _bdot(qk[lo:hi, lo:hi], v_new))
        s = s * eg_last[lo:lo + 1, :] + _bdot_tn(k_dec[lo:hi], v_new)
    state_ref[...] = s
    o = jnp.concatenate(outs, axis=0)

    o = o * lax.rsqrt(jnp.mean(o * o, axis=-1, keepdims=True) + EPS) * anorm_ref[...]
    o_ref[...] = (o * _silu(z_ref[...])).astype(o_ref.dtype)


def _mixer_a(z_main, z_tail, conv_a, a_log, dt_bias, a_norm):
    t = z_main.shape[0]
    nh = A_HEADS
    pad8 = lambda p: jnp.pad(p.astype(F32), (0, LANE - p.shape[0])).reshape(1, LANE)
    blk = lambda off: pl.BlockSpec((TB, LANE), lambda h, i: (i, off + h))
    cblk = lambda off: pl.BlockSpec((A_CONV, LANE), lambda h, i: (0, off + h))
    const = pl.BlockSpec((1, LANE), lambda h, i: (0, 0))
    return pl.pallas_call(
        _mixer_a_kernel,
        grid=(nh, t // TB),
        in_specs=[blk(0), blk(nh), blk(2 * nh), blk(3 * nh),
                  pl.BlockSpec((TB, LANE), lambda h, i: (i, 0)),
                  cblk(0), cblk(nh), cblk(2 * nh), const, const, const],
        out_specs=pl.BlockSpec((TB, LANE), lambda h, i: (i, h)),
        out_shape=jax.ShapeDtypeStruct((t, A_V), BF16),
        scratch_shapes=[pltpu.VMEM((3, 8, LANE), F32), pltpu.VMEM((3, 8 + TB, LANE), F32),
                        pltpu.VMEM((A_DK, A_DV), F32)],
        compiler_params=_cparams(("parallel", "arbitrary"), 32),
    )(z_main, z_main, z_main, z_main, z_tail, conv_a, conv_a, conv_a,
      pad8(a_log), pad8(dt_bias), a_norm.reshape(1, A_DV).astype(F32))


def _mixer_b_kernel(q_ref, k_ref, v_ref, zg_ref, tail_ref, wg_ref, bgk_ref, bnorm_ref, o_ref,
                    state_ref, b_sc, q_sc, k_sc, v_sc, od_sc):
    tb = q_ref.shape[0]

    @pl.when(pl.program_id(1) == 0)
    def _():
        state_ref[...] = jnp.zeros_like(state_ref)

    q = q_ref[...] * (B_DK ** -0.5)
    k = k_ref[...]
    v = v_ref[...]
    x = _fdot(tail_ref[...], wg_ref[...]) + bgk_ref[...]
    gk = -_softplus(-x) * (1.0 / B_GATE_NORM)

    row, col, same = _chunk_masks(tb)
    causal = same & (col <= row)
    b = _fdot(jnp.where(causal, 1.0, 0.0), gk)
    b_last = _fdot(jnp.where(same, 1.0, 0.0), gk)
    gk_t = gk.T

    b_sc[...] = b
    q_sc[...] = q
    k_sc[...] = k
    v_sc[...] = v

    rid = lax.broadcasted_iota(jnp.int32, (SUB, B_DK), 0)

    def diag_block(sb, carry):
        r0 = pl.multiple_of(sb * SUB, SUB)
        qb = q_sc[pl.ds(r0, SUB), :]
        bb = b_sc[pl.ds(r0, SUB), :]
        acc = jnp.zeros((SUB, B_DV), F32)
        for j in range(SUB):
            kj = k_sc[pl.ds(r0 + j, 1), :]
            bj = b_sc[pl.ds(r0 + j, 1), :]
            vj = v_sc[pl.ds(r0 + j, 1), :]
            e = jnp.exp(jnp.where(rid >= j, bb - bj, NEG))
            a = jnp.sum(qb * kj * e, axis=-1, keepdims=True)
            acc = acc + a * vj
        od_sc[pl.ds(r0, SUB), :] = acc
        return carry

    lax.fori_loop(0, tb // SUB, diag_block, 0)

    s = state_ref[...]
    outs = []
    crow = lax.broadcasted_iota(jnp.int32, (CHUNK, B_DK), 0)
    for c in range(tb // CHUNK):
        lo, hi = c * CHUNK, (c + 1) * CHUNK
        qc, kc, vc, bc = q[lo:hi], k[lo:hi], v[lo:hi], b[lo:hi]
        o_c = od_sc[lo:hi, :] + _bdot(qc * jnp.exp(bc), s)
        parts = [jnp.zeros((SUB, B_DV), F32)]
        for si in range(1, CHUNK // SUB):
            r0 = si * SUB
            bref = bc[r0:r0 + 1, :]
            qt = qc[r0:r0 + SUB] * jnp.exp(bc[r0:r0 + SUB] - bref)
            kt = kc * jnp.exp(jnp.where(crow < r0, bref - bc, NEG))
            parts.append(_bdot(_bdot_nt(qt, kt), vc))
        outs.append(o_c + jnp.concatenate(parts, axis=0))
        bl = b_last[lo:lo + 1, :]
        kd = kc * jnp.exp(bl - bc)
        bl_col = jnp.sum(gk_t[:, lo:hi], axis=1, keepdims=True)
        s = s * jnp.exp(bl_col) + _bdot_tn(kd, vc)
    state_ref[...] = s
    o = jnp.concatenate(outs, axis=0)

    o = o * lax.rsqrt(jnp.mean(o * o, axis=-1, keepdims=True) + EPS) * bnorm_ref[...]
    o_ref[...] = (o * _silu(zg_ref[...])).astype(o_ref.dtype)


def _mixer_b(z_main, z_tail, w_gk2, b_gk, b_norm):
    t = z_main.shape[0]
    nh = B_HEADS
    qoff = COL_B // LANE
    koff = (COL_B + B_QK) // LANE
    voff = (COL_B + 2 * B_QK) // B_DV
    goff = (COL_B + 2 * B_QK + B_V) // B_DV
    wg = jnp.zeros((LANE, B_QK), F32).at[2 * A_HEADS:2 * A_HEADS + B_GATE_RANK].set(w_gk2.astype(F32))
    return pl.pallas_call(
        _mixer_b_kernel,
        grid=(nh, t // TB),
        in_specs=[pl.BlockSpec((TB, B_DK), lambda h, i: (i, qoff + h)),
                  pl.BlockSpec((TB, B_DK), lambda h, i: (i, koff + h)),
                  pl.BlockSpec((TB, B_DV), lambda h, i: (i, voff + h)),
                  pl.BlockSpec((TB, B_DV), lambda h, i: (i, goff + h)),
                  pl.BlockSpec((TB, LANE), lambda h, i: (i, 0)),
                  pl.BlockSpec((LANE, B_DK), lambda h, i: (0, h)),
                  pl.BlockSpec((1, B_DK), lambda h, i: (0, h)),
                  pl.BlockSpec((1, B_DV), lambda h, i: (0, 0))],
        out_specs=pl.BlockSpec((TB, B_DV), lambda h, i: (i, h)),
        out_shape=jax.ShapeDtypeStruct((t, B_V), BF16),
        scratch_shapes=[pltpu.VMEM((B_DK, B_DV), F32), pltpu.VMEM((TB, B_DK), F32),
                        pltpu.VMEM((TB, B_DK), F32), pltpu.VMEM((TB, B_DK), F32),
                        pltpu.VMEM((TB, B_DV), F32), pltpu.VMEM((TB, B_DV), F32)],
        compiler_params=_cparams(("parallel", "arbitrary"), 32),
    )(z_main, z_main, z_main, z_main, z_tail, wg, b_gk.reshape(1, B_QK).astype(F32),
      b_norm.reshape(1, B_DV).astype(F32))


def _merge_kernel(oa_ref, ob_ref, ma_ref, mb_ref, x_ref, woa_ref, wob_ref, wout_ref, nf_ref,
                  wr_ref, br_ref, x2_ref, h2_ref, lt_ref):
    ya = jnp.dot(oa_ref[...], woa_ref[...], preferred_element_type=F32)
    yb = jnp.dot(ob_ref[...], wob_ref[...], preferred_element_type=F32)
    m = _sigmoid(ma_ref[...]) * ya + _sigmoid(mb_ref[...]) * yb
    x2 = x_ref[...] + jnp.dot(m.astype(BF16), wout_ref[...], preferred_element_type=F32)
    x2_ref[...] = x2
    h2 = x2 * lax.rsqrt(jnp.mean(x2 * x2, axis=-1, keepdims=True) + EPS) * nf_ref[...]
    h2_ref[...] = h2
    lt_ref[...] = lax.dot_general(wr_ref[...], h2, (((1,), (1,)), ((), ())),
                                  preferred_element_type=F32, precision=HI) + br_ref[...]


def _merge(oa_g, ob_g, z_main, x, w_oa, w_ob, w_out, norm_ffn, wr_t, br_t, tm):
    t, d = x.shape
    row = lambda w, c: pl.BlockSpec((tm, w), lambda i: (i, c))
    full = lambda a: pl.BlockSpec(a.shape, lambda i: (0, 0), pipeline_mode=pl.Buffered(1))
    return pl.pallas_call(
        _merge_kernel,
        grid=(t // tm,),
        in_specs=[row(A_V, 0), row(B_V, 0), row(d, COL_MIX // d), row(d, COL_MIX // d + 1), row(d, 0),
                  full(w_oa), full(w_ob), full(w_out), pl.BlockSpec((1, d), lambda i: (0, 0)),
                  full(wr_t), full(br_t)],
        out_specs=[row(d, 0), row(d, 0), pl.BlockSpec((LANE, tm), lambda i: (0, i))],
        out_shape=[jax.ShapeDtypeStruct((t, d), F32), jax.ShapeDtypeStruct((t, d), F32),
                   jax.ShapeDtypeStruct((LANE, t), F32)],
        compiler_params=_cparams(("parallel",), 56),
    )(oa_g, ob_g, z_main, z_main, x, w_oa, w_ob, w_out, norm_ffn.reshape(1, d).astype(F32), wr_t, br_t)


SEG = 256


def _route_kernel(lt_ref, pos_ref, gate_ref, blk_ref, oh_ref):
    t = lt_ref.shape[1]
    rid8 = lax.broadcasted_iota(jnp.int32, (8, t), 0)
    lg = jnp.where(rid8 < N_GROUPS, lt_ref[0:8, :], -jnp.inf)
    gmax = jnp.max(lg, axis=0, keepdims=True)
    g_idx = jnp.min(jnp.where(lg == gmax, rid8, 8), axis=0, keepdims=True)
    p_top = 1.0 / jnp.sum(jnp.exp(lg - gmax), axis=0, keepdims=True)

    les = jnp.zeros((EXP_PER_GROUP, t), F32)
    for g in range(N_GROUPS):
        les = jnp.where(g_idx == g, lt_ref[8 + g * EXP_PER_GROUP:8 + (g + 1) * EXP_PER_GROUP, :], les)
    m1 = jnp.max(les, axis=0, keepdims=True)
    i1 = jnp.min(jnp.where(les == m1, rid8, 8), axis=0, keepdims=True)
    les2 = jnp.where(rid8 == i1, -jnp.inf, les)
    m2 = jnp.max(les2, axis=0, keepdims=True)
    i2 = jnp.min(jnp.where(les2 == m2, rid8, 8), axis=0, keepdims=True)
    r = jnp.exp(m2 - m1)
    gate_ref[...] = jnp.zeros_like(gate_ref)
    gate_ref[0:1, :] = p_top / (1.0 + r)
    gate_ref[1:2, :] = p_top * r / (1.0 + r)
    e1 = g_idx * EXP_PER_GROUP + i1
    e2 = g_idx * EXP_PER_GROUP + i2

    rid32 = lax.broadcasted_iota(jnp.int32, (N_EXPERTS, t), 0)
    oh_ref[0] = jnp.where(rid32 == e1, 1.0, 0.0)
    oh_ref[1] = jnp.where(rid32 == e2, 1.0, 0.0)

    ui = lax.broadcasted_iota(jnp.int32, (SEG, SEG), 0)
    uj = lax.broadcasted_iota(jnp.int32, (SEG, SEG), 1)
    upper = jnp.where(ui < uj, 1.0, 0.0).astype(BF16)
    carry = jnp.zeros((N_EXPERTS, 1), F32)
    ranks = []
    for kk in range(2):
        segs = []
        for sg in range(t // SEG):
            oh = oh_ref[kk, :, sg * SEG:(sg + 1) * SEG]
            pre = jnp.dot(oh.astype(BF16), upper, preferred_element_type=F32) + carry
            segs.append(jnp.sum(oh * pre, axis=0, keepdims=True))
            carry = carry + jnp.sum(oh, axis=1, keepdims=True)
        ranks.append(jnp.concatenate(segs, axis=1))
    counts = carry
    nblk = jnp.floor((counts + (ROW_BLOCK - 1)) * (1.0 / ROW_BLOCK))
    li = lax.broadcasted_iota(jnp.int32, (N_EXPERTS, N_EXPERTS), 0)
    lj = lax.broadcasted_iota(jnp.int32, (N_EXPERTS, N_EXPERTS), 1)
    nb_b = jnp.broadcast_to(nblk, (N_EXPERTS, LANE))
    start_blk = _fdot(jnp.where(lj < li, 1.0, 0.0), nb_b)[:, 0:1]
    end_blk = start_blk + nblk
    start_row = start_blk * ROW_BLOCK

    pos_ref[...] = jnp.zeros_like(pos_ref)
    for kk in range(2):
        base = jnp.sum(oh_ref[kk] * start_row, axis=0, keepdims=True)
        pos_ref[kk:kk + 1, :] = (base + ranks[kk]).astype(jnp.int32)

    nb = blk_ref.shape[1]
    bid = lax.broadcasted_iota(jnp.int32, (N_EXPERTS, nb), 1).astype(F32)
    be = jnp.sum(jnp.where(end_blk <= bid, 1.0, 0.0), axis=0, keepdims=True)
    blk_ref[...] = jnp.zeros_like(blk_ref)
    blk_ref[0:1, :] = jnp.minimum(be, N_EXPERTS - 1.0).astype(jnp.int32)
    blk_ref[1:2, :] = jnp.broadcast_to(end_blk[N_EXPERTS - 1:N_EXPERTS, :], (1, nb)).astype(jnp.int32)


def _route(lt, n_blk):
    t = lt.shape[1]
    nb = -(-n_blk // LANE) * LANE
    return pl.pallas_call(
        _route_kernel,
        out_shape=[jax.ShapeDtypeStruct((8, t), jnp.int32), jax.ShapeDtypeStruct((8, t), F32),
                   jax.ShapeDtypeStruct((8, nb), jnp.int32)],
        scratch_shapes=[pltpu.VMEM((2, N_EXPERTS, t), F32)],
        compiler_params=pltpu.CompilerParams(vmem_limit_bytes=48 * 2 ** 20),
    )(lt)


def _scatter_kernel(pos_ref, h_ref, xs_in_ref, xs_ref, sem):
    del xs_in_ref
    tm = h_ref.shape[0]
    t = pl.num_programs(0) * tm
    base = pl.program_id(0) * tm

    def row_copy(r, kk):
        p = pos_ref[kk * t + base + r]
        return pltpu.make_async_copy(h_ref.at[pl.ds(r, 1), :], xs_ref.at[pl.ds(p, 1), :], sem)

    def start(r, c):
        row_copy(r, 0).start()
        row_copy(r, 1).start()
        return c

    def wait(r, c):
        row_copy(r, 0).wait()
        row_copy(r, 1).wait()
        return c

    lax.fori_loop(0, tm, start, 0)
    lax.fori_loop(0, tm, wait, 0)


def _scatter_rows(pos_flat, h2, n_rows, tm):
    t, d = h2.shape
    return pl.pallas_call(
        _scatter_kernel,
        grid_spec=pltpu.PrefetchScalarGridSpec(
            num_scalar_prefetch=1,
            grid=(t // tm,),
            in_specs=[pl.BlockSpec((tm, d), lambda i, pos: (i, 0)), pl.BlockSpec(memory_space=pl.ANY)],
            out_specs=pl.BlockSpec(memory_space=pl.ANY),
            scratch_shapes=[pltpu.SemaphoreType.DMA(())],
        ),
        out_shape=jax.ShapeDtypeStruct((n_rows, d), h2.dtype),
        input_output_aliases={2: 0},
        compiler_params=_cparams(("arbitrary",), 32),
    )(pos_flat, h2, jnp.zeros((n_rows, d), h2.dtype))


def _expert_kernel(be_ref, nu_ref, x_ref, w1_ref, w3_ref, w2_ref, o_ref, w1b, w3b, w2b):
    b = pl.program_id(0)
    prev = be_ref[jnp.maximum(b - 1, 0)]
    changed = jnp.logical_or(b == 0, be_ref[b] != prev)

    @pl.when(jnp.logical_and(changed, b < nu_ref[0]))
    def _():
        w1b[...] = w1_ref[0].astype(BF16)
        w3b[...] = w3_ref[0].astype(BF16)
        w2b[...] = w2_ref[0].astype(BF16)

    @pl.when(b < nu_ref[0])
    def _():
        xb = x_ref[...].astype(BF16)
        a = jnp.dot(xb, w1b[...], preferred_element_type=F32)
        g = jnp.dot(xb, w3b[...], preferred_element_type=F32)
        o_ref[...] = jnp.dot((_silu(a) * g).astype(BF16), w2b[...], preferred_element_type=F32)

    @pl.when(b >= nu_ref[0])
    def _():
        o_ref[...] = jnp.zeros_like(o_ref)


def _experts(blk_exp, n_used, xs, w1, w3, w2):
    n_rows, d = xs.shape
    n_blk = n_rows // ROW_BLOCK
    rows = lambda b, be, nu: (jnp.minimum(b, nu[0] - 1), 0)
    wsel = lambda b, be, nu: (be[jnp.minimum(b, nu[0] - 1)], 0, 0)
    return pl.pallas_call(
        _expert_kernel,
        grid_spec=pltpu.PrefetchScalarGridSpec(
            num_scalar_prefetch=2,
            grid=(n_blk,),
            in_specs=[pl.BlockSpec((ROW_BLOCK, d), rows),
                      pl.BlockSpec((1, d, D_FF), wsel), pl.BlockSpec((1, d, D_FF), wsel),
                      pl.BlockSpec((1, D_FF, d), wsel)],
            out_specs=pl.BlockSpec((ROW_BLOCK, d), lambda b, be, nu: (b, 0)),
            scratch_shapes=[pltpu.VMEM((d, D_FF), BF16), pltpu.VMEM((d, D_FF), BF16),
                            pltpu.VMEM((D_FF, d), BF16)],
        ),
        out_shape=jax.ShapeDtypeStruct((n_rows, d), F32),
        compiler_params=_cparams(("arbitrary",), 56),
    )(blk_exp, n_used, xs, w1, w3, w2)


def _combine_kernel(pos_ref, x2_ref, gt_ref, nw_ref, yb_ref, o_ref, buf, sem, *, final):
    tm = x2_ref.shape[0]
    n = pl.num_programs(0)
    t = n * tm
    i = pl.program_id(0)

    def row_copy(step, slot, r, kk):
        p = pos_ref[kk * t + step * tm + r]
        return pltpu.make_async_copy(yb_ref.at[pl.ds(p, 1), :], buf.at[slot, kk, pl.ds(r, 1), :], sem.at[slot])

    def issue(step, slot):
        def body(r, c):
            row_copy(step, slot, r, 0).start()
            row_copy(step, slot, r, 1).start()
            return c
        lax.fori_loop(0, tm, body, 0)

    @pl.when(i == 0)
    def _():
        issue(0, 0)

    @pl.when(i + 1 < n)
    def _():
        issue(i + 1, (i + 1) % 2)

    slot = i % 2

    def wbody(r, c):
        row_copy(i, slot, r, 0).wait()
        row_copy(i, slot, r, 1).wait()
        return c
    lax.fori_loop(0, tm, wbody, 0)

    gt = gt_ref[...]
    y = x2_ref[...] + gt[:, 0:1] * buf[slot, 0] + gt[:, 1:2] * buf[slot, 1]
    if final:
        y = y * lax.rsqrt(jnp.mean(y * y, axis=-1, keepdims=True) + EPS) * nw_ref[...]
    o_ref[...] = y


def _combine(pos_flat, x2, gates_t, norm_final, yb, tm, final):
    t, d = x2.shape
    return pl.pallas_call(
        functools.partial(_combine_kernel, final=final),
        grid_spec=pltpu.PrefetchScalarGridSpec(
            num_scalar_prefetch=1,
            grid=(t // tm,),
            in_specs=[pl.BlockSpec((tm, d), lambda i, pos: (i, 0)),
                      pl.BlockSpec((tm, LANE), lambda i, pos: (i, 0)),
                      pl.BlockSpec((1, d), lambda i, pos: (0, 0)),
                      pl.BlockSpec(memory_space=pl.ANY)],
            out_specs=pl.BlockSpec((tm, d), lambda i, pos: (i, 0)),
            scratch_shapes=[pltpu.VMEM((2, 2, tm, d), F32), pltpu.SemaphoreType.DMA((2,))],
        ),
        out_shape=jax.ShapeDtypeStruct((t, d), F32),
        compiler_params=_cparams(("arbitrary",), 32),
    )(pos_flat, x2, gates_t, norm_final.reshape(1, d).astype(F32), yb)


def _layer(x, norm_mix, w_in, conv_a, a_log, dt_bias, a_norm, w_gk2, b_gk, b_norm,
           w_oa, w_ob, w_out, norm_ffn, w_rg, b_rg, w_re, b_re, w1, w3, w2, norm_final, final):
    t, d = x.shape
    tm = min(512, t)

    a_end = 4 * A_QK
    ga_end = a_end + 2 * A_HEADS
    b_end = ga_end + 2 * B_QK + 2 * B_V
    lr_end = b_end + B_GATE_RANK
    w_main = jnp.concatenate([w_in[:, :a_end], w_in[:, lr_end:], w_in[:, ga_end:b_end]], axis=1).astype(BF16)
    w_tail = jnp.concatenate([w_in[:, a_end:ga_end], w_in[:, b_end:lr_end],
                              jnp.zeros((d, LANE - 2 * A_HEADS - B_GATE_RANK), w_in.dtype)], axis=1).astype(BF16)

    h = _rmsnorm(x, norm_mix, tm)
    z_main = _matmul(h, w_main, tm, 1024, F32)
    z_tail = _matmul(h, w_tail, tm, LANE, F32)

    oa_g = _mixer_a(z_main, z_tail, conv_a.astype(F32), a_log, dt_bias, a_norm)
    ob_g = _mixer_b(z_main, z_tail, w_gk2, b_gk, b_norm)

    wr_t = jnp.zeros((LANE, d), F32).at[0:N_GROUPS].set(w_rg.T.astype(F32))
    wr_t = wr_t.at[8:8 + N_EXPERTS].set(w_re.reshape(d, N_EXPERTS).T.astype(F32))
    br_t = jnp.zeros((LANE, 1), F32).at[0:N_GROUPS, 0].set(b_rg.astype(F32))
    br_t = br_t.at[8:8 + N_EXPERTS, 0].set(b_re.reshape(N_EXPERTS).astype(F32))
    x2, h2, lt = _merge(oa_g, ob_g, z_main, x, w_oa.astype(BF16), w_ob.astype(BF16), w_out.astype(BF16),
                        norm_ffn, wr_t, br_t, min(256, t))

    n_blk = (2 * t + ROW_BLOCK - 1) // ROW_BLOCK + N_EXPERTS
    n_rows = n_blk * ROW_BLOCK
    pos, gates, blk = _route(lt, n_blk)
    pos_flat = pos[0:2].reshape(2 * t)
    xs = _scatter_rows(pos_flat, h2, n_rows, min(256, t))
    yb = _experts(blk[0, :n_blk], blk[1, 0:1], xs, w1, w3, w2)
    gates_t = jnp.pad(gates[0:2].T, ((0, 0), (0, LANE - 2)))
    return _combine(pos_flat, x2, gates_t, norm_final, yb, min(128, t), final)


def kernel(x, norm_mix, w_in, conv_a, a_log, dt_bias, a_norm, w_gk2, b_gk, b_norm, w_oa, w_ob, w_out,
           norm_ffn, w_rg, b_rg, w_re, b_re, w1, w3, w2, norm_final):
    bsz, seq, d = x.shape
    assert bsz == 1, "one sequence per call"
    depth = norm_mix.shape[0]
    y = x.reshape(seq, d)
    for l in range(depth):
        y = _layer(y, norm_mix[l], w_in[l], conv_a[l], a_log[l], dt_bias[l], a_norm[l], w_gk2[l], b_gk[l],
                   b_norm[l], w_oa[l], w_ob[l], w_out[l], norm_ffn[l], w_rg[l], b_rg[l], w_re[l], b_re[l],
                   w1[l], w3[l], w2[l], norm_final, l == depth - 1)
    return y.reshape(bsz, seq, d)
```

```python
import functools

import jax
import jax.numpy as jnp
from jax import lax
from jax.experimental import pallas as pl
from jax.experimental.pallas import tpu as pltpu

D_MODEL = 2048
CHUNK = 64
EPS = 1e-6
A_HEADS, A_DK, A_DV, A_CONV = 8, 128, 128, 4
A_QK, A_V = A_HEADS * A_DK, A_HEADS * A_DV
B_HEADS, B_DK, B_DV, B_GATE_RANK, B_GATE_NORM = 4, 128, 256, 16, 16.0
B_QK, B_V = B_HEADS * B_DK, B_HEADS * B_DV
N_GROUPS, EXP_PER_GROUP, D_FF = 4, 8, 512
N_EXPERTS = N_GROUPS * EXP_PER_GROUP
ROW_BLOCK = 128
LANE = 128
SUB = 16
TB = 256
A_HG = 4
NEG = -1e30

F32 = jnp.float32
BF16 = jnp.bfloat16
HI = lax.Precision.HIGHEST

COL_A = 0
COL_MIX = 4 * A_QK
COL_B = COL_MIX + 2 * D_MODEL
N_MAIN = COL_B + 2 * B_QK + 2 * B_V


def _cparams(sem, vmem_mib):
    return pltpu.CompilerParams(dimension_semantics=sem, vmem_limit_bytes=vmem_mib * 2 ** 20)


def _bdot(a, b):
    return jnp.dot(a.astype(BF16), b.astype(BF16), preferred_element_type=F32)


def _bdot_nt(a, b):
    return lax.dot_general(a.astype(BF16), b.astype(BF16), (((1,), (1,)), ((), ())),
                           preferred_element_type=F32)


def _bdot_tn(a, b):
    return lax.dot_general(a.astype(BF16), b.astype(BF16), (((0,), (0,)), ((), ())),
                           preferred_element_type=F32)


def _fdot(a, b):
    return jnp.dot(a, b, preferred_element_type=F32, precision=HI)


def _sigmoid(x):
    return 1.0 / (1.0 + jnp.exp(-x))


def _silu(x):
    return x * _sigmoid(x)


def _softplus(x):
    return jnp.maximum(x, 0.0) + jnp.log(1.0 + jnp.exp(-jnp.abs(x)))


def _rmsnorm_kernel(x_ref, w_ref, o_ref):
    x = x_ref[...]
    ms = jnp.mean(x * x, axis=-1, keepdims=True)
    o_ref[...] = (x * lax.rsqrt(ms + EPS) * w_ref[...]).astype(o_ref.dtype)


def _rmsnorm(x, w, tm):
    t, d = x.shape
    return pl.pallas_call(
        _rmsnorm_kernel,
        grid=(t // tm,),
        in_specs=[pl.BlockSpec((tm, d), lambda i: (i, 0)), pl.BlockSpec((1, d), lambda i: (0, 0))],
        out_specs=pl.BlockSpec((tm, d), lambda i: (i, 0)),
        out_shape=jax.ShapeDtypeStruct((t, d), BF16),
        compiler_params=_cparams(("parallel",), 32),
    )(x, w.reshape(1, d))


def _mm_kernel(a_ref, b_ref, o_ref):
    o_ref[...] = jnp.dot(a_ref[...], b_ref[...], preferred_element_type=F32).astype(o_ref.dtype)


def _matmul(a, b, tm, tn, out_dtype):
    m, k = a.shape
    n = b.shape[1]
    return pl.pallas_call(
        _mm_kernel,
        grid=(n // tn, m // tm),
        in_specs=[pl.BlockSpec((tm, k), lambda j, i: (i, 0)), pl.BlockSpec((k, tn), lambda j, i: (0, j))],
        out_specs=pl.BlockSpec((tm, tn), lambda j, i: (i, j)),
        out_shape=jax.ShapeDtypeStruct((m, n), out_dtype),
        compiler_params=_cparams(("parallel", "parallel"), 48),
    )(a, b)


def _chunk_masks(tb):
    row = lax.broadcasted_iota(jnp.int32, (tb, tb), 0)
    col = lax.broadcasted_iota(jnp.int32, (tb, tb), 1)
    same = (row // CHUNK) == (col // CHUNK)
    return row, col, same, same & (col <= row), same & (col < row)


def _lane_pick(x, idx):
    lane = lax.broadcasted_iota(jnp.int32, x.shape, 1)
    return jnp.sum(jnp.where(lane == idx, x, 0.0), axis=-1, keepdims=True)


def _causal_conv_silu(x_ref, w_ref, halo_ref, cbuf_ref, idx, tb):
    cbuf_ref[idx, 0:8, :] = halo_ref[idx]
    cbuf_ref[idx, 8:8 + tb, :] = x_ref[...]
    halo_ref[idx] = x_ref[tb - 8:tb, :]
    w = w_ref[...]
    acc = w[A_CONV - 1:A_CONV, :] * x_ref[...]
    for j in range(A_CONV - 1):
        off = 8 - (A_CONV - 1) + j
        acc = acc + w[j:j + 1, :] * cbuf_ref[idx, off:off + tb, :]
    return _silu(acc)


def _gates_a_kernel(tail_ref, alog_ref, dtb_ref, ga_ref, gct_ref):
    tb = tail_ref.shape[0]
    gl = tail_ref[...]
    beta = _sigmoid(gl)
    g = -jnp.exp(alog_ref[...]) * _softplus(gl + dtb_ref[...])
    _, _, same, causal, _ = _chunk_masks(tb)
    gc = _fdot(jnp.where(causal, 1.0, 0.0), g)
    glast = _fdot(jnp.where(same, 1.0, 0.0), g)
    lane = lax.broadcasted_iota(jnp.int32, (tb, LANE), 1)
    ga_ref[...] = jnp.where(lane < A_HEADS, beta,
                            jnp.where(lane < 2 * A_HEADS, gc, pltpu.roll(glast, A_HEADS, 1)))
    gct_ref[...] = gc.T[A_HEADS:2 * A_HEADS, :]


def _gates_a(z_tail, a_log, dt_bias):
    t = z_tail.shape[0]
    pad = lambda p: jnp.pad(p.astype(F32), (A_HEADS, LANE - 2 * A_HEADS)).reshape(1, LANE)
    const = pl.BlockSpec((1, LANE), lambda i: (0, 0))
    return pl.pallas_call(
        _gates_a_kernel,
        grid=(t // TB,),
        in_specs=[pl.BlockSpec((TB, LANE), lambda i: (i, 0)), const, const],
        out_specs=[pl.BlockSpec((TB, LANE), lambda i: (i, 0)), pl.BlockSpec((A_HEADS, TB), lambda i: (0, i))],
        out_shape=[jax.ShapeDtypeStruct((t, LANE), F32), jax.ShapeDtypeStruct((A_HEADS, t), F32)],
        compiler_params=_cparams(("parallel",), 32),
    )(z_tail, pad(a_log), pad(dt_bias))


def _delta_heads(qs, ks, vs, ga, gc_rows, hs, sts, tb):
    n = len(qs)
    rng = range(n)
    _, _, _, causal, strict = _chunk_masks(tb)
    qs = [q * lax.rsqrt(jnp.sum(q * q, axis=-1, keepdims=True) + EPS) * (A_DK ** -0.5) for q in qs]
    ks = [k * lax.rsqrt(jnp.sum(k * k, axis=-1, keepdims=True) + EPS) for k in ks]
    beta = [_lane_pick(ga, h) for h in hs]
    gc = [_lane_pick(ga, h + A_HEADS) for h in hs]
    glast = [_lane_pick(ga, h + 2 * A_HEADS) for h in hs]
    decay = [jnp.exp(jnp.where(causal, gc[i] - gc_rows[i], NEG)) for i in rng]
    kb = [ks[i] * beta[i] for i in rng]

    n_pow = [jnp.where(strict, _bdot_nt(kb[i], ks[i]) * decay[i], 0.0) * -1.0 for i in rng]
    t_mat = list(n_pow)
    lvl = 2
    while lvl < CHUNK:
        n_pow = [_bdot(m, m) for m in n_pow]
        t_mat = [t_mat[i] + n_pow[i] + _bdot(t_mat[i], n_pow[i]) for i in rng]
        lvl *= 2

    egc = [jnp.exp(g) for g in gc]
    rhs = [jnp.concatenate([vs[i] * beta[i], kb[i] * egc[i]], axis=1) for i in rng]
    uw = [rhs[i] + _bdot(t_mat[i], rhs[i]) for i in rng]
    qk = [_bdot_nt(qs[i], ks[i]) * decay[i] for i in rng]
    qkuw = [_bdot(qk[i], uw[i]) for i in rng]
    o_local = [x[:, :A_DV] for x in qkuw]
    q_eff = [qs[i] * egc[i] - qkuw[i][:, A_DV:] for i in rng]
    k_dec = [ks[i] * jnp.exp(glast[i] - gc[i]) for i in rng]
    eg_last = [jnp.exp(g) for g in glast]

    sts = list(sts)
    outs = [[] for _ in rng]
    for c in range(tb // CHUNK):
        lo, hi = c * CHUNK, (c + 1) * CHUNK
        bg = [_bdot_tn(uw[i][lo:hi], k_dec[i][lo:hi]) for i in rng]
        for i in rng:
            outs[i].append(o_local[i][lo:hi] + _bdot_nt(q_eff[i][lo:hi], sts[i]))
        sts = [sts[i] * eg_last[i][lo:lo + 1, :] + bg[i][:A_DV] - _bdot(sts[i], bg[i][A_DV:]) for i in rng]
    return [jnp.concatenate(o, axis=0) for o in outs], sts


def _mixer_a_kernel(xq_ref, xk_ref, xv_ref, z_ref, ga_ref, gct_ref, wq_ref, wk_ref, wv_ref,
                    anorm_ref, o_ref, halo_ref, cbuf_ref, state_ref):
    tb = xq_ref.shape[0]
    hg = xq_ref.shape[1] // A_DK

    @pl.when(pl.program_id(1) == 0)
    def _():
        halo_ref[...] = jnp.zeros_like(halo_ref)
        state_ref[...] = jnp.zeros_like(state_ref)

    q = _causal_conv_silu(xq_ref, wq_ref, halo_ref, cbuf_ref, 0, tb)
    k = _causal_conv_silu(xk_ref, wk_ref, halo_ref, cbuf_ref, 1, tb)
    v = _causal_conv_silu(xv_ref, wv_ref, halo_ref, cbuf_ref, 2, tb)
    ga = ga_ref[...]
    hs = [pl.program_id(0) * hg + j for j in range(hg)]
    sls = [slice(j * A_DK, (j + 1) * A_DK) for j in range(hg)]
    outs, sts = _delta_heads([q[:, s] for s in sls], [k[:, s] for s in sls], [v[:, s] for s in sls], ga,
                             [gct_ref[pl.ds(h, 1), :] for h in hs], hs, [state_ref[j] for j in range(hg)], tb)
    for j in range(hg):
        state_ref[j] = sts[j]
        o = outs[j]
        o = o * lax.rsqrt(jnp.mean(o * o, axis=-1, keepdims=True) + EPS) * anorm_ref[...]
        o_ref[:, sls[j]] = (o * _silu(z_ref[:, sls[j]])).astype(o_ref.dtype)


def _mixer_a(z_main, z_tail, conv_a, a_log, dt_bias, a_norm):
    t = z_main.shape[0]
    ng = A_HEADS // A_HG
    wid = A_HG * A_DK
    ga, gct = _gates_a(z_tail, a_log, dt_bias)
    blk = lambda off: pl.BlockSpec((TB, wid), lambda g, i: (i, off + g))
    cblk = lambda off: pl.BlockSpec((A_CONV, wid), lambda g, i: (0, off + g))
    return pl.pallas_call(
        _mixer_a_kernel,
        grid=(ng, t // TB),
        in_specs=[blk(0), blk(ng), blk(2 * ng), blk(3 * ng),
                  pl.BlockSpec((TB, LANE), lambda g, i: (i, 0)),
                  pl.BlockSpec((A_HEADS, TB), lambda g, i: (0, i)),
                  cblk(0), cblk(ng), cblk(2 * ng), pl.BlockSpec((1, LANE), lambda g, i: (0, 0))],
        out_specs=pl.BlockSpec((TB, wid), lambda g, i: (i, g)),
        out_shape=jax.ShapeDtypeStruct((t, A_V), BF16),
        scratch_shapes=[pltpu.VMEM((3, 8, wid), F32), pltpu.VMEM((3, 8 + TB, wid), F32),
                        pltpu.VMEM((A_HG, A_DV, A_DK), F32)],
        compiler_params=_cparams(("parallel", "arbitrary"), 48),
    )(z_main, z_main, z_main, z_main, ga, gct, conv_a, conv_a, conv_a,
      a_norm.reshape(1, A_DV).astype(F32))


def _cumsum_rows(mask, x):
    hi = x.astype(BF16)
    r1 = x - hi.astype(F32)
    mid = r1.astype(BF16)
    lo = (r1 - mid.astype(F32)).astype(BF16)
    dot = lambda p: jnp.dot(mask, p, preferred_element_type=F32)
    return dot(hi) + dot(mid) + dot(lo)


def _gla_chunk_scores(qc, kc, bc):
    rid = lax.broadcasted_iota(jnp.int32, (SUB, B_DK), 0)
    rid_lo = lax.broadcasted_iota(jnp.int32, (SUB // 2, B_DK), 0) + SUB // 2
    lane_c = lax.broadcasted_iota(jnp.int32, (SUB, CHUNK), 1)
    crow = lax.broadcasted_iota(jnp.int32, (CHUNK, B_DK), 0)
    half = SUB // 2
    rows = []
    for si in range(CHUNK // SUB):
        r0 = si * SUB
        qb, bb = qc[r0:r0 + SUB], bc[r0:r0 + SUB]
        ys = []
        for j in range(SUB):
            bj = bc[r0 + j:r0 + j + 1]
            if j < half:
                ys.append(qb * jnp.exp(jnp.where(rid >= j, bb - bj, NEG)))
            else:
                ys.append(jnp.zeros((half, B_DK), F32))
                ys.append(qb[half:] * jnp.exp(jnp.where(rid_lo >= j, bb[half:] - bj, NEG)))
        r = _bdot_nt(jnp.concatenate(ys, axis=0), kc)
        blk = jnp.zeros((SUB, CHUNK), F32)
        for j in range(SUB):
            blk = jnp.where(lane_c == r0 + j, r[j * SUB:(j + 1) * SUB], blk)
        if si > 0:
            bref = bc[r0:r0 + 1]
            qt = qb * jnp.exp(bb - bref)
            kt = kc * jnp.exp(jnp.where(crow < r0, bref - bc, NEG))
            blk = blk + _bdot_nt(qt, kt)
        rows.append(blk)
    return jnp.concatenate(rows, axis=0)


def _mixer_b_kernel(q_ref, k_ref, v_ref, zg_ref, tail_ref, wg_ref, bgk_ref, bnorm_ref, o_ref, state_ref):
    tb = q_ref.shape[0]
    nh = q_ref.shape[1] // B_DK

    @pl.when(pl.program_id(0) == 0)
    def _():
        state_ref[...] = jnp.zeros_like(state_ref)

    x = _fdot(tail_ref[...], wg_ref[...]) + bgk_ref[...]
    gk = -_softplus(-x) * (1.0 / B_GATE_NORM)
    _, _, _, causal, _ = _chunk_masks(tb)
    b_all = _cumsum_rows(jnp.where(causal, 1.0, 0.0).astype(BF16), gk)

    sts = [state_ref[h] for h in range(nh)]
    outs = [[] for _ in range(nh)]
    for c in range(tb // CHUNK):
        lo, hi = c * CHUNK, (c + 1) * CHUNK
        for h in range(nh):
            qc = q_ref[lo:hi, h * B_DK:(h + 1) * B_DK] * (B_DK ** -0.5)
            kc = k_ref[lo:hi, h * B_DK:(h + 1) * B_DK]
            vc = v_ref[lo:hi, h * B_DV:(h + 1) * B_DV]
            bc = b_all[lo:hi, h * B_DK:(h + 1) * B_DK]
            a_c = _gla_chunk_scores(qc, kc, bc)
            bl = bc[CHUNK - 1:CHUNK]
            outs[h].append(_bdot(a_c, vc) + _bdot_nt(qc * jnp.exp(bc), sts[h]))
            sts[h] = sts[h] * jnp.exp(bl) + _bdot_tn(vc, kc * jnp.exp(bl - bc))
    for h in range(nh):
        state_ref[h] = sts[h]
        o = jnp.concatenate(outs[h], axis=0)
        o = o * lax.rsqrt(jnp.mean(o * o, axis=-1, keepdims=True) + EPS) * bnorm_ref[...]
        sl = slice(h * B_DV, (h + 1) * B_DV)
        o_ref[:, sl] = (o * _silu(zg_ref[:, sl])).astype(o_ref.dtype)


def _mixer_b(z_main, z_tail, w_gk2, b_gk, b_norm):
    t = z_main.shape[0]
    wg = jnp.zeros((LANE, B_QK), F32).at[2 * A_HEADS:2 * A_HEADS + B_GATE_RANK].set(w_gk2.astype(F32))
    col = lambda w, off: pl.BlockSpec((TB, w), lambda i: (i, off // w))
    full = lambda r, c: pl.BlockSpec((r, c), lambda i: (0, 0))
    return pl.pallas_call(
        _mixer_b_kernel,
        grid=(t // TB,),
        in_specs=[col(B_QK, COL_B), col(B_QK, COL_B + B_QK), col(B_V, COL_B + 2 * B_QK),
                  col(B_V, COL_B + 2 * B_QK + B_V), col(LANE, 0),
                  full(LANE, B_QK), full(1, B_QK), full(1, B_DV)],
        out_specs=pl.BlockSpec((TB, B_V), lambda i: (i, 0)),
        out_shape=jax.ShapeDtypeStruct((t, B_V), BF16),
        scratch_shapes=[pltpu.VMEM((B_HEADS, B_DV, B_DK), F32)],
        compiler_params=_cparams(("arbitrary",), 48),
    )(z_main, z_main, z_main, z_main, z_tail, wg, b_gk.reshape(1, B_QK).astype(F32),
      b_norm.reshape(1, B_DV).astype(F32))


def _merge_kernel(oa_ref, ob_ref, ma_ref, mb_ref, x_ref, woa_ref, wob_ref, wout_ref, nf_ref,
                  wr_ref, br_ref, x2_ref, h2_ref, lt_ref):
    ya = jnp.dot(oa_ref[...], woa_ref[...], preferred_element_type=F32)
    yb = jnp.dot(ob_ref[...], wob_ref[...], preferred_element_type=F32)
    m = _sigmoid(ma_ref[...]) * ya + _sigmoid(mb_ref[...]) * yb
    x2 = x_ref[...] + jnp.dot(m.astype(BF16), wout_ref[...], preferred_element_type=F32)
    x2_ref[...] = x2
    h2 = x2 * lax.rsqrt(jnp.mean(x2 * x2, axis=-1, keepdims=True) + EPS) * nf_ref[...]
    h2_ref[...] = h2
    lt_ref[...] = lax.dot_general(wr_ref[...], h2, (((1,), (1,)), ((), ())),
                                  preferred_element_type=F32, precision=HI) + br_ref[...]


def _merge(oa_g, ob_g, z_main, x, w_oa, w_ob, w_out, norm_ffn, wr_t, br_t, tm):
    t, d = x.shape
    row = lambda w, c: pl.BlockSpec((tm, w), lambda i: (i, c))
    full = lambda a: pl.BlockSpec(a.shape, lambda i: (0, 0), pipeline_mode=pl.Buffered(1))
    return pl.pallas_call(
        _merge_kernel,
        grid=(t // tm,),
        in_specs=[row(A_V, 0), row(B_V, 0), row(d, COL_MIX // d), row(d, COL_MIX // d + 1), row(d, 0),
                  full(w_oa), full(w_ob), full(w_out), pl.BlockSpec((1, d), lambda i: (0, 0)),
                  full(wr_t), full(br_t)],
        out_specs=[row(d, 0), row(d, 0), pl.BlockSpec((LANE, tm), lambda i: (0, i))],
        out_shape=[jax.ShapeDtypeStruct((t, d), F32), jax.ShapeDtypeStruct((t, d), F32),
                   jax.ShapeDtypeStruct((LANE, t), F32)],
        compiler_params=_cparams(("parallel",), 56),
    )(oa_g, ob_g, z_main, z_main, x, w_oa, w_ob, w_out, norm_ffn.reshape(1, d).astype(F32), wr_t, br_t)


SEG = 256


def _route_kernel(lt_ref, pos_ref, gate_ref, blk_ref, oh_ref):
    t = lt_ref.shape[1]
    rid8 = lax.broadcasted_iota(jnp.int32, (8, t), 0)
    lg = jnp.where(rid8 < N_GROUPS, lt_ref[0:8, :], -jnp.inf)
    gmax = jnp.max(lg, axis=0, keepdims=True)
    g_idx = jnp.min(jnp.where(lg == gmax, rid8, 8), axis=0, keepdims=True)
    p_top = 1.0 / jnp.sum(jnp.exp(lg - gmax), axis=0, keepdims=True)

    les = jnp.zeros((EXP_PER_GROUP, t), F32)
    for g in range(N_GROUPS):
        les = jnp.where(g_idx == g, lt_ref[8 + g * EXP_PER_GROUP:8 + (g + 1) * EXP_PER_GROUP, :], les)
    m1 = jnp.max(les, axis=0, keepdims=True)
    i1 = jnp.min(jnp.where(les == m1, rid8, 8), axis=0, keepdims=True)
    les2 = jnp.where(rid8 == i1, -jnp.inf, les)
    m2 = jnp.max(les2, axis=0, keepdims=True)
    i2 = jnp.min(jnp.where(les2 == m2, rid8, 8), axis=0, keepdims=True)
    r = jnp.exp(m2 - m1)
    gate_ref[...] = jnp.zeros_like(gate_ref)
    gate_ref[0:1, :] = p_top / (1.0 + r)
    gate_ref[1:2, :] = p_top * r / (1.0 + r)
    e1 = g_idx * EXP_PER_GROUP + i1
    e2 = g_idx * EXP_PER_GROUP + i2

    rid32 = lax.broadcasted_iota(jnp.int32, (N_EXPERTS, t), 0)
    oh_ref[0] = jnp.where(rid32 == e1, 1.0, 0.0)
    oh_ref[1] = jnp.where(rid32 == e2, 1.0, 0.0)

    ui = lax.broadcasted_iota(jnp.int32, (SEG, SEG), 0)
    uj = lax.broadcasted_iota(jnp.int32, (SEG, SEG), 1)
    upper = jnp.where(ui < uj, 1.0, 0.0).astype(BF16)
    carry = jnp.zeros((N_EXPERTS, 1), F32)
    ranks = []
    for kk in range(2):
        segs = []
        for sg in range(t // SEG):
            oh = oh_ref[kk, :, sg * SEG:(sg + 1) * SEG]
            pre = jnp.dot(oh.astype(BF16), upper, preferred_element_type=F32) + carry
            segs.append(jnp.sum(oh * pre, axis=0, keepdims=True))
            carry = carry + jnp.sum(oh, axis=1, keepdims=True)
        ranks.append(jnp.concatenate(segs, axis=1))
    counts = carry
    nblk = jnp.floor((counts + (ROW_BLOCK - 1)) * (1.0 / ROW_BLOCK))
    li = lax.broadcasted_iota(jnp.int32, (N_EXPERTS, N_EXPERTS), 0)
    lj = lax.broadcasted_iota(jnp.int32, (N_EXPERTS, N_EXPERTS), 1)
    nb_b = jnp.broadcast_to(nblk, (N_EXPERTS, LANE))
    start_blk = _fdot(jnp.where(lj < li, 1.0, 0.0), nb_b)[:, 0:1]
    end_blk = start_blk + nblk
    start_row = start_blk * ROW_BLOCK

    pos_ref[...] = jnp.zeros_like(pos_ref)
    for kk in range(2):
        base = jnp.sum(oh_ref[kk] * start_row, axis=0, keepdims=True)
        pos_ref[kk:kk + 1, :] = (base + ranks[kk]).astype(jnp.int32)

    nb = blk_ref.shape[1]
    bid = lax.broadcasted_iota(jnp.int32, (N_EXPERTS, nb), 1).astype(F32)
    be = jnp.sum(jnp.where(end_blk <= bid, 1.0, 0.0), axis=0, keepdims=True)
    blk_ref[...] = jnp.zeros_like(blk_ref)
    blk_ref[0:1, :] = jnp.minimum(be, N_EXPERTS - 1.0).astype(jnp.int32)
    blk_ref[1:2, :] = jnp.broadcast_to(end_blk[N_EXPERTS - 1:N_EXPERTS, :], (1, nb)).astype(jnp.int32)


def _route(lt, n_blk):
    t = lt.shape[1]
    nb = -(-n_blk // LANE) * LANE
    return pl.pallas_call(
        _route_kernel,
        out_shape=[jax.ShapeDtypeStruct((8, t), jnp.int32), jax.ShapeDtypeStruct((8, t), F32),
                   jax.ShapeDtypeStruct((8, nb), jnp.int32)],
        scratch_shapes=[pltpu.VMEM((2, N_EXPERTS, t), F32)],
        compiler_params=pltpu.CompilerParams(vmem_limit_bytes=48 * 2 ** 20),
    )(lt)


def _scatter_kernel(pos_ref, h_ref, xs_in_ref, xs_ref, sem):
    del xs_in_ref
    tm = h_ref.shape[0]
    t = pl.num_programs(0) * tm
    base = pl.program_id(0) * tm

    def row_copy(r, kk):
        p = pos_ref[kk * t + base + r]
        return pltpu.make_async_copy(h_ref.at[pl.ds(r, 1), :], xs_ref.at[pl.ds(p, 1), :], sem)

    def start(r, c):
        row_copy(r, 0).start()
        row_copy(r, 1).start()
        return c

    def wait(r, c):
        row_copy(r, 0).wait()
        row_copy(r, 1).wait()
        return c

    lax.fori_loop(0, tm, start, 0)
    lax.fori_loop(0, tm, wait, 0)


def _scatter_rows(pos_flat, h2, n_rows, tm):
    t, d = h2.shape
    return pl.pallas_call(
        _scatter_kernel,
        grid_spec=pltpu.PrefetchScalarGridSpec(
            num_scalar_prefetch=1,
            grid=(t // tm,),
            in_specs=[pl.BlockSpec((tm, d), lambda i, pos: (i, 0)), pl.BlockSpec(memory_space=pl.ANY)],
            out_specs=pl.BlockSpec(memory_space=pl.ANY),
            scratch_shapes=[pltpu.SemaphoreType.DMA(())],
        ),
        out_shape=jax.ShapeDtypeStruct((n_rows, d), h2.dtype),
        input_output_aliases={2: 0},
        compiler_params=_cparams(("arbitrary",), 32),
    )(pos_flat, h2, jnp.zeros((n_rows, d), h2.dtype))


def _expert_kernel(be_ref, nu_ref, x_ref, w1_ref, w3_ref, w2_ref, o_ref, w1b, w3b, w2b):
    b = pl.program_id(0)
    prev = be_ref[jnp.maximum(b - 1, 0)]
    changed = jnp.logical_or(b == 0, be_ref[b] != prev)

    @pl.when(jnp.logical_and(changed, b < nu_ref[0]))
    def _():
        w1b[...] = w1_ref[0].astype(BF16)
        w3b[...] = w3_ref[0].astype(BF16)
        w2b[...] = w2_ref[0].astype(BF16)

    @pl.when(b < nu_ref[0])
    def _():
        xb = x_ref[...].astype(BF16)
        a = jnp.dot(xb, w1b[...], preferred_element_type=F32)
        g = jnp.dot(xb, w3b[...], preferred_element_type=F32)
        o_ref[...] = jnp.dot((_silu(a) * g).astype(BF16), w2b[...], preferred_element_type=F32)

    @pl.when(b >= nu_ref[0])
    def _():
        o_ref[...] = jnp.zeros_like(o_ref)


def _experts(blk_exp, n_used, xs, w1, w3, w2):
    n_rows, d = xs.shape
    n_blk = n_rows // ROW_BLOCK
    rows = lambda b, be, nu: (jnp.minimum(b, nu[0] - 1), 0)
    wsel = lambda b, be, nu: (be[jnp.minimum(b, nu[0] - 1)], 0, 0)
    return pl.pallas_call(
        _expert_kernel,
        grid_spec=pltpu.PrefetchScalarGridSpec(
            num_scalar_prefetch=2,
            grid=(n_blk,),
            in_specs=[pl.BlockSpec((ROW_BLOCK, d), rows),
                      pl.BlockSpec((1, d, D_FF), wsel), pl.BlockSpec((1, d, D_FF), wsel),
                      pl.BlockSpec((1, D_FF, d), wsel)],
            out_specs=pl.BlockSpec((ROW_BLOCK, d), lambda b, be, nu: (b, 0)),
            scratch_shapes=[pltpu.VMEM((d, D_FF), BF16), pltpu.VMEM((d, D_FF), BF16),
                            pltpu.VMEM((D_FF, d), BF16)],
        ),
        out_shape=jax.ShapeDtypeStruct((n_rows, d), F32),
        compiler_params=_cparams(("arbitrary",), 56),
    )(blk_exp, n_used, xs, w1, w3, w2)


def _combine_kernel(pos_ref, x2_ref, gt_ref, nw_ref, yb_ref, o_ref, buf, sem, *, final):
    tm = x2_ref.shape[0]
    n = pl.num_programs(0)
    t = n * tm
    i = pl.program_id(0)

    def row_copy(step, slot, r, kk):
        p = pos_ref[kk * t + step * tm + r]
        return pltpu.make_async_copy(yb_ref.at[pl.ds(p, 1), :], buf.at[slot, kk, pl.ds(r, 1), :], sem.at[slot])

    def issue(step, slot):
        def body(r, c):
            row_copy(step, slot, r, 0).start()
            row_copy(step, slot, r, 1).start()
            return c
        lax.fori_loop(0, tm, body, 0)

    @pl.when(i == 0)
    def _():
        issue(0, 0)

    @pl.when(i + 1 < n)
    def _():
        issue(i + 1, (i + 1) % 2)

    slot = i % 2

    def wbody(r, c):
        row_copy(i, slot, r, 0).wait()
        row_copy(i, slot, r, 1).wait()
        return c
    lax.fori_loop(0, tm, wbody, 0)

    gt = gt_ref[...]
    y = x2_ref[...] + gt[:, 0:1] * buf[slot, 0] + gt[:, 1:2] * buf[slot, 1]
    if final:
        y = y * lax.rsqrt(jnp.mean(y * y, axis=-1, keepdims=True) + EPS) * nw_ref[...]
    o_ref[...] = y


def _combine(pos_flat, x2, gates_t, norm_final, yb, tm, final):
    t, d = x2.shape
    return pl.pallas_call(
        functools.partial(_combine_kernel, final=final),
        grid_spec=pltpu.PrefetchScalarGridSpec(
            num_scalar_prefetch=1,
            grid=(t // tm,),
            in_specs=[pl.BlockSpec((tm, d), lambda i, pos: (i, 0)),
                      pl.BlockSpec((tm, LANE), lambda i, pos: (i, 0)),
                      pl.BlockSpec((1, d), lambda i, pos: (0, 0)),
                      pl.BlockSpec(memory_space=pl.ANY)],
            out_specs=pl.BlockSpec((tm, d), lambda i, pos: (i, 0)),
            scratch_shapes=[pltpu.VMEM((2, 2, tm, d), F32), pltpu.SemaphoreType.DMA((2,))],
        ),
        out_shape=jax.ShapeDtypeStruct((t, d), F32),
        compiler_params=_cparams(("arbitrary",), 32),
    )(pos_flat, x2, gates_t, norm_final.reshape(1, d).astype(F32), yb)


def _layer(x, norm_mix, w_in, conv_a, a_log, dt_bias, a_norm, w_gk2, b_gk, b_norm,
           w_oa, w_ob, w_out, norm_ffn, w_rg, b_rg, w_re, b_re, w1, w3, w2, norm_final, final):
    t, d = x.shape
    tm = min(512, t)

    a_end = 4 * A_QK
    ga_end = a_end + 2 * A_HEADS
    b_end = ga_end + 2 * B_QK + 2 * B_V
    lr_end = b_end + B_GATE_RANK
    w_main = jnp.concatenate([w_in[:, :a_end], w_in[:, lr_end:], w_in[:, ga_end:b_end]], axis=1).astype(BF16)
    w_tail = jnp.concatenate([w_in[:, a_end:ga_end], w_in[:, b_end:lr_end],
                              jnp.zeros((d, LANE - 2 * A_HEADS - B_GATE_RANK), w_in.dtype)], axis=1).astype(BF16)

    h = _rmsnorm(x, norm_mix, tm)
    z_main = _matmul(h, w_main, tm, 1024, F32)
    z_tail = _matmul(h, w_tail, tm, LANE, F32)

    oa_g = _mixer_a(z_main, z_tail, conv_a.astype(F32), a_log, dt_bias, a_norm)
    ob_g = _mixer_b(z_main, z_tail, w_gk2, b_gk, b_norm)

    wr_t = jnp.zeros((LANE, d), F32).at[0:N_GROUPS].set(w_rg.T.astype(F32))
    wr_t = wr_t.at[8:8 + N_EXPERTS].set(w_re.reshape(d, N_EXPERTS).T.astype(F32))
    br_t = jnp.zeros((LANE, 1), F32).at[0:N_GROUPS, 0].set(b_rg.astype(F32))
    br_t = br_t.at[8:8 + N_EXPERTS, 0].set(b_re.reshape(N_EXPERTS).astype(F32))
    x2, h2, lt = _merge(oa_g, ob_g, z_main, x, w_oa.astype(BF16), w_ob.astype(BF16), w_out.astype(BF16),
                        norm_ffn, wr_t, br_t, min(256, t))

    n_blk = (2 * t + ROW_BLOCK - 1) // ROW_BLOCK + N_EXPERTS
    n_rows = n_blk * ROW_BLOCK
    pos, gates, blk = _route(lt, n_blk)
    pos_flat = pos[0:2].reshape(2 * t)
    xs = _scatter_rows(pos_flat, h2, n_rows, min(256, t))
    yb = _experts(blk[0, :n_blk], blk[1, 0:1], xs, w1, w3, w2)
    gates_t = jnp.pad(gates[0:2].T, ((0, 0), (0, LANE - 2)))
    return _combine(pos_flat, x2, gates_t, norm_final, yb, min(128, t), final)


def kernel(x, norm_mix, w_in, conv_a, a_log, dt_bias, a_norm, w_gk2, b_gk, b_norm, w_oa, w_ob, w_out,
           norm_ffn, w_rg, b_rg, w_re, b_re, w1, w3, w2, norm_final):
    bsz, seq, d = x.shape
    assert bsz == 1, "one sequence per call"
    depth = norm_mix.shape[0]
    y = x.reshape(seq, d)
    for l in range(depth):
        y = _layer(y, norm_mix[l], w_in[l], conv_a[l], a_log[l], dt_bias[l], a_norm[l], w_gk2[l], b_gk[l],
                   b_norm[l], w_oa[l], w_ob[l], w_out[l], norm_ffn[l], w_rg[l], b_rg[l], w_re[l], b_re[l],
                   w1[l], w3[l], w2[l], norm_final, l == depth - 1)
    return y.reshape(bsz, seq, d)
```

```python
import functools

import jax
import jax.numpy as jnp
from jax import lax
from jax.experimental import pallas as pl
from jax.experimental.pallas import tpu as pltpu

D_MODEL = 2048
CHUNK = 64
EPS = 1e-6
A_HEADS, A_DK, A_DV, A_CONV = 8, 128, 128, 4
A_QK, A_V = A_HEADS * A_DK, A_HEADS * A_DV
B_HEADS, B_DK, B_DV, B_GATE_RANK, B_GATE_NORM = 4, 128, 256, 16, 16.0
B_QK, B_V = B_HEADS * B_DK, B_HEADS * B_DV
N_GROUPS, EXP_PER_GROUP, D_FF = 4, 8, 512
N_EXPERTS = N_GROUPS * EXP_PER_GROUP
ROW_BLOCK = 256
LANE = 128
SUB = 16
TB = 256
A_HG = 4
NEG = -1e30

F32 = jnp.float32
BF16 = jnp.bfloat16
HI = lax.Precision.HIGHEST

COL_A = 0
COL_MIX = 4 * A_QK
COL_B = COL_MIX + 2 * D_MODEL
N_MAIN = COL_B + 2 * B_QK + 2 * B_V


def _cparams(sem, vmem_mib):
    return pltpu.CompilerParams(dimension_semantics=sem, vmem_limit_bytes=vmem_mib * 2 ** 20)


def _bdot(a, b):
    return jnp.dot(a.astype(BF16), b.astype(BF16), preferred_element_type=F32)


def _bdot_nt(a, b):
    return lax.dot_general(a.astype(BF16), b.astype(BF16), (((1,), (1,)), ((), ())),
                           preferred_element_type=F32)


def _bdot_tn(a, b):
    return lax.dot_general(a.astype(BF16), b.astype(BF16), (((0,), (0,)), ((), ())),
                           preferred_element_type=F32)


def _fdot(a, b):
    return jnp.dot(a, b, preferred_element_type=F32, precision=HI)


def _sigmoid(x):
    return 1.0 / (1.0 + jnp.exp(-x))


def _silu(x):
    return x * _sigmoid(x)


def _softplus(x):
    return jnp.maximum(x, 0.0) + jnp.log(1.0 + jnp.exp(-jnp.abs(x)))


def _rmsnorm_kernel(x_ref, w_ref, o_ref):
    x = x_ref[...]
    ms = jnp.mean(x * x, axis=-1, keepdims=True)
    o_ref[...] = (x * lax.rsqrt(ms + EPS) * w_ref[...]).astype(o_ref.dtype)


def _rmsnorm(x, w, tm):
    t, d = x.shape
    return pl.pallas_call(
        _rmsnorm_kernel,
        grid=(t // tm,),
        in_specs=[pl.BlockSpec((tm, d), lambda i: (i, 0)), pl.BlockSpec((1, d), lambda i: (0, 0))],
        out_specs=pl.BlockSpec((tm, d), lambda i: (i, 0)),
        out_shape=jax.ShapeDtypeStruct((t, d), BF16),
        compiler_params=_cparams(("parallel",), 32),
    )(x, w.reshape(1, d))


def _mm_kernel(a_ref, b_ref, o_ref):
    o_ref[...] = jnp.dot(a_ref[...], b_ref[...], preferred_element_type=F32).astype(o_ref.dtype)


def _matmul(a, b, tm, tn, out_dtype):
    m, k = a.shape
    n = b.shape[1]
    return pl.pallas_call(
        _mm_kernel,
        grid=(n // tn, m // tm),
        in_specs=[pl.BlockSpec((tm, k), lambda j, i: (i, 0)), pl.BlockSpec((k, tn), lambda j, i: (0, j))],
        out_specs=pl.BlockSpec((tm, tn), lambda j, i: (i, j)),
        out_shape=jax.ShapeDtypeStruct((m, n), out_dtype),
        compiler_params=_cparams(("parallel", "parallel"), 48),
    )(a, b)


def _chunk_masks(tb):
    row = lax.broadcasted_iota(jnp.int32, (tb, tb), 0)
    col = lax.broadcasted_iota(jnp.int32, (tb, tb), 1)
    same = (row // CHUNK) == (col // CHUNK)
    return row, col, same, same & (col <= row), same & (col < row)


def _lane_pick(x, idx):
    lane = lax.broadcasted_iota(jnp.int32, x.shape, 1)
    return jnp.sum(jnp.where(lane == idx, x, 0.0), axis=-1, keepdims=True)


def _causal_conv_silu(x_ref, w_ref, halo_ref, cbuf_ref, idx, tb):
    cbuf_ref[idx, 0:8, :] = halo_ref[idx]
    cbuf_ref[idx, 8:8 + tb, :] = x_ref[...]
    halo_ref[idx] = x_ref[tb - 8:tb, :]
    w = w_ref[...]
    acc = w[A_CONV - 1:A_CONV, :] * x_ref[...]
    for j in range(A_CONV - 1):
        off = 8 - (A_CONV - 1) + j
        acc = acc + w[j:j + 1, :] * cbuf_ref[idx, off:off + tb, :]
    return _silu(acc)


def _gates_a_kernel(tail_ref, alog_ref, dtb_ref, ga_ref, gct_ref):
    tb = tail_ref.shape[0]
    gl = tail_ref[...]
    beta = _sigmoid(gl)
    g = -jnp.exp(alog_ref[...]) * _softplus(gl + dtb_ref[...])
    _, _, same, causal, _ = _chunk_masks(tb)
    gc = _fdot(jnp.where(causal, 1.0, 0.0), g)
    glast = _fdot(jnp.where(same, 1.0, 0.0), g)
    lane = lax.broadcasted_iota(jnp.int32, (tb, LANE), 1)
    ga_ref[...] = jnp.where(lane < A_HEADS, beta,
                            jnp.where(lane < 2 * A_HEADS, gc, pltpu.roll(glast, A_HEADS, 1)))
    gct_ref[...] = gc.T[A_HEADS:2 * A_HEADS, :]


def _gates_a(z_tail, a_log, dt_bias):
    t = z_tail.shape[0]
    pad = lambda p: jnp.pad(p.astype(F32), (A_HEADS, LANE - 2 * A_HEADS)).reshape(1, LANE)
    const = pl.BlockSpec((1, LANE), lambda i: (0, 0))
    return pl.pallas_call(
        _gates_a_kernel,
        grid=(t // TB,),
        in_specs=[pl.BlockSpec((TB, LANE), lambda i: (i, 0)), const, const],
        out_specs=[pl.BlockSpec((TB, LANE), lambda i: (i, 0)), pl.BlockSpec((A_HEADS, TB), lambda i: (0, i))],
        out_shape=[jax.ShapeDtypeStruct((t, LANE), F32), jax.ShapeDtypeStruct((A_HEADS, t), F32)],
        compiler_params=_cparams(("parallel",), 32),
    )(z_tail, pad(a_log), pad(dt_bias))


def _delta_heads(qs, ks, vs, ga, gc_rows, hs, sts, tb):
    n = len(qs)
    rng = range(n)
    _, _, _, causal, strict = _chunk_masks(tb)
    qs = [q * lax.rsqrt(jnp.sum(q * q, axis=-1, keepdims=True) + EPS) * (A_DK ** -0.5) for q in qs]
    ks = [k * lax.rsqrt(jnp.sum(k * k, axis=-1, keepdims=True) + EPS) for k in ks]
    beta = [_lane_pick(ga, h) for h in hs]
    gc = [_lane_pick(ga, h + A_HEADS) for h in hs]
    glast = [_lane_pick(ga, h + 2 * A_HEADS) for h in hs]
    decay = [jnp.exp(jnp.where(causal, gc[i] - gc_rows[i], NEG)) for i in rng]
    kb = [ks[i] * beta[i] for i in rng]

    n_pow = [jnp.where(strict, _bdot_nt(kb[i], ks[i]) * decay[i], 0.0) * -1.0 for i in rng]
    t_mat = list(n_pow)
    lvl = 2
    while lvl < CHUNK:
        n_pow = [_bdot(m, m) for m in n_pow]
        t_mat = [t_mat[i] + n_pow[i] + _bdot(t_mat[i], n_pow[i]) for i in rng]
        lvl *= 2

    egc = [jnp.exp(g) for g in gc]
    rhs = [jnp.concatenate([vs[i] * beta[i], kb[i] * egc[i]], axis=1) for i in rng]
    uw = [rhs[i] + _bdot(t_mat[i], rhs[i]) for i in rng]
    qk = [_bdot_nt(qs[i], ks[i]) * decay[i] for i in rng]
    qkuw = [_bdot(qk[i], uw[i]) for i in rng]
    o_local = [x[:, :A_DV] for x in qkuw]
    q_eff = [qs[i] * egc[i] - qkuw[i][:, A_DV:] for i in rng]
    k_dec = [ks[i] * jnp.exp(glast[i] - gc[i]) for i in rng]
    eg_last = [jnp.exp(g) for g in glast]

    sts = list(sts)
    outs = [[] for _ in rng]
    for c in range(tb // CHUNK):
        lo, hi = c * CHUNK, (c + 1) * CHUNK
        bg = [_bdot_tn(uw[i][lo:hi], k_dec[i][lo:hi]) for i in rng]
        for i in rng:
            outs[i].append(o_local[i][lo:hi] + _bdot_nt(q_eff[i][lo:hi], sts[i]))
        sts = [sts[i] * eg_last[i][lo:lo + 1, :] + bg[i][:A_DV] - _bdot(sts[i], bg[i][A_DV:]) for i in rng]
    return [jnp.concatenate(o, axis=0) for o in outs], sts


def _mixer_a_kernel(xq_ref, xk_ref, xv_ref, z_ref, ga_ref, gct_ref, wq_ref, wk_ref, wv_ref,
                    anorm_ref, o_ref, halo_ref, cbuf_ref, state_ref):
    tb = xq_ref.shape[0]
    hg = xq_ref.shape[1] // A_DK

    @pl.when(pl.program_id(1) == 0)
    def _():
        halo_ref[...] = jnp.zeros_like(halo_ref)
        state_ref[...] = jnp.zeros_like(state_ref)

    q = _causal_conv_silu(xq_ref, wq_ref, halo_ref, cbuf_ref, 0, tb)
    k = _causal_conv_silu(xk_ref, wk_ref, halo_ref, cbuf_ref, 1, tb)
    v = _causal_conv_silu(xv_ref, wv_ref, halo_ref, cbuf_ref, 2, tb)
    ga = ga_ref[...]
    hs = [pl.program_id(0) * hg + j for j in range(hg)]
    sls = [slice(j * A_DK, (j + 1) * A_DK) for j in range(hg)]
    outs, sts = _delta_heads([q[:, s] for s in sls], [k[:, s] for s in sls], [v[:, s] for s in sls], ga,
                             [gct_ref[pl.ds(h, 1), :] for h in hs], hs, [state_ref[j] for j in range(hg)], tb)
    for j in range(hg):
        state_ref[j] = sts[j]
        o = outs[j]
        o = o * lax.rsqrt(jnp.mean(o * o, axis=-1, keepdims=True) + EPS) * anorm_ref[...]
        o_ref[:, sls[j]] = (o * _silu(z_ref[:, sls[j]])).astype(o_ref.dtype)


def _mixer_a(z_main, z_tail, conv_a, a_log, dt_bias, a_norm):
    t = z_main.shape[0]
    ng = A_HEADS // A_HG
    wid = A_HG * A_DK
    ga, gct = _gates_a(z_tail, a_log, dt_bias)
    blk = lambda off: pl.BlockSpec((TB, wid), lambda g, i: (i, off + g))
    cblk = lambda off: pl.BlockSpec((A_CONV, wid), lambda g, i: (0, off + g))
    return pl.pallas_call(
        _mixer_a_kernel,
        grid=(ng, t // TB),
        in_specs=[blk(0), blk(ng), blk(2 * ng), blk(3 * ng),
                  pl.BlockSpec((TB, LANE), lambda g, i: (i, 0)),
                  pl.BlockSpec((A_HEADS, TB), lambda g, i: (0, i)),
                  cblk(0), cblk(ng), cblk(2 * ng), pl.BlockSpec((1, LANE), lambda g, i: (0, 0))],
        out_specs=pl.BlockSpec((TB, wid), lambda g, i: (i, g)),
        out_shape=jax.ShapeDtypeStruct((t, A_V), BF16),
        scratch_shapes=[pltpu.VMEM((3, 8, wid), F32), pltpu.VMEM((3, 8 + TB, wid), F32),
                        pltpu.VMEM((A_HG, A_DV, A_DK), F32)],
        compiler_params=_cparams(("parallel", "arbitrary"), 48),
    )(z_main, z_main, z_main, z_main, ga, gct, conv_a, conv_a, conv_a,
      a_norm.reshape(1, A_DV).astype(F32))


def _cumsum_rows(mask, x):
    hi = x.astype(BF16)
    r1 = x - hi.astype(F32)
    mid = r1.astype(BF16)
    lo = (r1 - mid.astype(F32)).astype(BF16)
    dot = lambda p: jnp.dot(mask, p, preferred_element_type=F32)
    return dot(hi) + dot(mid) + dot(lo)


def _gla_chunk_scores(qc, kc, bc):
    rid = lax.broadcasted_iota(jnp.int32, (SUB, B_DK), 0)
    rid_lo = lax.broadcasted_iota(jnp.int32, (SUB // 2, B_DK), 0) + SUB // 2
    lane_c = lax.broadcasted_iota(jnp.int32, (SUB, CHUNK), 1)
    crow = lax.broadcasted_iota(jnp.int32, (CHUNK, B_DK), 0)
    half = SUB // 2
    rows = []
    for si in range(CHUNK // SUB):
        r0 = si * SUB
        qb, bb = qc[r0:r0 + SUB], bc[r0:r0 + SUB]
        ys = []
        for j in range(SUB):
            bj = bc[r0 + j:r0 + j + 1]
            if j < half:
                ys.append(qb * jnp.exp(jnp.where(rid >= j, bb - bj, NEG)))
            else:
                ys.append(jnp.zeros((half, B_DK), F32))
                ys.append(qb[half:] * jnp.exp(jnp.where(rid_lo >= j, bb[half:] - bj, NEG)))
        r = _bdot_nt(jnp.concatenate(ys, axis=0), kc)
        blk = jnp.zeros((SUB, CHUNK), F32)
        for j in range(SUB):
            blk = jnp.where(lane_c == r0 + j, r[j * SUB:(j + 1) * SUB], blk)
        if si > 0:
            bref = bc[r0:r0 + 1]
            qt = qb * jnp.exp(bb - bref)
            kt = kc * jnp.exp(jnp.where(crow < r0, bref - bc, NEG))
            blk = blk + _bdot_nt(qt, kt)
        rows.append(blk)
    return jnp.concatenate(rows, axis=0)


def _mixer_b_kernel(q_ref, k_ref, v_ref, zg_ref, tail_ref, wg_ref, bgk_ref, bnorm_ref, o_ref, state_ref):
    tb = q_ref.shape[0]
    nh = q_ref.shape[1] // B_DK

    @pl.when(pl.program_id(0) == 0)
    def _():
        state_ref[...] = jnp.zeros_like(state_ref)

    x = _fdot(tail_ref[...], wg_ref[...]) + bgk_ref[...]
    gk = -_softplus(-x) * (1.0 / B_GATE_NORM)
    _, _, _, causal, _ = _chunk_masks(tb)
    b_all = _cumsum_rows(jnp.where(causal, 1.0, 0.0).astype(BF16), gk)

    sts = [state_ref[h] for h in range(nh)]
    outs = [[] for _ in range(nh)]
    for c in range(tb // CHUNK):
        lo, hi = c * CHUNK, (c + 1) * CHUNK
        for h in range(nh):
            qc = q_ref[lo:hi, h * B_DK:(h + 1) * B_DK] * (B_DK ** -0.5)
            kc = k_ref[lo:hi, h * B_DK:(h + 1) * B_DK]
            vc = v_ref[lo:hi, h * B_DV:(h + 1) * B_DV]
            bc = b_all[lo:hi, h * B_DK:(h + 1) * B_DK]
            a_c = _gla_chunk_scores(qc, kc, bc)
            bl = bc[CHUNK - 1:CHUNK]
            outs[h].append(_bdot(a_c, vc) + _bdot_nt(qc * jnp.exp(bc), sts[h]))
            sts[h] = sts[h] * jnp.exp(bl) + _bdot_tn(vc, kc * jnp.exp(bl - bc))
    for h in range(nh):
        state_ref[h] = sts[h]
        o = jnp.concatenate(outs[h], axis=0)
        o = o * lax.rsqrt(jnp.mean(o * o, axis=-1, keepdims=True) + EPS) * bnorm_ref[...]
        sl = slice(h * B_DV, (h + 1) * B_DV)
        o_ref[:, sl] = (o * _silu(zg_ref[:, sl])).astype(o_ref.dtype)


def _mixer_b(z_main, z_tail, w_gk2, b_gk, b_norm):
    t = z_main.shape[0]
    wg = jnp.zeros((LANE, B_QK), F32).at[2 * A_HEADS:2 * A_HEADS + B_GATE_RANK].set(w_gk2.astype(F32))
    col = lambda w, off: pl.BlockSpec((TB, w), lambda i: (i, off // w))
    full = lambda r, c: pl.BlockSpec((r, c), lambda i: (0, 0))
    return pl.pallas_call(
        _mixer_b_kernel,
        grid=(t // TB,),
        in_specs=[col(B_QK, COL_B), col(B_QK, COL_B + B_QK), col(B_V, COL_B + 2 * B_QK),
                  col(B_V, COL_B + 2 * B_QK + B_V), col(LANE, 0),
                  full(LANE, B_QK), full(1, B_QK), full(1, B_DV)],
        out_specs=pl.BlockSpec((TB, B_V), lambda i: (i, 0)),
        out_shape=jax.ShapeDtypeStruct((t, B_V), BF16),
        scratch_shapes=[pltpu.VMEM((B_HEADS, B_DV, B_DK), F32)],
        compiler_params=_cparams(("arbitrary",), 48),
    )(z_main, z_main, z_main, z_main, z_tail, wg, b_gk.reshape(1, B_QK).astype(F32),
      b_norm.reshape(1, B_DV).astype(F32))


def _merge_kernel(oa_ref, ob_ref, ma_ref, mb_ref, x_ref, woa_ref, wob_ref, wout_ref, nf_ref,
                  wr_ref, br_ref, x2_ref, h2_ref, lt_ref):
    ya = jnp.dot(oa_ref[...], woa_ref[...], preferred_element_type=F32)
    yb = jnp.dot(ob_ref[...], wob_ref[...], preferred_element_type=F32)
    m = _sigmoid(ma_ref[...]) * ya + _sigmoid(mb_ref[...]) * yb
    x2 = x_ref[...] + jnp.dot(m.astype(BF16), wout_ref[...], preferred_element_type=F32)
    x2_ref[...] = x2
    h2 = x2 * lax.rsqrt(jnp.mean(x2 * x2, axis=-1, keepdims=True) + EPS) * nf_ref[...]
    h2_ref[...] = h2
    lt_ref[...] = lax.dot_general(wr_ref[...], h2, (((1,), (1,)), ((), ())),
                                  preferred_element_type=F32, precision=HI) + br_ref[...]


def _merge(oa_g, ob_g, z_main, x, w_oa, w_ob, w_out, norm_ffn, wr_t, br_t, tm):
    t, d = x.shape
    row = lambda w, c: pl.BlockSpec((tm, w), lambda i: (i, c))
    full = lambda a: pl.BlockSpec(a.shape, lambda i: (0, 0), pipeline_mode=pl.Buffered(1))
    return pl.pallas_call(
        _merge_kernel,
        grid=(t // tm,),
        in_specs=[row(A_V, 0), row(B_V, 0), row(d, COL_MIX // d), row(d, COL_MIX // d + 1), row(d, 0),
                  full(w_oa), full(w_ob), full(w_out), pl.BlockSpec((1, d), lambda i: (0, 0)),
                  full(wr_t), full(br_t)],
        out_specs=[row(d, 0), row(d, 0), pl.BlockSpec((LANE, tm), lambda i: (0, i))],
        out_shape=[jax.ShapeDtypeStruct((t, d), F32), jax.ShapeDtypeStruct((t, d), F32),
                   jax.ShapeDtypeStruct((LANE, t), F32)],
        compiler_params=_cparams(("parallel",), 56),
    )(oa_g, ob_g, z_main, z_main, x, w_oa, w_ob, w_out, norm_ffn.reshape(1, d).astype(F32), wr_t, br_t)


SEG = 256


def _route_kernel(lt_ref, pos_ref, gate_ref, blk_ref, oh_ref):
    t = lt_ref.shape[1]
    rid8 = lax.broadcasted_iota(jnp.int32, (8, t), 0)
    lg = jnp.where(rid8 < N_GROUPS, lt_ref[0:8, :], -jnp.inf)
    gmax = jnp.max(lg, axis=0, keepdims=True)
    g_idx = jnp.min(jnp.where(lg == gmax, rid8, 8), axis=0, keepdims=True)
    p_top = 1.0 / jnp.sum(jnp.exp(lg - gmax), axis=0, keepdims=True)

    les = jnp.zeros((EXP_PER_GROUP, t), F32)
    for g in range(N_GROUPS):
        les = jnp.where(g_idx == g, lt_ref[8 + g * EXP_PER_GROUP:8 + (g + 1) * EXP_PER_GROUP, :], les)
    m1 = jnp.max(les, axis=0, keepdims=True)
    i1 = jnp.min(jnp.where(les == m1, rid8, 8), axis=0, keepdims=True)
    les2 = jnp.where(rid8 == i1, -jnp.inf, les)
    m2 = jnp.max(les2, axis=0, keepdims=True)
    i2 = jnp.min(jnp.where(les2 == m2, rid8, 8), axis=0, keepdims=True)
    r = jnp.exp(m2 - m1)
    gate_ref[...] = jnp.zeros_like(gate_ref)
    gate_ref[0:1, :] = p_top / (1.0 + r)
    gate_ref[1:2, :] = p_top * r / (1.0 + r)
    e1 = g_idx * EXP_PER_GROUP + i1
    e2 = g_idx * EXP_PER_GROUP + i2

    rid32 = lax.broadcasted_iota(jnp.int32, (N_EXPERTS, t), 0)
    oh_ref[0] = jnp.where(rid32 == e1, 1.0, 0.0)
    oh_ref[1] = jnp.where(rid32 == e2, 1.0, 0.0)

    ui = lax.broadcasted_iota(jnp.int32, (SEG, SEG), 0)
    uj = lax.broadcasted_iota(jnp.int32, (SEG, SEG), 1)
    upper = jnp.where(ui < uj, 1.0, 0.0).astype(BF16)
    carry = jnp.zeros((N_EXPERTS, 1), F32)
    ranks = []
    for kk in range(2):
        segs = []
        for sg in range(t // SEG):
            oh = oh_ref[kk, :, sg * SEG:(sg + 1) * SEG]
            pre = jnp.dot(oh.astype(BF16), upper, preferred_element_type=F32) + carry
            segs.append(jnp.sum(oh * pre, axis=0, keepdims=True))
            carry = carry + jnp.sum(oh, axis=1, keepdims=True)
        ranks.append(jnp.concatenate(segs, axis=1))
    counts = carry
    nblk = jnp.floor((counts + (ROW_BLOCK - 1)) * (1.0 / ROW_BLOCK))
    li = lax.broadcasted_iota(jnp.int32, (N_EXPERTS, N_EXPERTS), 0)
    lj = lax.broadcasted_iota(jnp.int32, (N_EXPERTS, N_EXPERTS), 1)
    nb_b = jnp.broadcast_to(nblk, (N_EXPERTS, LANE))
    start_blk = _fdot(jnp.where(lj < li, 1.0, 0.0), nb_b)[:, 0:1]
    end_blk = start_blk + nblk
    start_row = start_blk * ROW_BLOCK

    pos_ref[...] = jnp.zeros_like(pos_ref)
    for kk in range(2):
        base = jnp.sum(oh_ref[kk] * start_row, axis=0, keepdims=True)
        pos_ref[kk:kk + 1, :] = (base + ranks[kk]).astype(jnp.int32)

    nb = blk_ref.shape[1]
    bid = lax.broadcasted_iota(jnp.int32, (N_EXPERTS, nb), 1).astype(F32)
    be = jnp.sum(jnp.where(end_blk <= bid, 1.0, 0.0), axis=0, keepdims=True)
    blk_ref[...] = jnp.zeros_like(blk_ref)
    blk_ref[0:1, :] = jnp.minimum(be, N_EXPERTS - 1.0).astype(jnp.int32)
    blk_ref[1:2, :] = jnp.broadcast_to(end_blk[N_EXPERTS - 1:N_EXPERTS, :], (1, nb)).astype(jnp.int32)


def _route(lt, n_blk):
    t = lt.shape[1]
    nb = -(-n_blk // LANE) * LANE
    return pl.pallas_call(
        _route_kernel,
        out_shape=[jax.ShapeDtypeStruct((8, t), jnp.int32), jax.ShapeDtypeStruct((8, t), F32),
                   jax.ShapeDtypeStruct((8, nb), jnp.int32)],
        scratch_shapes=[pltpu.VMEM((2, N_EXPERTS, t), F32)],
        compiler_params=pltpu.CompilerParams(vmem_limit_bytes=48 * 2 ** 20),
    )(lt)


def _scatter_kernel(pos_ref, h_ref, xs_in_ref, xs_ref, sem):
    del xs_in_ref
    tm = h_ref.shape[0]
    t = pl.num_programs(0) * tm
    base = pl.program_id(0) * tm

    def row_copy(r, kk):
        p = pos_ref[kk * t + base + r]
        return pltpu.make_async_copy(h_ref.at[pl.ds(r, 1), :], xs_ref.at[pl.ds(p, 1), :], sem)

    def start(r, c):
        row_copy(r, 0).start()
        row_copy(r, 1).start()
        return c

    def wait(r, c):
        row_copy(r, 0).wait()
        row_copy(r, 1).wait()
        return c

    lax.fori_loop(0, tm, start, 0, unroll=8)
    lax.fori_loop(0, tm, wait, 0, unroll=8)


def _scatter_rows(pos_flat, h2, n_rows, tm):
    t, d = h2.shape
    return pl.pallas_call(
        _scatter_kernel,
        grid_spec=pltpu.PrefetchScalarGridSpec(
            num_scalar_prefetch=1,
            grid=(t // tm,),
            in_specs=[pl.BlockSpec((tm, d), lambda i, pos: (i, 0)), pl.BlockSpec(memory_space=pl.ANY)],
            out_specs=pl.BlockSpec(memory_space=pl.ANY),
            scratch_shapes=[pltpu.SemaphoreType.DMA(())],
        ),
        out_shape=jax.ShapeDtypeStruct((n_rows, d), h2.dtype),
        input_output_aliases={2: 0},
        compiler_params=_cparams(("arbitrary",), 32),
    )(pos_flat, h2, jnp.zeros((n_rows, d), h2.dtype))


def _expert_kernel(be_ref, nu_ref, x_ref, w1_ref, w3_ref, w2_ref, o_ref, w1b, w3b, w2b):
    b = pl.program_id(0)
    prev = be_ref[jnp.maximum(b - 1, 0)]
    changed = jnp.logical_or(b == 0, be_ref[b] != prev)

    @pl.when(jnp.logical_and(changed, b < nu_ref[0]))
    def _():
        w1b[...] = w1_ref[0].astype(BF16)
        w3b[...] = w3_ref[0].astype(BF16)
        w2b[...] = w2_ref[0].astype(BF16)

    @pl.when(b < nu_ref[0])
    def _():
        xb = x_ref[...].astype(BF16)
        a = jnp.dot(xb, w1b[...], preferred_element_type=F32)
        g = jnp.dot(xb, w3b[...], preferred_element_type=F32)
        o_ref[...] = jnp.dot((_silu(a) * g).astype(BF16), w2b[...], preferred_element_type=F32)

    @pl.when(b >= nu_ref[0])
    def _():
        o_ref[...] = jnp.zeros_like(o_ref)


def _experts(blk_exp, n_used, xs, w1, w3, w2):
    n_rows, d = xs.shape
    n_blk = n_rows // ROW_BLOCK
    last = lambda b, nu: jnp.minimum(b, jnp.maximum(nu[0] - 1, 0))
    rows = lambda b, be, nu: (last(b, nu), 0)
    wsel = lambda b, be, nu: (be[last(b, nu)], 0, 0)
    return pl.pallas_call(
        _expert_kernel,
        grid_spec=pltpu.PrefetchScalarGridSpec(
            num_scalar_prefetch=2,
            grid=(n_blk,),
            in_specs=[pl.BlockSpec((ROW_BLOCK, d), rows),
                      pl.BlockSpec((1, d, D_FF), wsel), pl.BlockSpec((1, d, D_FF), wsel),
                      pl.BlockSpec((1, D_FF, d), wsel)],
            out_specs=pl.BlockSpec((ROW_BLOCK, d), lambda b, be, nu: (b, 0)),
            scratch_shapes=[pltpu.VMEM((d, D_FF), BF16), pltpu.VMEM((d, D_FF), BF16),
                            pltpu.VMEM((D_FF, d), BF16)],
        ),
        out_shape=jax.ShapeDtypeStruct((n_rows, d), F32),
        compiler_params=_cparams(("arbitrary",), 56),
    )(blk_exp, n_used, xs, w1, w3, w2)


def _combine_kernel(pos_ref, x2_ref, gt_ref, nw_ref, yb_ref, o_ref, buf, sem, *, final):
    tm = x2_ref.shape[0]
    n = pl.num_programs(0)
    t = n * tm
    i = pl.program_id(0)

    def row_copy(step, slot, r, kk):
        p = pos_ref[kk * t + step * tm + r]
        return pltpu.make_async_copy(yb_ref.at[pl.ds(p, 1), :], buf.at[slot, kk, pl.ds(r, 1), :], sem.at[slot])

    def issue(step, slot):
        def body(r, c):
            row_copy(step, slot, r, 0).start()
            row_copy(step, slot, r, 1).start()
            return c
        lax.fori_loop(0, tm, body, 0, unroll=8)

    @pl.when(i == 0)
    def _():
        issue(0, 0)

    @pl.when(i + 1 < n)
    def _():
        issue(i + 1, (i + 1) % 2)

    slot = i % 2

    def wbody(r, c):
        row_copy(i, slot, r, 0).wait()
        row_copy(i, slot, r, 1).wait()
        return c
    lax.fori_loop(0, tm, wbody, 0, unroll=8)

    gt = gt_ref[...]
    y = x2_ref[...] + gt[:, 0:1] * buf[slot, 0] + gt[:, 1:2] * buf[slot, 1]
    if final:
        y = y * lax.rsqrt(jnp.mean(y * y, axis=-1, keepdims=True) + EPS) * nw_ref[...]
    o_ref[...] = y


def _combine(pos_flat, x2, gates_t, norm_final, yb, tm, final):
    t, d = x2.shape
    return pl.pallas_call(
        functools.partial(_combine_kernel, final=final),
        grid_spec=pltpu.PrefetchScalarGridSpec(
            num_scalar_prefetch=1,
            grid=(t // tm,),
            in_specs=[pl.BlockSpec((tm, d), lambda i, pos: (i, 0)),
                      pl.BlockSpec((tm, LANE), lambda i, pos: (i, 0)),
                      pl.BlockSpec((1, d), lambda i, pos: (0, 0)),
                      pl.BlockSpec(memory_space=pl.ANY)],
            out_specs=pl.BlockSpec((tm, d), lambda i, pos: (i, 0)),
            scratch_shapes=[pltpu.VMEM((2, 2, tm, d), F32), pltpu.SemaphoreType.DMA((2,))],
        ),
        out_shape=jax.ShapeDtypeStruct((t, d), F32),
        compiler_params=_cparams(("arbitrary",), 32),
    )(pos_flat, x2, gates_t, norm_final.reshape(1, d).astype(F32), yb)


def _layer(x, norm_mix, w_in, conv_a, a_log, dt_bias, a_norm, w_gk2, b_gk, b_norm,
           w_oa, w_ob, w_out, norm_ffn, w_rg, b_rg, w_re, b_re, w1, w3, w2, norm_final, final):
    t, d = x.shape
    tm = min(512, t)

    a_end = 4 * A_QK
    ga_end = a_end + 2 * A_HEADS
    b_end = ga_end + 2 * B_QK + 2 * B_V
    lr_end = b_end + B_GATE_RANK
    w_main = jnp.concatenate([w_in[:, :a_end], w_in[:, lr_end:], w_in[:, ga_end:b_end]], axis=1).astype(BF16)
    w_tail = jnp.concatenate([w_in[:, a_end:ga_end], w_in[:, b_end:lr_end],
                              jnp.zeros((d, LANE - 2 * A_HEADS - B_GATE_RANK), w_in.dtype)], axis=1).astype(BF16)

    h = _rmsnorm(x, norm_mix, tm)
    z_main = _matmul(h, w_main, tm, 1024, F32)
    z_tail = _matmul(h, w_tail, tm, LANE, F32)

    oa_g = _mixer_a(z_main, z_tail, conv_a.astype(F32), a_log, dt_bias, a_norm)
    ob_g = _mixer_b(z_main, z_tail, w_gk2, b_gk, b_norm)

    wr_t = jnp.zeros((LANE, d), F32).at[0:N_GROUPS].set(w_rg.T.astype(F32))
    wr_t = wr_t.at[8:8 + N_EXPERTS].set(w_re.reshape(d, N_EXPERTS).T.astype(F32))
    br_t = jnp.zeros((LANE, 1), F32).at[0:N_GROUPS, 0].set(b_rg.astype(F32))
    br_t = br_t.at[8:8 + N_EXPERTS, 0].set(b_re.reshape(N_EXPERTS).astype(F32))
    x2, h2, lt = _merge(oa_g, ob_g, z_main, x, w_oa.astype(BF16), w_ob.astype(BF16), w_out.astype(BF16),
                        norm_ffn, wr_t, br_t, min(256, t))

    n_blk = (2 * t + ROW_BLOCK - 1) // ROW_BLOCK + N_EXPERTS
    n_rows = n_blk * ROW_BLOCK
    pos, gates, blk = _route(lt, n_blk)
    pos_flat = pos[0:2].reshape(2 * t)
    xs = _scatter_rows(pos_flat, h2, n_rows, min(256, t))
    yb = _experts(blk[0, :n_blk], blk[1, 0:1], xs, w1, w3, w2)
    gates_t = jnp.pad(gates[0:2].T, ((0, 0), (0, LANE - 2)))
    return _combine(pos_flat, x2, gates_t, norm_final, yb, min(128, t), final)


def kernel(x, norm_mix, w_in, conv_a, a_log, dt_bias, a_norm, w_gk2, b_gk, b_norm, w_oa, w_ob, w_out,
           norm_ffn, w_rg, b_rg, w_re, b_re, w1, w3, w2, norm_final):
    bsz, seq, d = x.shape
    assert bsz == 1, "one sequence per call"
    depth = norm_mix.shape[0]
    y = x.reshape(seq, d)
    for l in range(depth):
        y = _layer(y, norm_mix[l], w_in[l], conv_a[l], a_log[l], dt_bias[l], a_norm[l], w_gk2[l], b_gk[l],
                   b_norm[l], w_oa[l], w_ob[l], w_out[l], norm_ffn[l], w_rg[l], b_rg[l], w_re[l], b_re[l],
                   w1[l], w3[l], w2[l], norm_final, l == depth - 1)
    return y.reshape(bsz, seq, d)
```

```python
import functools

import jax
import jax.numpy as jnp
from jax import lax
from jax.experimental import pallas as pl
from jax.experimental.pallas import tpu as pltpu

D_MODEL = 2048
CHUNK = 64
EPS = 1e-6
A_HEADS, A_DK, A_DV, A_CONV = 8, 128, 128, 4
A_QK, A_V = A_HEADS * A_DK, A_HEADS * A_DV
B_HEADS, B_DK, B_DV, B_GATE_RANK, B_GATE_NORM = 4, 128, 256, 16, 16.0
B_QK, B_V = B_HEADS * B_DK, B_HEADS * B_DV
N_GROUPS, EXP_PER_GROUP, D_FF = 4, 8, 512
N_EXPERTS = N_GROUPS * EXP_PER_GROUP
ROW_BLOCK = 256
LANE = 128
SUB = 16
TB = 256
A_HG = 4
NEG = -1e30

F32 = jnp.float32
BF16 = jnp.bfloat16
HI = lax.Precision.HIGHEST

COL_A = 0
COL_MIX = 4 * A_QK
COL_B = COL_MIX + 2 * D_MODEL
N_MAIN = COL_B + 2 * B_QK + 2 * B_V


def _cparams(sem, vmem_mib):
    return pltpu.CompilerParams(dimension_semantics=sem, vmem_limit_bytes=vmem_mib * 2 ** 20)


def _bdot(a, b):
    return jnp.dot(a.astype(BF16), b.astype(BF16), preferred_element_type=F32)


def _bdot_nt(a, b):
    return lax.dot_general(a.astype(BF16), b.astype(BF16), (((1,), (1,)), ((), ())),
                           preferred_element_type=F32)


def _bdot_tn(a, b):
    return lax.dot_general(a.astype(BF16), b.astype(BF16), (((0,), (0,)), ((), ())),
                           preferred_element_type=F32)


def _fdot(a, b):
    return jnp.dot(a, b, preferred_element_type=F32, precision=HI)


def _sigmoid(x):
    return 1.0 / (1.0 + jnp.exp(-x))


def _silu(x):
    return x * _sigmoid(x)


def _softplus(x):
    return jnp.maximum(x, 0.0) + jnp.log(1.0 + jnp.exp(-jnp.abs(x)))


def _rmsnorm_kernel(x_ref, w_ref, o_ref):
    x = x_ref[...]
    ms = jnp.mean(x * x, axis=-1, keepdims=True)
    o_ref[...] = (x * lax.rsqrt(ms + EPS) * w_ref[...]).astype(o_ref.dtype)


def _rmsnorm(x, w, tm):
    t, d = x.shape
    return pl.pallas_call(
        _rmsnorm_kernel,
        grid=(t // tm,),
        in_specs=[pl.BlockSpec((tm, d), lambda i: (i, 0)), pl.BlockSpec((1, d), lambda i: (0, 0))],
        out_specs=pl.BlockSpec((tm, d), lambda i: (i, 0)),
        out_shape=jax.ShapeDtypeStruct((t, d), BF16),
        compiler_params=_cparams(("parallel",), 32),
    )(x, w.reshape(1, d))


def _mm_kernel(a_ref, b_ref, o_ref):
    o_ref[...] = jnp.dot(a_ref[...], b_ref[...], preferred_element_type=F32).astype(o_ref.dtype)


def _matmul(a, b, tm, tn, out_dtype):
    m, k = a.shape
    n = b.shape[1]
    return pl.pallas_call(
        _mm_kernel,
        grid=(n // tn, m // tm),
        in_specs=[pl.BlockSpec((tm, k), lambda j, i: (i, 0)), pl.BlockSpec((k, tn), lambda j, i: (0, j))],
        out_specs=pl.BlockSpec((tm, tn), lambda j, i: (i, j)),
        out_shape=jax.ShapeDtypeStruct((m, n), out_dtype),
        compiler_params=_cparams(("parallel", "parallel"), 48),
    )(a, b)


def _chunk_masks(tb):
    row = lax.broadcasted_iota(jnp.int32, (tb, tb), 0)
    col = lax.broadcasted_iota(jnp.int32, (tb, tb), 1)
    same = (row // CHUNK) == (col // CHUNK)
    return row, col, same, same & (col <= row), same & (col < row)


def _lane_pick(x, idx):
    lane = lax.broadcasted_iota(jnp.int32, x.shape, 1)
    return jnp.sum(jnp.where(lane == idx, x, 0.0), axis=-1, keepdims=True)


def _causal_conv_silu(x_ref, w_ref, halo_ref, cbuf_ref, idx, tb):
    cbuf_ref[idx, 0:8, :] = halo_ref[idx]
    cbuf_ref[idx, 8:8 + tb, :] = x_ref[...]
    halo_ref[idx] = x_ref[tb - 8:tb, :]
    w = w_ref[...]
    acc = w[A_CONV - 1:A_CONV, :] * x_ref[...]
    for j in range(A_CONV - 1):
        off = 8 - (A_CONV - 1) + j
        acc = acc + w[j:j + 1, :] * cbuf_ref[idx, off:off + tb, :]
    return _silu(acc)


def _gates_a_kernel(tail_ref, alog_ref, dtb_ref, ga_ref, gct_ref):
    tb = tail_ref.shape[0]
    gl = tail_ref[...]
    beta = _sigmoid(gl)
    g = -jnp.exp(alog_ref[...]) * _softplus(gl + dtb_ref[...])
    _, _, same, causal, _ = _chunk_masks(tb)
    gc = _fdot(jnp.where(causal, 1.0, 0.0), g)
    glast = _fdot(jnp.where(same, 1.0, 0.0), g)
    lane = lax.broadcasted_iota(jnp.int32, (tb, LANE), 1)
    ga_ref[...] = jnp.where(lane < A_HEADS, beta,
                            jnp.where(lane < 2 * A_HEADS, gc, pltpu.roll(glast, A_HEADS, 1)))
    gct_ref[...] = gc.T[A_HEADS:2 * A_HEADS, :]


def _gates_a(z_tail, a_log, dt_bias):
    t = z_tail.shape[0]
    pad = lambda p: jnp.pad(p.astype(F32), (A_HEADS, LANE - 2 * A_HEADS)).reshape(1, LANE)
    const = pl.BlockSpec((1, LANE), lambda i: (0, 0))
    return pl.pallas_call(
        _gates_a_kernel,
        grid=(t // TB,),
        in_specs=[pl.BlockSpec((TB, LANE), lambda i: (i, 0)), const, const],
        out_specs=[pl.BlockSpec((TB, LANE), lambda i: (i, 0)), pl.BlockSpec((A_HEADS, TB), lambda i: (0, i))],
        out_shape=[jax.ShapeDtypeStruct((t, LANE), F32), jax.ShapeDtypeStruct((A_HEADS, t), F32)],
        compiler_params=_cparams(("parallel",), 32),
    )(z_tail, pad(a_log), pad(dt_bias))


def _delta_heads(qs, ks, vs, ga, gc_rows, hs, sts, tb):
    n = len(qs)
    rng = range(n)
    _, _, _, causal, strict = _chunk_masks(tb)
    qs = [q * lax.rsqrt(jnp.sum(q * q, axis=-1, keepdims=True) + EPS) * (A_DK ** -0.5) for q in qs]
    ks = [k * lax.rsqrt(jnp.sum(k * k, axis=-1, keepdims=True) + EPS) for k in ks]
    beta = [_lane_pick(ga, h) for h in hs]
    gc = [_lane_pick(ga, h + A_HEADS) for h in hs]
    glast = [_lane_pick(ga, h + 2 * A_HEADS) for h in hs]
    decay = [jnp.exp(jnp.where(causal, gc[i] - gc_rows[i], NEG)) for i in rng]
    kb = [ks[i] * beta[i] for i in rng]

    n_pow = [jnp.where(strict, _bdot_nt(kb[i], ks[i]) * decay[i], 0.0) * -1.0 for i in rng]
    t_mat = list(n_pow)
    lvl = 2
    while lvl < CHUNK:
        n_pow = [_bdot(m, m) for m in n_pow]
        t_mat = [t_mat[i] + n_pow[i] + _bdot(t_mat[i], n_pow[i]) for i in rng]
        lvl *= 2

    egc = [jnp.exp(g) for g in gc]
    rhs = [jnp.concatenate([vs[i] * beta[i], kb[i] * egc[i]], axis=1) for i in rng]
    uw = [rhs[i] + _bdot(t_mat[i], rhs[i]) for i in rng]
    qk = [_bdot_nt(qs[i], ks[i]) * decay[i] for i in rng]
    qkuw = [_bdot(qk[i], uw[i]) for i in rng]
    o_local = [x[:, :A_DV] for x in qkuw]
    q_eff = [qs[i] * egc[i] - qkuw[i][:, A_DV:] for i in rng]
    k_dec = [ks[i] * jnp.exp(glast[i] - gc[i]) for i in rng]
    eg_last = [jnp.exp(g) for g in glast]

    sts = list(sts)
    outs = [[] for _ in rng]
    for c in range(tb // CHUNK):
        lo, hi = c * CHUNK, (c + 1) * CHUNK
        bg = [_bdot_tn(uw[i][lo:hi], k_dec[i][lo:hi]) for i in rng]
        for i in rng:
            outs[i].append(o_local[i][lo:hi] + _bdot_nt(q_eff[i][lo:hi], sts[i]))
        sts = [sts[i] * eg_last[i][lo:lo + 1, :] + bg[i][:A_DV] - _bdot(sts[i], bg[i][A_DV:]) for i in rng]
    return [jnp.concatenate(o, axis=0) for o in outs], sts


def _mixer_a_kernel(xq_ref, xk_ref, xv_ref, z_ref, ga_ref, gct_ref, wq_ref, wk_ref, wv_ref,
                    anorm_ref, o_ref, halo_ref, cbuf_ref, state_ref):
    tb = xq_ref.shape[0]
    hg = xq_ref.shape[1] // A_DK

    @pl.when(pl.program_id(1) == 0)
    def _():
        halo_ref[...] = jnp.zeros_like(halo_ref)
        state_ref[...] = jnp.zeros_like(state_ref)

    q = _causal_conv_silu(xq_ref, wq_ref, halo_ref, cbuf_ref, 0, tb)
    k = _causal_conv_silu(xk_ref, wk_ref, halo_ref, cbuf_ref, 1, tb)
    v = _causal_conv_silu(xv_ref, wv_ref, halo_ref, cbuf_ref, 2, tb)
    ga = ga_ref[...]
    hs = [pl.program_id(0) * hg + j for j in range(hg)]
    sls = [slice(j * A_DK, (j + 1) * A_DK) for j in range(hg)]
    outs, sts = _delta_heads([q[:, s] for s in sls], [k[:, s] for s in sls], [v[:, s] for s in sls], ga,
                             [gct_ref[pl.ds(h, 1), :] for h in hs], hs, [state_ref[j] for j in range(hg)], tb)
    for j in range(hg):
        state_ref[j] = sts[j]
        o = outs[j]
        o = o * lax.rsqrt(jnp.mean(o * o, axis=-1, keepdims=True) + EPS) * anorm_ref[...]
        o_ref[:, sls[j]] = (o * _silu(z_ref[:, sls[j]])).astype(o_ref.dtype)


def _mixer_a(z_main, z_tail, conv_a, a_log, dt_bias, a_norm):
    t = z_main.shape[0]
    ng = A_HEADS // A_HG
    wid = A_HG * A_DK
    ga, gct = _gates_a(z_tail, a_log, dt_bias)
    blk = lambda off: pl.BlockSpec((TB, wid), lambda g, i: (i, off + g))
    cblk = lambda off: pl.BlockSpec((A_CONV, wid), lambda g, i: (0, off + g))
    return pl.pallas_call(
        _mixer_a_kernel,
        grid=(ng, t // TB),
        in_specs=[blk(0), blk(ng), blk(2 * ng), blk(3 * ng),
                  pl.BlockSpec((TB, LANE), lambda g, i: (i, 0)),
                  pl.BlockSpec((A_HEADS, TB), lambda g, i: (0, i)),
                  cblk(0), cblk(ng), cblk(2 * ng), pl.BlockSpec((1, LANE), lambda g, i: (0, 0))],
        out_specs=pl.BlockSpec((TB, wid), lambda g, i: (i, g)),
        out_shape=jax.ShapeDtypeStruct((t, A_V), BF16),
        scratch_shapes=[pltpu.VMEM((3, 8, wid), F32), pltpu.VMEM((3, 8 + TB, wid), F32),
                        pltpu.VMEM((A_HG, A_DV, A_DK), F32)],
        compiler_params=_cparams(("parallel", "arbitrary"), 48),
    )(z_main, z_main, z_main, z_main, ga, gct, conv_a, conv_a, conv_a,
      a_norm.reshape(1, A_DV).astype(F32))


def _cumsum_rows(mask, x):
    hi = x.astype(BF16)
    r1 = x - hi.astype(F32)
    mid = r1.astype(BF16)
    lo = (r1 - mid.astype(F32)).astype(BF16)
    dot = lambda p: jnp.dot(mask, p, preferred_element_type=F32)
    return dot(hi) + dot(mid) + dot(lo)


def _gla_chunk_scores(qc, kc, bc):
    rid = lax.broadcasted_iota(jnp.int32, (SUB, B_DK), 0)
    rid_lo = lax.broadcasted_iota(jnp.int32, (SUB // 2, B_DK), 0) + SUB // 2
    lane_c = lax.broadcasted_iota(jnp.int32, (SUB, CHUNK), 1)
    crow = lax.broadcasted_iota(jnp.int32, (CHUNK, B_DK), 0)
    half = SUB // 2
    rows = []
    for si in range(CHUNK // SUB):
        r0 = si * SUB
        qb, bb = qc[r0:r0 + SUB], bc[r0:r0 + SUB]
        ys = []
        for j in range(SUB):
            bj = bc[r0 + j:r0 + j + 1]
            if j < half:
                ys.append(qb * jnp.exp(jnp.where(rid >= j, bb - bj, NEG)))
            else:
                ys.append(jnp.zeros((half, B_DK), F32))
                ys.append(qb[half:] * jnp.exp(jnp.where(rid_lo >= j, bb[half:] - bj, NEG)))
        r = _bdot_nt(jnp.concatenate(ys, axis=0), kc)
        blk = jnp.zeros((SUB, CHUNK), F32)
        for j in range(SUB):
            blk = jnp.where(lane_c == r0 + j, r[j * SUB:(j + 1) * SUB], blk)
        if si > 0:
            bref = bc[r0:r0 + 1]
            qt = qb * jnp.exp(bb - bref)
            kt = kc * jnp.exp(jnp.where(crow < r0, bref - bc, NEG))
            blk = blk + _bdot_nt(qt, kt)
        rows.append(blk)
    return jnp.concatenate(rows, axis=0)


def _mixer_b_kernel(q_ref, k_ref, v_ref, zg_ref, tail_ref, wg_ref, bgk_ref, bnorm_ref, o_ref, state_ref):
    tb = q_ref.shape[0]
    nh = q_ref.shape[1] // B_DK

    @pl.when(pl.program_id(0) == 0)
    def _():
        state_ref[...] = jnp.zeros_like(state_ref)

    x = _fdot(tail_ref[...], wg_ref[...]) + bgk_ref[...]
    gk = -_softplus(-x) * (1.0 / B_GATE_NORM)
    _, _, _, causal, _ = _chunk_masks(tb)
    b_all = _cumsum_rows(jnp.where(causal, 1.0, 0.0).astype(BF16), gk)

    sts = [state_ref[h] for h in range(nh)]
    outs = [[] for _ in range(nh)]
    for c in range(tb // CHUNK):
        lo, hi = c * CHUNK, (c + 1) * CHUNK
        for h in range(nh):
            qc = q_ref[lo:hi, h * B_DK:(h + 1) * B_DK] * (B_DK ** -0.5)
            kc = k_ref[lo:hi, h * B_DK:(h + 1) * B_DK]
            vc = v_ref[lo:hi, h * B_DV:(h + 1) * B_DV]
            bc = b_all[lo:hi, h * B_DK:(h + 1) * B_DK]
            a_c = _gla_chunk_scores(qc, kc, bc)
            bl = bc[CHUNK - 1:CHUNK]
            outs[h].append(_bdot(a_c, vc) + _bdot_nt(qc * jnp.exp(bc), sts[h]))
            sts[h] = sts[h] * jnp.exp(bl) + _bdot_tn(vc, kc * jnp.exp(bl - bc))
    for h in range(nh):
        state_ref[h] = sts[h]
        o = jnp.concatenate(outs[h], axis=0)
        o = o * lax.rsqrt(jnp.mean(o * o, axis=-1, keepdims=True) + EPS) * bnorm_ref[...]
        sl = slice(h * B_DV, (h + 1) * B_DV)
        o_ref[:, sl] = (o * _silu(zg_ref[:, sl])).astype(o_ref.dtype)


def _mixer_b(z_main, z_tail, w_gk2, b_gk, b_norm):
    t = z_main.shape[0]
    wg = jnp.zeros((LANE, B_QK), F32).at[2 * A_HEADS:2 * A_HEADS + B_GATE_RANK].set(w_gk2.astype(F32))
    col = lambda w, off: pl.BlockSpec((TB, w), lambda i: (i, off // w))
    full = lambda r, c: pl.BlockSpec((r, c), lambda i: (0, 0))
    return pl.pallas_call(
        _mixer_b_kernel,
        grid=(t // TB,),
        in_specs=[col(B_QK, COL_B), col(B_QK, COL_B + B_QK), col(B_V, COL_B + 2 * B_QK),
                  col(B_V, COL_B + 2 * B_QK + B_V), col(LANE, 0),
                  full(LANE, B_QK), full(1, B_QK), full(1, B_DV)],
        out_specs=pl.BlockSpec((TB, B_V), lambda i: (i, 0)),
        out_shape=jax.ShapeDtypeStruct((t, B_V), BF16),
        scratch_shapes=[pltpu.VMEM((B_HEADS, B_DV, B_DK), F32)],
        compiler_params=_cparams(("arbitrary",), 48),
    )(z_main, z_main, z_main, z_main, z_tail, wg, b_gk.reshape(1, B_QK).astype(F32),
      b_norm.reshape(1, B_DV).astype(F32))


def _merge_kernel(oa_ref, ob_ref, ma_ref, mb_ref, x_ref, woa_ref, wob_ref, wout_ref, nf_ref,
                  wr_ref, br_ref, x2_ref, h2_ref, lt_ref):
    ya = jnp.dot(oa_ref[...], woa_ref[...], preferred_element_type=F32)
    yb = jnp.dot(ob_ref[...], wob_ref[...], preferred_element_type=F32)
    m = _sigmoid(ma_ref[...]) * ya + _sigmoid(mb_ref[...]) * yb
    x2 = x_ref[...] + jnp.dot(m.astype(BF16), wout_ref[...], preferred_element_type=F32)
    x2_ref[...] = x2
    h2 = x2 * lax.rsqrt(jnp.mean(x2 * x2, axis=-1, keepdims=True) + EPS) * nf_ref[...]
    h2_ref[...] = h2
    lt_ref[...] = lax.dot_general(wr_ref[...], h2, (((1,), (1,)), ((), ())),
                                  preferred_element_type=F32, precision=HI) + br_ref[...]


def _merge(oa_g, ob_g, z_main, x, w_oa, w_ob, w_out, norm_ffn, wr_t, br_t, tm):
    t, d = x.shape
    row = lambda w, c: pl.BlockSpec((tm, w), lambda i: (i, c))
    full = lambda a: pl.BlockSpec(a.shape, lambda i: (0, 0), pipeline_mode=pl.Buffered(1))
    return pl.pallas_call(
        _merge_kernel,
        grid=(t // tm,),
        in_specs=[row(A_V, 0), row(B_V, 0), row(d, COL_MIX // d), row(d, COL_MIX // d + 1), row(d, 0),
                  full(w_oa), full(w_ob), full(w_out), pl.BlockSpec((1, d), lambda i: (0, 0)),
                  full(wr_t), full(br_t)],
        out_specs=[row(d, 0), row(d, 0), pl.BlockSpec((LANE, tm), lambda i: (0, i))],
        out_shape=[jax.ShapeDtypeStruct((t, d), F32), jax.ShapeDtypeStruct((t, d), F32),
                   jax.ShapeDtypeStruct((LANE, t), F32)],
        compiler_params=_cparams(("parallel",), 56),
    )(oa_g, ob_g, z_main, z_main, x, w_oa, w_ob, w_out, norm_ffn.reshape(1, d).astype(F32), wr_t, br_t)


SEG = 256


def _route_kernel(lt_ref, pos_ref, gate_ref, blk_ref, oh_ref):
    t = lt_ref.shape[1]
    rid8 = lax.broadcasted_iota(jnp.int32, (8, t), 0)
    lg = jnp.where(rid8 < N_GROUPS, lt_ref[0:8, :], -jnp.inf)
    gmax = jnp.max(lg, axis=0, keepdims=True)
    g_idx = jnp.min(jnp.where(lg == gmax, rid8, 8), axis=0, keepdims=True)
    p_top = 1.0 / jnp.sum(jnp.exp(lg - gmax), axis=0, keepdims=True)

    les = jnp.zeros((EXP_PER_GROUP, t), F32)
    for g in range(N_GROUPS):
        les = jnp.where(g_idx == g, lt_ref[8 + g * EXP_PER_GROUP:8 + (g + 1) * EXP_PER_GROUP, :], les)
    m1 = jnp.max(les, axis=0, keepdims=True)
    i1 = jnp.min(jnp.where(les == m1, rid8, 8), axis=0, keepdims=True)
    les2 = jnp.where(rid8 == i1, -jnp.inf, les)
    m2 = jnp.max(les2, axis=0, keepdims=True)
    i2 = jnp.min(jnp.where(les2 == m2, rid8, 8), axis=0, keepdims=True)
    r = jnp.exp(m2 - m1)
    gate_ref[...] = jnp.concatenate([p_top / (1.0 + r), p_top * r / (1.0 + r),
                                     jnp.zeros((LANE - 2, t), F32)], axis=0).T
    e1 = g_idx * EXP_PER_GROUP + i1
    e2 = g_idx * EXP_PER_GROUP + i2

    rid32 = lax.broadcasted_iota(jnp.int32, (N_EXPERTS, t), 0)
    oh_ref[0] = jnp.where(rid32 == e1, 1.0, 0.0)
    oh_ref[1] = jnp.where(rid32 == e2, 1.0, 0.0)

    ui = lax.broadcasted_iota(jnp.int32, (SEG, SEG), 0)
    uj = lax.broadcasted_iota(jnp.int32, (SEG, SEG), 1)
    upper = jnp.where(ui < uj, 1.0, 0.0).astype(BF16)
    carry = jnp.zeros((N_EXPERTS, 1), F32)
    ranks = []
    for kk in range(2):
        segs = []
        for sg in range(t // SEG):
            oh = oh_ref[kk, :, sg * SEG:(sg + 1) * SEG]
            pre = jnp.dot(oh.astype(BF16), upper, preferred_element_type=F32) + carry
            segs.append(jnp.sum(oh * pre, axis=0, keepdims=True))
            carry = carry + jnp.sum(oh, axis=1, keepdims=True)
        ranks.append(jnp.concatenate(segs, axis=1))
    counts = carry
    nblk = jnp.floor((counts + (ROW_BLOCK - 1)) * (1.0 / ROW_BLOCK))
    li = lax.broadcasted_iota(jnp.int32, (N_EXPERTS, N_EXPERTS), 0)
    lj = lax.broadcasted_iota(jnp.int32, (N_EXPERTS, N_EXPERTS), 1)
    nb_b = jnp.broadcast_to(nblk, (N_EXPERTS, LANE))
    start_blk = _fdot(jnp.where(lj < li, 1.0, 0.0), nb_b)[:, 0:1]
    end_blk = start_blk + nblk
    start_row = start_blk * ROW_BLOCK

    pos_ref[...] = jnp.zeros_like(pos_ref)
    for kk in range(2):
        base = jnp.sum(oh_ref[kk] * start_row, axis=0, keepdims=True)
        pos_ref[kk:kk + 1, :] = (base + ranks[kk]).astype(jnp.int32)

    nb = blk_ref.shape[1]
    bid = lax.broadcasted_iota(jnp.int32, (N_EXPERTS, nb), 1).astype(F32)
    be = jnp.sum(jnp.where(end_blk <= bid, 1.0, 0.0), axis=0, keepdims=True)
    blk_ref[...] = jnp.zeros_like(blk_ref)
    blk_ref[0:1, :] = jnp.minimum(be, N_EXPERTS - 1.0).astype(jnp.int32)
    blk_ref[1:2, :] = jnp.broadcast_to(end_blk[N_EXPERTS - 1:N_EXPERTS, :], (1, nb)).astype(jnp.int32)
    ends = jnp.sum(jnp.where(li == lj, end_blk, 0.0), axis=0, keepdims=True)
    blk_ref[2:3, 0:N_EXPERTS] = ends.astype(jnp.int32)


def _route(lt, n_blk):
    t = lt.shape[1]
    nb = -(-n_blk // LANE) * LANE
    return pl.pallas_call(
        _route_kernel,
        out_shape=[jax.ShapeDtypeStruct((8, t), jnp.int32), jax.ShapeDtypeStruct((t, LANE), F32),
                   jax.ShapeDtypeStruct((8, nb), jnp.int32)],
        scratch_shapes=[pltpu.VMEM((2, N_EXPERTS, t), F32)],
        compiler_params=pltpu.CompilerParams(vmem_limit_bytes=48 * 2 ** 20),
    )(lt)


def _scatter_kernel(pos_ref, ends_ref, nu_ref, h_ref, xs_ref, zbuf, sem, zsem):
    tm = h_ref.shape[0]
    t = pl.num_programs(0) * tm
    base = pl.program_id(0) * tm
    n_blk = xs_ref.shape[0] // ROW_BLOCK

    def zero_block(blk):
        return pltpu.make_async_copy(zbuf, xs_ref.at[pl.ds(blk * ROW_BLOCK, ROW_BLOCK), :], zsem)

    def for_each_zero_block(fn):
        for e in range(N_EXPERTS):
            first = ends_ref[e - 1] if e else 0

            @pl.when(ends_ref[e] > first)
            def _():
                fn(zero_block(ends_ref[e] - 1))

        def tail(blk, c):
            fn(zero_block(blk))
            return c
        lax.fori_loop(nu_ref[0], n_blk, tail, 0)

    @pl.when(pl.program_id(0) == 0)
    def _():
        zbuf[...] = jnp.zeros_like(zbuf)
        for_each_zero_block(lambda c: c.start())
        for_each_zero_block(lambda c: c.wait())

    def row_copy(r, kk):
        p = pos_ref[kk * t + base + r]
        return pltpu.make_async_copy(h_ref.at[pl.ds(r, 1), :], xs_ref.at[pl.ds(p, 1), :], sem)

    def start(r, c):
        row_copy(r, 0).start()
        row_copy(r, 1).start()
        return c

    def wait(r, c):
        row_copy(r, 0).wait()
        row_copy(r, 1).wait()
        return c

    lax.fori_loop(0, tm, start, 0, unroll=8)
    lax.fori_loop(0, tm, wait, 0, unroll=8)


def _scatter_rows(pos_flat, ends, n_used, h2, n_rows, tm):
    t, d = h2.shape
    return pl.pallas_call(
        _scatter_kernel,
        grid_spec=pltpu.PrefetchScalarGridSpec(
            num_scalar_prefetch=3,
            grid=(t // tm,),
            in_specs=[pl.BlockSpec((tm, d), lambda i, *_: (i, 0))],
            out_specs=pl.BlockSpec(memory_space=pl.ANY),
            scratch_shapes=[pltpu.VMEM((ROW_BLOCK, d), h2.dtype), pltpu.SemaphoreType.DMA(()),
                            pltpu.SemaphoreType.DMA(())],
        ),
        out_shape=jax.ShapeDtypeStruct((n_rows, d), h2.dtype),
        compiler_params=_cparams(("arbitrary",), 32),
    )(pos_flat, ends, n_used, h2)


def _expert_kernel(be_ref, nu_ref, ends_ref, x_ref, w1_hbm, w3_hbm, w2_hbm, o_ref,
                   w1s, w3s, w2s, w1b, w3b, w2b, sem):
    b = pl.program_id(0)
    e = be_ref[b]
    prev = be_ref[jnp.maximum(b - 1, 0)]
    changed = jnp.logical_or(b == 0, e != prev)

    def weight_copies(ex):
        return (pltpu.make_async_copy(w1_hbm.at[ex], w1s, sem.at[0]),
                pltpu.make_async_copy(w3_hbm.at[ex], w3s, sem.at[1]),
                pltpu.make_async_copy(w2_hbm.at[ex], w2s, sem.at[2]))

    @pl.when(b == 0)
    def _():
        for c in weight_copies(e):
            c.start()

    @pl.when(jnp.logical_and(changed, b < nu_ref[0]))
    def _():
        for c in weight_copies(e):
            c.wait()
        w1b[...] = w1s[...].astype(BF16)
        w3b[...] = w3s[...].astype(BF16)
        w2b[...] = w2s[...].astype(BF16)
        nxt = ends_ref[e]

        @pl.when(nxt < nu_ref[0])
        def _():
            for c in weight_copies(be_ref[nxt]):
                c.start()

    @pl.when(b < nu_ref[0])
    def _():
        xb = x_ref[...].astype(BF16)
        a = jnp.dot(xb, w1b[...], preferred_element_type=F32)
        g = jnp.dot(xb, w3b[...], preferred_element_type=F32)
        o_ref[...] = jnp.dot((_silu(a) * g).astype(BF16), w2b[...], preferred_element_type=F32)

    @pl.when(b >= nu_ref[0])
    def _():
        o_ref[...] = jnp.zeros_like(o_ref)


def _experts(blk_exp, n_used, ends, xs, w1, w3, w2):
    n_rows, d = xs.shape
    n_blk = n_rows // ROW_BLOCK
    rows = lambda b, be, nu, en: (jnp.minimum(b, jnp.maximum(nu[0] - 1, 0)), 0)
    hbm = pl.BlockSpec(memory_space=pl.ANY)
    return pl.pallas_call(
        _expert_kernel,
        grid_spec=pltpu.PrefetchScalarGridSpec(
            num_scalar_prefetch=3,
            grid=(n_blk,),
            in_specs=[pl.BlockSpec((ROW_BLOCK, d), rows), hbm, hbm, hbm],
            out_specs=pl.BlockSpec((ROW_BLOCK, d), lambda b, be, nu, en: (b, 0)),
            scratch_shapes=[pltpu.VMEM((d, D_FF), w1.dtype), pltpu.VMEM((d, D_FF), w3.dtype),
                            pltpu.VMEM((D_FF, d), w2.dtype),
                            pltpu.VMEM((d, D_FF), BF16), pltpu.VMEM((d, D_FF), BF16),
                            pltpu.VMEM((D_FF, d), BF16), pltpu.SemaphoreType.DMA((3,))],
        ),
        out_shape=jax.ShapeDtypeStruct((n_rows, d), F32),
        compiler_params=_cparams(("arbitrary",), 56),
    )(blk_exp, n_used, ends, xs, w1, w3, w2)


def _combine_kernel(pos_ref, x2_ref, gt_ref, nw_ref, yb_ref, o_ref, buf, sem, *, final):
    tm = x2_ref.shape[0]
    n = pl.num_programs(0)
    t = n * tm
    i = pl.program_id(0)

    def row_copy(step, slot, r, kk):
        p = pos_ref[kk * t + step * tm + r]
        return pltpu.make_async_copy(yb_ref.at[pl.ds(p, 1), :], buf.at[slot, kk, pl.ds(r, 1), :], sem.at[slot])

    def issue(step, slot):
        def body(r, c):
            row_copy(step, slot, r, 0).start()
            row_copy(step, slot, r, 1).start()
            return c
        lax.fori_loop(0, tm, body, 0, unroll=8)

    @pl.when(i == 0)
    def _():
        issue(0, 0)

    @pl.when(i + 1 < n)
    def _():
        issue(i + 1, (i + 1) % 2)

    slot = i % 2

    def wbody(r, c):
        row_copy(i, slot, r, 0).wait()
        row_copy(i, slot, r, 1).wait()
        return c
    lax.fori_loop(0, tm, wbody, 0, unroll=8)

    gt = gt_ref[...]
    y = x2_ref[...] + gt[:, 0:1] * buf[slot, 0] + gt[:, 1:2] * buf[slot, 1]
    if final:
        y = y * lax.rsqrt(jnp.mean(y * y, axis=-1, keepdims=True) + EPS) * nw_ref[...]
    o_ref[...] = y


def _combine(pos_flat, x2, gates_t, norm_final, yb, tm, final):
    t, d = x2.shape
    return pl.pallas_call(
        functools.partial(_combine_kernel, final=final),
        grid_spec=pltpu.PrefetchScalarGridSpec(
            num_scalar_prefetch=1,
            grid=(t // tm,),
            in_specs=[pl.BlockSpec((tm, d), lambda i, pos: (i, 0)),
                      pl.BlockSpec((tm, LANE), lambda i, pos: (i, 0)),
                      pl.BlockSpec((1, d), lambda i, pos: (0, 0)),
                      pl.BlockSpec(memory_space=pl.ANY)],
            out_specs=pl.BlockSpec((tm, d), lambda i, pos: (i, 0)),
            scratch_shapes=[pltpu.VMEM((2, 2, tm, d), F32), pltpu.SemaphoreType.DMA((2,))],
        ),
        out_shape=jax.ShapeDtypeStruct((t, d), F32),
        compiler_params=_cparams(("arbitrary",), 32),
    )(pos_flat, x2, gates_t, norm_final.reshape(1, d).astype(F32), yb)


def _layer(x, norm_mix, w_in, conv_a, a_log, dt_bias, a_norm, w_gk2, b_gk, b_norm,
           w_oa, w_ob, w_out, norm_ffn, w_rg, b_rg, w_re, b_re, w1, w3, w2, norm_final, final):
    t, d = x.shape
    tm = min(512, t)

    a_end = 4 * A_QK
    ga_end = a_end + 2 * A_HEADS
    b_end = ga_end + 2 * B_QK + 2 * B_V
    lr_end = b_end + B_GATE_RANK
    w_main = jnp.concatenate([w_in[:, :a_end], w_in[:, lr_end:], w_in[:, ga_end:b_end]], axis=1).astype(BF16)
    w_tail = jnp.concatenate([w_in[:, a_end:ga_end], w_in[:, b_end:lr_end],
                              jnp.zeros((d, LANE - 2 * A_HEADS - B_GATE_RANK), w_in.dtype)], axis=1).astype(BF16)

    h = _rmsnorm(x, norm_mix, tm)
    z_main = _matmul(h, w_main, min(1024, t), 1024, F32)
    z_tail = _matmul(h, w_tail, tm, LANE, F32)

    oa_g = _mixer_a(z_main, z_tail, conv_a.astype(F32), a_log, dt_bias, a_norm)
    ob_g = _mixer_b(z_main, z_tail, w_gk2, b_gk, b_norm)

    wr_t = jnp.zeros((LANE, d), F32).at[0:N_GROUPS].set(w_rg.T.astype(F32))
    wr_t = wr_t.at[8:8 + N_EXPERTS].set(w_re.reshape(d, N_EXPERTS).T.astype(F32))
    br_t = jnp.zeros((LANE, 1), F32).at[0:N_GROUPS, 0].set(b_rg.astype(F32))
    br_t = br_t.at[8:8 + N_EXPERTS, 0].set(b_re.reshape(N_EXPERTS).astype(F32))
    x2, h2, lt = _merge(oa_g, ob_g, z_main, x, w_oa.astype(BF16), w_ob.astype(BF16), w_out.astype(BF16),
                        norm_ffn, wr_t, br_t, min(256, t))

    n_blk = (2 * t + ROW_BLOCK - 1) // ROW_BLOCK + N_EXPERTS
    n_rows = n_blk * ROW_BLOCK
    pos, gates, blk = _route(lt, n_blk)
    pos_flat = pos[0:2].reshape(2 * t)
    blk_exp, n_used, ends = blk[0, :n_blk], blk[1, 0:1], blk[2, :N_EXPERTS]
    xs = _scatter_rows(pos_flat, ends, n_used, h2, n_rows, min(256, t))
    yb = _experts(blk_exp, n_used, ends, xs, w1, w3, w2)
    return _combine(pos_flat, x2, gates, norm_final, yb, min(128, t), final)


def kernel(x, norm_mix, w_in, conv_a, a_log, dt_bias, a_norm, w_gk2, b_gk, b_norm, w_oa, w_ob, w_out,
           norm_ffn, w_rg, b_rg, w_re, b_re, w1, w3, w2, norm_final):
    bsz, seq, d = x.shape
    assert bsz == 1, "one sequence per call"
    depth = norm_mix.shape[0]
    y = x.reshape(seq, d)
    for l in range(depth):
        y = _layer(y, norm_mix[l], w_in[l], conv_a[l], a_log[l], dt_bias[l], a_norm[l], w_gk2[l], b_gk[l],
                   b_norm[l], w_oa[l], w_ob[l], w_out[l], norm_ffn[l], w_rg[l], b_rg[l], w_re[l], b_re[l],
                   w1[l], w3[l], w2[l], norm_final, l == depth - 1)
    return y.reshape(bsz, seq, d)
```

```python
import functools

import jax
import jax.numpy as jnp
from jax import lax
from jax.experimental import pallas as pl
from jax.experimental.pallas import tpu as pltpu

D_MODEL = 2048
CHUNK = 64
EPS = 1e-6
A_HEADS, A_DK, A_DV, A_CONV = 8, 128, 128, 4
A_QK, A_V = A_HEADS * A_DK, A_HEADS * A_DV
B_HEADS, B_DK, B_DV, B_GATE_RANK, B_GATE_NORM = 4, 128, 256, 16, 16.0
B_QK, B_V = B_HEADS * B_DK, B_HEADS * B_DV
N_GROUPS, EXP_PER_GROUP, D_FF = 4, 8, 512
N_EXPERTS = N_GROUPS * EXP_PER_GROUP
ROW_BLOCK = 256
LANE = 128
SUB = 16
TB = 256
A_HG = 8
NEG = -1e30

F32 = jnp.float32
BF16 = jnp.bfloat16
HI = lax.Precision.HIGHEST


def _cparams(sem, vmem_mib):
    return pltpu.CompilerParams(dimension_semantics=sem, vmem_limit_bytes=vmem_mib * 2 ** 20)


def _bdot(a, b):
    return jnp.dot(a.astype(BF16), b.astype(BF16), preferred_element_type=F32)


def _bdot_nt(a, b):
    return lax.dot_general(a.astype(BF16), b.astype(BF16), (((1,), (1,)), ((), ())),
                           preferred_element_type=F32)


def _bdot_tn(a, b):
    return lax.dot_general(a.astype(BF16), b.astype(BF16), (((0,), (0,)), ((), ())),
                           preferred_element_type=F32)


def _fdot(a, b):
    return jnp.dot(a, b, preferred_element_type=F32, precision=HI)


def _sigmoid(x):
    return 1.0 / (1.0 + jnp.exp(-x))


def _silu(x):
    return x * _sigmoid(x)


def _softplus(x):
    return jnp.maximum(x, 0.0) + jnp.log(1.0 + jnp.exp(-jnp.abs(x)))


def _rmsnorm_kernel(x_ref, w_ref, o_ref):
    x = x_ref[...]
    ms = jnp.mean(x * x, axis=-1, keepdims=True)
    o_ref[...] = (x * lax.rsqrt(ms + EPS) * w_ref[...]).astype(o_ref.dtype)


def _rmsnorm(x, w, tm):
    t, d = x.shape
    return pl.pallas_call(
        _rmsnorm_kernel,
        grid=(t // tm,),
        in_specs=[pl.BlockSpec((tm, d), lambda i: (i, 0)), pl.BlockSpec((1, d), lambda i: (0, 0))],
        out_specs=pl.BlockSpec((tm, d), lambda i: (i, 0)),
        out_shape=jax.ShapeDtypeStruct((t, d), BF16),
        compiler_params=_cparams(("parallel",), 32),
    )(x, w.reshape(1, d))


def _mm_kernel(a_ref, b_ref, o_ref):
    o_ref[...] = jnp.dot(a_ref[...], b_ref[...], preferred_element_type=F32).astype(o_ref.dtype)


def _matmul(a, b, tm, tn, out_dtype):
    m, k = a.shape
    n = b.shape[1]
    return pl.pallas_call(
        _mm_kernel,
        grid=(n // tn, m // tm),
        in_specs=[pl.BlockSpec((tm, k), lambda j, i: (i, 0)), pl.BlockSpec((k, tn), lambda j, i: (0, j))],
        out_specs=pl.BlockSpec((tm, tn), lambda j, i: (i, j)),
        out_shape=jax.ShapeDtypeStruct((m, n), out_dtype),
        compiler_params=_cparams(("parallel", "parallel"), 48),
    )(a, b)


def _chunk_masks(tb):
    row = lax.broadcasted_iota(jnp.int32, (tb, tb), 0)
    col = lax.broadcasted_iota(jnp.int32, (tb, tb), 1)
    same = (row // CHUNK) == (col // CHUNK)
    return row, col, same, same & (col <= row), same & (col < row)


def _lane_pick(x, idx):
    lane = lax.broadcasted_iota(jnp.int32, x.shape, 1)
    return jnp.sum(jnp.where(lane == idx, x, 0.0), axis=-1, keepdims=True)


def _causal_conv_silu(x_ref, w_ref, halo_ref, cbuf_ref, idx, tb):
    cbuf_ref[idx, 0:8, :] = halo_ref[idx]
    cbuf_ref[idx, 8:8 + tb, :] = x_ref[...]
    halo_ref[idx] = x_ref[tb - 8:tb, :]
    w = w_ref[...]
    acc = w[A_CONV - 1:A_CONV, :] * x_ref[...]
    for j in range(A_CONV - 1):
        off = 8 - (A_CONV - 1) + j
        acc = acc + w[j:j + 1, :] * cbuf_ref[idx, off:off + tb, :]
    return _silu(acc)


def _gates_a_kernel(tail_ref, alog_ref, dtb_ref, ga_ref, gct_ref):
    tb = tail_ref.shape[0]
    gl = tail_ref[...]
    beta = _sigmoid(gl)
    g = -jnp.exp(alog_ref[...]) * _softplus(gl + dtb_ref[...])
    _, _, same, causal, _ = _chunk_masks(tb)
    gc = _fdot(jnp.where(causal, 1.0, 0.0), g)
    glast = _fdot(jnp.where(same, 1.0, 0.0), g)
    lane = lax.broadcasted_iota(jnp.int32, (tb, LANE), 1)
    ga_ref[...] = jnp.where(lane < A_HEADS, beta,
                            jnp.where(lane < 2 * A_HEADS, gc, pltpu.roll(glast, A_HEADS, 1)))
    gct_ref[...] = gc.T[A_HEADS:2 * A_HEADS, :]


def _gates_a(z_tail, a_log, dt_bias):
    t = z_tail.shape[0]
    pad = lambda p: jnp.pad(p.astype(F32), (A_HEADS, LANE - 2 * A_HEADS)).reshape(1, LANE)
    const = pl.BlockSpec((1, LANE), lambda i: (0, 0))
    return pl.pallas_call(
        _gates_a_kernel,
        grid=(t // TB,),
        in_specs=[pl.BlockSpec((TB, LANE), lambda i: (i, 0)), const, const],
        out_specs=[pl.BlockSpec((TB, LANE), lambda i: (i, 0)), pl.BlockSpec((A_HEADS, TB), lambda i: (0, i))],
        out_shape=[jax.ShapeDtypeStruct((t, LANE), F32), jax.ShapeDtypeStruct((A_HEADS, t), F32)],
        compiler_params=_cparams(("parallel",), 32),
    )(z_tail, pad(a_log), pad(dt_bias))


def _delta_heads(qs, ks, vs, ga, gc_rows, hs, sts, tb):
    n = len(qs)
    rng = range(n)
    _, _, _, causal, strict = _chunk_masks(tb)
    qs = [q * lax.rsqrt(jnp.sum(q * q, axis=-1, keepdims=True) + EPS) * (A_DK ** -0.5) for q in qs]
    ks = [k * lax.rsqrt(jnp.sum(k * k, axis=-1, keepdims=True) + EPS) for k in ks]
    beta = [_lane_pick(ga, h) for h in hs]
    gc = [_lane_pick(ga, h + A_HEADS) for h in hs]
    glast = [_lane_pick(ga, h + 2 * A_HEADS) for h in hs]
    decay = [jnp.exp(jnp.where(causal, gc[i] - gc_rows[i], NEG)) for i in rng]
    kb = [ks[i] * beta[i] for i in rng]

    n_pow = [jnp.where(strict, _bdot_nt(kb[i], ks[i]) * decay[i], 0.0) * -1.0 for i in rng]
    t_mat = list(n_pow)
    lvl = 2
    while lvl < CHUNK:
        n_pow = [_bdot(m, m) for m in n_pow]
        t_mat = [t_mat[i] + n_pow[i] + _bdot(t_mat[i], n_pow[i]) for i in rng]
        lvl *= 2

    egc = [jnp.exp(g) for g in gc]
    rhs = [jnp.concatenate([vs[i] * beta[i], kb[i] * egc[i]], axis=1) for i in rng]
    uw = [rhs[i] + _bdot(t_mat[i], rhs[i]) for i in rng]
    qk = [_bdot_nt(qs[i], ks[i]) * decay[i] for i in rng]
    qkuw = [_bdot(qk[i], uw[i]) for i in rng]
    o_local = [x[:, :A_DV] for x in qkuw]
    q_eff = [qs[i] * egc[i] - qkuw[i][:, A_DV:] for i in rng]
    k_dec = [ks[i] * jnp.exp(glast[i] - gc[i]) for i in rng]
    eg_last = [jnp.exp(g) for g in glast]

    sts = list(sts)
    outs = [[] for _ in rng]
    for c in range(tb // CHUNK):
        lo, hi = c * CHUNK, (c + 1) * CHUNK
        bg = [_bdot_tn(uw[i][lo:hi], k_dec[i][lo:hi]) for i in rng]
        for i in rng:
            outs[i].append(o_local[i][lo:hi] + _bdot_nt(q_eff[i][lo:hi], sts[i]))
        sts = [sts[i] * eg_last[i][lo:lo + 1, :] + bg[i][:A_DV] - _bdot(sts[i], bg[i][A_DV:]) for i in rng]
    return [jnp.concatenate(o, axis=0) for o in outs], sts


def _mixer_a_kernel(xq_ref, xk_ref, xv_ref, z_ref, ga_ref, gct_ref, wq_ref, wk_ref, wv_ref,
                    anorm_ref, o_ref, halo_ref, cbuf_ref, state_ref):
    tb = xq_ref.shape[0]
    hg = xq_ref.shape[1] // A_DK

    @pl.when(pl.program_id(1) == 0)
    def _():
        halo_ref[...] = jnp.zeros_like(halo_ref)
        state_ref[...] = jnp.zeros_like(state_ref)

    q = _causal_conv_silu(xq_ref, wq_ref, halo_ref, cbuf_ref, 0, tb)
    k = _causal_conv_silu(xk_ref, wk_ref, halo_ref, cbuf_ref, 1, tb)
    v = _causal_conv_silu(xv_ref, wv_ref, halo_ref, cbuf_ref, 2, tb)
    ga = ga_ref[...]
    hs = [pl.program_id(0) * hg + j for j in range(hg)]
    sls = [slice(j * A_DK, (j + 1) * A_DK) for j in range(hg)]
    outs, sts = _delta_heads([q[:, s] for s in sls], [k[:, s] for s in sls], [v[:, s] for s in sls], ga,
                             [gct_ref[pl.ds(h, 1), :] for h in hs], hs, [state_ref[j] for j in range(hg)], tb)
    for j in range(hg):
        state_ref[j] = sts[j]
        o = outs[j]
        o = o * lax.rsqrt(jnp.mean(o * o, axis=-1, keepdims=True) + EPS) * anorm_ref[...]
        o_ref[:, sls[j]] = (o * _silu(z_ref[:, sls[j]])).astype(o_ref.dtype)


def _mixer_a(z_a, z_tail, conv_a, a_log, dt_bias, a_norm):
    t = z_a.shape[0]
    ng = A_HEADS // A_HG
    wid = A_HG * A_DK
    ga, gct = _gates_a(z_tail, a_log, dt_bias)
    blk = lambda off: pl.BlockSpec((TB, wid), lambda g, i: (i, off + g))
    cblk = lambda off: pl.BlockSpec((A_CONV, wid), lambda g, i: (0, off + g))
    return pl.pallas_call(
        _mixer_a_kernel,
        grid=(ng, t // TB),
        in_specs=[blk(0), blk(ng), blk(2 * ng), blk(3 * ng),
                  pl.BlockSpec((TB, LANE), lambda g, i: (i, 0)),
                  pl.BlockSpec((A_HEADS, TB), lambda g, i: (0, i)),
                  cblk(0), cblk(ng), cblk(2 * ng), pl.BlockSpec((1, LANE), lambda g, i: (0, 0))],
        out_specs=pl.BlockSpec((TB, wid), lambda g, i: (i, g)),
        out_shape=jax.ShapeDtypeStruct((t, A_V), BF16),
        scratch_shapes=[pltpu.VMEM((3, 8, wid), F32), pltpu.VMEM((3, 8 + TB, wid), F32),
                        pltpu.VMEM((A_HG, A_DV, A_DK), F32)],
        compiler_params=_cparams(("parallel", "arbitrary"), 48),
    )(z_a, z_a, z_a, z_a, ga, gct, conv_a, conv_a, conv_a,
      a_norm.reshape(1, A_DV).astype(F32))


def _cumsum_rows(mask, x):
    hi = x.astype(BF16)
    r1 = x - hi.astype(F32)
    mid = r1.astype(BF16)
    lo = (r1 - mid.astype(F32)).astype(BF16)
    dot = lambda p: jnp.dot(mask, p, preferred_element_type=F32)
    return dot(hi) + dot(mid) + dot(lo)


def _gla_chunk_scores(qc, kc, bc):
    rid = lax.broadcasted_iota(jnp.int32, (SUB, B_DK), 0)
    rid_lo = lax.broadcasted_iota(jnp.int32, (SUB // 2, B_DK), 0) + SUB // 2
    lane_c = lax.broadcasted_iota(jnp.int32, (SUB, CHUNK), 1)
    crow = lax.broadcasted_iota(jnp.int32, (CHUNK, B_DK), 0)
    half = SUB // 2
    rows = []
    for si in range(CHUNK // SUB):
        r0 = si * SUB
        qb, bb = qc[r0:r0 + SUB], bc[r0:r0 + SUB]
        ys = []
        for j in range(SUB):
            bj = bc[r0 + j:r0 + j + 1]
            if j < half:
                ys.append(qb * jnp.exp(jnp.where(rid >= j, bb - bj, NEG)))
            else:
                ys.append(jnp.zeros((half, B_DK), F32))
                ys.append(qb[half:] * jnp.exp(jnp.where(rid_lo >= j, bb[half:] - bj, NEG)))
        r = _bdot_nt(jnp.concatenate(ys, axis=0), kc)
        blk = jnp.zeros((SUB, CHUNK), F32)
        for j in range(SUB):
            blk = jnp.where(lane_c == r0 + j, r[j * SUB:(j + 1) * SUB], blk)
        if si > 0:
            bref = bc[r0:r0 + 1]
            qt = qb * jnp.exp(bb - bref)
            kt = kc * jnp.exp(jnp.where(crow < r0, bref - bc, NEG))
            blk = blk + _bdot_nt(qt, kt)
        rows.append(blk)
    return jnp.concatenate(rows, axis=0)


def _mixer_b_kernel(q_ref, k_ref, v_ref, zg_ref, tail_ref, wg_ref, bgk_ref, bnorm_ref, o_ref, state_ref):
    tb = q_ref.shape[0]
    nh = q_ref.shape[1] // B_DK

    @pl.when(pl.program_id(0) == 0)
    def _():
        state_ref[...] = jnp.zeros_like(state_ref)

    x = _fdot(tail_ref[...], wg_ref[...]) + bgk_ref[...]
    gk = -_softplus(-x) * (1.0 / B_GATE_NORM)
    _, _, _, causal, _ = _chunk_masks(tb)
    b_all = _cumsum_rows(jnp.where(causal, 1.0, 0.0).astype(BF16), gk)

    sts = [state_ref[h] for h in range(nh)]
    outs = [[] for _ in range(nh)]
    for c in range(tb // CHUNK):
        lo, hi = c * CHUNK, (c + 1) * CHUNK
        for h in range(nh):
            qc = q_ref[lo:hi, h * B_DK:(h + 1) * B_DK] * (B_DK ** -0.5)
            kc = k_ref[lo:hi, h * B_DK:(h + 1) * B_DK]
            vc = v_ref[lo:hi, h * B_DV:(h + 1) * B_DV]
            bc = b_all[lo:hi, h * B_DK:(h + 1) * B_DK]
            a_c = _gla_chunk_scores(qc, kc, bc)
            bl = bc[CHUNK - 1:CHUNK]
            outs[h].append(_bdot(a_c, vc) + _bdot_nt(qc * jnp.exp(bc), sts[h]))
            sts[h] = sts[h] * jnp.exp(bl) + _bdot_tn(vc, kc * jnp.exp(bl - bc))
    for h in range(nh):
        state_ref[h] = sts[h]
        o = jnp.concatenate(outs[h], axis=0)
        o = o * lax.rsqrt(jnp.mean(o * o, axis=-1, keepdims=True) + EPS) * bnorm_ref[...]
        sl = slice(h * B_DV, (h + 1) * B_DV)
        o_ref[:, sl] = (o * _silu(zg_ref[:, sl])).astype(o_ref.dtype)


def _mixer_b(z_b, z_tail, w_gk2, b_gk, b_norm):
    t = z_b.shape[0]
    wg = jnp.zeros((LANE, B_QK), F32).at[2 * A_HEADS:2 * A_HEADS + B_GATE_RANK].set(w_gk2.astype(F32))
    col = lambda w, off: pl.BlockSpec((TB, w), lambda i: (i, off // w))
    full = lambda r, c: pl.BlockSpec((r, c), lambda i: (0, 0))
    return pl.pallas_call(
        _mixer_b_kernel,
        grid=(t // TB,),
        in_specs=[col(B_QK, 0), col(B_QK, B_QK), col(B_V, 2 * B_QK), col(B_V, 2 * B_QK + B_V), col(LANE, 0),
                  full(LANE, B_QK), full(1, B_QK), full(1, B_DV)],
        out_specs=pl.BlockSpec((TB, B_V), lambda i: (i, 0)),
        out_shape=jax.ShapeDtypeStruct((t, B_V), BF16),
        scratch_shapes=[pltpu.VMEM((B_HEADS, B_DV, B_DK), F32)],
        compiler_params=_cparams(("arbitrary",), 48),
    )(z_b, z_b, z_b, z_b, z_tail, wg, b_gk.reshape(1, B_QK).astype(F32),
      b_norm.reshape(1, B_DV).astype(F32))


def _merge_kernel(oa_ref, ob_ref, ma_ref, mb_ref, x_ref, woa_ref, wob_ref, wout_ref, nf_ref,
                  wr_ref, br_ref, x2_ref, h2_ref, lt_ref):
    ya = jnp.dot(oa_ref[...], woa_ref[...], preferred_element_type=F32)
    yb = jnp.dot(ob_ref[...], wob_ref[...], preferred_element_type=F32)
    m = _sigmoid(ma_ref[...]) * ya + _sigmoid(mb_ref[...]) * yb
    x2 = x_ref[...] + jnp.dot(m.astype(BF16), wout_ref[...], preferred_element_type=F32)
    x2_ref[...] = x2
    h2 = x2 * lax.rsqrt(jnp.mean(x2 * x2, axis=-1, keepdims=True) + EPS) * nf_ref[...]
    h2_ref[...] = h2
    h_hi = h2.astype(BF16)
    h_mid = (h2 - h_hi.astype(F32)).astype(BF16)
    dot = lambda a, b: jnp.dot(a, b, preferred_element_type=F32)
    logits = dot(h_hi, wr_ref[0]) + (dot(h_mid, wr_ref[0]) + dot(h_hi, wr_ref[1]))
    lt_ref[...] = logits.T + br_ref[...]


def _merge(oa_g, ob_g, z_mix, x, w_oa, w_ob, w_out, norm_ffn, wr, br_t, tm):
    t, d = x.shape
    row = lambda w, c: pl.BlockSpec((tm, w), lambda i: (i, c))
    full = lambda a: pl.BlockSpec(a.shape, lambda i: (0,) * a.ndim, pipeline_mode=pl.Buffered(1))
    return pl.pallas_call(
        _merge_kernel,
        grid=(t // tm,),
        in_specs=[row(A_V, 0), row(B_V, 0), row(d, 0), row(d, 1), row(d, 0),
                  full(w_oa), full(w_ob), full(w_out), pl.BlockSpec((1, d), lambda i: (0, 0)),
                  full(wr), full(br_t)],
        out_specs=[row(d, 0), row(d, 0), pl.BlockSpec((LANE, tm), lambda i: (0, i))],
        out_shape=[jax.ShapeDtypeStruct((t, d), F32), jax.ShapeDtypeStruct((t, d), F32),
                   jax.ShapeDtypeStruct((LANE, t), F32)],
        compiler_params=_cparams(("parallel",), 56),
    )(oa_g, ob_g, z_mix, z_mix, x, w_oa, w_ob, w_out, norm_ffn.reshape(1, d).astype(F32), wr, br_t)


SEG = 256


def _route_kernel(lt_ref, pos_ref, gate_ref, blk_ref, oh_ref):
    t = lt_ref.shape[1]
    rid8 = lax.broadcasted_iota(jnp.int32, (8, t), 0)
    lg = jnp.where(rid8 < N_GROUPS, lt_ref[0:8, :], -jnp.inf)
    gmax = jnp.max(lg, axis=0, keepdims=True)
    g_idx = jnp.min(jnp.where(lg == gmax, rid8, 8), axis=0, keepdims=True)
    p_top = 1.0 / jnp.sum(jnp.exp(lg - gmax), axis=0, keepdims=True)

    les = jnp.zeros((EXP_PER_GROUP, t), F32)
    for g in range(N_GROUPS):
        les = jnp.where(g_idx == g, lt_ref[8 + g * EXP_PER_GROUP:8 + (g + 1) * EXP_PER_GROUP, :], les)
    m1 = jnp.max(les, axis=0, keepdims=True)
    i1 = jnp.min(jnp.where(les == m1, rid8, 8), axis=0, keepdims=True)
    les2 = jnp.where(rid8 == i1, -jnp.inf, les)
    m2 = jnp.max(les2, axis=0, keepdims=True)
    i2 = jnp.min(jnp.where(les2 == m2, rid8, 8), axis=0, keepdims=True)
    r = jnp.exp(m2 - m1)
    gate_ref[...] = jnp.concatenate([p_top / (1.0 + r), p_top * r / (1.0 + r),
                                     jnp.zeros((LANE - 2, t), F32)], axis=0).T
    e1 = g_idx * EXP_PER_GROUP + i1
    e2 = g_idx * EXP_PER_GROUP + i2

    rid32 = lax.broadcasted_iota(jnp.int32, (N_EXPERTS, t), 0)
    oh_ref[0] = jnp.where(rid32 == e1, 1.0, 0.0)
    oh_ref[1] = jnp.where(rid32 == e2, 1.0, 0.0)

    ui = lax.broadcasted_iota(jnp.int32, (SEG, SEG), 0)
    uj = lax.broadcasted_iota(jnp.int32, (SEG, SEG), 1)
    upper = jnp.where(ui < uj, 1.0, 0.0).astype(BF16)
    carry = jnp.zeros((N_EXPERTS, 1), F32)
    ranks = []
    for kk in range(2):
        segs = []
        for sg in range(t // SEG):
            oh = oh_ref[kk, :, sg * SEG:(sg + 1) * SEG]
            pre = jnp.dot(oh.astype(BF16), upper, preferred_element_type=F32) + carry
            segs.append(jnp.sum(oh * pre, axis=0, keepdims=True))
            carry = carry + jnp.sum(oh, axis=1, keepdims=True)
        ranks.append(jnp.concatenate(segs, axis=1))
    counts = carry
    nblk = jnp.floor((counts + (ROW_BLOCK - 1)) * (1.0 / ROW_BLOCK))
    li = lax.broadcasted_iota(jnp.int32, (N_EXPERTS, N_EXPERTS), 0)
    lj = lax.broadcasted_iota(jnp.int32, (N_EXPERTS, N_EXPERTS), 1)
    nb_b = jnp.broadcast_to(nblk, (N_EXPERTS, LANE))
    start_blk = _fdot(jnp.where(lj < li, 1.0, 0.0), nb_b)[:, 0:1]
    end_blk = start_blk + nblk
    start_row = start_blk * ROW_BLOCK

    pos_ref[...] = jnp.zeros_like(pos_ref)
    for kk in range(2):
        base = jnp.sum(oh_ref[kk] * start_row, axis=0, keepdims=True)
        pos_ref[kk:kk + 1, :] = (base + ranks[kk]).astype(jnp.int32)

    nb = blk_ref.shape[1]
    bid = lax.broadcasted_iota(jnp.int32, (N_EXPERTS, nb), 1).astype(F32)
    be = jnp.sum(jnp.where(end_blk <= bid, 1.0, 0.0), axis=0, keepdims=True)
    blk_ref[...] = jnp.zeros_like(blk_ref)
    blk_ref[0:1, :] = jnp.minimum(be, N_EXPERTS - 1.0).astype(jnp.int32)
    blk_ref[1:2, :] = jnp.broadcast_to(end_blk[N_EXPERTS - 1:N_EXPERTS, :], (1, nb)).astype(jnp.int32)
    ends = jnp.sum(jnp.where(li == lj, end_blk, 0.0), axis=0, keepdims=True)
    blk_ref[2:3, 0:N_EXPERTS] = ends.astype(jnp.int32)


def _route(lt, n_blk):
    t = lt.shape[1]
    nb = -(-n_blk // LANE) * LANE
    return pl.pallas_call(
        _route_kernel,
        out_shape=[jax.ShapeDtypeStruct((8, t), jnp.int32), jax.ShapeDtypeStruct((t, LANE), F32),
                   jax.ShapeDtypeStruct((8, nb), jnp.int32)],
        scratch_shapes=[pltpu.VMEM((2, N_EXPERTS, t), F32)],
        compiler_params=pltpu.CompilerParams(vmem_limit_bytes=48 * 2 ** 20),
    )(lt)


def _scatter_kernel(pos_ref, ends_ref, nu_ref, h_ref, xs_ref, zbuf, sem, zsem):
    tm = h_ref.shape[0]
    t = pl.num_programs(0) * tm
    base = pl.program_id(0) * tm
    n_blk = xs_ref.shape[0] // ROW_BLOCK

    def zero_block(blk):
        return pltpu.make_async_copy(zbuf, xs_ref.at[pl.ds(blk * ROW_BLOCK, ROW_BLOCK), :], zsem)

    def for_each_zero_block(fn):
        for e in range(N_EXPERTS):
            first = ends_ref[e - 1] if e else 0

            @pl.when(ends_ref[e] > first)
            def _():
                fn(zero_block(ends_ref[e] - 1))

        def tail(blk, c):
            fn(zero_block(blk))
            return c
        lax.fori_loop(nu_ref[0], n_blk, tail, 0)

    @pl.when(pl.program_id(0) == 0)
    def _():
        zbuf[...] = jnp.zeros_like(zbuf)
        for_each_zero_block(lambda c: c.start())
        for_each_zero_block(lambda c: c.wait())

    def row_copy(r, kk):
        p = pos_ref[kk * t + base + r]
        return pltpu.make_async_copy(h_ref.at[pl.ds(r, 1), :], xs_ref.at[pl.ds(p, 1), :], sem)

    def start(r, c):
        row_copy(r, 0).start()
        row_copy(r, 1).start()
        return c

    def wait(r, c):
        row_copy(r, 0).wait()
        row_copy(r, 1).wait()
        return c

    lax.fori_loop(0, tm, start, 0, unroll=8)
    lax.fori_loop(0, tm, wait, 0, unroll=8)


def _scatter_rows(pos_flat, ends, n_used, h2, n_rows, tm):
    t, d = h2.shape
    return pl.pallas_call(
        _scatter_kernel,
        grid_spec=pltpu.PrefetchScalarGridSpec(
            num_scalar_prefetch=3,
            grid=(t // tm,),
            in_specs=[pl.BlockSpec((tm, d), lambda i, *_: (i, 0))],
            out_specs=pl.BlockSpec(memory_space=pl.ANY),
            scratch_shapes=[pltpu.VMEM((ROW_BLOCK, d), h2.dtype), pltpu.SemaphoreType.DMA(()),
                            pltpu.SemaphoreType.DMA(())],
        ),
        out_shape=jax.ShapeDtypeStruct((n_rows, d), h2.dtype),
        compiler_params=_cparams(("arbitrary",), 32),
    )(pos_flat, ends, n_used, h2)


def _expert_kernel(be_ref, nu_ref, ends_ref, x_ref, w1_hbm, w3_hbm, w2_hbm, o_ref,
                   w1s, w3s, w2s, w1b, w3b, w2b, sem):
    b = pl.program_id(0)
    e = be_ref[b]
    prev = be_ref[jnp.maximum(b - 1, 0)]
    changed = jnp.logical_or(b == 0, e != prev)

    def weight_copies(ex):
        return (pltpu.make_async_copy(w1_hbm.at[ex], w1s, sem.at[0]),
                pltpu.make_async_copy(w3_hbm.at[ex], w3s, sem.at[1]),
                pltpu.make_async_copy(w2_hbm.at[ex], w2s, sem.at[2]))

    @pl.when(b == 0)
    def _():
        for c in weight_copies(e):
            c.start()

    @pl.when(jnp.logical_and(changed, b < nu_ref[0]))
    def _():
        for c in weight_copies(e):
            c.wait()
        w1b[...] = w1s[...].astype(BF16)
        w3b[...] = w3s[...].astype(BF16)
        w2b[...] = w2s[...].astype(BF16)
        nxt = ends_ref[e]

        @pl.when(nxt < nu_ref[0])
        def _():
            for c in weight_copies(be_ref[nxt]):
                c.start()

    @pl.when(b < nu_ref[0])
    def _():
        xb = x_ref[...].astype(BF16)
        a = jnp.dot(xb, w1b[...], preferred_element_type=F32)
        g = jnp.dot(xb, w3b[...], preferred_element_type=F32)
        o_ref[...] = jnp.dot((_silu(a) * g).astype(BF16), w2b[...], preferred_element_type=F32)

    @pl.when(b >= nu_ref[0])
    def _():
        o_ref[...] = jnp.zeros_like(o_ref)


def _experts(blk_exp, n_used, ends, xs, w1, w3, w2):
    n_rows, d = xs.shape
    n_blk = n_rows // ROW_BLOCK
    rows = lambda b, be, nu, en: (jnp.minimum(b, jnp.maximum(nu[0] - 1, 0)), 0)
    hbm = pl.BlockSpec(memory_space=pl.ANY)
    return pl.pallas_call(
        _expert_kernel,
        grid_spec=pltpu.PrefetchScalarGridSpec(
            num_scalar_prefetch=3,
            grid=(n_blk,),
            in_specs=[pl.BlockSpec((ROW_BLOCK, d), rows), hbm, hbm, hbm],
            out_specs=pl.BlockSpec((ROW_BLOCK, d), lambda b, be, nu, en: (b, 0)),
            scratch_shapes=[pltpu.VMEM((d, D_FF), w1.dtype), pltpu.VMEM((d, D_FF), w3.dtype),
                            pltpu.VMEM((D_FF, d), w2.dtype),
                            pltpu.VMEM((d, D_FF), BF16), pltpu.VMEM((d, D_FF), BF16),
                            pltpu.VMEM((D_FF, d), BF16), pltpu.SemaphoreType.DMA((3,))],
        ),
        out_shape=jax.ShapeDtypeStruct((n_rows, d), F32),
        compiler_params=_cparams(("arbitrary",), 56),
    )(blk_exp, n_used, ends, xs, w1, w3, w2)


def _combine_kernel(pos_ref, x2_ref, gt_ref, nw_ref, yb_ref, o_ref, buf, sem, *, final):
    tm = x2_ref.shape[0]
    n = pl.num_programs(0)
    t = n * tm
    i = pl.program_id(0)

    def row_copy(step, slot, r, kk):
        p = pos_ref[kk * t + step * tm + r]
        return pltpu.make_async_copy(yb_ref.at[pl.ds(p, 1), :], buf.at[slot, kk, pl.ds(r, 1), :], sem.at[slot])

    def issue(step, slot):
        def body(r, c):
            row_copy(step, slot, r, 0).start()
            row_copy(step, slot, r, 1).start()
            return c
        lax.fori_loop(0, tm, body, 0, unroll=8)

    @pl.when(i == 0)
    def _():
        issue(0, 0)

    @pl.when(i + 1 < n)
    def _():
        issue(i + 1, (i + 1) % 2)

    slot = i % 2

    def wbody(r, c):
        row_copy(i, slot, r, 0).wait()
        row_copy(i, slot, r, 1).wait()
        return c
    lax.fori_loop(0, tm, wbody, 0, unroll=8)

    gt = gt_ref[...]
    y = x2_ref[...] + gt[:, 0:1] * buf[slot, 0] + gt[:, 1:2] * buf[slot, 1]
    if final:
        y = y * lax.rsqrt(jnp.mean(y * y, axis=-1, keepdims=True) + EPS) * nw_ref[...]
    o_ref[...] = y


def _combine(pos_flat, x2, gates_t, norm_final, yb, tm, final):
    t, d = x2.shape
    return pl.pallas_call(
        functools.partial(_combine_kernel, final=final),
        grid_spec=pltpu.PrefetchScalarGridSpec(
            num_scalar_prefetch=1,
            grid=(t // tm,),
            in_specs=[pl.BlockSpec((tm, d), lambda i, pos: (i, 0)),
                      pl.BlockSpec((tm, LANE), lambda i, pos: (i, 0)),
                      pl.BlockSpec((1, d), lambda i, pos: (0, 0)),
                      pl.BlockSpec(memory_space=pl.ANY)],
            out_specs=pl.BlockSpec((tm, d), lambda i, pos: (i, 0)),
            scratch_shapes=[pltpu.VMEM((2, 2, tm, d), F32), pltpu.SemaphoreType.DMA((2,))],
        ),
        out_shape=jax.ShapeDtypeStruct((t, d), F32),
        compiler_params=_cparams(("arbitrary",), 32),
    )(pos_flat, x2, gates_t, norm_final.reshape(1, d).astype(F32), yb)


def _layer(x, norm_mix, w_in, conv_a, a_log, dt_bias, a_norm, w_gk2, b_gk, b_norm,
           w_oa, w_ob, w_out, norm_ffn, w_rg, b_rg, w_re, b_re, w1, w3, w2, norm_final, final):
    t, d = x.shape
    tm = min(512, t)

    a_end = 4 * A_QK
    ga_end = a_end + 2 * A_HEADS
    b_end = ga_end + 2 * B_QK + 2 * B_V
    lr_end = b_end + B_GATE_RANK
    w_a = w_in[:, :a_end].astype(BF16)
    w_b = w_in[:, ga_end:b_end].astype(BF16)
    w_mix = w_in[:, lr_end:].astype(BF16)
    w_tail = jnp.concatenate([w_in[:, a_end:ga_end], w_in[:, b_end:lr_end],
                              jnp.zeros((d, LANE - 2 * A_HEADS - B_GATE_RANK), w_in.dtype)], axis=1).astype(BF16)

    h = _rmsnorm(x, norm_mix, tm)
    tm_big = min(1024, t)
    z_a = _matmul(h, w_a, tm_big, 1024, F32)
    z_b = _matmul(h, w_b, tm_big, 1024, F32)
    z_mix = _matmul(h, w_mix, tm_big, 1024, F32)
    z_tail = _matmul(h, w_tail, tm, LANE, F32)

    oa_g = _mixer_a(z_a, z_tail, conv_a.astype(F32), a_log, dt_bias, a_norm)
    ob_g = _mixer_b(z_b, z_tail, w_gk2, b_gk, b_norm)

    wr = jnp.zeros((d, LANE), F32).at[:, 0:N_GROUPS].set(w_rg.astype(F32))
    wr = wr.at[:, 8:8 + N_EXPERTS].set(w_re.reshape(d, N_EXPERTS).astype(F32))
    wr_hi = wr.astype(BF16)
    wr = jnp.stack([wr_hi, (wr - wr_hi.astype(F32)).astype(BF16)])
    br_t = jnp.zeros((LANE, 1), F32).at[0:N_GROUPS, 0].set(b_rg.astype(F32))
    br_t = br_t.at[8:8 + N_EXPERTS, 0].set(b_re.reshape(N_EXPERTS).astype(F32))
    x2, h2, lt = _merge(oa_g, ob_g, z_mix, x, w_oa.astype(BF16), w_ob.astype(BF16), w_out.astype(BF16),
                        norm_ffn, wr, br_t, min(256, t))

    n_blk = (2 * t + ROW_BLOCK - 1) // ROW_BLOCK + N_EXPERTS
    n_rows = n_blk * ROW_BLOCK
    pos, gates, blk = _route(lt, n_blk)
    pos_flat = pos[0:2].reshape(2 * t)
    blk_exp, n_used, ends = blk[0, :n_blk], blk[1, 0:1], blk[2, :N_EXPERTS]
    xs = _scatter_rows(pos_flat, ends, n_used, h2, n_rows, min(256, t))
    yb = _experts(blk_exp, n_used, ends, xs, w1, w3, w2)
    return _combine(pos_flat, x2, gates, norm_final, yb, min(128, t), final)


def kernel(x, norm_mix, w_in, conv_a, a_log, dt_bias, a_norm, w_gk2, b_gk, b_norm, w_oa, w_ob, w_out,
           norm_ffn, w_rg, b_rg, w_re, b_re, w1, w3, w2, norm_final):
    bsz, seq, d = x.shape
    assert bsz == 1, "one sequence per call"
    depth = norm_mix.shape[0]
    y = x.reshape(seq, d)
    for l in range(depth):
        y = _layer(y, norm_mix[l], w_in[l], conv_a[l], a_log[l], dt_bias[l], a_norm[l], w_gk2[l], b_gk[l],
                   b_norm[l], w_oa[l], w_ob[l], w_out[l], norm_ffn[l], w_rg[l], b_rg[l], w_re[l], b_re[l],
                   w1[l], w3[l], w2[l], norm_final, l == depth - 1)
    return y.reshape(bsz, seq, d)
```

```python
import functools

import jax
import jax.numpy as jnp
from jax import lax
from jax.experimental import pallas as pl
from jax.experimental.pallas import tpu as pltpu

D_MODEL = 2048
CHUNK = 64
EPS = 1e-6
A_HEADS, A_DK, A_DV, A_CONV = 8, 128, 128, 4
A_QK, A_V = A_HEADS * A_DK, A_HEADS * A_DV
B_HEADS, B_DK, B_DV, B_GATE_RANK, B_GATE_NORM = 4, 128, 256, 16, 16.0
B_QK, B_V = B_HEADS * B_DK, B_HEADS * B_DV
N_GROUPS, EXP_PER_GROUP, D_FF = 4, 8, 512
N_EXPERTS = N_GROUPS * EXP_PER_GROUP
ROW_BLOCK = 256
LANE = 128
SUB = 16
TB = 256
A_HG = 8
NEG = -1e30

F32 = jnp.float32
BF16 = jnp.bfloat16
HI = lax.Precision.HIGHEST


def _cparams(sem, vmem_mib):
    return pltpu.CompilerParams(dimension_semantics=sem, vmem_limit_bytes=vmem_mib * 2 ** 20)


def _bdot(a, b):
    return jnp.dot(a.astype(BF16), b.astype(BF16), preferred_element_type=F32)


def _bdot_nt(a, b):
    return lax.dot_general(a.astype(BF16), b.astype(BF16), (((1,), (1,)), ((), ())),
                           preferred_element_type=F32)


def _bdot_tn(a, b):
    return lax.dot_general(a.astype(BF16), b.astype(BF16), (((0,), (0,)), ((), ())),
                           preferred_element_type=F32)


def _fdot(a, b):
    return jnp.dot(a, b, preferred_element_type=F32, precision=HI)


def _sigmoid(x):
    return 1.0 / (1.0 + jnp.exp(-x))


def _silu(x):
    return x * _sigmoid(x)


def _softplus(x):
    return jnp.maximum(x, 0.0) + jnp.log(1.0 + jnp.exp(-jnp.abs(x)))


def _rmsnorm_kernel(x_ref, w_ref, o_ref):
    x = x_ref[...]
    ms = jnp.mean(x * x, axis=-1, keepdims=True)
    o_ref[...] = (x * lax.rsqrt(ms + EPS) * w_ref[...]).astype(o_ref.dtype)


def _rmsnorm(x, w, tm):
    t, d = x.shape
    return pl.pallas_call(
        _rmsnorm_kernel,
        grid=(t // tm,),
        in_specs=[pl.BlockSpec((tm, d), lambda i: (i, 0)), pl.BlockSpec((1, d), lambda i: (0, 0))],
        out_specs=pl.BlockSpec((tm, d), lambda i: (i, 0)),
        out_shape=jax.ShapeDtypeStruct((t, d), BF16),
        compiler_params=_cparams(("parallel",), 32),
    )(x, w.reshape(1, d))


PREP_ROWS = 256


def _proj_kernel(a_ref, wt_ref, o_ref, wb_ref):
    tn = wt_ref.shape[0]

    @pl.when(pl.program_id(1) == 0)
    def _():
        for r in range(0, tn, PREP_ROWS):
            wb_ref[r:r + PREP_ROWS, :] = wt_ref[r:r + PREP_ROWS, :].astype(BF16)

    o_ref[...] = lax.dot_general(a_ref[...], wb_ref[...], (((1,), (1,)), ((), ())), preferred_element_type=F32)


def _proj(a, wt, col0, n, tm, tn):
    m, k = a.shape
    assert n % tn == 0 and col0 % 8 == 0
    return pl.pallas_call(
        _proj_kernel,
        grid=(n // tn, m // tm),
        in_specs=[pl.BlockSpec((tm, k), lambda j, i: (i, 0)),
                  pl.BlockSpec((pl.Element(tn), pl.Element(k)), lambda j, i: (pl.multiple_of(col0 + j * tn, 8), 0))],
        out_specs=pl.BlockSpec((tm, tn), lambda j, i: (i, j)),
        out_shape=jax.ShapeDtypeStruct((m, n), F32),
        scratch_shapes=[pltpu.VMEM((tn, k), BF16)],
        compiler_params=_cparams(("arbitrary", "arbitrary"), 48),
    )(a, wt)


def _chunk_masks(tb):
    row = lax.broadcasted_iota(jnp.int32, (tb, tb), 0)
    col = lax.broadcasted_iota(jnp.int32, (tb, tb), 1)
    same = (row // CHUNK) == (col // CHUNK)
    return row, col, same, same & (col <= row), same & (col < row)


def _lane_pick(x, idx):
    lane = lax.broadcasted_iota(jnp.int32, x.shape, 1)
    return jnp.sum(jnp.where(lane == idx, x, 0.0), axis=-1, keepdims=True)


def _causal_conv_silu(x_ref, w_ref, halo_ref, cbuf_ref, idx, tb):
    cbuf_ref[idx, 0:8, :] = halo_ref[idx]
    cbuf_ref[idx, 8:8 + tb, :] = x_ref[...]
    halo_ref[idx] = x_ref[tb - 8:tb, :]
    w = w_ref[...]
    acc = w[A_CONV - 1:A_CONV, :] * x_ref[...]
    for j in range(A_CONV - 1):
        off = 8 - (A_CONV - 1) + j
        acc = acc + w[j:j + 1, :] * cbuf_ref[idx, off:off + tb, :]
    return _silu(acc)


def _gates_a_kernel(tail_ref, alog_ref, dtb_ref, ga_ref, gct_ref):
    tb = tail_ref.shape[0]
    gl = tail_ref[...]
    beta = _sigmoid(gl)
    g = -jnp.exp(alog_ref[...]) * _softplus(gl + dtb_ref[...])
    _, _, same, causal, _ = _chunk_masks(tb)
    gc = _fdot(jnp.where(causal, 1.0, 0.0), g)
    glast = _fdot(jnp.where(same, 1.0, 0.0), g)
    lane = lax.broadcasted_iota(jnp.int32, (tb, LANE), 1)
    ga_ref[...] = jnp.where(lane < A_HEADS, beta,
                            jnp.where(lane < 2 * A_HEADS, gc, pltpu.roll(glast, A_HEADS, 1)))
    gct_ref[...] = gc.T[A_HEADS:2 * A_HEADS, :]


def _gates_a(z_tail, a_log, dt_bias):
    t = z_tail.shape[0]
    pad = lambda p: jnp.pad(p.astype(F32), (A_HEADS, LANE - 2 * A_HEADS)).reshape(1, LANE)
    const = pl.BlockSpec((1, LANE), lambda i: (0, 0))
    return pl.pallas_call(
        _gates_a_kernel,
        grid=(t // TB,),
        in_specs=[pl.BlockSpec((TB, LANE), lambda i: (i, 0)), const, const],
        out_specs=[pl.BlockSpec((TB, LANE), lambda i: (i, 0)), pl.BlockSpec((A_HEADS, TB), lambda i: (0, i))],
        out_shape=[jax.ShapeDtypeStruct((t, LANE), F32), jax.ShapeDtypeStruct((A_HEADS, t), F32)],
        compiler_params=_cparams(("parallel",), 32),
    )(z_tail, pad(a_log), pad(dt_bias))


def _delta_heads(qs, ks, vs, ga, gc_rows, hs, sts, tb):
    n = len(qs)
    rng = range(n)
    _, _, _, causal, strict = _chunk_masks(tb)
    qs = [q * lax.rsqrt(jnp.sum(q * q, axis=-1, keepdims=True) + EPS) * (A_DK ** -0.5) for q in qs]
    ks = [k * lax.rsqrt(jnp.sum(k * k, axis=-1, keepdims=True) + EPS) for k in ks]
    beta = [_lane_pick(ga, h) for h in hs]
    gc = [_lane_pick(ga, h + A_HEADS) for h in hs]
    glast = [_lane_pick(ga, h + 2 * A_HEADS) for h in hs]
    decay = [jnp.exp(jnp.where(causal, gc[i] - gc_rows[i], NEG)) for i in rng]
    kb = [ks[i] * beta[i] for i in rng]

    n_pow = [jnp.where(strict, _bdot_nt(kb[i], ks[i]) * decay[i], 0.0) * -1.0 for i in rng]
    t_mat = list(n_pow)
    lvl = 2
    while lvl < CHUNK:
        n_pow = [_bdot(m, m) for m in n_pow]
        t_mat = [t_mat[i] + n_pow[i] + _bdot(t_mat[i], n_pow[i]) for i in rng]
        lvl *= 2

    egc = [jnp.exp(g) for g in gc]
    rhs = [jnp.concatenate([vs[i] * beta[i], kb[i] * egc[i]], axis=1) for i in rng]
    uw = [rhs[i] + _bdot(t_mat[i], rhs[i]) for i in rng]
    qk = [_bdot_nt(qs[i], ks[i]) * decay[i] for i in rng]
    qkuw = [_bdot(qk[i], uw[i]) for i in rng]
    o_local = [x[:, :A_DV] for x in qkuw]
    q_eff = [qs[i] * egc[i] - qkuw[i][:, A_DV:] for i in rng]
    k_dec = [ks[i] * jnp.exp(glast[i] - gc[i]) for i in rng]
    eg_last = [jnp.exp(g) for g in glast]

    sts = list(sts)
    outs = [[] for _ in rng]
    for c in range(tb // CHUNK):
        lo, hi = c * CHUNK, (c + 1) * CHUNK
        bg = [_bdot_tn(uw[i][lo:hi], k_dec[i][lo:hi]) for i in rng]
        for i in rng:
            outs[i].append(o_local[i][lo:hi] + _bdot_nt(q_eff[i][lo:hi], sts[i]))
        sts = [sts[i] * eg_last[i][lo:lo + 1, :] + bg[i][:A_DV] - _bdot(sts[i], bg[i][A_DV:]) for i in rng]
    return [jnp.concatenate(o, axis=0) for o in outs], sts


def _mixer_a_kernel(xq_ref, xk_ref, xv_ref, z_ref, ga_ref, gct_ref, wq_ref, wk_ref, wv_ref,
                    anorm_ref, o_ref, halo_ref, cbuf_ref, state_ref):
    tb = xq_ref.shape[0]
    hg = xq_ref.shape[1] // A_DK

    @pl.when(pl.program_id(1) == 0)
    def _():
        halo_ref[...] = jnp.zeros_like(halo_ref)
        state_ref[...] = jnp.zeros_like(state_ref)

    q = _causal_conv_silu(xq_ref, wq_ref, halo_ref, cbuf_ref, 0, tb)
    k = _causal_conv_silu(xk_ref, wk_ref, halo_ref, cbuf_ref, 1, tb)
    v = _causal_conv_silu(xv_ref, wv_ref, halo_ref, cbuf_ref, 2, tb)
    ga = ga_ref[...]
    hs = [pl.program_id(0) * hg + j for j in range(hg)]
    sls = [slice(j * A_DK, (j + 1) * A_DK) for j in range(hg)]
    outs, sts = _delta_heads([q[:, s] for s in sls], [k[:, s] for s in sls], [v[:, s] for s in sls], ga,
                             [gct_ref[pl.ds(h, 1), :] for h in hs], hs, [state_ref[j] for j in range(hg)], tb)
    for j in range(hg):
        state_ref[j] = sts[j]
        o = outs[j]
        o = o * lax.rsqrt(jnp.mean(o * o, axis=-1, keepdims=True) + EPS) * anorm_ref[...]
        o_ref[:, sls[j]] = (o * _silu(z_ref[:, sls[j]])).astype(o_ref.dtype)


def _mixer_a(z_a, z_tail, conv_a, a_log, dt_bias, a_norm):
    t = z_a.shape[0]
    ng = A_HEADS // A_HG
    wid = A_HG * A_DK
    ga, gct = _gates_a(z_tail, a_log, dt_bias)
    blk = lambda off: pl.BlockSpec((TB, wid), lambda g, i: (i, off + g))
    cblk = lambda off: pl.BlockSpec((A_CONV, wid), lambda g, i: (0, off + g))
    return pl.pallas_call(
        _mixer_a_kernel,
        grid=(ng, t // TB),
        in_specs=[blk(0), blk(ng), blk(2 * ng), blk(3 * ng),
                  pl.BlockSpec((TB, LANE), lambda g, i: (i, 0)),
                  pl.BlockSpec((A_HEADS, TB), lambda g, i: (0, i)),
                  cblk(0), cblk(ng), cblk(2 * ng), pl.BlockSpec((1, LANE), lambda g, i: (0, 0))],
        out_specs=pl.BlockSpec((TB, wid), lambda g, i: (i, g)),
        out_shape=jax.ShapeDtypeStruct((t, A_V), BF16),
        scratch_shapes=[pltpu.VMEM((3, 8, wid), F32), pltpu.VMEM((3, 8 + TB, wid), F32),
                        pltpu.VMEM((A_HG, A_DV, A_DK), F32)],
        compiler_params=_cparams(("parallel", "arbitrary"), 48),
    )(z_a, z_a, z_a, z_a, ga, gct, conv_a, conv_a, conv_a,
      a_norm.reshape(1, A_DV).astype(F32))


def _cumsum_rows(mask, x):
    hi = x.astype(BF16)
    r1 = x - hi.astype(F32)
    mid = r1.astype(BF16)
    lo = (r1 - mid.astype(F32)).astype(BF16)
    dot = lambda p: jnp.dot(mask, p, preferred_element_type=F32)
    return dot(hi) + dot(mid) + dot(lo)


def _gla_chunk_scores(qc, kc, bc):
    rid = lax.broadcasted_iota(jnp.int32, (SUB, B_DK), 0)
    rid_lo = lax.broadcasted_iota(jnp.int32, (SUB // 2, B_DK), 0) + SUB // 2
    lane_c = lax.broadcasted_iota(jnp.int32, (SUB, CHUNK), 1)
    crow = lax.broadcasted_iota(jnp.int32, (CHUNK, B_DK), 0)
    half = SUB // 2
    rows = []
    for si in range(CHUNK // SUB):
        r0 = si * SUB
        qb, bb = qc[r0:r0 + SUB], bc[r0:r0 + SUB]
        ys = []
        for j in range(SUB):
            bj = bc[r0 + j:r0 + j + 1]
            if j < half:
                ys.append(qb * jnp.exp(jnp.where(rid >= j, bb - bj, NEG)))
            else:
                ys.append(jnp.zeros((half, B_DK), F32))
                ys.append(qb[half:] * jnp.exp(jnp.where(rid_lo >= j, bb[half:] - bj, NEG)))
        r = _bdot_nt(jnp.concatenate(ys, axis=0), kc)
        blk = jnp.zeros((SUB, CHUNK), F32)
        for j in range(SUB):
            blk = jnp.where(lane_c == r0 + j, r[j * SUB:(j + 1) * SUB], blk)
        if si > 0:
            bref = bc[r0:r0 + 1]
            qt = qb * jnp.exp(bb - bref)
            kt = kc * jnp.exp(jnp.where(crow < r0, bref - bc, NEG))
            blk = blk + _bdot_nt(qt, kt)
        rows.append(blk)
    return jnp.concatenate(rows, axis=0)


def _mixer_b_kernel(q_ref, k_ref, v_ref, zg_ref, tail_ref, wg_ref, bgk_ref, bnorm_ref, o_ref, state_ref):
    tb = q_ref.shape[0]
    nh = q_ref.shape[1] // B_DK

    @pl.when(pl.program_id(0) == 0)
    def _():
        state_ref[...] = jnp.zeros_like(state_ref)

    x = _fdot(tail_ref[...], wg_ref[...]) + bgk_ref[...]
    gk = -_softplus(-x) * (1.0 / B_GATE_NORM)
    _, _, _, causal, _ = _chunk_masks(tb)
    b_all = _cumsum_rows(jnp.where(causal, 1.0, 0.0).astype(BF16), gk)

    sts = [state_ref[h] for h in range(nh)]
    outs = [[] for _ in range(nh)]
    for c in range(tb // CHUNK):
        lo, hi = c * CHUNK, (c + 1) * CHUNK
        for h in range(nh):
            qc = q_ref[lo:hi, h * B_DK:(h + 1) * B_DK] * (B_DK ** -0.5)
            kc = k_ref[lo:hi, h * B_DK:(h + 1) * B_DK]
            vc = v_ref[lo:hi, h * B_DV:(h + 1) * B_DV]
            bc = b_all[lo:hi, h * B_DK:(h + 1) * B_DK]
            a_c = _gla_chunk_scores(qc, kc, bc)
            bl = bc[CHUNK - 1:CHUNK]
            outs[h].append(_bdot(a_c, vc) + _bdot_nt(qc * jnp.exp(bc), sts[h]))
            sts[h] = sts[h] * jnp.exp(bl) + _bdot_tn(vc, kc * jnp.exp(bl - bc))
    for h in range(nh):
        state_ref[h] = sts[h]
        o = jnp.concatenate(outs[h], axis=0)
        o = o * lax.rsqrt(jnp.mean(o * o, axis=-1, keepdims=True) + EPS) * bnorm_ref[...]
        sl = slice(h * B_DV, (h + 1) * B_DV)
        o_ref[:, sl] = (o * _silu(zg_ref[:, sl])).astype(o_ref.dtype)


def _mixer_b(z_b, z_tail, w_gk2, b_gk, b_norm):
    t = z_b.shape[0]
    wg = jnp.zeros((LANE, B_QK), F32).at[2 * A_HEADS:2 * A_HEADS + B_GATE_RANK].set(w_gk2.astype(F32))
    col = lambda w, off: pl.BlockSpec((TB, w), lambda i: (i, off // w))
    full = lambda r, c: pl.BlockSpec((r, c), lambda i: (0, 0))
    return pl.pallas_call(
        _mixer_b_kernel,
        grid=(t // TB,),
        in_specs=[col(B_QK, 0), col(B_QK, B_QK), col(B_V, 2 * B_QK), col(B_V, 2 * B_QK + B_V), col(LANE, 0),
                  full(LANE, B_QK), full(1, B_QK), full(1, B_DV)],
        out_specs=pl.BlockSpec((TB, B_V), lambda i: (i, 0)),
        out_shape=jax.ShapeDtypeStruct((t, B_V), BF16),
        scratch_shapes=[pltpu.VMEM((B_HEADS, B_DV, B_DK), F32)],
        compiler_params=_cparams(("arbitrary",), 48),
    )(z_b, z_b, z_b, z_b, z_tail, wg, b_gk.reshape(1, B_QK).astype(F32),
      b_norm.reshape(1, B_DV).astype(F32))


def _merge_kernel(oa_ref, ob_ref, ma_ref, mb_ref, x_ref, woa_ref, wob_ref, wout_ref, nf_ref,
                  wr_ref, br_ref, x2_ref, h2_ref, lt_ref):
    ya = jnp.dot(oa_ref[...], woa_ref[...], preferred_element_type=F32)
    yb = jnp.dot(ob_ref[...], wob_ref[...], preferred_element_type=F32)
    m = _sigmoid(ma_ref[...]) * ya + _sigmoid(mb_ref[...]) * yb
    x2 = x_ref[...] + jnp.dot(m.astype(BF16), wout_ref[...], preferred_element_type=F32)
    x2_ref[...] = x2
    h2 = x2 * lax.rsqrt(jnp.mean(x2 * x2, axis=-1, keepdims=True) + EPS) * nf_ref[...]
    h2_ref[...] = h2
    h_hi = h2.astype(BF16)
    h_mid = (h2 - h_hi.astype(F32)).astype(BF16)
    dot = lambda a, b: jnp.dot(a, b, preferred_element_type=F32)
    logits = dot(h_hi, wr_ref[0]) + (dot(h_mid, wr_ref[0]) + dot(h_hi, wr_ref[1]))
    lt_ref[...] = logits.T + br_ref[...]


def _merge(oa_g, ob_g, z_mix, x, w_oa, w_ob, w_out, norm_ffn, wr, br_t, tm):
    t, d = x.shape
    row = lambda w, c: pl.BlockSpec((tm, w), lambda i: (i, c))
    full = lambda a: pl.BlockSpec(a.shape, lambda i: (0,) * a.ndim, pipeline_mode=pl.Buffered(1))
    return pl.pallas_call(
        _merge_kernel,
        grid=(t // tm,),
        in_specs=[row(A_V, 0), row(B_V, 0), row(d, 0), row(d, 1), row(d, 0),
                  full(w_oa), full(w_ob), full(w_out), pl.BlockSpec((1, d), lambda i: (0, 0)),
                  full(wr), full(br_t)],
        out_specs=[row(d, 0), row(d, 0), pl.BlockSpec((LANE, tm), lambda i: (0, i))],
        out_shape=[jax.ShapeDtypeStruct((t, d), F32), jax.ShapeDtypeStruct((t, d), F32),
                   jax.ShapeDtypeStruct((LANE, t), F32)],
        compiler_params=_cparams(("parallel",), 56),
    )(oa_g, ob_g, z_mix, z_mix, x, w_oa, w_ob, w_out, norm_ffn.reshape(1, d).astype(F32), wr, br_t)


SEG = 256


def _route_kernel(lt_ref, pos_ref, gate_ref, blk_ref, oh_ref):
    t = lt_ref.shape[1]
    rid8 = lax.broadcasted_iota(jnp.int32, (8, t), 0)
    lg = jnp.where(rid8 < N_GROUPS, lt_ref[0:8, :], -jnp.inf)
    gmax = jnp.max(lg, axis=0, keepdims=True)
    g_idx = jnp.min(jnp.where(lg == gmax, rid8, 8), axis=0, keepdims=True)
    p_top = 1.0 / jnp.sum(jnp.exp(lg - gmax), axis=0, keepdims=True)

    les = jnp.zeros((EXP_PER_GROUP, t), F32)
    for g in range(N_GROUPS):
        les = jnp.where(g_idx == g, lt_ref[8 + g * EXP_PER_GROUP:8 + (g + 1) * EXP_PER_GROUP, :], les)
    m1 = jnp.max(les, axis=0, keepdims=True)
    i1 = jnp.min(jnp.where(les == m1, rid8, 8), axis=0, keepdims=True)
    les2 = jnp.where(rid8 == i1, -jnp.inf, les)
    m2 = jnp.max(les2, axis=0, keepdims=True)
    i2 = jnp.min(jnp.where(les2 == m2, rid8, 8), axis=0, keepdims=True)
    r = jnp.exp(m2 - m1)
    gate_ref[...] = jnp.concatenate([p_top / (1.0 + r), p_top * r / (1.0 + r),
                                     jnp.zeros((LANE - 2, t), F32)], axis=0).T
    e1 = g_idx * EXP_PER_GROUP + i1
    e2 = g_idx * EXP_PER_GROUP + i2

    rid32 = lax.broadcasted_iota(jnp.int32, (N_EXPERTS, t), 0)
    oh_ref[0] = jnp.where(rid32 == e1, 1.0, 0.0)
    oh_ref[1] = jnp.where(rid32 == e2, 1.0, 0.0)

    ui = lax.broadcasted_iota(jnp.int32, (SEG, SEG), 0)
    uj = lax.broadcasted_iota(jnp.int32, (SEG, SEG), 1)
    upper = jnp.where(ui < uj, 1.0, 0.0).astype(BF16)
    carry = jnp.zeros((N_EXPERTS, 1), F32)
    ranks = []
    for kk in range(2):
        segs = []
        for sg in range(t // SEG):
            oh = oh_ref[kk, :, sg * SEG:(sg + 1) * SEG]
            pre = jnp.dot(oh.astype(BF16), upper, preferred_element_type=F32) + carry
            segs.append(jnp.sum(oh * pre, axis=0, keepdims=True))
            carry = carry + jnp.sum(oh, axis=1, keepdims=True)
        ranks.append(jnp.concatenate(segs, axis=1))
    counts = carry
    nblk = jnp.floor((counts + (ROW_BLOCK - 1)) * (1.0 / ROW_BLOCK))
    li = lax.broadcasted_iota(jnp.int32, (N_EXPERTS, N_EXPERTS), 0)
    lj = lax.broadcasted_iota(jnp.int32, (N_EXPERTS, N_EXPERTS), 1)
    nb_b = jnp.broadcast_to(nblk, (N_EXPERTS, LANE))
    start_blk = _fdot(jnp.where(lj < li, 1.0, 0.0), nb_b)[:, 0:1]
    end_blk = start_blk + nblk
    start_row = start_blk * ROW_BLOCK

    pos_ref[...] = jnp.zeros_like(pos_ref)
    for kk in range(2):
        base = jnp.sum(oh_ref[kk] * start_row, axis=0, keepdims=True)
        pos_ref[kk:kk + 1, :] = (base + ranks[kk]).astype(jnp.int32)

    nb = blk_ref.shape[1]
    bid = lax.broadcasted_iota(jnp.int32, (N_EXPERTS, nb), 1).astype(F32)
    be = jnp.sum(jnp.where(end_blk <= bid, 1.0, 0.0), axis=0, keepdims=True)
    blk_ref[...] = jnp.zeros_like(blk_ref)
    blk_ref[0:1, :] = jnp.minimum(be, N_EXPERTS - 1.0).astype(jnp.int32)
    blk_ref[1:2, :] = jnp.broadcast_to(end_blk[N_EXPERTS - 1:N_EXPERTS, :], (1, nb)).astype(jnp.int32)
    ends = jnp.sum(jnp.where(li == lj, end_blk, 0.0), axis=0, keepdims=True)
    blk_ref[2:3, 0:N_EXPERTS] = ends.astype(jnp.int32)


def _route(lt, n_blk):
    t = lt.shape[1]
    nb = -(-n_blk // LANE) * LANE
    return pl.pallas_call(
        _route_kernel,
        out_shape=[jax.ShapeDtypeStruct((8, t), jnp.int32), jax.ShapeDtypeStruct((t, LANE), F32),
                   jax.ShapeDtypeStruct((8, nb), jnp.int32)],
        scratch_shapes=[pltpu.VMEM((2, N_EXPERTS, t), F32)],
        compiler_params=pltpu.CompilerParams(vmem_limit_bytes=48 * 2 ** 20),
    )(lt)


def _scatter_kernel(pos_ref, ends_ref, nu_ref, h_ref, xs_ref, zbuf, sem, zsem):
    tm = h_ref.shape[0]
    t = pl.num_programs(0) * tm
    base = pl.program_id(0) * tm
    n_blk = xs_ref.shape[0] // ROW_BLOCK

    def zero_block(blk):
        return pltpu.make_async_copy(zbuf, xs_ref.at[pl.ds(blk * ROW_BLOCK, ROW_BLOCK), :], zsem)

    def for_each_zero_block(fn):
        for e in range(N_EXPERTS):
            first = ends_ref[e - 1] if e else 0

            @pl.when(ends_ref[e] > first)
            def _():
                fn(zero_block(ends_ref[e] - 1))

        def tail(blk, c):
            fn(zero_block(blk))
            return c
        lax.fori_loop(nu_ref[0], n_blk, tail, 0)

    @pl.when(pl.program_id(0) == 0)
    def _():
        zbuf[...] = jnp.zeros_like(zbuf)
        for_each_zero_block(lambda c: c.start())
        for_each_zero_block(lambda c: c.wait())

    def row_copy(r, kk):
        p = pos_ref[kk * t + base + r]
        return pltpu.make_async_copy(h_ref.at[pl.ds(r, 1), :], xs_ref.at[pl.ds(p, 1), :], sem)

    def start(r, c):
        row_copy(r, 0).start()
        row_copy(r, 1).start()
        return c

    def wait(r, c):
        row_copy(r, 0).wait()
        row_copy(r, 1).wait()
        return c

    lax.fori_loop(0, tm, start, 0, unroll=8)
    lax.fori_loop(0, tm, wait, 0, unroll=8)


def _scatter_rows(pos_flat, ends, n_used, h2, n_rows, tm):
    t, d = h2.shape
    return pl.pallas_call(
        _scatter_kernel,
        grid_spec=pltpu.PrefetchScalarGridSpec(
            num_scalar_prefetch=3,
            grid=(t // tm,),
            in_specs=[pl.BlockSpec((tm, d), lambda i, *_: (i, 0))],
            out_specs=pl.BlockSpec(memory_space=pl.ANY),
            scratch_shapes=[pltpu.VMEM((ROW_BLOCK, d), h2.dtype), pltpu.SemaphoreType.DMA(()),
                            pltpu.SemaphoreType.DMA(())],
        ),
        out_shape=jax.ShapeDtypeStruct((n_rows, d), h2.dtype),
        compiler_params=_cparams(("arbitrary",), 32),
    )(pos_flat, ends, n_used, h2)


def _expert_kernel(be_ref, nu_ref, ends_ref, x_ref, w1_hbm, w3_hbm, w2_hbm, o_ref,
                   w1s, w3s, w2s, w1b, w3b, w2b, sem):
    b = pl.program_id(0)
    e = be_ref[b]
    prev = be_ref[jnp.maximum(b - 1, 0)]
    changed = jnp.logical_or(b == 0, e != prev)

    def weight_copies(ex):
        return (pltpu.make_async_copy(w1_hbm.at[ex], w1s, sem.at[0]),
                pltpu.make_async_copy(w3_hbm.at[ex], w3s, sem.at[1]),
                pltpu.make_async_copy(w2_hbm.at[ex], w2s, sem.at[2]))

    @pl.when(b == 0)
    def _():
        for c in weight_copies(e):
            c.start()

    @pl.when(jnp.logical_and(changed, b < nu_ref[0]))
    def _():
        for c in weight_copies(e):
            c.wait()
        w1b[...] = w1s[...].astype(BF16)
        w3b[...] = w3s[...].astype(BF16)
        w2b[...] = w2s[...].astype(BF16)
        nxt = ends_ref[e]

        @pl.when(nxt < nu_ref[0])
        def _():
            for c in weight_copies(be_ref[nxt]):
                c.start()

    @pl.when(b < nu_ref[0])
    def _():
        xb = x_ref[...].astype(BF16)
        a = jnp.dot(xb, w1b[...], preferred_element_type=F32)
        g = jnp.dot(xb, w3b[...], preferred_element_type=F32)
        o_ref[...] = jnp.dot((_silu(a) * g).astype(BF16), w2b[...], preferred_element_type=F32)

    @pl.when(b >= nu_ref[0])
    def _():
        o_ref[...] = jnp.zeros_like(o_ref)


def _experts(blk_exp, n_used, ends, xs, w1, w3, w2):
    n_rows, d = xs.shape
    n_blk = n_rows // ROW_BLOCK
    rows = lambda b, be, nu, en: (jnp.minimum(b, jnp.maximum(nu[0] - 1, 0)), 0)
    hbm = pl.BlockSpec(memory_space=pl.ANY)
    return pl.pallas_call(
        _expert_kernel,
        grid_spec=pltpu.PrefetchScalarGridSpec(
            num_scalar_prefetch=3,
            grid=(n_blk,),
            in_specs=[pl.BlockSpec((ROW_BLOCK, d), rows), hbm, hbm, hbm],
            out_specs=pl.BlockSpec((ROW_BLOCK, d), lambda b, be, nu, en: (b, 0)),
            scratch_shapes=[pltpu.VMEM((d, D_FF), w1.dtype), pltpu.VMEM((d, D_FF), w3.dtype),
                            pltpu.VMEM((D_FF, d), w2.dtype),
                            pltpu.VMEM((d, D_FF), BF16), pltpu.VMEM((d, D_FF), BF16),
                            pltpu.VMEM((D_FF, d), BF16), pltpu.SemaphoreType.DMA((3,))],
        ),
        out_shape=jax.ShapeDtypeStruct((n_rows, d), F32),
        compiler_params=_cparams(("arbitrary",), 56),
    )(blk_exp, n_used, ends, xs, w1, w3, w2)


def _combine_kernel(pos_ref, x2_ref, gt_ref, nw_ref, yb_ref, o_ref, buf, sem, *, final):
    tm = x2_ref.shape[0]
    n = pl.num_programs(0)
    t = n * tm
    i = pl.program_id(0)

    def row_copy(step, slot, r, kk):
        p = pos_ref[kk * t + step * tm + r]
        return pltpu.make_async_copy(yb_ref.at[pl.ds(p, 1), :], buf.at[slot, kk, pl.ds(r, 1), :], sem.at[slot])

    def issue(step, slot):
        def body(r, c):
            row_copy(step, slot, r, 0).start()
            row_copy(step, slot, r, 1).start()
            return c
        lax.fori_loop(0, tm, body, 0, unroll=8)

    @pl.when(i == 0)
    def _():
        issue(0, 0)

    @pl.when(i + 1 < n)
    def _():
        issue(i + 1, (i + 1) % 2)

    slot = i % 2

    def wbody(r, c):
        row_copy(i, slot, r, 0).wait()
        row_copy(i, slot, r, 1).wait()
        return c
    lax.fori_loop(0, tm, wbody, 0, unroll=8)

    gt = gt_ref[...]
    y = x2_ref[...] + gt[:, 0:1] * buf[slot, 0] + gt[:, 1:2] * buf[slot, 1]
    if final:
        y = y * lax.rsqrt(jnp.mean(y * y, axis=-1, keepdims=True) + EPS) * nw_ref[...]
    o_ref[...] = y


def _combine(pos_flat, x2, gates_t, norm_final, yb, tm, final):
    t, d = x2.shape
    return pl.pallas_call(
        functools.partial(_combine_kernel, final=final),
        grid_spec=pltpu.PrefetchScalarGridSpec(
            num_scalar_prefetch=1,
            grid=(t // tm,),
            in_specs=[pl.BlockSpec((tm, d), lambda i, pos: (i, 0)),
                      pl.BlockSpec((tm, LANE), lambda i, pos: (i, 0)),
                      pl.BlockSpec((1, d), lambda i, pos: (0, 0)),
                      pl.BlockSpec(memory_space=pl.ANY)],
            out_specs=pl.BlockSpec((tm, d), lambda i, pos: (i, 0)),
            scratch_shapes=[pltpu.VMEM((2, 2, tm, d), F32), pltpu.SemaphoreType.DMA((2,))],
        ),
        out_shape=jax.ShapeDtypeStruct((t, d), F32),
        compiler_params=_cparams(("arbitrary",), 32),
    )(pos_flat, x2, gates_t, norm_final.reshape(1, d).astype(F32), yb)


def _layer(x, norm_mix, w_in, conv_a, a_log, dt_bias, a_norm, w_gk2, b_gk, b_norm,
           w_oa, w_ob, w_out, norm_ffn, w_rg, b_rg, w_re, b_re, w1, w3, w2, norm_final, final):
    t, d = x.shape
    tm = min(512, t)

    a_end = 4 * A_QK
    ga_end = a_end + 2 * A_HEADS
    b_end = ga_end + 2 * B_QK + 2 * B_V
    lr_end = b_end + B_GATE_RANK
    assert a_end % LANE == 0 and b_end % LANE == 2 * A_HEADS

    h = _rmsnorm(x, norm_mix, tm)
    tm_big = min(1024, t)
    wt = w_in.T
    z_a = _proj(h, wt, 0, a_end, tm_big, 1024)
    z_b = _proj(h, wt, ga_end, b_end - ga_end, tm_big, 1024)
    z_mix = _proj(h, wt, lr_end, 2 * d, tm_big, 1024)
    z_ga = _proj(h, wt, a_end, LANE, tm_big, LANE)
    z_lr = _proj(h, wt, b_end - b_end % LANE, LANE, tm_big, LANE)

    oa_g = _mixer_a(z_a, z_ga, conv_a.astype(F32), a_log, dt_bias, a_norm)
    ob_g = _mixer_b(z_b, z_lr, w_gk2, b_gk, b_norm)

    wr = jnp.zeros((d, LANE), F32).at[:, 0:N_GROUPS].set(w_rg.astype(F32))
    wr = wr.at[:, 8:8 + N_EXPERTS].set(w_re.reshape(d, N_EXPERTS).astype(F32))
    wr_hi = wr.astype(BF16)
    wr = jnp.stack([wr_hi, (wr - wr_hi.astype(F32)).astype(BF16)])
    br_t = jnp.zeros((LANE, 1), F32).at[0:N_GROUPS, 0].set(b_rg.astype(F32))
    br_t = br_t.at[8:8 + N_EXPERTS, 0].set(b_re.reshape(N_EXPERTS).astype(F32))
    x2, h2, lt = _merge(oa_g, ob_g, z_mix, x, w_oa.astype(BF16), w_ob.astype(BF16), w_out.astype(BF16),
                        norm_ffn, wr, br_t, min(256, t))

    n_blk = (2 * t + ROW_BLOCK - 1) // ROW_BLOCK + N_EXPERTS
    n_rows = n_blk * ROW_BLOCK
    pos, gates, blk = _route(lt, n_blk)
    pos_flat = pos[0:2].reshape(2 * t)
    blk_exp, n_used, ends = blk[0, :n_blk], blk[1, 0:1], blk[2, :N_EXPERTS]
    xs = _scatter_rows(pos_flat, ends, n_used, h2, n_rows, min(256, t))
    yb = _experts(blk_exp, n_used, ends, xs, w1, w3, w2)
    return _combine(pos_flat, x2, gates, norm_final, yb, min(128, t), final)


def kernel(x, norm_mix, w_in, conv_a, a_log, dt_bias, a_norm, w_gk2, b_gk, b_norm, w_oa, w_ob, w_out,
           norm_ffn, w_rg, b_rg, w_re, b_re, w1, w3, w2, norm_final):
    bsz, seq, d = x.shape
    assert bsz == 1, "one sequence per call"
    depth = norm_mix.shape[0]
    y = x.reshape(seq, d)
    for l in range(depth):
        y = _layer(y, norm_mix[l], w_in[l], conv_a[l], a_log[l], dt_bias[l], a_norm[l], w_gk2[l], b_gk[l],
                   b_norm[l], w_oa[l], w_ob[l], w_out[l], norm_ffn[l], w_rg[l], b_rg[l], w_re[l], b_re[l],
                   w1[l], w3[l], w2[l], norm_final, l == depth - 1)
    return y.reshape(bsz, seq, d)
```

```python
import functools

import jax
import jax.numpy as jnp
from jax import lax
from jax.experimental import pallas as pl
from jax.experimental.pallas import tpu as pltpu

D_MODEL = 2048
CHUNK = 64
EPS = 1e-6
A_HEADS, A_DK, A_DV, A_CONV = 8, 128, 128, 4
A_QK, A_V = A_HEADS * A_DK, A_HEADS * A_DV
B_HEADS, B_DK, B_DV, B_GATE_RANK, B_GATE_NORM = 4, 128, 256, 16, 16.0
B_QK, B_V = B_HEADS * B_DK, B_HEADS * B_DV
N_GROUPS, EXP_PER_GROUP, D_FF = 4, 8, 512
N_EXPERTS = N_GROUPS * EXP_PER_GROUP
ROW_BLOCK = 256
LANE = 128
SUB = 16
TB = 256
A_HG = 8
NEG = -1e30

F32 = jnp.float32
BF16 = jnp.bfloat16
HI = lax.Precision.HIGHEST


def _cparams(sem, vmem_mib):
    return pltpu.CompilerParams(dimension_semantics=sem, vmem_limit_bytes=vmem_mib * 2 ** 20)


def _bdot(a, b):
    return jnp.dot(a.astype(BF16), b.astype(BF16), preferred_element_type=F32)


def _bdot_nt(a, b):
    return lax.dot_general(a.astype(BF16), b.astype(BF16), (((1,), (1,)), ((), ())),
                           preferred_element_type=F32)


def _bdot_tn(a, b):
    return lax.dot_general(a.astype(BF16), b.astype(BF16), (((0,), (0,)), ((), ())),
                           preferred_element_type=F32)


def _fdot(a, b):
    return jnp.dot(a, b, preferred_element_type=F32, precision=HI)


def _sigmoid(x):
    return 1.0 / (1.0 + jnp.exp(-x))


def _silu(x):
    return x * _sigmoid(x)


def _softplus(x):
    return jnp.maximum(x, 0.0) + jnp.log(1.0 + jnp.exp(-jnp.abs(x)))


def _rmsnorm_kernel(x_ref, w_ref, o_ref):
    x = x_ref[...]
    ms = jnp.mean(x * x, axis=-1, keepdims=True)
    o_ref[...] = (x * lax.rsqrt(ms + EPS) * w_ref[...]).astype(o_ref.dtype)


def _rmsnorm(x, w, tm):
    t, d = x.shape
    return pl.pallas_call(
        _rmsnorm_kernel,
        grid=(t // tm,),
        in_specs=[pl.BlockSpec((tm, d), lambda i: (i, 0)), pl.BlockSpec((1, d), lambda i: (0, 0))],
        out_specs=pl.BlockSpec((tm, d), lambda i: (i, 0)),
        out_shape=jax.ShapeDtypeStruct((t, d), BF16),
        compiler_params=_cparams(("parallel",), 32),
    )(x, w.reshape(1, d))


PREP_ROWS = 256


def _proj_kernel(a_ref, wt_ref, o_ref, wb_ref):
    tn = wt_ref.shape[0]

    @pl.when(pl.program_id(1) == 0)
    def _():
        for r in range(0, tn, PREP_ROWS):
            wb_ref[r:r + PREP_ROWS, :] = wt_ref[r:r + PREP_ROWS, :].astype(BF16)

    o_ref[...] = lax.dot_general(a_ref[...], wb_ref[...], (((1,), (1,)), ((), ())), preferred_element_type=F32)


def _proj(a, wt, col0, n, tm, tn):
    m, k = a.shape
    assert n % tn == 0 and col0 % 8 == 0
    return pl.pallas_call(
        _proj_kernel,
        grid=(n // tn, m // tm),
        in_specs=[pl.BlockSpec((tm, k), lambda j, i: (i, 0)),
                  pl.BlockSpec((pl.Element(tn), pl.Element(k)), lambda j, i: (pl.multiple_of(col0 + j * tn, 8), 0))],
        out_specs=pl.BlockSpec((tm, tn), lambda j, i: (i, j)),
        out_shape=jax.ShapeDtypeStruct((m, n), F32),
        scratch_shapes=[pltpu.VMEM((tn, k), BF16)],
        compiler_params=_cparams(("arbitrary", "arbitrary"), 48),
    )(a, wt)


def _chunk_masks(tb):
    row = lax.broadcasted_iota(jnp.int32, (tb, tb), 0)
    col = lax.broadcasted_iota(jnp.int32, (tb, tb), 1)
    same = (row // CHUNK) == (col // CHUNK)
    return row, col, same, same & (col <= row), same & (col < row)


def _lane_pick(x, idx):
    lane = lax.broadcasted_iota(jnp.int32, x.shape, 1)
    return jnp.sum(jnp.where(lane == idx, x, 0.0), axis=-1, keepdims=True)


def _causal_conv_silu(x_ref, w_ref, halo_ref, cbuf_ref, idx, tb):
    cbuf_ref[idx, 0:8, :] = halo_ref[idx]
    cbuf_ref[idx, 8:8 + tb, :] = x_ref[...]
    halo_ref[idx] = x_ref[tb - 8:tb, :]
    w = w_ref[...]
    acc = w[A_CONV - 1:A_CONV, :] * x_ref[...]
    for j in range(A_CONV - 1):
        off = 8 - (A_CONV - 1) + j
        acc = acc + w[j:j + 1, :] * cbuf_ref[idx, off:off + tb, :]
    return _silu(acc)


def _gates_a_kernel(tail_ref, alog_ref, dtb_ref, ga_ref, gct_ref):
    tb = tail_ref.shape[0]
    gl = tail_ref[...]
    beta = _sigmoid(gl)
    g = -jnp.exp(alog_ref[...]) * _softplus(gl + dtb_ref[...])
    _, _, same, causal, _ = _chunk_masks(tb)
    gc = _fdot(jnp.where(causal, 1.0, 0.0), g)
    glast = _fdot(jnp.where(same, 1.0, 0.0), g)
    lane = lax.broadcasted_iota(jnp.int32, (tb, LANE), 1)
    ga_ref[...] = jnp.where(lane < A_HEADS, beta,
                            jnp.where(lane < 2 * A_HEADS, gc, pltpu.roll(glast, A_HEADS, 1)))
    gct_ref[...] = gc.T[A_HEADS:2 * A_HEADS, :]


def _gates_a(z_tail, a_log, dt_bias):
    t = z_tail.shape[0]
    pad = lambda p: jnp.pad(p.astype(F32), (A_HEADS, LANE - 2 * A_HEADS)).reshape(1, LANE)
    const = pl.BlockSpec((1, LANE), lambda i: (0, 0))
    return pl.pallas_call(
        _gates_a_kernel,
        grid=(t // TB,),
        in_specs=[pl.BlockSpec((TB, LANE), lambda i: (i, 0)), const, const],
        out_specs=[pl.BlockSpec((TB, LANE), lambda i: (i, 0)), pl.BlockSpec((A_HEADS, TB), lambda i: (0, i))],
        out_shape=[jax.ShapeDtypeStruct((t, LANE), F32), jax.ShapeDtypeStruct((A_HEADS, t), F32)],
        compiler_params=_cparams(("parallel",), 32),
    )(z_tail, pad(a_log), pad(dt_bias))


def _delta_heads(qs, ks, vs, ga, gc_rows, hs, sts, tb):
    n = len(qs)
    rng = range(n)
    _, _, _, causal, strict = _chunk_masks(tb)
    qs = [q * lax.rsqrt(jnp.sum(q * q, axis=-1, keepdims=True) + EPS) * (A_DK ** -0.5) for q in qs]
    ks = [k * lax.rsqrt(jnp.sum(k * k, axis=-1, keepdims=True) + EPS) for k in ks]
    beta = [_lane_pick(ga, h) for h in hs]
    gc = [_lane_pick(ga, h + A_HEADS) for h in hs]
    glast = [_lane_pick(ga, h + 2 * A_HEADS) for h in hs]
    decay = [jnp.exp(jnp.where(causal, gc[i] - gc_rows[i], NEG)) for i in rng]
    kb = [ks[i] * beta[i] for i in rng]

    n_pow = [jnp.where(strict, _bdot_nt(kb[i], ks[i]) * decay[i], 0.0) * -1.0 for i in rng]
    t_mat = list(n_pow)
    lvl = 2
    while lvl < CHUNK:
        n_pow = [_bdot(m, m) for m in n_pow]
        t_mat = [t_mat[i] + n_pow[i] + _bdot(t_mat[i], n_pow[i]) for i in rng]
        lvl *= 2

    egc = [jnp.exp(g) for g in gc]
    rhs = [jnp.concatenate([vs[i] * beta[i], kb[i] * egc[i]], axis=1) for i in rng]
    uw = [rhs[i] + _bdot(t_mat[i], rhs[i]) for i in rng]
    qk = [_bdot_nt(qs[i], ks[i]) * decay[i] for i in rng]
    qkuw = [_bdot(qk[i], uw[i]) for i in rng]
    o_local = [x[:, :A_DV] for x in qkuw]
    q_eff = [qs[i] * egc[i] - qkuw[i][:, A_DV:] for i in rng]
    k_dec = [ks[i] * jnp.exp(glast[i] - gc[i]) for i in rng]
    eg_last = [jnp.exp(g) for g in glast]

    sts = list(sts)
    outs = [[] for _ in rng]
    for c in range(tb // CHUNK):
        lo, hi = c * CHUNK, (c + 1) * CHUNK
        bg = [_bdot_tn(uw[i][lo:hi], k_dec[i][lo:hi]) for i in rng]
        for i in rng:
            outs[i].append(o_local[i][lo:hi] + _bdot_nt(q_eff[i][lo:hi], sts[i]))
        sts = [sts[i] * eg_last[i][lo:lo + 1, :] + bg[i][:A_DV] - _bdot(sts[i], bg[i][A_DV:]) for i in rng]
    return [jnp.concatenate(o, axis=0) for o in outs], sts


def _mixer_a_kernel(xq_ref, xk_ref, xv_ref, z_ref, ga_ref, gct_ref, wq_ref, wk_ref, wv_ref,
                    anorm_ref, o_ref, halo_ref, cbuf_ref, state_ref):
    tb = xq_ref.shape[0]
    hg = xq_ref.shape[1] // A_DK

    @pl.when(pl.program_id(1) == 0)
    def _():
        halo_ref[...] = jnp.zeros_like(halo_ref)
        state_ref[...] = jnp.zeros_like(state_ref)

    q = _causal_conv_silu(xq_ref, wq_ref, halo_ref, cbuf_ref, 0, tb)
    k = _causal_conv_silu(xk_ref, wk_ref, halo_ref, cbuf_ref, 1, tb)
    v = _causal_conv_silu(xv_ref, wv_ref, halo_ref, cbuf_ref, 2, tb)
    ga = ga_ref[...]
    hs = [pl.program_id(0) * hg + j for j in range(hg)]
    sls = [slice(j * A_DK, (j + 1) * A_DK) for j in range(hg)]
    outs, sts = _delta_heads([q[:, s] for s in sls], [k[:, s] for s in sls], [v[:, s] for s in sls], ga,
                             [gct_ref[pl.ds(h, 1), :] for h in hs], hs, [state_ref[j] for j in range(hg)], tb)
    for j in range(hg):
        state_ref[j] = sts[j]
        o = outs[j]
        o = o * lax.rsqrt(jnp.mean(o * o, axis=-1, keepdims=True) + EPS) * anorm_ref[...]
        o_ref[:, sls[j]] = (o * _silu(z_ref[:, sls[j]])).astype(o_ref.dtype)


def _mixer_a(z_a, z_tail, conv_a, a_log, dt_bias, a_norm):
    t = z_a.shape[0]
    ng = A_HEADS // A_HG
    wid = A_HG * A_DK
    ga, gct = _gates_a(z_tail, a_log, dt_bias)
    blk = lambda off: pl.BlockSpec((TB, wid), lambda g, i: (i, off + g))
    cblk = lambda off: pl.BlockSpec((A_CONV, wid), lambda g, i: (0, off + g))
    return pl.pallas_call(
        _mixer_a_kernel,
        grid=(ng, t // TB),
        in_specs=[blk(0), blk(ng), blk(2 * ng), blk(3 * ng),
                  pl.BlockSpec((TB, LANE), lambda g, i: (i, 0)),
                  pl.BlockSpec((A_HEADS, TB), lambda g, i: (0, i)),
                  cblk(0), cblk(ng), cblk(2 * ng), pl.BlockSpec((1, LANE), lambda g, i: (0, 0))],
        out_specs=pl.BlockSpec((TB, wid), lambda g, i: (i, g)),
        out_shape=jax.ShapeDtypeStruct((t, A_V), BF16),
        scratch_shapes=[pltpu.VMEM((3, 8, wid), F32), pltpu.VMEM((3, 8 + TB, wid), F32),
                        pltpu.VMEM((A_HG, A_DV, A_DK), F32)],
        compiler_params=_cparams(("parallel", "arbitrary"), 48),
    )(z_a, z_a, z_a, z_a, ga, gct, conv_a, conv_a, conv_a,
      a_norm.reshape(1, A_DV).astype(F32))


def _cumsum_rows(mask, x):
    hi = x.astype(BF16)
    r1 = x - hi.astype(F32)
    mid = r1.astype(BF16)
    lo = (r1 - mid.astype(F32)).astype(BF16)
    dot = lambda p: jnp.dot(mask, p, preferred_element_type=F32)
    return dot(hi) + dot(mid) + dot(lo)


def _gla_chunk_scores(qc, kc, bc):
    rid = lax.broadcasted_iota(jnp.int32, (SUB, B_DK), 0)
    rid_lo = lax.broadcasted_iota(jnp.int32, (SUB // 2, B_DK), 0) + SUB // 2
    lane_c = lax.broadcasted_iota(jnp.int32, (SUB, CHUNK), 1)
    crow = lax.broadcasted_iota(jnp.int32, (CHUNK, B_DK), 0)
    half = SUB // 2
    rows = []
    for si in range(CHUNK // SUB):
        r0 = si * SUB
        qb, bb = qc[r0:r0 + SUB], bc[r0:r0 + SUB]
        ys = []
        for j in range(SUB):
            bj = bc[r0 + j:r0 + j + 1]
            if j < half:
                ys.append(qb * jnp.exp(jnp.where(rid >= j, bb - bj, NEG)))
            else:
                ys.append(jnp.zeros((half, B_DK), F32))
                ys.append(qb[half:] * jnp.exp(jnp.where(rid_lo >= j, bb[half:] - bj, NEG)))
        r = _bdot_nt(jnp.concatenate(ys, axis=0), kc)
        blk = jnp.zeros((SUB, CHUNK), F32)
        for j in range(SUB):
            blk = jnp.where(lane_c == r0 + j, r[j * SUB:(j + 1) * SUB], blk)
        if si > 0:
            bref = bc[r0:r0 + 1]
            qt = qb * jnp.exp(bb - bref)
            kt = kc * jnp.exp(jnp.where(crow < r0, bref - bc, NEG))
            blk = blk + _bdot_nt(qt, kt)
        rows.append(blk)
    return jnp.concatenate(rows, axis=0)


def _mixer_b_kernel(q_ref, k_ref, v_ref, zg_ref, tail_ref, wg_ref, bgk_ref, bnorm_ref, o_ref, state_ref):
    tb = q_ref.shape[0]
    nh = q_ref.shape[1] // B_DK

    @pl.when(pl.program_id(0) == 0)
    def _():
        state_ref[...] = jnp.zeros_like(state_ref)

    x = _fdot(tail_ref[...], wg_ref[...]) + bgk_ref[...]
    gk = -_softplus(-x) * (1.0 / B_GATE_NORM)
    _, _, _, causal, _ = _chunk_masks(tb)
    b_all = _cumsum_rows(jnp.where(causal, 1.0, 0.0).astype(BF16), gk)

    sts = [state_ref[h] for h in range(nh)]
    outs = [[] for _ in range(nh)]
    for c in range(tb // CHUNK):
        lo, hi = c * CHUNK, (c + 1) * CHUNK
        for h in range(nh):
            qc = q_ref[lo:hi, h * B_DK:(h + 1) * B_DK] * (B_DK ** -0.5)
            kc = k_ref[lo:hi, h * B_DK:(h + 1) * B_DK]
            vc = v_ref[lo:hi, h * B_DV:(h + 1) * B_DV]
            bc = b_all[lo:hi, h * B_DK:(h + 1) * B_DK]
            a_c = _gla_chunk_scores(qc, kc, bc)
            bl = bc[CHUNK - 1:CHUNK]
            outs[h].append(_bdot(a_c, vc) + _bdot_nt(qc * jnp.exp(bc), sts[h]))
            sts[h] = sts[h] * jnp.exp(bl) + _bdot_tn(vc, kc * jnp.exp(bl - bc))
    for h in range(nh):
        state_ref[h] = sts[h]
        o = jnp.concatenate(outs[h], axis=0)
        o = o * lax.rsqrt(jnp.mean(o * o, axis=-1, keepdims=True) + EPS) * bnorm_ref[...]
        sl = slice(h * B_DV, (h + 1) * B_DV)
        o_ref[:, sl] = (o * _silu(zg_ref[:, sl])).astype(o_ref.dtype)


def _mixer_b(z_b, z_tail, w_gk2, b_gk, b_norm):
    t = z_b.shape[0]
    wg = jnp.zeros((LANE, B_QK), F32).at[2 * A_HEADS:2 * A_HEADS + B_GATE_RANK].set(w_gk2.astype(F32))
    col = lambda w, off: pl.BlockSpec((TB, w), lambda i: (i, off // w))
    full = lambda r, c: pl.BlockSpec((r, c), lambda i: (0, 0))
    return pl.pallas_call(
        _mixer_b_kernel,
        grid=(t // TB,),
        in_specs=[col(B_QK, 0), col(B_QK, B_QK), col(B_V, 2 * B_QK), col(B_V, 2 * B_QK + B_V), col(LANE, 0),
                  full(LANE, B_QK), full(1, B_QK), full(1, B_DV)],
        out_specs=pl.BlockSpec((TB, B_V), lambda i: (i, 0)),
        out_shape=jax.ShapeDtypeStruct((t, B_V), BF16),
        scratch_shapes=[pltpu.VMEM((B_HEADS, B_DV, B_DK), F32)],
        compiler_params=_cparams(("arbitrary",), 48),
    )(z_b, z_b, z_b, z_b, z_tail, wg, b_gk.reshape(1, B_QK).astype(F32),
      b_norm.reshape(1, B_DV).astype(F32))


def _merge_kernel(oa_ref, ob_ref, ma_ref, mb_ref, x_ref, woa_ref, wob_ref, wout_ref, nf_ref,
                  wr_ref, br_ref, x2_ref, h2_ref, lt_ref):
    ya = jnp.dot(oa_ref[...], woa_ref[...], preferred_element_type=F32)
    yb = jnp.dot(ob_ref[...], wob_ref[...], preferred_element_type=F32)
    m = _sigmoid(ma_ref[...]) * ya + _sigmoid(mb_ref[...]) * yb
    x2 = x_ref[...] + jnp.dot(m.astype(BF16), wout_ref[...], preferred_element_type=F32)
    x2_ref[...] = x2
    h2 = x2 * lax.rsqrt(jnp.mean(x2 * x2, axis=-1, keepdims=True) + EPS) * nf_ref[...]
    h2_ref[...] = h2
    h_hi = h2.astype(BF16)
    h_mid = (h2 - h_hi.astype(F32)).astype(BF16)
    dot = lambda a, b: jnp.dot(a, b, preferred_element_type=F32)
    logits = dot(h_hi, wr_ref[0]) + (dot(h_mid, wr_ref[0]) + dot(h_hi, wr_ref[1]))
    lt_ref[...] = logits.T + br_ref[...]


def _merge(oa_g, ob_g, z_mix, x, w_oa, w_ob, w_out, norm_ffn, wr, br_t, tm):
    t, d = x.shape
    row = lambda w, c: pl.BlockSpec((tm, w), lambda i: (i, c))
    full = lambda a: pl.BlockSpec(a.shape, lambda i: (0,) * a.ndim, pipeline_mode=pl.Buffered(1))
    return pl.pallas_call(
        _merge_kernel,
        grid=(t // tm,),
        in_specs=[row(A_V, 0), row(B_V, 0), row(d, 0), row(d, 1), row(d, 0),
                  full(w_oa), full(w_ob), full(w_out), pl.BlockSpec((1, d), lambda i: (0, 0)),
                  full(wr), full(br_t)],
        out_specs=[row(d, 0), row(d, 0), pl.BlockSpec((LANE, tm), lambda i: (0, i))],
        out_shape=[jax.ShapeDtypeStruct((t, d), F32), jax.ShapeDtypeStruct((t, d), F32),
                   jax.ShapeDtypeStruct((LANE, t), F32)],
        compiler_params=_cparams(("parallel",), 56),
    )(oa_g, ob_g, z_mix, z_mix, x, w_oa, w_ob, w_out, norm_ffn.reshape(1, d).astype(F32), wr, br_t)


SEG = 256


def _route_kernel(lt_ref, pos_ref, gate_ref, blk_ref, oh_ref):
    t = lt_ref.shape[1]
    rid8 = lax.broadcasted_iota(jnp.int32, (8, t), 0)
    lg = jnp.where(rid8 < N_GROUPS, lt_ref[0:8, :], -jnp.inf)
    gmax = jnp.max(lg, axis=0, keepdims=True)
    g_idx = jnp.min(jnp.where(lg == gmax, rid8, 8), axis=0, keepdims=True)
    p_top = 1.0 / jnp.sum(jnp.exp(lg - gmax), axis=0, keepdims=True)

    les = jnp.zeros((EXP_PER_GROUP, t), F32)
    for g in range(N_GROUPS):
        les = jnp.where(g_idx == g, lt_ref[8 + g * EXP_PER_GROUP:8 + (g + 1) * EXP_PER_GROUP, :], les)
    m1 = jnp.max(les, axis=0, keepdims=True)
    i1 = jnp.min(jnp.where(les == m1, rid8, 8), axis=0, keepdims=True)
    les2 = jnp.where(rid8 == i1, -jnp.inf, les)
    m2 = jnp.max(les2, axis=0, keepdims=True)
    i2 = jnp.min(jnp.where(les2 == m2, rid8, 8), axis=0, keepdims=True)
    r = jnp.exp(m2 - m1)
    gate_ref[...] = jnp.concatenate([p_top / (1.0 + r), p_top * r / (1.0 + r),
                                     jnp.zeros((LANE - 2, t), F32)], axis=0).T
    e1 = g_idx * EXP_PER_GROUP + i1
    e2 = g_idx * EXP_PER_GROUP + i2

    rid32 = lax.broadcasted_iota(jnp.int32, (N_EXPERTS, t), 0)
    oh_ref[0] = jnp.where(rid32 == e1, 1.0, 0.0)
    oh_ref[1] = jnp.where(rid32 == e2, 1.0, 0.0)

    ui = lax.broadcasted_iota(jnp.int32, (SEG, SEG), 0)
    uj = lax.broadcasted_iota(jnp.int32, (SEG, SEG), 1)
    upper = jnp.where(ui < uj, 1.0, 0.0).astype(BF16)
    carry = jnp.zeros((N_EXPERTS, 1), F32)
    ranks = []
    for kk in range(2):
        segs = []
        for sg in range(t // SEG):
            oh = oh_ref[kk, :, sg * SEG:(sg + 1) * SEG]
            pre = jnp.dot(oh.astype(BF16), upper, preferred_element_type=F32) + carry
            segs.append(jnp.sum(oh * pre, axis=0, keepdims=True))
            carry = carry + jnp.sum(oh, axis=1, keepdims=True)
        ranks.append(jnp.concatenate(segs, axis=1))
    counts = carry
    nblk = jnp.floor((counts + (ROW_BLOCK - 1)) * (1.0 / ROW_BLOCK))
    li = lax.broadcasted_iota(jnp.int32, (N_EXPERTS, N_EXPERTS), 0)
    lj = lax.broadcasted_iota(jnp.int32, (N_EXPERTS, N_EXPERTS), 1)
    nb_b = jnp.broadcast_to(nblk, (N_EXPERTS, LANE))
    start_blk = _fdot(jnp.where(lj < li, 1.0, 0.0), nb_b)[:, 0:1]
    end_blk = start_blk + nblk
    start_row = start_blk * ROW_BLOCK

    pos_ref[...] = jnp.zeros_like(pos_ref)
    for kk in range(2):
        base = jnp.sum(oh_ref[kk] * start_row, axis=0, keepdims=True)
        pos_ref[kk:kk + 1, :] = (base + ranks[kk]).astype(jnp.int32)

    nb = blk_ref.shape[1]
    bid = lax.broadcasted_iota(jnp.int32, (N_EXPERTS, nb), 1).astype(F32)
    be = jnp.sum(jnp.where(end_blk <= bid, 1.0, 0.0), axis=0, keepdims=True)
    blk_ref[...] = jnp.zeros_like(blk_ref)
    blk_ref[0:1, :] = jnp.minimum(be, N_EXPERTS - 1.0).astype(jnp.int32)
    blk_ref[1:2, :] = jnp.broadcast_to(end_blk[N_EXPERTS - 1:N_EXPERTS, :], (1, nb)).astype(jnp.int32)
    ends = jnp.sum(jnp.where(li == lj, end_blk, 0.0), axis=0, keepdims=True)
    blk_ref[2:3, 0:N_EXPERTS] = ends.astype(jnp.int32)


def _route(lt, n_blk):
    t = lt.shape[1]
    nb = -(-n_blk // LANE) * LANE
    return pl.pallas_call(
        _route_kernel,
        out_shape=[jax.ShapeDtypeStruct((8, t), jnp.int32), jax.ShapeDtypeStruct((t, LANE), F32),
                   jax.ShapeDtypeStruct((8, nb), jnp.int32)],
        scratch_shapes=[pltpu.VMEM((2, N_EXPERTS, t), F32)],
        compiler_params=pltpu.CompilerParams(vmem_limit_bytes=48 * 2 ** 20),
    )(lt)


def _expert_kernel(be_ref, nu_ref, ends_ref, src_ref, h_hbm, w1_hbm, w3_hbm, w2_hbm, o_ref,
                   xbuf, w1s, w3s, w2s, w1b, w3b, w2b, wsem, gsem):
    nu = nu_ref[0]

    def weight_copies(ex):
        return (pltpu.make_async_copy(w1_hbm.at[ex], w1s, wsem.at[0]),
                pltpu.make_async_copy(w3_hbm.at[ex], w3s, wsem.at[1]),
                pltpu.make_async_copy(w2_hbm.at[ex], w2s, wsem.at[2]))

    def start_gather(blk, slot):
        for r in range(ROW_BLOCK):
            tok = src_ref[blk * ROW_BLOCK + r]
            pltpu.make_async_copy(h_hbm.at[pl.ds(tok, 1), :], xbuf.at[slot, pl.ds(r, 1), :], gsem.at[slot]).start()

    def wait_gather(slot):
        pltpu.make_async_copy(h_hbm.at[pl.ds(0, ROW_BLOCK), :], xbuf.at[slot], gsem.at[slot]).wait()

    def row_group(b, slot):
        e = be_ref[b]
        changed = jnp.logical_or(b == 0, e != be_ref[jnp.maximum(b - 1, 0)])
        rows = slice(slot * ROW_BLOCK, (slot + 1) * ROW_BLOCK)

        if slot == 0:
            @pl.when(jnp.logical_and(b == 0, nu > 0))
            def _():
                for c in weight_copies(e):
                    c.start()

            @pl.when(b == 0)
            def _():
                start_gather(0, 0)

        @pl.when(b <= nu)
        def _():
            wait_gather(slot)

        @pl.when(jnp.logical_and(changed, b < nu))
        def _():
            for c in weight_copies(e):
                c.wait()
            for stage, dst in ((w1s, w1b), (w3s, w3b), (w2s, w2b)):
                for r in range(0, stage.shape[0], PREP_ROWS):
                    dst[r:r + PREP_ROWS, :] = stage[r:r + PREP_ROWS, :].astype(BF16)
            nxt = ends_ref[e]

            @pl.when(nxt < nu)
            def _():
                for c in weight_copies(be_ref[nxt]):
                    c.start()

        @pl.when(b < nu)
        def _():
            start_gather(b + 1, 1 - slot)
            xb = xbuf[slot].astype(BF16)
            a = jnp.dot(xb, w1b[...], preferred_element_type=F32)
            g = jnp.dot(xb, w3b[...], preferred_element_type=F32)
            o_ref[rows, :] = jnp.dot((_silu(a) * g).astype(BF16), w2b[...], preferred_element_type=F32)

        @pl.when(b >= nu)
        def _():
            o_ref[rows, :] = jnp.zeros((ROW_BLOCK, o_ref.shape[1]), o_ref.dtype)

    row_group(2 * pl.program_id(0), 0)
    row_group(2 * pl.program_id(0) + 1, 1)


def _experts(blk_exp, n_used, ends, src, h2, w1, w3, w2, n_blk):
    t, d = h2.shape
    assert n_blk % 2 == 0 and src.shape[0] == (n_blk + 1) * ROW_BLOCK
    hbm = pl.BlockSpec(memory_space=pl.ANY)
    return pl.pallas_call(
        _expert_kernel,
        grid_spec=pltpu.PrefetchScalarGridSpec(
            num_scalar_prefetch=4,
            grid=(n_blk // 2,),
            in_specs=[hbm, hbm, hbm, hbm],
            out_specs=pl.BlockSpec((2 * ROW_BLOCK, d), lambda g, *_: (g, 0)),
            scratch_shapes=[pltpu.VMEM((2, ROW_BLOCK, d), h2.dtype),
                            pltpu.VMEM((d, D_FF), w1.dtype), pltpu.VMEM((d, D_FF), w3.dtype),
                            pltpu.VMEM((D_FF, d), w2.dtype),
                            pltpu.VMEM((d, D_FF), BF16), pltpu.VMEM((d, D_FF), BF16),
                            pltpu.VMEM((D_FF, d), BF16),
                            pltpu.SemaphoreType.DMA((3,)), pltpu.SemaphoreType.DMA((2,))],
        ),
        out_shape=jax.ShapeDtypeStruct((n_blk * ROW_BLOCK, d), F32),
        compiler_params=_cparams(("arbitrary",), 56),
    )(blk_exp, n_used, ends, src, h2, w1, w3, w2)


def _combine_kernel(pos_ref, x2_ref, gt_ref, nw_ref, yb_ref, o_ref, buf, sem, *, final):
    tm = x2_ref.shape[0]
    n = pl.num_programs(0)
    t = n * tm
    i = pl.program_id(0)

    def row_copy(step, slot, r, kk):
        p = pos_ref[kk * t + step * tm + r]
        return pltpu.make_async_copy(yb_ref.at[pl.ds(p, 1), :], buf.at[slot, kk, pl.ds(r, 1), :], sem.at[slot])

    def issue(step, slot):
        def body(r, c):
            row_copy(step, slot, r, 0).start()
            row_copy(step, slot, r, 1).start()
            return c
        lax.fori_loop(0, tm, body, 0, unroll=8)

    @pl.when(i == 0)
    def _():
        issue(0, 0)

    @pl.when(i + 1 < n)
    def _():
        issue(i + 1, (i + 1) % 2)

    slot = i % 2

    def wbody(r, c):
        row_copy(i, slot, r, 0).wait()
        row_copy(i, slot, r, 1).wait()
        return c
    lax.fori_loop(0, tm, wbody, 0, unroll=8)

    gt = gt_ref[...]
    y = x2_ref[...] + gt[:, 0:1] * buf[slot, 0] + gt[:, 1:2] * buf[slot, 1]
    if final:
        y = y * lax.rsqrt(jnp.mean(y * y, axis=-1, keepdims=True) + EPS) * nw_ref[...]
    o_ref[...] = y


def _combine(pos_flat, x2, gates_t, norm_final, yb, tm, final):
    t, d = x2.shape
    return pl.pallas_call(
        functools.partial(_combine_kernel, final=final),
        grid_spec=pltpu.PrefetchScalarGridSpec(
            num_scalar_prefetch=1,
            grid=(t // tm,),
            in_specs=[pl.BlockSpec((tm, d), lambda i, pos: (i, 0)),
                      pl.BlockSpec((tm, LANE), lambda i, pos: (i, 0)),
                      pl.BlockSpec((1, d), lambda i, pos: (0, 0)),
                      pl.BlockSpec(memory_space=pl.ANY)],
            out_specs=pl.BlockSpec((tm, d), lambda i, pos: (i, 0)),
            scratch_shapes=[pltpu.VMEM((2, 2, tm, d), F32), pltpu.SemaphoreType.DMA((2,))],
        ),
        out_shape=jax.ShapeDtypeStruct((t, d), F32),
        compiler_params=_cparams(("arbitrary",), 32),
    )(pos_flat, x2, gates_t, norm_final.reshape(1, d).astype(F32), yb)


def _layer(x, norm_mix, w_in, conv_a, a_log, dt_bias, a_norm, w_gk2, b_gk, b_norm,
           w_oa, w_ob, w_out, norm_ffn, w_rg, b_rg, w_re, b_re, w1, w3, w2, norm_final, final):
    t, d = x.shape
    tm = min(512, t)

    a_end = 4 * A_QK
    ga_end = a_end + 2 * A_HEADS
    b_end = ga_end + 2 * B_QK + 2 * B_V
    lr_end = b_end + B_GATE_RANK
    assert a_end % LANE == 0 and b_end % LANE == 2 * A_HEADS

    h = _rmsnorm(x, norm_mix, tm)
    tm_big = min(1024, t)
    wt = w_in.T
    z_a = _proj(h, wt, 0, a_end, tm_big, 1024)
    z_b = _proj(h, wt, ga_end, b_end - ga_end, tm_big, 1024)
    z_mix = _proj(h, wt, lr_end, 2 * d, tm_big, 1024)
    z_ga = _proj(h, wt, a_end, LANE, tm_big, LANE)
    z_lr = _proj(h, wt, b_end - b_end % LANE, LANE, tm_big, LANE)

    oa_g = _mixer_a(z_a, z_ga, conv_a.astype(F32), a_log, dt_bias, a_norm)
    ob_g = _mixer_b(z_b, z_lr, w_gk2, b_gk, b_norm)

    wr = jnp.zeros((d, LANE), F32).at[:, 0:N_GROUPS].set(w_rg.astype(F32))
    wr = wr.at[:, 8:8 + N_EXPERTS].set(w_re.reshape(d, N_EXPERTS).astype(F32))
    wr_hi = wr.astype(BF16)
    wr = jnp.stack([wr_hi, (wr - wr_hi.astype(F32)).astype(BF16)])
    br_t = jnp.zeros((LANE, 1), F32).at[0:N_GROUPS, 0].set(b_rg.astype(F32))
    br_t = br_t.at[8:8 + N_EXPERTS, 0].set(b_re.reshape(N_EXPERTS).astype(F32))
    x2, h2, lt = _merge(oa_g, ob_g, z_mix, x, w_oa.astype(BF16), w_ob.astype(BF16), w_out.astype(BF16),
                        norm_ffn, wr, br_t, min(256, t))

    n_blk = (2 * t + ROW_BLOCK - 1) // ROW_BLOCK + N_EXPERTS + 1
    n_blk += n_blk % 2
    pos, gates, blk = _route(lt, n_blk)
    pos_flat = pos[0:2].reshape(2 * t)
    blk_exp, n_used, ends = blk[0, :n_blk], blk[1, 0:1], blk[2, :N_EXPERTS]
    tok = jnp.tile(jnp.arange(t, dtype=jnp.int32), 2)
    src = jnp.zeros(((n_blk + 1) * ROW_BLOCK,), jnp.int32).at[pos_flat].set(tok, unique_indices=True)
    yb = _experts(blk_exp, n_used, ends, src, h2, w1, w3, w2, n_blk)
    return _combine(pos_flat, x2, gates, norm_final, yb, min(128, t), final)


def kernel(x, norm_mix, w_in, conv_a, a_log, dt_bias, a_norm, w_gk2, b_gk, b_norm, w_oa, w_ob, w_out,
           norm_ffn, w_rg, b_rg, w_re, b_re, w1, w3, w2, norm_final):
    bsz, seq, d = x.shape
    assert bsz == 1, "one sequence per call"
    depth = norm_mix.shape[0]
    y = x.reshape(seq, d)
    for l in range(depth):
        y = _layer(y, norm_mix[l], w_in[l], conv_a[l], a_log[l], dt_bias[l], a_norm[l], w_gk2[l], b_gk[l],
                   b_norm[l], w_oa[l], w_ob[l], w_out[l], norm_ffn[l], w_rg[l], b_rg[l], w_re[l], b_re[l],
                   w1[l], w3[l], w2[l], norm_final, l == depth - 1)
    return y.reshape(bsz, seq, d)
```

```python
import functools

import jax
import jax.numpy as jnp
from jax import lax
from jax.experimental import pallas as pl
from jax.experimental.pallas import tpu as pltpu

D_MODEL = 2048
CHUNK = 64
EPS = 1e-6
A_HEADS, A_DK, A_DV, A_CONV = 8, 128, 128, 4
A_QK, A_V = A_HEADS * A_DK, A_HEADS * A_DV
B_HEADS, B_DK, B_DV, B_GATE_RANK, B_GATE_NORM = 4, 128, 256, 16, 16.0
B_QK, B_V = B_HEADS * B_DK, B_HEADS * B_DV
N_GROUPS, EXP_PER_GROUP, D_FF = 4, 8, 512
N_EXPERTS = N_GROUPS * EXP_PER_GROUP
ROW_BLOCK = 256
LANE = 128
SUB = 16
TB = 256
A_HG = 8
NEG = -1e30

F32 = jnp.float32
BF16 = jnp.bfloat16
HI = lax.Precision.HIGHEST


def _cparams(sem, vmem_mib):
    return pltpu.CompilerParams(dimension_semantics=sem, vmem_limit_bytes=vmem_mib * 2 ** 20)


def _bdot(a, b):
    return jnp.dot(a.astype(BF16), b.astype(BF16), preferred_element_type=F32)


def _bdot_nt(a, b):
    return lax.dot_general(a.astype(BF16), b.astype(BF16), (((1,), (1,)), ((), ())),
                           preferred_element_type=F32)


def _bdot_tn(a, b):
    return lax.dot_general(a.astype(BF16), b.astype(BF16), (((0,), (0,)), ((), ())),
                           preferred_element_type=F32)


def _fdot(a, b):
    return jnp.dot(a, b, preferred_element_type=F32, precision=HI)


def _sigmoid(x):
    return 1.0 / (1.0 + jnp.exp(-x))


def _silu(x):
    return x * _sigmoid(x)


def _softplus(x):
    return jnp.maximum(x, 0.0) + jnp.log(1.0 + jnp.exp(-jnp.abs(x)))


def _rmsnorm_kernel(x_ref, w_ref, o_ref):
    x = x_ref[...]
    ms = jnp.mean(x * x, axis=-1, keepdims=True)
    o_ref[...] = (x * lax.rsqrt(ms + EPS) * w_ref[...]).astype(o_ref.dtype)


def _rmsnorm(x, w, tm):
    t, d = x.shape
    return pl.pallas_call(
        _rmsnorm_kernel,
        grid=(t // tm,),
        in_specs=[pl.BlockSpec((tm, d), lambda i: (i, 0)), pl.BlockSpec((1, d), lambda i: (0, 0))],
        out_specs=pl.BlockSpec((tm, d), lambda i: (i, 0)),
        out_shape=jax.ShapeDtypeStruct((t, d), BF16),
        compiler_params=_cparams(("parallel",), 32),
    )(x, w.reshape(1, d))


PREP_ROWS = 256


def _proj_kernel(a_ref, wt_ref, o_ref, wb_ref):
    tn = wt_ref.shape[0]

    @pl.when(pl.program_id(1) == 0)
    def _():
        for r in range(0, tn, PREP_ROWS):
            wb_ref[r:r + PREP_ROWS, :] = wt_ref[r:r + PREP_ROWS, :].astype(BF16)

    o_ref[...] = lax.dot_general(a_ref[...], wb_ref[...], (((1,), (1,)), ((), ())), preferred_element_type=F32)


def _proj(a, wt, col0, n, tm, tn):
    m, k = a.shape
    assert n % tn == 0 and col0 % 8 == 0
    return pl.pallas_call(
        _proj_kernel,
        grid=(n // tn, m // tm),
        in_specs=[pl.BlockSpec((tm, k), lambda j, i: (i, 0)),
                  pl.BlockSpec((pl.Element(tn), pl.Element(k)), lambda j, i: (pl.multiple_of(col0 + j * tn, 8), 0))],
        out_specs=pl.BlockSpec((tm, tn), lambda j, i: (i, j)),
        out_shape=jax.ShapeDtypeStruct((m, n), F32),
        scratch_shapes=[pltpu.VMEM((tn, k), BF16)],
        compiler_params=_cparams(("arbitrary", "arbitrary"), 48),
    )(a, wt)


def _chunk_masks(tb):
    row = lax.broadcasted_iota(jnp.int32, (tb, tb), 0)
    col = lax.broadcasted_iota(jnp.int32, (tb, tb), 1)
    same = (row // CHUNK) == (col // CHUNK)
    return row, col, same, same & (col <= row), same & (col < row)


def _lane_pick(x, idx):
    lane = lax.broadcasted_iota(jnp.int32, x.shape, 1)
    return jnp.sum(jnp.where(lane == idx, x, 0.0), axis=-1, keepdims=True)


def _causal_conv_silu(x_ref, w_ref, halo_ref, cbuf_ref, idx, tb):
    cbuf_ref[idx, 0:8, :] = halo_ref[idx]
    cbuf_ref[idx, 8:8 + tb, :] = x_ref[...]
    halo_ref[idx] = x_ref[tb - 8:tb, :]
    w = w_ref[...]
    acc = w[A_CONV - 1:A_CONV, :] * x_ref[...]
    for j in range(A_CONV - 1):
        off = 8 - (A_CONV - 1) + j
        acc = acc + w[j:j + 1, :] * cbuf_ref[idx, off:off + tb, :]
    return _silu(acc)


def _gates_a_kernel(tail_ref, alog_ref, dtb_ref, ga_ref, gct_ref):
    tb = tail_ref.shape[0]
    gl = tail_ref[...]
    beta = _sigmoid(gl)
    g = -jnp.exp(alog_ref[...]) * _softplus(gl + dtb_ref[...])
    _, _, same, causal, _ = _chunk_masks(tb)
    gc = _fdot(jnp.where(causal, 1.0, 0.0), g)
    glast = _fdot(jnp.where(same, 1.0, 0.0), g)
    lane = lax.broadcasted_iota(jnp.int32, (tb, LANE), 1)
    ga_ref[...] = jnp.where(lane < A_HEADS, beta,
                            jnp.where(lane < 2 * A_HEADS, gc, pltpu.roll(glast, A_HEADS, 1)))
    gct_ref[...] = gc.T[A_HEADS:2 * A_HEADS, :]


def _gates_a(z_tail, a_log, dt_bias):
    t = z_tail.shape[0]
    pad = lambda p: jnp.pad(p.astype(F32), (A_HEADS, LANE - 2 * A_HEADS)).reshape(1, LANE)
    const = pl.BlockSpec((1, LANE), lambda i: (0, 0))
    return pl.pallas_call(
        _gates_a_kernel,
        grid=(t // TB,),
        in_specs=[pl.BlockSpec((TB, LANE), lambda i: (i, 0)), const, const],
        out_specs=[pl.BlockSpec((TB, LANE), lambda i: (i, 0)), pl.BlockSpec((A_HEADS, TB), lambda i: (0, i))],
        out_shape=[jax.ShapeDtypeStruct((t, LANE), F32), jax.ShapeDtypeStruct((A_HEADS, t), F32)],
        compiler_params=_cparams(("parallel",), 32),
    )(z_tail, pad(a_log), pad(dt_bias))


def _delta_heads(qs, ks, vs, ga, gc_rows, hs, sts, tb):
    n = len(qs)
    rng = range(n)
    _, _, _, causal, strict = _chunk_masks(tb)
    qs = [q * lax.rsqrt(jnp.sum(q * q, axis=-1, keepdims=True) + EPS) * (A_DK ** -0.5) for q in qs]
    ks = [k * lax.rsqrt(jnp.sum(k * k, axis=-1, keepdims=True) + EPS) for k in ks]
    beta = [_lane_pick(ga, h) for h in hs]
    gc = [_lane_pick(ga, h + A_HEADS) for h in hs]
    glast = [_lane_pick(ga, h + 2 * A_HEADS) for h in hs]
    decay = [jnp.exp(jnp.where(causal, gc[i] - gc_rows[i], NEG)) for i in rng]
    kb = [ks[i] * beta[i] for i in rng]

    n_pow = [jnp.where(strict, _bdot_nt(kb[i], ks[i]) * decay[i], 0.0) * -1.0 for i in rng]
    t_mat = list(n_pow)
    lvl = 2
    while lvl < CHUNK:
        n_pow = [_bdot(m, m) for m in n_pow]
        t_mat = [t_mat[i] + n_pow[i] + _bdot(t_mat[i], n_pow[i]) for i in rng]
        lvl *= 2

    egc = [jnp.exp(g) for g in gc]
    rhs = [jnp.concatenate([vs[i] * beta[i], kb[i] * egc[i]], axis=1) for i in rng]
    uw = [rhs[i] + _bdot(t_mat[i], rhs[i]) for i in rng]
    qk = [_bdot_nt(qs[i], ks[i]) * decay[i] for i in rng]
    qkuw = [_bdot(qk[i], uw[i]) for i in rng]
    o_local = [x[:, :A_DV] for x in qkuw]
    q_eff = [qs[i] * egc[i] - qkuw[i][:, A_DV:] for i in rng]
    k_dec = [ks[i] * jnp.exp(glast[i] - gc[i]) for i in rng]
    eg_last = [jnp.exp(g) for g in glast]

    sts = list(sts)
    outs = [[] for _ in rng]
    for c in range(tb // CHUNK):
        lo, hi = c * CHUNK, (c + 1) * CHUNK
        bg = [_bdot_tn(uw[i][lo:hi], k_dec[i][lo:hi]) for i in rng]
        for i in rng:
            outs[i].append(o_local[i][lo:hi] + _bdot_nt(q_eff[i][lo:hi], sts[i]))
        sts = [sts[i] * eg_last[i][lo:lo + 1, :] + bg[i][:A_DV] - _bdot(sts[i], bg[i][A_DV:]) for i in rng]
    return [jnp.concatenate(o, axis=0) for o in outs], sts


def _mixer_a_kernel(xq_ref, xk_ref, xv_ref, z_ref, ga_ref, gct_ref, wq_ref, wk_ref, wv_ref,
                    anorm_ref, o_ref, halo_ref, cbuf_ref, state_ref):
    tb = xq_ref.shape[0]
    hg = xq_ref.shape[1] // A_DK

    @pl.when(pl.program_id(1) == 0)
    def _():
        halo_ref[...] = jnp.zeros_like(halo_ref)
        state_ref[...] = jnp.zeros_like(state_ref)

    q = _causal_conv_silu(xq_ref, wq_ref, halo_ref, cbuf_ref, 0, tb)
    k = _causal_conv_silu(xk_ref, wk_ref, halo_ref, cbuf_ref, 1, tb)
    v = _causal_conv_silu(xv_ref, wv_ref, halo_ref, cbuf_ref, 2, tb)
    ga = ga_ref[...]
    hs = [pl.program_id(0) * hg + j for j in range(hg)]
    sls = [slice(j * A_DK, (j + 1) * A_DK) for j in range(hg)]
    outs, sts = _delta_heads([q[:, s] for s in sls], [k[:, s] for s in sls], [v[:, s] for s in sls], ga,
                             [gct_ref[pl.ds(h, 1), :] for h in hs], hs, [state_ref[j] for j in range(hg)], tb)
    for j in range(hg):
        state_ref[j] = sts[j]
        o = outs[j]
        o = o * lax.rsqrt(jnp.mean(o * o, axis=-1, keepdims=True) + EPS) * anorm_ref[...]
        o_ref[:, sls[j]] = (o * _silu(z_ref[:, sls[j]])).astype(o_ref.dtype)


def _mixer_a(z_a, z_tail, conv_a, a_log, dt_bias, a_norm):
    t = z_a.shape[0]
    ng = A_HEADS // A_HG
    wid = A_HG * A_DK
    ga, gct = _gates_a(z_tail, a_log, dt_bias)
    blk = lambda off: pl.BlockSpec((TB, wid), lambda g, i: (i, off + g))
    cblk = lambda off: pl.BlockSpec((A_CONV, wid), lambda g, i: (0, off + g))
    return pl.pallas_call(
        _mixer_a_kernel,
        grid=(ng, t // TB),
        in_specs=[blk(0), blk(ng), blk(2 * ng), blk(3 * ng),
                  pl.BlockSpec((TB, LANE), lambda g, i: (i, 0)),
                  pl.BlockSpec((A_HEADS, TB), lambda g, i: (0, i)),
                  cblk(0), cblk(ng), cblk(2 * ng), pl.BlockSpec((1, LANE), lambda g, i: (0, 0))],
        out_specs=pl.BlockSpec((TB, wid), lambda g, i: (i, g)),
        out_shape=jax.ShapeDtypeStruct((t, A_V), BF16),
        scratch_shapes=[pltpu.VMEM((3, 8, wid), F32), pltpu.VMEM((3, 8 + TB, wid), F32),
                        pltpu.VMEM((A_HG, A_DV, A_DK), F32)],
        compiler_params=_cparams(("parallel", "arbitrary"), 48),
    )(z_a, z_a, z_a, z_a, ga, gct, conv_a, conv_a, conv_a,
      a_norm.reshape(1, A_DV).astype(F32))


def _cumsum_rows(mask, x):
    hi = x.astype(BF16)
    r1 = x - hi.astype(F32)
    mid = r1.astype(BF16)
    lo = (r1 - mid.astype(F32)).astype(BF16)
    dot = lambda p: jnp.dot(mask, p, preferred_element_type=F32)
    return dot(hi) + dot(mid) + dot(lo)


def _gla_chunk_scores(qc, kc, bc):
    rid = lax.broadcasted_iota(jnp.int32, (SUB, B_DK), 0)
    rid_lo = lax.broadcasted_iota(jnp.int32, (SUB // 2, B_DK), 0) + SUB // 2
    lane_c = lax.broadcasted_iota(jnp.int32, (SUB, CHUNK), 1)
    crow = lax.broadcasted_iota(jnp.int32, (CHUNK, B_DK), 0)
    half = SUB // 2
    rows = []
    for si in range(CHUNK // SUB):
        r0 = si * SUB
        qb, bb = qc[r0:r0 + SUB], bc[r0:r0 + SUB]
        ys = []
        for j in range(SUB):
            bj = bc[r0 + j:r0 + j + 1]
            if j < half:
                ys.append(qb * jnp.exp(jnp.where(rid >= j, bb - bj, NEG)))
            else:
                ys.append(jnp.zeros((half, B_DK), F32))
                ys.append(qb[half:] * jnp.exp(jnp.where(rid_lo >= j, bb[half:] - bj, NEG)))
        r = _bdot_nt(jnp.concatenate(ys, axis=0), kc)
        blk = jnp.zeros((SUB, CHUNK), F32)
        for j in range(SUB):
            blk = jnp.where(lane_c == r0 + j, r[j * SUB:(j + 1) * SUB], blk)
        if si > 0:
            bref = bc[r0:r0 + 1]
            qt = qb * jnp.exp(bb - bref)
            kt = kc * jnp.exp(jnp.where(crow < r0, bref - bc, NEG))
            blk = blk + _bdot_nt(qt, kt)
        rows.append(blk)
    return jnp.concatenate(rows, axis=0)


def _mixer_b_kernel(q_ref, k_ref, v_ref, zg_ref, tail_ref, wg_ref, bgk_ref, bnorm_ref, o_ref, state_ref):
    tb = q_ref.shape[0]
    nh = q_ref.shape[1] // B_DK

    @pl.when(pl.program_id(0) == 0)
    def _():
        state_ref[...] = jnp.zeros_like(state_ref)

    x = _fdot(tail_ref[...], wg_ref[...]) + bgk_ref[...]
    gk = -_softplus(-x) * (1.0 / B_GATE_NORM)
    _, _, _, causal, _ = _chunk_masks(tb)
    b_all = _cumsum_rows(jnp.where(causal, 1.0, 0.0).astype(BF16), gk)

    sts = [state_ref[h] for h in range(nh)]
    outs = [[] for _ in range(nh)]
    for c in range(tb // CHUNK):
        lo, hi = c * CHUNK, (c + 1) * CHUNK
        for h in range(nh):
            qc = q_ref[lo:hi, h * B_DK:(h + 1) * B_DK] * (B_DK ** -0.5)
            kc = k_ref[lo:hi, h * B_DK:(h + 1) * B_DK]
            vc = v_ref[lo:hi, h * B_DV:(h + 1) * B_DV]
            bc = b_all[lo:hi, h * B_DK:(h + 1) * B_DK]
            a_c = _gla_chunk_scores(qc, kc, bc)
            bl = bc[CHUNK - 1:CHUNK]
            outs[h].append(_bdot(a_c, vc) + _bdot_nt(qc * jnp.exp(bc), sts[h]))
            sts[h] = sts[h] * jnp.exp(bl) + _bdot_tn(vc, kc * jnp.exp(bl - bc))
    for h in range(nh):
        state_ref[h] = sts[h]
        o = jnp.concatenate(outs[h], axis=0)
        o = o * lax.rsqrt(jnp.mean(o * o, axis=-1, keepdims=True) + EPS) * bnorm_ref[...]
        sl = slice(h * B_DV, (h + 1) * B_DV)
        o_ref[:, sl] = (o * _silu(zg_ref[:, sl])).astype(o_ref.dtype)


def _mixer_b(z_b, z_tail, w_gk2, b_gk, b_norm):
    t = z_b.shape[0]
    wg = jnp.zeros((LANE, B_QK), F32).at[2 * A_HEADS:2 * A_HEADS + B_GATE_RANK].set(w_gk2.astype(F32))
    col = lambda w, off: pl.BlockSpec((TB, w), lambda i: (i, off // w))
    full = lambda r, c: pl.BlockSpec((r, c), lambda i: (0, 0))
    return pl.pallas_call(
        _mixer_b_kernel,
        grid=(t // TB,),
        in_specs=[col(B_QK, 0), col(B_QK, B_QK), col(B_V, 2 * B_QK), col(B_V, 2 * B_QK + B_V), col(LANE, 0),
                  full(LANE, B_QK), full(1, B_QK), full(1, B_DV)],
        out_specs=pl.BlockSpec((TB, B_V), lambda i: (i, 0)),
        out_shape=jax.ShapeDtypeStruct((t, B_V), BF16),
        scratch_shapes=[pltpu.VMEM((B_HEADS, B_DV, B_DK), F32)],
        compiler_params=_cparams(("arbitrary",), 48),
    )(z_b, z_b, z_b, z_b, z_tail, wg, b_gk.reshape(1, B_QK).astype(F32),
      b_norm.reshape(1, B_DV).astype(F32))


def _merge_kernel(oa_ref, ob_ref, ma_ref, mb_ref, x_ref, woa_ref, wob_ref, wout_ref, nf_ref,
                  wr_ref, br_ref, x2_ref, h2_ref, lt_ref):
    ya = jnp.dot(oa_ref[...], woa_ref[...], preferred_element_type=F32)
    yb = jnp.dot(ob_ref[...], wob_ref[...], preferred_element_type=F32)
    m = _sigmoid(ma_ref[...]) * ya + _sigmoid(mb_ref[...]) * yb
    x2 = x_ref[...] + jnp.dot(m.astype(BF16), wout_ref[...], preferred_element_type=F32)
    x2_ref[...] = x2
    h2 = x2 * lax.rsqrt(jnp.mean(x2 * x2, axis=-1, keepdims=True) + EPS) * nf_ref[...]
    h2_ref[...] = h2
    h_hi = h2.astype(BF16)
    h_mid = (h2 - h_hi.astype(F32)).astype(BF16)
    dot = lambda a, b: jnp.dot(a, b, preferred_element_type=F32)
    logits = dot(h_hi, wr_ref[0]) + (dot(h_mid, wr_ref[0]) + dot(h_hi, wr_ref[1]))
    lt_ref[...] = logits.T + br_ref[...]


def _merge(oa_g, ob_g, z_mix, x, w_oa, w_ob, w_out, norm_ffn, wr, br_t, tm):
    t, d = x.shape
    row = lambda w, c: pl.BlockSpec((tm, w), lambda i: (i, c))
    full = lambda a: pl.BlockSpec(a.shape, lambda i: (0,) * a.ndim, pipeline_mode=pl.Buffered(1))
    return pl.pallas_call(
        _merge_kernel,
        grid=(t // tm,),
        in_specs=[row(A_V, 0), row(B_V, 0), row(d, 0), row(d, 1), row(d, 0),
                  full(w_oa), full(w_ob), full(w_out), pl.BlockSpec((1, d), lambda i: (0, 0)),
                  full(wr), full(br_t)],
        out_specs=[row(d, 0), row(d, 0), pl.BlockSpec((LANE, tm), lambda i: (0, i))],
        out_shape=[jax.ShapeDtypeStruct((t, d), F32), jax.ShapeDtypeStruct((t, d), F32),
                   jax.ShapeDtypeStruct((LANE, t), F32)],
        compiler_params=_cparams(("parallel",), 56),
    )(oa_g, ob_g, z_mix, z_mix, x, w_oa, w_ob, w_out, norm_ffn.reshape(1, d).astype(F32), wr, br_t)


SEG = 256


def _route_kernel(lt_ref, pos_ref, gate_ref, blk_ref, oh_ref):
    t = lt_ref.shape[1]
    rid8 = lax.broadcasted_iota(jnp.int32, (8, t), 0)
    lg = jnp.where(rid8 < N_GROUPS, lt_ref[0:8, :], -jnp.inf)
    gmax = jnp.max(lg, axis=0, keepdims=True)
    g_idx = jnp.min(jnp.where(lg == gmax, rid8, 8), axis=0, keepdims=True)
    p_top = 1.0 / jnp.sum(jnp.exp(lg - gmax), axis=0, keepdims=True)

    les = jnp.zeros((EXP_PER_GROUP, t), F32)
    for g in range(N_GROUPS):
        les = jnp.where(g_idx == g, lt_ref[8 + g * EXP_PER_GROUP:8 + (g + 1) * EXP_PER_GROUP, :], les)
    m1 = jnp.max(les, axis=0, keepdims=True)
    i1 = jnp.min(jnp.where(les == m1, rid8, 8), axis=0, keepdims=True)
    les2 = jnp.where(rid8 == i1, -jnp.inf, les)
    m2 = jnp.max(les2, axis=0, keepdims=True)
    i2 = jnp.min(jnp.where(les2 == m2, rid8, 8), axis=0, keepdims=True)
    r = jnp.exp(m2 - m1)
    gate_ref[...] = jnp.concatenate([p_top / (1.0 + r), p_top * r / (1.0 + r),
                                     jnp.zeros((LANE - 2, t), F32)], axis=0).T
    e1 = g_idx * EXP_PER_GROUP + i1
    e2 = g_idx * EXP_PER_GROUP + i2

    rid32 = lax.broadcasted_iota(jnp.int32, (N_EXPERTS, t), 0)
    oh_ref[0] = jnp.where(rid32 == e1, 1.0, 0.0)
    oh_ref[1] = jnp.where(rid32 == e2, 1.0, 0.0)

    ui = lax.broadcasted_iota(jnp.int32, (SEG, SEG), 0)
    uj = lax.broadcasted_iota(jnp.int32, (SEG, SEG), 1)
    upper = jnp.where(ui < uj, 1.0, 0.0).astype(BF16)
    carry = jnp.zeros((N_EXPERTS, 1), F32)
    ranks = []
    for kk in range(2):
        segs = []
        for sg in range(t // SEG):
            oh = oh_ref[kk, :, sg * SEG:(sg + 1) * SEG]
            pre = jnp.dot(oh.astype(BF16), upper, preferred_element_type=F32) + carry
            segs.append(jnp.sum(oh * pre, axis=0, keepdims=True))
            carry = carry + jnp.sum(oh, axis=1, keepdims=True)
        ranks.append(jnp.concatenate(segs, axis=1))
    counts = carry
    nblk = jnp.floor((counts + (ROW_BLOCK - 1)) * (1.0 / ROW_BLOCK))
    li = lax.broadcasted_iota(jnp.int32, (N_EXPERTS, N_EXPERTS), 0)
    lj = lax.broadcasted_iota(jnp.int32, (N_EXPERTS, N_EXPERTS), 1)
    nb_b = jnp.broadcast_to(nblk, (N_EXPERTS, LANE))
    start_blk = _fdot(jnp.where(lj < li, 1.0, 0.0), nb_b)[:, 0:1]
    end_blk = start_blk + nblk
    start_row = start_blk * ROW_BLOCK

    pos_ref[...] = jnp.zeros_like(pos_ref)
    for kk in range(2):
        base = jnp.sum(oh_ref[kk] * start_row, axis=0, keepdims=True)
        pos_ref[kk:kk + 1, :] = (base + ranks[kk]).astype(jnp.int32)

    nb = blk_ref.shape[1]
    bid = lax.broadcasted_iota(jnp.int32, (N_EXPERTS, nb), 1).astype(F32)
    be = jnp.sum(jnp.where(end_blk <= bid, 1.0, 0.0), axis=0, keepdims=True)
    blk_ref[...] = jnp.zeros_like(blk_ref)
    blk_ref[0:1, :] = jnp.minimum(be, N_EXPERTS - 1.0).astype(jnp.int32)
    blk_ref[1:2, :] = jnp.broadcast_to(end_blk[N_EXPERTS - 1:N_EXPERTS, :], (1, nb)).astype(jnp.int32)
    ends = jnp.sum(jnp.where(li == lj, end_blk, 0.0), axis=0, keepdims=True)
    blk_ref[2:3, 0:N_EXPERTS] = ends.astype(jnp.int32)


def _route(lt, n_blk):
    t = lt.shape[1]
    nb = -(-n_blk // LANE) * LANE
    return pl.pallas_call(
        _route_kernel,
        out_shape=[jax.ShapeDtypeStruct((8, t), jnp.int32), jax.ShapeDtypeStruct((t, LANE), F32),
                   jax.ShapeDtypeStruct((8, nb), jnp.int32)],
        scratch_shapes=[pltpu.VMEM((2, N_EXPERTS, t), F32)],
        compiler_params=pltpu.CompilerParams(vmem_limit_bytes=48 * 2 ** 20),
    )(lt)


def _scatter_kernel(pos_ref, ends_ref, nu_ref, h_ref, xs_ref, zbuf, sem, zsem):
    tm = h_ref.shape[0]
    t = pl.num_programs(0) * tm
    base = pl.program_id(0) * tm
    n_blk = xs_ref.shape[0] // ROW_BLOCK

    def zero_block(blk):
        return pltpu.make_async_copy(zbuf, xs_ref.at[pl.ds(blk * ROW_BLOCK, ROW_BLOCK), :], zsem)

    def for_each_zero_block(fn):
        for e in range(N_EXPERTS):
            first = ends_ref[e - 1] if e else 0

            @pl.when(ends_ref[e] > first)
            def _():
                fn(zero_block(ends_ref[e] - 1))

        def tail(blk, c):
            fn(zero_block(blk))
            return c
        lax.fori_loop(nu_ref[0], n_blk, tail, 0)

    @pl.when(pl.program_id(0) == 0)
    def _():
        zbuf[...] = jnp.zeros_like(zbuf)
        for_each_zero_block(lambda c: c.start())
        for_each_zero_block(lambda c: c.wait())

    def row_copy(r, p):
        return pltpu.make_async_copy(h_ref.at[pl.ds(r, 1), :], xs_ref.at[pl.ds(p, 1), :], sem)

    for r in range(tm):
        for kk in range(2):
            row_copy(r, pos_ref[kk * t + base + r]).start()
    for r in range(2 * tm):
        row_copy(0, 0).wait()


def _scatter_rows(pos_flat, ends, n_used, h2, n_rows, tm):
    t, d = h2.shape
    return pl.pallas_call(
        _scatter_kernel,
        grid_spec=pltpu.PrefetchScalarGridSpec(
            num_scalar_prefetch=3,
            grid=(t // tm,),
            in_specs=[pl.BlockSpec((tm, d), lambda i, *_: (i, 0))],
            out_specs=pl.BlockSpec(memory_space=pl.ANY),
            scratch_shapes=[pltpu.VMEM((ROW_BLOCK, d), h2.dtype), pltpu.SemaphoreType.DMA(()),
                            pltpu.SemaphoreType.DMA(())],
        ),
        out_shape=jax.ShapeDtypeStruct((n_rows, d), h2.dtype),
        compiler_params=_cparams(("arbitrary",), 32),
    )(pos_flat, ends, n_used, h2)


def _expert_kernel(be_ref, nu_ref, ends_ref, x_ref, w1_hbm, w3_hbm, w2_hbm, o_ref,
                   w1s, w3s, w2s, w1b, w3b, w2b, sem):
    b = pl.program_id(0)
    e = be_ref[b]
    prev = be_ref[jnp.maximum(b - 1, 0)]
    changed = jnp.logical_or(b == 0, e != prev)

    def weight_copies(ex):
        return (pltpu.make_async_copy(w1_hbm.at[ex], w1s, sem.at[0]),
                pltpu.make_async_copy(w3_hbm.at[ex], w3s, sem.at[1]),
                pltpu.make_async_copy(w2_hbm.at[ex], w2s, sem.at[2]))

    @pl.when(b == 0)
    def _():
        for c in weight_copies(e):
            c.start()

    @pl.when(jnp.logical_and(changed, b < nu_ref[0]))
    def _():
        for c in weight_copies(e):
            c.wait()
        w1b[...] = w1s[...].astype(BF16)
        w3b[...] = w3s[...].astype(BF16)
        w2b[...] = w2s[...].astype(BF16)
        nxt = ends_ref[e]

        @pl.when(nxt < nu_ref[0])
        def _():
            for c in weight_copies(be_ref[nxt]):
                c.start()

    @pl.when(b < nu_ref[0])
    def _():
        xb = x_ref[...].astype(BF16)
        a = jnp.dot(xb, w1b[...], preferred_element_type=F32)
        g = jnp.dot(xb, w3b[...], preferred_element_type=F32)
        o_ref[...] = jnp.dot((_silu(a) * g).astype(BF16), w2b[...], preferred_element_type=F32)

    @pl.when(b >= nu_ref[0])
    def _():
        o_ref[...] = jnp.zeros_like(o_ref)


def _experts(blk_exp, n_used, ends, xs, w1, w3, w2):
    n_rows, d = xs.shape
    n_blk = n_rows // ROW_BLOCK
    rows = lambda b, be, nu, en: (jnp.minimum(b, jnp.maximum(nu[0] - 1, 0)), 0)
    hbm = pl.BlockSpec(memory_space=pl.ANY)
    return pl.pallas_call(
        _expert_kernel,
        grid_spec=pltpu.PrefetchScalarGridSpec(
            num_scalar_prefetch=3,
            grid=(n_blk,),
            in_specs=[pl.BlockSpec((ROW_BLOCK, d), rows), hbm, hbm, hbm],
            out_specs=pl.BlockSpec((ROW_BLOCK, d), lambda b, be, nu, en: (b, 0)),
            scratch_shapes=[pltpu.VMEM((d, D_FF), w1.dtype), pltpu.VMEM((d, D_FF), w3.dtype),
                            pltpu.VMEM((D_FF, d), w2.dtype),
                            pltpu.VMEM((d, D_FF), BF16), pltpu.VMEM((d, D_FF), BF16),
                            pltpu.VMEM((D_FF, d), BF16), pltpu.SemaphoreType.DMA((3,))],
        ),
        out_shape=jax.ShapeDtypeStruct((n_rows, d), F32),
        compiler_params=_cparams(("arbitrary",), 56),
    )(blk_exp, n_used, ends, xs, w1, w3, w2)


def _combine_kernel(pos_ref, x2_ref, gt_ref, nw_ref, yb_ref, o_ref, buf, sem, *, final):
    tm = x2_ref.shape[0]
    n = pl.num_programs(0)
    t = n * tm
    i = pl.program_id(0)

    def row_copy(slot, r, kk, p):
        return pltpu.make_async_copy(yb_ref.at[pl.ds(p, 1), :], buf.at[slot, kk, pl.ds(r, 1), :], sem.at[slot])

    def issue(step, slot):
        for r in range(tm):
            for kk in range(2):
                row_copy(slot, r, kk, pos_ref[kk * t + step * tm + r]).start()

    @pl.when(i == 0)
    def _():
        issue(0, 0)

    @pl.when(i + 1 < n)
    def _():
        issue(i + 1, (i + 1) % 2)

    slot = i % 2
    for r in range(2 * tm):
        row_copy(slot, 0, 0, 0).wait()

    gt = gt_ref[...]
    y = x2_ref[...] + gt[:, 0:1] * buf[slot, 0] + gt[:, 1:2] * buf[slot, 1]
    if final:
        y = y * lax.rsqrt(jnp.mean(y * y, axis=-1, keepdims=True) + EPS) * nw_ref[...]
    o_ref[...] = y


def _combine(pos_flat, x2, gates_t, norm_final, yb, tm, final):
    t, d = x2.shape
    return pl.pallas_call(
        functools.partial(_combine_kernel, final=final),
        grid_spec=pltpu.PrefetchScalarGridSpec(
            num_scalar_prefetch=1,
            grid=(t // tm,),
            in_specs=[pl.BlockSpec((tm, d), lambda i, pos: (i, 0)),
                      pl.BlockSpec((tm, LANE), lambda i, pos: (i, 0)),
                      pl.BlockSpec((1, d), lambda i, pos: (0, 0)),
                      pl.BlockSpec(memory_space=pl.ANY)],
            out_specs=pl.BlockSpec((tm, d), lambda i, pos: (i, 0)),
            scratch_shapes=[pltpu.VMEM((2, 2, tm, d), F32), pltpu.SemaphoreType.DMA((2,))],
        ),
        out_shape=jax.ShapeDtypeStruct((t, d), F32),
        compiler_params=_cparams(("arbitrary",), 32),
    )(pos_flat, x2, gates_t, norm_final.reshape(1, d).astype(F32), yb)


def _layer(x, norm_mix, w_in, conv_a, a_log, dt_bias, a_norm, w_gk2, b_gk, b_norm,
           w_oa, w_ob, w_out, norm_ffn, w_rg, b_rg, w_re, b_re, w1, w3, w2, norm_final, final):
    t, d = x.shape
    tm = min(512, t)

    a_end = 4 * A_QK
    ga_end = a_end + 2 * A_HEADS
    b_end = ga_end + 2 * B_QK + 2 * B_V
    lr_end = b_end + B_GATE_RANK
    assert a_end % LANE == 0 and b_end % LANE == 2 * A_HEADS

    h = _rmsnorm(x, norm_mix, tm)
    tm_big = min(1024, t)
    wt = w_in.T
    z_a = _proj(h, wt, 0, a_end, tm_big, 1024)
    z_b = _proj(h, wt, ga_end, b_end - ga_end, tm_big, 1024)
    z_mix = _proj(h, wt, lr_end, 2 * d, tm_big, 1024)
    z_ga = _proj(h, wt, a_end, LANE, tm_big, LANE)
    z_lr = _proj(h, wt, b_end - b_end % LANE, LANE, tm_big, LANE)

    oa_g = _mixer_a(z_a, z_ga, conv_a.astype(F32), a_log, dt_bias, a_norm)
    ob_g = _mixer_b(z_b, z_lr, w_gk2, b_gk, b_norm)

    wr = jnp.zeros((d, LANE), F32).at[:, 0:N_GROUPS].set(w_rg.astype(F32))
    wr = wr.at[:, 8:8 + N_EXPERTS].set(w_re.reshape(d, N_EXPERTS).astype(F32))
    wr_hi = wr.astype(BF16)
    wr = jnp.stack([wr_hi, (wr - wr_hi.astype(F32)).astype(BF16)])
    br_t = jnp.zeros((LANE, 1), F32).at[0:N_GROUPS, 0].set(b_rg.astype(F32))
    br_t = br_t.at[8:8 + N_EXPERTS, 0].set(b_re.reshape(N_EXPERTS).astype(F32))
    x2, h2, lt = _merge(oa_g, ob_g, z_mix, x, w_oa.astype(BF16), w_ob.astype(BF16), w_out.astype(BF16),
                        norm_ffn, wr, br_t, min(256, t))

    n_blk = (2 * t + ROW_BLOCK - 1) // ROW_BLOCK + N_EXPERTS
    n_rows = n_blk * ROW_BLOCK
    pos, gates, blk = _route(lt, n_blk)
    pos_flat = pos[0:2].reshape(2 * t)
    blk_exp, n_used, ends = blk[0, :n_blk], blk[1, 0:1], blk[2, :N_EXPERTS]
    xs = _scatter_rows(pos_flat, ends, n_used, h2, n_rows, min(256, t))
    yb = _experts(blk_exp, n_used, ends, xs, w1, w3, w2)
    return _combine(pos_flat, x2, gates, norm_final, yb, min(128, t), final)


def kernel(x, norm_mix, w_in, conv_a, a_log, dt_bias, a_norm, w_gk2, b_gk, b_norm, w_oa, w_ob, w_out,
           norm_ffn, w_rg, b_rg, w_re, b_re, w1, w3, w2, norm_final):
    bsz, seq, d = x.shape
    assert bsz == 1, "one sequence per call"
    depth = norm_mix.shape[0]
    y = x.reshape(seq, d)
    for l in range(depth):
        y = _layer(y, norm_mix[l], w_in[l], conv_a[l], a_log[l], dt_bias[l], a_norm[l], w_gk2[l], b_gk[l],
                   b_norm[l], w_oa[l], w_ob[l], w_out[l], norm_ffn[l], w_rg[l], b_rg[l], w_re[l], b_re[l],
                   w1[l], w3[l], w2[l], norm_final, l == depth - 1)
    return y.reshape(bsz, seq, d)
```

```python
import functools

import jax
import jax.numpy as jnp
from jax import lax
from jax.experimental import pallas as pl
from jax.experimental.pallas import tpu as pltpu

D_MODEL = 2048
CHUNK = 64
EPS = 1e-6
A_HEADS, A_DK, A_DV, A_CONV = 8, 128, 128, 4
A_QK, A_V = A_HEADS * A_DK, A_HEADS * A_DV
B_HEADS, B_DK, B_DV, B_GATE_RANK, B_GATE_NORM = 4, 128, 256, 16, 16.0
B_QK, B_V = B_HEADS * B_DK, B_HEADS * B_DV
N_GROUPS, EXP_PER_GROUP, D_FF = 4, 8, 512
N_EXPERTS = N_GROUPS * EXP_PER_GROUP
ROW_BLOCK = 256
LANE = 128
SUB = 16
TB = 256
A_HG = 8
TA = 128
NEG = -1e30

F32 = jnp.float32
BF16 = jnp.bfloat16
HI = lax.Precision.HIGHEST


def _cparams(sem, vmem_mib):
    return pltpu.CompilerParams(dimension_semantics=sem, vmem_limit_bytes=vmem_mib * 2 ** 20)


def _bdot(a, b):
    return jnp.dot(a.astype(BF16), b.astype(BF16), preferred_element_type=F32)


def _bdot_nt(a, b):
    return lax.dot_general(a.astype(BF16), b.astype(BF16), (((1,), (1,)), ((), ())),
                           preferred_element_type=F32)


def _bdot_tn(a, b):
    return lax.dot_general(a.astype(BF16), b.astype(BF16), (((0,), (0,)), ((), ())),
                           preferred_element_type=F32)


def _fdot(a, b):
    return jnp.dot(a, b, preferred_element_type=F32, precision=HI)


def _sigmoid(x):
    return 1.0 / (1.0 + jnp.exp(-x))


def _silu(x):
    return x * _sigmoid(x)


def _softplus(x):
    return jnp.maximum(x, 0.0) + jnp.log(1.0 + jnp.exp(-jnp.abs(x)))


def _rmsnorm_kernel(x_ref, w_ref, o_ref):
    x = x_ref[...]
    ms = jnp.mean(x * x, axis=-1, keepdims=True)
    o_ref[...] = (x * lax.rsqrt(ms + EPS) * w_ref[...]).astype(o_ref.dtype)


def _rmsnorm(x, w, tm):
    t, d = x.shape
    return pl.pallas_call(
        _rmsnorm_kernel,
        grid=(t // tm,),
        in_specs=[pl.BlockSpec((tm, d), lambda i: (i, 0)), pl.BlockSpec((1, d), lambda i: (0, 0))],
        out_specs=pl.BlockSpec((tm, d), lambda i: (i, 0)),
        out_shape=jax.ShapeDtypeStruct((t, d), BF16),
        compiler_params=_cparams(("parallel",), 32),
    )(x, w.reshape(1, d))


PREP_ROWS = 256


def _proj_kernel(a_ref, wt_ref, o_ref, wb_ref):
    tn = wt_ref.shape[0]

    @pl.when(pl.program_id(1) == 0)
    def _():
        for r in range(0, tn, PREP_ROWS):
            wb_ref[r:r + PREP_ROWS, :] = wt_ref[r:r + PREP_ROWS, :].astype(BF16)

    o_ref[...] = lax.dot_general(a_ref[...], wb_ref[...], (((1,), (1,)), ((), ())), preferred_element_type=F32)


def _proj(a, wt, col0, n, tm, tn):
    m, k = a.shape
    assert n % tn == 0 and col0 % 8 == 0
    return pl.pallas_call(
        _proj_kernel,
        grid=(n // tn, m // tm),
        in_specs=[pl.BlockSpec((tm, k), lambda j, i: (i, 0)),
                  pl.BlockSpec((pl.Element(tn), pl.Element(k)), lambda j, i: (pl.multiple_of(col0 + j * tn, 8), 0))],
        out_specs=pl.BlockSpec((tm, tn), lambda j, i: (i, j)),
        out_shape=jax.ShapeDtypeStruct((m, n), F32),
        scratch_shapes=[pltpu.VMEM((tn, k), BF16)],
        compiler_params=_cparams(("arbitrary", "arbitrary"), 48),
    )(a, wt)


def _chunk_masks(tb):
    row = lax.broadcasted_iota(jnp.int32, (tb, tb), 0)
    col = lax.broadcasted_iota(jnp.int32, (tb, tb), 1)
    same = (row // CHUNK) == (col // CHUNK)
    return row, col, same, same & (col <= row), same & (col < row)


def _lane_pick(x, idx):
    lane = lax.broadcasted_iota(jnp.int32, x.shape, 1)
    return jnp.sum(jnp.where(lane == idx, x, 0.0), axis=-1, keepdims=True)


def _causal_conv_silu(x_ref, w_ref, halo_ref, cbuf_ref, idx, tb):
    cbuf_ref[idx, 0:8, :] = halo_ref[idx]
    cbuf_ref[idx, 8:8 + tb, :] = x_ref[...]
    halo_ref[idx] = x_ref[tb - 8:tb, :]
    w = w_ref[...]
    acc = w[A_CONV - 1:A_CONV, :] * x_ref[...]
    for j in range(A_CONV - 1):
        off = 8 - (A_CONV - 1) + j
        acc = acc + w[j:j + 1, :] * cbuf_ref[idx, off:off + tb, :]
    return _silu(acc)


def _gates_a_kernel(tail_ref, alog_ref, dtb_ref, ga_ref, gct_ref):
    tb = tail_ref.shape[0]
    gl = tail_ref[...]
    beta = _sigmoid(gl)
    g = -jnp.exp(alog_ref[...]) * _softplus(gl + dtb_ref[...])
    _, _, same, causal, _ = _chunk_masks(tb)
    gc = _fdot(jnp.where(causal, 1.0, 0.0), g)
    glast = _fdot(jnp.where(same, 1.0, 0.0), g)
    lane = lax.broadcasted_iota(jnp.int32, (tb, LANE), 1)
    ga_ref[...] = jnp.where(lane < A_HEADS, beta,
                            jnp.where(lane < 2 * A_HEADS, gc, pltpu.roll(glast, A_HEADS, 1)))
    gct_ref[...] = gc.T[A_HEADS:2 * A_HEADS, :]


def _gates_a(z_tail, a_log, dt_bias):
    t = z_tail.shape[0]
    pad = lambda p: jnp.pad(p.astype(F32), (A_HEADS, LANE - 2 * A_HEADS)).reshape(1, LANE)
    const = pl.BlockSpec((1, LANE), lambda i: (0, 0))
    return pl.pallas_call(
        _gates_a_kernel,
        grid=(t // TB,),
        in_specs=[pl.BlockSpec((TB, LANE), lambda i: (i, 0)), const, const],
        out_specs=[pl.BlockSpec((TB, LANE), lambda i: (i, 0)), pl.BlockSpec((A_HEADS, TB), lambda i: (0, i))],
        out_shape=[jax.ShapeDtypeStruct((t, LANE), F32), jax.ShapeDtypeStruct((A_HEADS, t), F32)],
        compiler_params=_cparams(("parallel",), 32),
    )(z_tail, pad(a_log), pad(dt_bias))


def _delta_heads(qs, ks, vs, ga, gc_rows, hs, sts, tb):
    n = len(qs)
    rng = range(n)
    _, _, _, causal, strict = _chunk_masks(tb)
    qs = [q * lax.rsqrt(jnp.sum(q * q, axis=-1, keepdims=True) + EPS) * (A_DK ** -0.5) for q in qs]
    ks = [k * lax.rsqrt(jnp.sum(k * k, axis=-1, keepdims=True) + EPS) for k in ks]
    beta = [_lane_pick(ga, h) for h in hs]
    gc = [_lane_pick(ga, h + A_HEADS) for h in hs]
    glast = [_lane_pick(ga, h + 2 * A_HEADS) for h in hs]
    decay = [jnp.exp(jnp.where(causal, gc[i] - gc_rows[i], NEG)) for i in rng]
    kb = [ks[i] * beta[i] for i in rng]

    n_pow = [jnp.where(strict, _bdot_nt(kb[i], ks[i]) * decay[i], 0.0) * -1.0 for i in rng]
    t_mat = list(n_pow)
    lvl = 2
    while lvl < CHUNK:
        n_pow = [_bdot(m, m) for m in n_pow]
        t_mat = [t_mat[i] + n_pow[i] + _bdot(t_mat[i], n_pow[i]) for i in rng]
        lvl *= 2

    egc = [jnp.exp(g) for g in gc]
    rhs = [jnp.concatenate([vs[i] * beta[i], kb[i] * egc[i]], axis=1) for i in rng]
    uw = [rhs[i] + _bdot(t_mat[i], rhs[i]) for i in rng]
    qk = [_bdot_nt(qs[i], ks[i]) * decay[i] for i in rng]
    qkuw = [_bdot(qk[i], uw[i]) for i in rng]
    o_local = [x[:, :A_DV] for x in qkuw]
    q_eff = [qs[i] * egc[i] - qkuw[i][:, A_DV:] for i in rng]
    k_dec = [ks[i] * jnp.exp(glast[i] - gc[i]) for i in rng]
    eg_last = [jnp.exp(g) for g in glast]

    sts = list(sts)
    outs = [[] for _ in rng]
    for c in range(tb // CHUNK):
        lo, hi = c * CHUNK, (c + 1) * CHUNK
        bg = [_bdot_tn(uw[i][lo:hi], k_dec[i][lo:hi]) for i in rng]
        for i in rng:
            outs[i].append(o_local[i][lo:hi] + _bdot_nt(q_eff[i][lo:hi], sts[i]))
        sts = [sts[i] * eg_last[i][lo:lo + 1, :] + bg[i][:A_DV] - _bdot(sts[i], bg[i][A_DV:]) for i in rng]
    return [jnp.concatenate(o, axis=0) for o in outs], sts


def _mixer_a_kernel(xq_ref, xk_ref, xv_ref, z_ref, ga_ref, gct_ref, wq_ref, wk_ref, wv_ref,
                    anorm_ref, o_ref, halo_ref, cbuf_ref, state_ref):
    tb = xq_ref.shape[0]
    hg = xq_ref.shape[1] // A_DK

    @pl.when(pl.program_id(1) == 0)
    def _():
        halo_ref[...] = jnp.zeros_like(halo_ref)
        state_ref[...] = jnp.zeros_like(state_ref)

    q = _causal_conv_silu(xq_ref, wq_ref, halo_ref, cbuf_ref, 0, tb)
    k = _causal_conv_silu(xk_ref, wk_ref, halo_ref, cbuf_ref, 1, tb)
    v = _causal_conv_silu(xv_ref, wv_ref, halo_ref, cbuf_ref, 2, tb)
    ga = ga_ref[...]
    hs = [pl.program_id(0) * hg + j for j in range(hg)]
    sls = [slice(j * A_DK, (j + 1) * A_DK) for j in range(hg)]
    outs, sts = _delta_heads([q[:, s] for s in sls], [k[:, s] for s in sls], [v[:, s] for s in sls], ga,
                             [gct_ref[pl.ds(h, 1), :] for h in hs], hs, [state_ref[j] for j in range(hg)], tb)
    for j in range(hg):
        state_ref[j] = sts[j]
        o = outs[j]
        o = o * lax.rsqrt(jnp.mean(o * o, axis=-1, keepdims=True) + EPS) * anorm_ref[...]
        o_ref[:, sls[j]] = (o * _silu(z_ref[:, sls[j]])).astype(o_ref.dtype)


def _mixer_a(z_a, z_tail, conv_a, a_log, dt_bias, a_norm):
    t = z_a.shape[0]
    ng = A_HEADS // A_HG
    wid = A_HG * A_DK
    ga, gct = _gates_a(z_tail, a_log, dt_bias)
    blk = lambda off: pl.BlockSpec((TA, wid), lambda g, i: (i, off + g))
    cblk = lambda off: pl.BlockSpec((A_CONV, wid), lambda g, i: (0, off + g))
    return pl.pallas_call(
        _mixer_a_kernel,
        grid=(ng, t // TA),
        in_specs=[blk(0), blk(ng), blk(2 * ng), blk(3 * ng),
                  pl.BlockSpec((TA, LANE), lambda g, i: (i, 0)),
                  pl.BlockSpec((A_HEADS, TA), lambda g, i: (0, i)),
                  cblk(0), cblk(ng), cblk(2 * ng), pl.BlockSpec((1, LANE), lambda g, i: (0, 0))],
        out_specs=pl.BlockSpec((TA, wid), lambda g, i: (i, g)),
        out_shape=jax.ShapeDtypeStruct((t, A_V), BF16),
        scratch_shapes=[pltpu.VMEM((3, 8, wid), F32), pltpu.VMEM((3, 8 + TA, wid), F32),
                        pltpu.VMEM((A_HG, A_DV, A_DK), F32)],
        compiler_params=_cparams(("parallel", "arbitrary"), 48),
    )(z_a, z_a, z_a, z_a, ga, gct, conv_a, conv_a, conv_a,
      a_norm.reshape(1, A_DV).astype(F32))


def _cumsum_rows(mask, x):
    hi = x.astype(BF16)
    r1 = x - hi.astype(F32)
    mid = r1.astype(BF16)
    lo = (r1 - mid.astype(F32)).astype(BF16)
    dot = lambda p: jnp.dot(mask, p, preferred_element_type=F32)
    return dot(hi) + dot(mid) + dot(lo)


def _gla_chunk_scores(qc, kc, bc):
    rid = lax.broadcasted_iota(jnp.int32, (SUB, B_DK), 0)
    rid_lo = lax.broadcasted_iota(jnp.int32, (SUB // 2, B_DK), 0) + SUB // 2
    lane_c = lax.broadcasted_iota(jnp.int32, (SUB, CHUNK), 1)
    crow = lax.broadcasted_iota(jnp.int32, (CHUNK, B_DK), 0)
    half = SUB // 2
    rows = []
    for si in range(CHUNK // SUB):
        r0 = si * SUB
        qb, bb = qc[r0:r0 + SUB], bc[r0:r0 + SUB]
        ys = []
        for j in range(SUB):
            bj = bc[r0 + j:r0 + j + 1]
            if j < half:
                ys.append(qb * jnp.exp(jnp.where(rid >= j, bb - bj, NEG)))
            else:
                ys.append(jnp.zeros((half, B_DK), F32))
                ys.append(qb[half:] * jnp.exp(jnp.where(rid_lo >= j, bb[half:] - bj, NEG)))
        r = _bdot_nt(jnp.concatenate(ys, axis=0), kc)
        blk = jnp.zeros((SUB, CHUNK), F32)
        for j in range(SUB):
            blk = jnp.where(lane_c == r0 + j, r[j * SUB:(j + 1) * SUB], blk)
        if si > 0:
            bref = bc[r0:r0 + 1]
            qt = qb * jnp.exp(bb - bref)
            kt = kc * jnp.exp(jnp.where(crow < r0, bref - bc, NEG))
            blk = blk + _bdot_nt(qt, kt)
        rows.append(blk)
    return jnp.concatenate(rows, axis=0)


def _mixer_b_kernel(q_ref, k_ref, v_ref, zg_ref, tail_ref, wg_ref, bgk_ref, bnorm_ref, o_ref, state_ref):
    tb = q_ref.shape[0]
    nh = q_ref.shape[1] // B_DK

    @pl.when(pl.program_id(0) == 0)
    def _():
        state_ref[...] = jnp.zeros_like(state_ref)

    x = _fdot(tail_ref[...], wg_ref[...]) + bgk_ref[...]
    gk = -_softplus(-x) * (1.0 / B_GATE_NORM)
    _, _, _, causal, _ = _chunk_masks(tb)
    b_all = _cumsum_rows(jnp.where(causal, 1.0, 0.0).astype(BF16), gk)

    sts = [state_ref[h] for h in range(nh)]
    outs = [[] for _ in range(nh)]
    for c in range(tb // CHUNK):
        lo, hi = c * CHUNK, (c + 1) * CHUNK
        for h in range(nh):
            qc = q_ref[lo:hi, h * B_DK:(h + 1) * B_DK] * (B_DK ** -0.5)
            kc = k_ref[lo:hi, h * B_DK:(h + 1) * B_DK]
            vc = v_ref[lo:hi, h * B_DV:(h + 1) * B_DV]
            bc = b_all[lo:hi, h * B_DK:(h + 1) * B_DK]
            a_c = _gla_chunk_scores(qc, kc, bc)
            bl = bc[CHUNK - 1:CHUNK]
            outs[h].append(_bdot(a_c, vc) + _bdot_nt(qc * jnp.exp(bc), sts[h]))
            sts[h] = sts[h] * jnp.exp(bl) + _bdot_tn(vc, kc * jnp.exp(bl - bc))
    for h in range(nh):
        state_ref[h] = sts[h]
        o = jnp.concatenate(outs[h], axis=0)
        o = o * lax.rsqrt(jnp.mean(o * o, axis=-1, keepdims=True) + EPS) * bnorm_ref[...]
        sl = slice(h * B_DV, (h + 1) * B_DV)
        o_ref[:, sl] = (o * _silu(zg_ref[:, sl])).astype(o_ref.dtype)


def _mixer_b(z_b, z_tail, w_gk2, b_gk, b_norm):
    t = z_b.shape[0]
    wg = jnp.zeros((LANE, B_QK), F32).at[2 * A_HEADS:2 * A_HEADS + B_GATE_RANK].set(w_gk2.astype(F32))
    col = lambda w, off: pl.BlockSpec((TB, w), lambda i: (i, off // w))
    full = lambda r, c: pl.BlockSpec((r, c), lambda i: (0, 0))
    return pl.pallas_call(
        _mixer_b_kernel,
        grid=(t // TB,),
        in_specs=[col(B_QK, 0), col(B_QK, B_QK), col(B_V, 2 * B_QK), col(B_V, 2 * B_QK + B_V), col(LANE, 0),
                  full(LANE, B_QK), full(1, B_QK), full(1, B_DV)],
        out_specs=pl.BlockSpec((TB, B_V), lambda i: (i, 0)),
        out_shape=jax.ShapeDtypeStruct((t, B_V), BF16),
        scratch_shapes=[pltpu.VMEM((B_HEADS, B_DV, B_DK), F32)],
        compiler_params=_cparams(("arbitrary",), 48),
    )(z_b, z_b, z_b, z_b, z_tail, wg, b_gk.reshape(1, B_QK).astype(F32),
      b_norm.reshape(1, B_DV).astype(F32))


def _merge_kernel(oa_ref, ob_ref, ma_ref, mb_ref, x_ref, woa_ref, wob_ref, wout_ref, nf_ref,
                  wr_ref, br_ref, x2_ref, h2_ref, lt_ref):
    ya = jnp.dot(oa_ref[...], woa_ref[...], preferred_element_type=F32)
    yb = jnp.dot(ob_ref[...], wob_ref[...], preferred_element_type=F32)
    m = _sigmoid(ma_ref[...]) * ya + _sigmoid(mb_ref[...]) * yb
    x2 = x_ref[...] + jnp.dot(m.astype(BF16), wout_ref[...], preferred_element_type=F32)
    x2_ref[...] = x2
    h2 = x2 * lax.rsqrt(jnp.mean(x2 * x2, axis=-1, keepdims=True) + EPS) * nf_ref[...]
    h2_ref[...] = h2
    h_hi = h2.astype(BF16)
    h_mid = (h2 - h_hi.astype(F32)).astype(BF16)
    dot = lambda a, b: jnp.dot(a, b, preferred_element_type=F32)
    logits = dot(h_hi, wr_ref[0]) + (dot(h_mid, wr_ref[0]) + dot(h_hi, wr_ref[1]))
    lt_ref[...] = logits.T + br_ref[...]


def _merge(oa_g, ob_g, z_mix, x, w_oa, w_ob, w_out, norm_ffn, wr, br_t, tm):
    t, d = x.shape
    row = lambda w, c: pl.BlockSpec((tm, w), lambda i: (i, c))
    full = lambda a: pl.BlockSpec(a.shape, lambda i: (0,) * a.ndim, pipeline_mode=pl.Buffered(1))
    return pl.pallas_call(
        _merge_kernel,
        grid=(t // tm,),
        in_specs=[row(A_V, 0), row(B_V, 0), row(d, 0), row(d, 1), row(d, 0),
                  full(w_oa), full(w_ob), full(w_out), pl.BlockSpec((1, d), lambda i: (0, 0)),
                  full(wr), full(br_t)],
        out_specs=[row(d, 0), row(d, 0), pl.BlockSpec((LANE, tm), lambda i: (0, i))],
        out_shape=[jax.ShapeDtypeStruct((t, d), F32), jax.ShapeDtypeStruct((t, d), F32),
                   jax.ShapeDtypeStruct((LANE, t), F32)],
        compiler_params=_cparams(("parallel",), 56),
    )(oa_g, ob_g, z_mix, z_mix, x, w_oa, w_ob, w_out, norm_ffn.reshape(1, d).astype(F32), wr, br_t)


SEG = 256


def _route_kernel(lt_ref, pos_ref, gate_ref, blk_ref, oh_ref):
    t = lt_ref.shape[1]
    rid8 = lax.broadcasted_iota(jnp.int32, (8, t), 0)
    lg = jnp.where(rid8 < N_GROUPS, lt_ref[0:8, :], -jnp.inf)
    gmax = jnp.max(lg, axis=0, keepdims=True)
    g_idx = jnp.min(jnp.where(lg == gmax, rid8, 8), axis=0, keepdims=True)
    p_top = 1.0 / jnp.sum(jnp.exp(lg - gmax), axis=0, keepdims=True)

    les = jnp.zeros((EXP_PER_GROUP, t), F32)
    for g in range(N_GROUPS):
        les = jnp.where(g_idx == g, lt_ref[8 + g * EXP_PER_GROUP:8 + (g + 1) * EXP_PER_GROUP, :], les)
    m1 = jnp.max(les, axis=0, keepdims=True)
    i1 = jnp.min(jnp.where(les == m1, rid8, 8), axis=0, keepdims=True)
    les2 = jnp.where(rid8 == i1, -jnp.inf, les)
    m2 = jnp.max(les2, axis=0, keepdims=True)
    i2 = jnp.min(jnp.where(les2 == m2, rid8, 8), axis=0, keepdims=True)
    r = jnp.exp(m2 - m1)
    gate_ref[...] = jnp.concatenate([p_top / (1.0 + r), p_top * r / (1.0 + r),
                                     jnp.zeros((LANE - 2, t), F32)], axis=0).T
    e1 = g_idx * EXP_PER_GROUP + i1
    e2 = g_idx * EXP_PER_GROUP + i2

    rid32 = lax.broadcasted_iota(jnp.int32, (N_EXPERTS, t), 0)
    oh_ref[0] = jnp.where(rid32 == e1, 1.0, 0.0)
    oh_ref[1] = jnp.where(rid32 == e2, 1.0, 0.0)

    ui = lax.broadcasted_iota(jnp.int32, (SEG, SEG), 0)
    uj = lax.broadcasted_iota(jnp.int32, (SEG, SEG), 1)
    upper = jnp.where(ui < uj, 1.0, 0.0).astype(BF16)
    carry = jnp.zeros((N_EXPERTS, 1), F32)
    ranks = []
    for kk in range(2):
        segs = []
        for sg in range(t // SEG):
            oh = oh_ref[kk, :, sg * SEG:(sg + 1) * SEG]
            pre = jnp.dot(oh.astype(BF16), upper, preferred_element_type=F32) + carry
            segs.append(jnp.sum(oh * pre, axis=0, keepdims=True))
            carry = carry + jnp.sum(oh, axis=1, keepdims=True)
        ranks.append(jnp.concatenate(segs, axis=1))
    counts = carry
    nblk = jnp.floor((counts + (ROW_BLOCK - 1)) * (1.0 / ROW_BLOCK))
    li = lax.broadcasted_iota(jnp.int32, (N_EXPERTS, N_EXPERTS), 0)
    lj = lax.broadcasted_iota(jnp.int32, (N_EXPERTS, N_EXPERTS), 1)
    nb_b = jnp.broadcast_to(nblk, (N_EXPERTS, LANE))
    start_blk = _fdot(jnp.where(lj < li, 1.0, 0.0), nb_b)[:, 0:1]
    end_blk = start_blk + nblk
    start_row = start_blk * ROW_BLOCK

    pos_ref[...] = jnp.zeros_like(pos_ref)
    for kk in range(2):
        base = jnp.sum(oh_ref[kk] * start_row, axis=0, keepdims=True)
        pos_ref[kk:kk + 1, :] = (base + ranks[kk]).astype(jnp.int32)

    nb = blk_ref.shape[1]
    bid = lax.broadcasted_iota(jnp.int32, (N_EXPERTS, nb), 1).astype(F32)
    be = jnp.sum(jnp.where(end_blk <= bid, 1.0, 0.0), axis=0, keepdims=True)
    blk_ref[...] = jnp.zeros_like(blk_ref)
    blk_ref[0:1, :] = jnp.minimum(be, N_EXPERTS - 1.0).astype(jnp.int32)
    blk_ref[1:2, :] = jnp.broadcast_to(end_blk[N_EXPERTS - 1:N_EXPERTS, :], (1, nb)).astype(jnp.int32)
    ends = jnp.sum(jnp.where(li == lj, end_blk, 0.0), axis=0, keepdims=True)
    blk_ref[2:3, 0:N_EXPERTS] = ends.astype(jnp.int32)


def _route(lt, n_blk):
    t = lt.shape[1]
    nb = -(-n_blk // LANE) * LANE
    return pl.pallas_call(
        _route_kernel,
        out_shape=[jax.ShapeDtypeStruct((8, t), jnp.int32), jax.ShapeDtypeStruct((t, LANE), F32),
                   jax.ShapeDtypeStruct((8, nb), jnp.int32)],
        scratch_shapes=[pltpu.VMEM((2, N_EXPERTS, t), F32)],
        compiler_params=pltpu.CompilerParams(vmem_limit_bytes=48 * 2 ** 20),
    )(lt)


def _scatter_kernel(pos_ref, ends_ref, nu_ref, h_ref, xs_ref, zbuf, sem, zsem):
    tm = h_ref.shape[0]
    t = pl.num_programs(0) * tm
    base = pl.program_id(0) * tm
    n_blk = xs_ref.shape[0] // ROW_BLOCK

    def zero_block(blk):
        return pltpu.make_async_copy(zbuf, xs_ref.at[pl.ds(blk * ROW_BLOCK, ROW_BLOCK), :], zsem)

    def for_each_zero_block(fn):
        for e in range(N_EXPERTS):
            first = ends_ref[e - 1] if e else 0

            @pl.when(ends_ref[e] > first)
            def _():
                fn(zero_block(ends_ref[e] - 1))

        def tail(blk, c):
            fn(zero_block(blk))
            return c
        lax.fori_loop(nu_ref[0], n_blk, tail, 0)

    @pl.when(pl.program_id(0) == 0)
    def _():
        zbuf[...] = jnp.zeros_like(zbuf)
        for_each_zero_block(lambda c: c.start())
        for_each_zero_block(lambda c: c.wait())

    def row_copy(r, p):
        return pltpu.make_async_copy(h_ref.at[pl.ds(r, 1), :], xs_ref.at[pl.ds(p, 1), :], sem)

    for r in range(tm):
        for kk in range(2):
            row_copy(r, pos_ref[kk * t + base + r]).start()
    for r in range(2 * tm):
        row_copy(0, 0).wait()


def _scatter_rows(pos_flat, ends, n_used, h2, n_rows, tm):
    t, d = h2.shape
    return pl.pallas_call(
        _scatter_kernel,
        grid_spec=pltpu.PrefetchScalarGridSpec(
            num_scalar_prefetch=3,
            grid=(t // tm,),
            in_specs=[pl.BlockSpec((tm, d), lambda i, *_: (i, 0))],
            out_specs=pl.BlockSpec(memory_space=pl.ANY),
            scratch_shapes=[pltpu.VMEM((ROW_BLOCK, d), h2.dtype), pltpu.SemaphoreType.DMA(()),
                            pltpu.SemaphoreType.DMA(())],
        ),
        out_shape=jax.ShapeDtypeStruct((n_rows, d), h2.dtype),
        compiler_params=_cparams(("arbitrary",), 32),
    )(pos_flat, ends, n_used, h2)


def _expert_kernel(be_ref, nu_ref, ends_ref, x_ref, w1_hbm, w3_hbm, w2_hbm, o_ref,
                   w1s, w3s, w2s, w1b, w3b, w2b, sem):
    b = pl.program_id(0)
    e = be_ref[b]
    prev = be_ref[jnp.maximum(b - 1, 0)]
    changed = jnp.logical_or(b == 0, e != prev)

    def weight_copies(ex):
        return (pltpu.make_async_copy(w1_hbm.at[ex], w1s, sem.at[0]),
                pltpu.make_async_copy(w3_hbm.at[ex], w3s, sem.at[1]),
                pltpu.make_async_copy(w2_hbm.at[ex], w2s, sem.at[2]))

    @pl.when(b == 0)
    def _():
        for c in weight_copies(e):
            c.start()

    @pl.when(jnp.logical_and(changed, b < nu_ref[0]))
    def _():
        for c in weight_copies(e):
            c.wait()
        w1b[...] = w1s[...].astype(BF16)
        w3b[...] = w3s[...].astype(BF16)
        w2b[...] = w2s[...].astype(BF16)
        nxt = ends_ref[e]

        @pl.when(nxt < nu_ref[0])
        def _():
            for c in weight_copies(be_ref[nxt]):
                c.start()

    @pl.when(b < nu_ref[0])
    def _():
        xb = x_ref[...].astype(BF16)
        a = jnp.dot(xb, w1b[...], preferred_element_type=F32)
        g = jnp.dot(xb, w3b[...], preferred_element_type=F32)
        o_ref[...] = jnp.dot((_silu(a) * g).astype(BF16), w2b[...], preferred_element_type=F32)

    @pl.when(b >= nu_ref[0])
    def _():
        o_ref[...] = jnp.zeros_like(o_ref)


def _experts(blk_exp, n_used, ends, xs, w1, w3, w2):
    n_rows, d = xs.shape
    n_blk = n_rows // ROW_BLOCK
    rows = lambda b, be, nu, en: (jnp.minimum(b, jnp.maximum(nu[0] - 1, 0)), 0)
    hbm = pl.BlockSpec(memory_space=pl.ANY)
    return pl.pallas_call(
        _expert_kernel,
        grid_spec=pltpu.PrefetchScalarGridSpec(
            num_scalar_prefetch=3,
            grid=(n_blk,),
            in_specs=[pl.BlockSpec((ROW_BLOCK, d), rows), hbm, hbm, hbm],
            out_specs=pl.BlockSpec((ROW_BLOCK, d), lambda b, be, nu, en: (b, 0)),
            scratch_shapes=[pltpu.VMEM((d, D_FF), w1.dtype), pltpu.VMEM((d, D_FF), w3.dtype),
                            pltpu.VMEM((D_FF, d), w2.dtype),
                            pltpu.VMEM((d, D_FF), BF16), pltpu.VMEM((d, D_FF), BF16),
                            pltpu.VMEM((D_FF, d), BF16), pltpu.SemaphoreType.DMA((3,))],
        ),
        out_shape=jax.ShapeDtypeStruct((n_rows, d), F32),
        compiler_params=_cparams(("arbitrary",), 56),
    )(blk_exp, n_used, ends, xs, w1, w3, w2)


def _combine_kernel(pos_ref, x2_ref, gt_ref, nw_ref, yb_ref, o_ref, buf, sem, *, final):
    tm = x2_ref.shape[0]
    n = pl.num_programs(0)
    t = n * tm
    i = pl.program_id(0)

    def row_copy(slot, r, kk, p):
        return pltpu.make_async_copy(yb_ref.at[pl.ds(p, 1), :], buf.at[slot, kk, pl.ds(r, 1), :], sem.at[slot])

    def issue(step, slot):
        for r in range(tm):
            for kk in range(2):
                row_copy(slot, r, kk, pos_ref[kk * t + step * tm + r]).start()

    @pl.when(i == 0)
    def _():
        issue(0, 0)

    @pl.when(i + 1 < n)
    def _():
        issue(i + 1, (i + 1) % 2)

    slot = i % 2
    for r in range(2 * tm):
        row_copy(slot, 0, 0, 0).wait()

    gt = gt_ref[...]
    y = x2_ref[...] + gt[:, 0:1] * buf[slot, 0] + gt[:, 1:2] * buf[slot, 1]
    if final:
        y = y * lax.rsqrt(jnp.mean(y * y, axis=-1, keepdims=True) + EPS) * nw_ref[...]
    o_ref[...] = y


def _combine(pos_flat, x2, gates_t, norm_final, yb, tm, final):
    t, d = x2.shape
    return pl.pallas_call(
        functools.partial(_combine_kernel, final=final),
        grid_spec=pltpu.PrefetchScalarGridSpec(
            num_scalar_prefetch=1,
            grid=(t // tm,),
            in_specs=[pl.BlockSpec((tm, d), lambda i, pos: (i, 0)),
                      pl.BlockSpec((tm, LANE), lambda i, pos: (i, 0)),
                      pl.BlockSpec((1, d), lambda i, pos: (0, 0)),
                      pl.BlockSpec(memory_space=pl.ANY)],
            out_specs=pl.BlockSpec((tm, d), lambda i, pos: (i, 0)),
            scratch_shapes=[pltpu.VMEM((2, 2, tm, d), F32), pltpu.SemaphoreType.DMA((2,))],
        ),
        out_shape=jax.ShapeDtypeStruct((t, d), F32),
        compiler_params=_cparams(("arbitrary",), 32),
    )(pos_flat, x2, gates_t, norm_final.reshape(1, d).astype(F32), yb)


def _layer(x, norm_mix, w_in, conv_a, a_log, dt_bias, a_norm, w_gk2, b_gk, b_norm,
           w_oa, w_ob, w_out, norm_ffn, w_rg, b_rg, w_re, b_re, w1, w3, w2, norm_final, final):
    t, d = x.shape
    tm = min(512, t)

    a_end = 4 * A_QK
    ga_end = a_end + 2 * A_HEADS
    b_end = ga_end + 2 * B_QK + 2 * B_V
    lr_end = b_end + B_GATE_RANK
    assert a_end % LANE == 0 and b_end % LANE == 2 * A_HEADS

    h = _rmsnorm(x, norm_mix, tm)
    tm_big = min(1024, t)
    wt = w_in.T
    z_a = _proj(h, wt, 0, a_end, tm_big, 1024)
    z_b = _proj(h, wt, ga_end, b_end - ga_end, tm_big, 1024)
    z_mix = _proj(h, wt, lr_end, 2 * d, tm_big, 1024)
    z_ga = _proj(h, wt, a_end, LANE, tm_big, LANE)
    z_lr = _proj(h, wt, b_end - b_end % LANE, LANE, tm_big, LANE)

    oa_g = _mixer_a(z_a, z_ga, conv_a.astype(F32), a_log, dt_bias, a_norm)
    ob_g = _mixer_b(z_b, z_lr, w_gk2, b_gk, b_norm)

    wr = jnp.zeros((d, LANE), F32).at[:, 0:N_GROUPS].set(w_rg.astype(F32))
    wr = wr.at[:, 8:8 + N_EXPERTS].set(w_re.reshape(d, N_EXPERTS).astype(F32))
    wr_hi = wr.astype(BF16)
    wr = jnp.stack([wr_hi, (wr - wr_hi.astype(F32)).astype(BF16)])
    br_t = jnp.zeros((LANE, 1), F32).at[0:N_GROUPS, 0].set(b_rg.astype(F32))
    br_t = br_t.at[8:8 + N_EXPERTS, 0].set(b_re.reshape(N_EXPERTS).astype(F32))
    x2, h2, lt = _merge(oa_g, ob_g, z_mix, x, w_oa.astype(BF16), w_ob.astype(BF16), w_out.astype(BF16),
                        norm_ffn, wr, br_t, min(256, t))

    n_blk = (2 * t + ROW_BLOCK - 1) // ROW_BLOCK + N_EXPERTS
    n_rows = n_blk * ROW_BLOCK
    pos, gates, blk = _route(lt, n_blk)
    pos_flat = pos[0:2].reshape(2 * t)
    blk_exp, n_used, ends = blk[0, :n_blk], blk[1, 0:1], blk[2, :N_EXPERTS]
    xs = _scatter_rows(pos_flat, ends, n_used, h2, n_rows, min(256, t))
    yb = _experts(blk_exp, n_used, ends, xs, w1, w3, w2)
    return _combine(pos_flat, x2, gates, norm_final, yb, min(128, t), final)


def kernel(x, norm_mix, w_in, conv_a, a_log, dt_bias, a_norm, w_gk2, b_gk, b_norm, w_oa, w_ob, w_out,
           norm_ffn, w_rg, b_rg, w_re, b_re, w1, w3, w2, norm_final):
    bsz, seq, d = x.shape
    assert bsz == 1, "one sequence per call"
    depth = norm_mix.shape[0]
    y = x.reshape(seq, d)
    for l in range(depth):
        y = _layer(y, norm_mix[l], w_in[l], conv_a[l], a_log[l], dt_bias[l], a_norm[l], w_gk2[l], b_gk[l],
                   b_norm[l], w_oa[l], w_ob[l], w_out[l], norm_ffn[l], w_rg[l], b_rg[l], w_re[l], b_re[l],
                   w1[l], w3[l], w2[l], norm_final, l == depth - 1)
    return y.reshape(bsz, seq, d)
```

```python
import functools

import jax
import jax.numpy as jnp
from jax import lax
from jax.experimental import pallas as pl
from jax.experimental.pallas import tpu as pltpu

D_MODEL = 2048
CHUNK = 64
EPS = 1e-6
A_HEADS, A_DK, A_DV, A_CONV = 8, 128, 128, 4
A_QK, A_V = A_HEADS * A_DK, A_HEADS * A_DV
B_HEADS, B_DK, B_DV, B_GATE_RANK, B_GATE_NORM = 4, 128, 256, 16, 16.0
B_QK, B_V = B_HEADS * B_DK, B_HEADS * B_DV
N_GROUPS, EXP_PER_GROUP, D_FF = 4, 8, 512
N_EXPERTS = N_GROUPS * EXP_PER_GROUP
ROW_BLOCK = 256
LANE = 128
SUB = 16
TB = 256
A_HG = 8
TA = 128
NEG = -1e30

F32 = jnp.float32
BF16 = jnp.bfloat16
HI = lax.Precision.HIGHEST


def _cparams(sem, vmem_mib):
    return pltpu.CompilerParams(dimension_semantics=sem, vmem_limit_bytes=vmem_mib * 2 ** 20)


def _bdot(a, b):
    return jnp.dot(a.astype(BF16), b.astype(BF16), preferred_element_type=F32)


def _bdot_nt(a, b):
    return lax.dot_general(a.astype(BF16), b.astype(BF16), (((1,), (1,)), ((), ())),
                           preferred_element_type=F32)


def _bdot_tn(a, b):
    return lax.dot_general(a.astype(BF16), b.astype(BF16), (((0,), (0,)), ((), ())),
                           preferred_element_type=F32)


def _fdot(a, b):
    return jnp.dot(a, b, preferred_element_type=F32, precision=HI)


def _sigmoid(x):
    return 1.0 / (1.0 + jnp.exp(-x))


def _silu(x):
    return x * _sigmoid(x)


def _softplus(x):
    return jnp.maximum(x, 0.0) + jnp.log(1.0 + jnp.exp(-jnp.abs(x)))


def _rmsnorm_kernel(x_ref, w_ref, o_ref):
    x = x_ref[...]
    ms = jnp.mean(x * x, axis=-1, keepdims=True)
    o_ref[...] = (x * lax.rsqrt(ms + EPS) * w_ref[...]).astype(o_ref.dtype)


def _rmsnorm(x, w, tm):
    t, d = x.shape
    return pl.pallas_call(
        _rmsnorm_kernel,
        grid=(t // tm,),
        in_specs=[pl.BlockSpec((tm, d), lambda i: (i, 0)), pl.BlockSpec((1, d), lambda i: (0, 0))],
        out_specs=pl.BlockSpec((tm, d), lambda i: (i, 0)),
        out_shape=jax.ShapeDtypeStruct((t, d), BF16),
        compiler_params=_cparams(("parallel",), 32),
    )(x, w.reshape(1, d))


PREP_ROWS = 256


def _proj_kernel(a_ref, wt_ref, *rest):
    if len(rest) == 4:
        side_ref, o_ref, side_out_ref, wb_ref = rest
        for r in range(0, side_ref.shape[1], PREP_ROWS):
            side_out_ref[0, r:r + PREP_ROWS, :] = side_ref[0, r:r + PREP_ROWS, :].astype(BF16)
    else:
        o_ref, wb_ref = rest
    tn = wt_ref.shape[0]

    @pl.when(pl.program_id(1) == 0)
    def _():
        for r in range(0, tn, PREP_ROWS):
            wb_ref[r:r + PREP_ROWS, :] = wt_ref[r:r + PREP_ROWS, :].astype(BF16)

    o_ref[...] = lax.dot_general(a_ref[...], wb_ref[...], (((1,), (1,)), ((), ())), preferred_element_type=F32)


def _proj(a, wt, col0, n, tm, tn, side=None):
    m, k = a.shape
    assert n % tn == 0 and col0 % 8 == 0
    ni = m // tm
    in_specs = [pl.BlockSpec((tm, k), lambda j, i: (i, 0)),
                pl.BlockSpec((pl.Element(tn), pl.Element(k)), lambda j, i: (pl.multiple_of(col0 + j * tn, 8), 0))]
    out_specs = [pl.BlockSpec((tm, tn), lambda j, i: (i, j))]
    out_shape = [jax.ShapeDtypeStruct((m, n), F32)]
    args = [a, wt]
    if side is not None:
        assert side.shape[0] == (n // tn) * ni and side.shape[1] % PREP_ROWS == 0
        slab = pl.BlockSpec((1,) + side.shape[1:], lambda j, i: (j * ni + i, 0, 0))
        in_specs.append(slab)
        out_specs.append(slab)
        out_shape.append(jax.ShapeDtypeStruct(side.shape, BF16))
        args.append(side)
    out = pl.pallas_call(
        _proj_kernel,
        grid=(n // tn, ni),
        in_specs=in_specs,
        out_specs=out_specs,
        out_shape=out_shape,
        scratch_shapes=[pltpu.VMEM((tn, k), BF16)],
        compiler_params=_cparams(("arbitrary", "arbitrary"), 56),
    )(*args)
    return out[0] if side is None else out


def _chunk_masks(tb):
    row = lax.broadcasted_iota(jnp.int32, (tb, tb), 0)
    col = lax.broadcasted_iota(jnp.int32, (tb, tb), 1)
    same = (row // CHUNK) == (col // CHUNK)
    return row, col, same, same & (col <= row), same & (col < row)


def _lane_pick(x, idx):
    lane = lax.broadcasted_iota(jnp.int32, x.shape, 1)
    return jnp.sum(jnp.where(lane == idx, x, 0.0), axis=-1, keepdims=True)


def _causal_conv_silu(x_ref, w_ref, halo_ref, cbuf_ref, idx, tb):
    cbuf_ref[idx, 0:8, :] = halo_ref[idx]
    cbuf_ref[idx, 8:8 + tb, :] = x_ref[...]
    halo_ref[idx] = x_ref[tb - 8:tb, :]
    w = w_ref[...]
    acc = w[A_CONV - 1:A_CONV, :] * x_ref[...]
    for j in range(A_CONV - 1):
        off = 8 - (A_CONV - 1) + j
        acc = acc + w[j:j + 1, :] * cbuf_ref[idx, off:off + tb, :]
    return _silu(acc)


def _gates_a_kernel(tail_ref, alog_ref, dtb_ref, ga_ref, gct_ref):
    tb = tail_ref.shape[0]
    gl = tail_ref[...]
    beta = _sigmoid(gl)
    g = -jnp.exp(alog_ref[...]) * _softplus(gl + dtb_ref[...])
    _, _, same, causal, _ = _chunk_masks(tb)
    gc = _fdot(jnp.where(causal, 1.0, 0.0), g)
    glast = _fdot(jnp.where(same, 1.0, 0.0), g)
    lane = lax.broadcasted_iota(jnp.int32, (tb, LANE), 1)
    ga_ref[...] = jnp.where(lane < A_HEADS, beta,
                            jnp.where(lane < 2 * A_HEADS, gc, pltpu.roll(glast, A_HEADS, 1)))
    gct_ref[...] = gc.T[A_HEADS:2 * A_HEADS, :]


def _gates_a(z_tail, a_log, dt_bias):
    t = z_tail.shape[0]
    pad = lambda p: jnp.pad(p.astype(F32), (A_HEADS, LANE - 2 * A_HEADS)).reshape(1, LANE)
    const = pl.BlockSpec((1, LANE), lambda i: (0, 0))
    return pl.pallas_call(
        _gates_a_kernel,
        grid=(t // TB,),
        in_specs=[pl.BlockSpec((TB, LANE), lambda i: (i, 0)), const, const],
        out_specs=[pl.BlockSpec((TB, LANE), lambda i: (i, 0)), pl.BlockSpec((A_HEADS, TB), lambda i: (0, i))],
        out_shape=[jax.ShapeDtypeStruct((t, LANE), F32), jax.ShapeDtypeStruct((A_HEADS, t), F32)],
        compiler_params=_cparams(("parallel",), 32),
    )(z_tail, pad(a_log), pad(dt_bias))


def _delta_heads(qs, ks, vs, ga, gc_rows, hs, sts, tb):
    n = len(qs)
    rng = range(n)
    _, _, _, causal, strict = _chunk_masks(tb)
    qs = [q * lax.rsqrt(jnp.sum(q * q, axis=-1, keepdims=True) + EPS) * (A_DK ** -0.5) for q in qs]
    ks = [k * lax.rsqrt(jnp.sum(k * k, axis=-1, keepdims=True) + EPS) for k in ks]
    beta = [_lane_pick(ga, h) for h in hs]
    gc = [_lane_pick(ga, h + A_HEADS) for h in hs]
    glast = [_lane_pick(ga, h + 2 * A_HEADS) for h in hs]
    decay = [jnp.exp(jnp.where(causal, gc[i] - gc_rows[i], NEG)) for i in rng]
    kb = [ks[i] * beta[i] for i in rng]

    n_pow = [jnp.where(strict, _bdot_nt(kb[i], ks[i]) * decay[i], 0.0) * -1.0 for i in rng]
    t_mat = list(n_pow)
    lvl = 2
    while lvl < CHUNK:
        n_pow = [_bdot(m, m) for m in n_pow]
        t_mat = [t_mat[i] + n_pow[i] + _bdot(t_mat[i], n_pow[i]) for i in rng]
        lvl *= 2

    egc = [jnp.exp(g) for g in gc]
    rhs = [jnp.concatenate([vs[i] * beta[i], kb[i] * egc[i]], axis=1) for i in rng]
    uw = [rhs[i] + _bdot(t_mat[i], rhs[i]) for i in rng]
    qk = [_bdot_nt(qs[i], ks[i]) * decay[i] for i in rng]
    qkuw = [_bdot(qk[i], uw[i]) for i in rng]
    o_local = [x[:, :A_DV] for x in qkuw]
    q_eff = [qs[i] * egc[i] - qkuw[i][:, A_DV:] for i in rng]
    k_dec = [ks[i] * jnp.exp(glast[i] - gc[i]) for i in rng]
    eg_last = [jnp.exp(g) for g in glast]

    sts = list(sts)
    outs = [[] for _ in rng]
    for c in range(tb // CHUNK):
        lo, hi = c * CHUNK, (c + 1) * CHUNK
        bg = [_bdot_tn(uw[i][lo:hi], k_dec[i][lo:hi]) for i in rng]
        for i in rng:
            outs[i].append(o_local[i][lo:hi] + _bdot_nt(q_eff[i][lo:hi], sts[i]))
        sts = [sts[i] * eg_last[i][lo:lo + 1, :] + bg[i][:A_DV] - _bdot(sts[i], bg[i][A_DV:]) for i in rng]
    return [jnp.concatenate(o, axis=0) for o in outs], sts


def _mixer_a_kernel(xq_ref, xk_ref, xv_ref, z_ref, ga_ref, gct_ref, wq_ref, wk_ref, wv_ref,
                    anorm_ref, o_ref, halo_ref, cbuf_ref, state_ref):
    tb = xq_ref.shape[0]
    hg = xq_ref.shape[1] // A_DK

    @pl.when(pl.program_id(1) == 0)
    def _():
        halo_ref[...] = jnp.zeros_like(halo_ref)
        state_ref[...] = jnp.zeros_like(state_ref)

    q = _causal_conv_silu(xq_ref, wq_ref, halo_ref, cbuf_ref, 0, tb)
    k = _causal_conv_silu(xk_ref, wk_ref, halo_ref, cbuf_ref, 1, tb)
    v = _causal_conv_silu(xv_ref, wv_ref, halo_ref, cbuf_ref, 2, tb)
    ga = ga_ref[...]
    hs = [pl.program_id(0) * hg + j for j in range(hg)]
    sls = [slice(j * A_DK, (j + 1) * A_DK) for j in range(hg)]
    outs, sts = _delta_heads([q[:, s] for s in sls], [k[:, s] for s in sls], [v[:, s] for s in sls], ga,
                             [gct_ref[pl.ds(h, 1), :] for h in hs], hs, [state_ref[j] for j in range(hg)], tb)
    for j in range(hg):
        state_ref[j] = sts[j]
        o = outs[j]
        o = o * lax.rsqrt(jnp.mean(o * o, axis=-1, keepdims=True) + EPS) * anorm_ref[...]
        o_ref[:, sls[j]] = (o * _silu(z_ref[:, sls[j]])).astype(o_ref.dtype)


def _mixer_a(z_a, z_tail, conv_a, a_log, dt_bias, a_norm):
    t = z_a.shape[0]
    ng = A_HEADS // A_HG
    wid = A_HG * A_DK
    ga, gct = _gates_a(z_tail, a_log, dt_bias)
    blk = lambda off: pl.BlockSpec((TA, wid), lambda g, i: (i, off + g))
    cblk = lambda off: pl.BlockSpec((A_CONV, wid), lambda g, i: (0, off + g))
    return pl.pallas_call(
        _mixer_a_kernel,
        grid=(ng, t // TA),
        in_specs=[blk(0), blk(ng), blk(2 * ng), blk(3 * ng),
                  pl.BlockSpec((TA, LANE), lambda g, i: (i, 0)),
                  pl.BlockSpec((A_HEADS, TA), lambda g, i: (0, i)),
                  cblk(0), cblk(ng), cblk(2 * ng), pl.BlockSpec((1, LANE), lambda g, i: (0, 0))],
        out_specs=pl.BlockSpec((TA, wid), lambda g, i: (i, g)),
        out_shape=jax.ShapeDtypeStruct((t, A_V), BF16),
        scratch_shapes=[pltpu.VMEM((3, 8, wid), F32), pltpu.VMEM((3, 8 + TA, wid), F32),
                        pltpu.VMEM((A_HG, A_DV, A_DK), F32)],
        compiler_params=_cparams(("parallel", "arbitrary"), 48),
    )(z_a, z_a, z_a, z_a, ga, gct, conv_a, conv_a, conv_a,
      a_norm.reshape(1, A_DV).astype(F32))


def _cumsum_rows(mask, x):
    hi = x.astype(BF16)
    r1 = x - hi.astype(F32)
    mid = r1.astype(BF16)
    lo = (r1 - mid.astype(F32)).astype(BF16)
    dot = lambda p: jnp.dot(mask, p, preferred_element_type=F32)
    return dot(hi) + dot(mid) + dot(lo)


def _gla_chunk_scores(qc, kc, bc):
    rid = lax.broadcasted_iota(jnp.int32, (SUB, B_DK), 0)
    rid_lo = lax.broadcasted_iota(jnp.int32, (SUB // 2, B_DK), 0) + SUB // 2
    lane_c = lax.broadcasted_iota(jnp.int32, (SUB, CHUNK), 1)
    crow = lax.broadcasted_iota(jnp.int32, (CHUNK, B_DK), 0)
    half = SUB // 2
    rows = []
    for si in range(CHUNK // SUB):
        r0 = si * SUB
        qb, bb = qc[r0:r0 + SUB], bc[r0:r0 + SUB]
        ys = []
        for j in range(SUB):
            bj = bc[r0 + j:r0 + j + 1]
            if j < half:
                ys.append(qb * jnp.exp(jnp.where(rid >= j, bb - bj, NEG)))
            else:
                ys.append(jnp.zeros((half, B_DK), F32))
                ys.append(qb[half:] * jnp.exp(jnp.where(rid_lo >= j, bb[half:] - bj, NEG)))
        r = _bdot_nt(jnp.concatenate(ys, axis=0), kc)
        blk = jnp.zeros((SUB, CHUNK), F32)
        for j in range(SUB):
            blk = jnp.where(lane_c == r0 + j, r[j * SUB:(j + 1) * SUB], blk)
        if si > 0:
            bref = bc[r0:r0 + 1]
            qt = qb * jnp.exp(bb - bref)
            kt = kc * jnp.exp(jnp.where(crow < r0, bref - bc, NEG))
            blk = blk + _bdot_nt(qt, kt)
        rows.append(blk)
    return jnp.concatenate(rows, axis=0)


def _mixer_b_kernel(q_ref, k_ref, v_ref, zg_ref, tail_ref, wg_ref, bgk_ref, bnorm_ref, o_ref, state_ref):
    tb = q_ref.shape[0]
    nh = q_ref.shape[1] // B_DK

    @pl.when(pl.program_id(0) == 0)
    def _():
        state_ref[...] = jnp.zeros_like(state_ref)

    x = _fdot(tail_ref[...], wg_ref[...]) + bgk_ref[...]
    gk = -_softplus(-x) * (1.0 / B_GATE_NORM)
    _, _, _, causal, _ = _chunk_masks(tb)
    b_all = _cumsum_rows(jnp.where(causal, 1.0, 0.0).astype(BF16), gk)

    sts = [state_ref[h] for h in range(nh)]
    outs = [[] for _ in range(nh)]
    for c in range(tb // CHUNK):
        lo, hi = c * CHUNK, (c + 1) * CHUNK
        for h in range(nh):
            qc = q_ref[lo:hi, h * B_DK:(h + 1) * B_DK] * (B_DK ** -0.5)
            kc = k_ref[lo:hi, h * B_DK:(h + 1) * B_DK]
            vc = v_ref[lo:hi, h * B_DV:(h + 1) * B_DV]
            bc = b_all[lo:hi, h * B_DK:(h + 1) * B_DK]
            a_c = _gla_chunk_scores(qc, kc, bc)
            bl = bc[CHUNK - 1:CHUNK]
            outs[h].append(_bdot(a_c, vc) + _bdot_nt(qc * jnp.exp(bc), sts[h]))
            sts[h] = sts[h] * jnp.exp(bl) + _bdot_tn(vc, kc * jnp.exp(bl - bc))
    for h in range(nh):
        state_ref[h] = sts[h]
        o = jnp.concatenate(outs[h], axis=0)
        o = o * lax.rsqrt(jnp.mean(o * o, axis=-1, keepdims=True) + EPS) * bnorm_ref[...]
        sl = slice(h * B_DV, (h + 1) * B_DV)
        o_ref[:, sl] = (o * _silu(zg_ref[:, sl])).astype(o_ref.dtype)


def _mixer_b(z_b, z_tail, w_gk2, b_gk, b_norm):
    t = z_b.shape[0]
    wg = jnp.zeros((LANE, B_QK), F32).at[2 * A_HEADS:2 * A_HEADS + B_GATE_RANK].set(w_gk2.astype(F32))
    col = lambda w, off: pl.BlockSpec((TB, w), lambda i: (i, off // w))
    full = lambda r, c: pl.BlockSpec((r, c), lambda i: (0, 0))
    return pl.pallas_call(
        _mixer_b_kernel,
        grid=(t // TB,),
        in_specs=[col(B_QK, 0), col(B_QK, B_QK), col(B_V, 2 * B_QK), col(B_V, 2 * B_QK + B_V), col(LANE, 0),
                  full(LANE, B_QK), full(1, B_QK), full(1, B_DV)],
        out_specs=pl.BlockSpec((TB, B_V), lambda i: (i, 0)),
        out_shape=jax.ShapeDtypeStruct((t, B_V), BF16),
        scratch_shapes=[pltpu.VMEM((B_HEADS, B_DV, B_DK), F32)],
        compiler_params=_cparams(("arbitrary",), 48),
    )(z_b, z_b, z_b, z_b, z_tail, wg, b_gk.reshape(1, B_QK).astype(F32),
      b_norm.reshape(1, B_DV).astype(F32))


def _merge_kernel(oa_ref, ob_ref, ma_ref, mb_ref, x_ref, woa_ref, wob_ref, wout_ref, nf_ref,
                  wr_ref, br_ref, x2_ref, h2_ref, lt_ref):
    ya = jnp.dot(oa_ref[...], woa_ref[...], preferred_element_type=F32)
    yb = jnp.dot(ob_ref[...], wob_ref[...], preferred_element_type=F32)
    m = _sigmoid(ma_ref[...]) * ya + _sigmoid(mb_ref[...]) * yb
    x2 = x_ref[...] + jnp.dot(m.astype(BF16), wout_ref[...], preferred_element_type=F32)
    x2_ref[...] = x2
    h2 = x2 * lax.rsqrt(jnp.mean(x2 * x2, axis=-1, keepdims=True) + EPS) * nf_ref[...]
    h2_ref[...] = h2
    h_hi = h2.astype(BF16)
    h_mid = (h2 - h_hi.astype(F32)).astype(BF16)
    dot = lambda a, b: jnp.dot(a, b, preferred_element_type=F32)
    logits = dot(h_hi, wr_ref[0]) + (dot(h_mid, wr_ref[0]) + dot(h_hi, wr_ref[1]))
    lt_ref[...] = logits.T + br_ref[...]


def _merge(oa_g, ob_g, z_mix, x, w_oa, w_ob, w_out, norm_ffn, wr, br_t, tm):
    t, d = x.shape
    row = lambda w, c: pl.BlockSpec((tm, w), lambda i: (i, c))
    full = lambda a: pl.BlockSpec(a.shape, lambda i: (0,) * a.ndim, pipeline_mode=pl.Buffered(1))
    return pl.pallas_call(
        _merge_kernel,
        grid=(t // tm,),
        in_specs=[row(A_V, 0), row(B_V, 0), row(d, 0), row(d, 1), row(d, 0),
                  full(w_oa), full(w_ob), full(w_out), pl.BlockSpec((1, d), lambda i: (0, 0)),
                  full(wr), full(br_t)],
        out_specs=[row(d, 0), row(d, 0), pl.BlockSpec((LANE, tm), lambda i: (0, i))],
        out_shape=[jax.ShapeDtypeStruct((t, d), F32), jax.ShapeDtypeStruct((t, d), F32),
                   jax.ShapeDtypeStruct((LANE, t), F32)],
        compiler_params=_cparams(("parallel",), 56),
    )(oa_g, ob_g, z_mix, z_mix, x, w_oa, w_ob, w_out, norm_ffn.reshape(1, d).astype(F32), wr, br_t)


SEG = 256


def _route_kernel(lt_ref, pos_ref, gate_ref, blk_ref, oh_ref):
    t = lt_ref.shape[1]
    rid8 = lax.broadcasted_iota(jnp.int32, (8, t), 0)
    lg = jnp.where(rid8 < N_GROUPS, lt_ref[0:8, :], -jnp.inf)
    gmax = jnp.max(lg, axis=0, keepdims=True)
    g_idx = jnp.min(jnp.where(lg == gmax, rid8, 8), axis=0, keepdims=True)
    p_top = 1.0 / jnp.sum(jnp.exp(lg - gmax), axis=0, keepdims=True)

    les = jnp.zeros((EXP_PER_GROUP, t), F32)
    for g in range(N_GROUPS):
        les = jnp.where(g_idx == g, lt_ref[8 + g * EXP_PER_GROUP:8 + (g + 1) * EXP_PER_GROUP, :], les)
    m1 = jnp.max(les, axis=0, keepdims=True)
    i1 = jnp.min(jnp.where(les == m1, rid8, 8), axis=0, keepdims=True)
    les2 = jnp.where(rid8 == i1, -jnp.inf, les)
    m2 = jnp.max(les2, axis=0, keepdims=True)
    i2 = jnp.min(jnp.where(les2 == m2, rid8, 8), axis=0, keepdims=True)
    r = jnp.exp(m2 - m1)
    gate_ref[...] = jnp.concatenate([p_top / (1.0 + r), p_top * r / (1.0 + r),
                                     jnp.zeros((LANE - 2, t), F32)], axis=0).T
    e1 = g_idx * EXP_PER_GROUP + i1
    e2 = g_idx * EXP_PER_GROUP + i2

    rid32 = lax.broadcasted_iota(jnp.int32, (N_EXPERTS, t), 0)
    oh_ref[0] = jnp.where(rid32 == e1, 1.0, 0.0)
    oh_ref[1] = jnp.where(rid32 == e2, 1.0, 0.0)

    ui = lax.broadcasted_iota(jnp.int32, (SEG, SEG), 0)
    uj = lax.broadcasted_iota(jnp.int32, (SEG, SEG), 1)
    upper = jnp.where(ui < uj, 1.0, 0.0).astype(BF16)
    carry = jnp.zeros((N_EXPERTS, 1), F32)
    ranks = []
    for kk in range(2):
        segs = []
        for sg in range(t // SEG):
            oh = oh_ref[kk, :, sg * SEG:(sg + 1) * SEG]
            pre = jnp.dot(oh.astype(BF16), upper, preferred_element_type=F32) + carry
            segs.append(jnp.sum(oh * pre, axis=0, keepdims=True))
            carry = carry + jnp.sum(oh, axis=1, keepdims=True)
        ranks.append(jnp.concatenate(segs, axis=1))
    counts = carry
    nblk = jnp.floor((counts + (ROW_BLOCK - 1)) * (1.0 / ROW_BLOCK))
    li = lax.broadcasted_iota(jnp.int32, (N_EXPERTS, N_EXPERTS), 0)
    lj = lax.broadcasted_iota(jnp.int32, (N_EXPERTS, N_EXPERTS), 1)
    nb_b = jnp.broadcast_to(nblk, (N_EXPERTS, LANE))
    start_blk = _fdot(jnp.where(lj < li, 1.0, 0.0), nb_b)[:, 0:1]
    end_blk = start_blk + nblk
    start_row = start_blk * ROW_BLOCK

    pos_ref[...] = jnp.zeros_like(pos_ref)
    for kk in range(2):
        base = jnp.sum(oh_ref[kk] * start_row, axis=0, keepdims=True)
        pos_ref[kk:kk + 1, :] = (base + ranks[kk]).astype(jnp.int32)

    nb = blk_ref.shape[1]
    bid = lax.broadcasted_iota(jnp.int32, (N_EXPERTS, nb), 1).astype(F32)
    be = jnp.sum(jnp.where(end_blk <= bid, 1.0, 0.0), axis=0, keepdims=True)
    blk_ref[...] = jnp.zeros_like(blk_ref)
    blk_ref[0:1, :] = jnp.minimum(be, N_EXPERTS - 1.0).astype(jnp.int32)
    blk_ref[1:2, :] = jnp.broadcast_to(end_blk[N_EXPERTS - 1:N_EXPERTS, :], (1, nb)).astype(jnp.int32)
    ends = jnp.sum(jnp.where(li == lj, end_blk, 0.0), axis=0, keepdims=True)
    blk_ref[2:3, 0:N_EXPERTS] = ends.astype(jnp.int32)


def _route(lt, n_blk):
    t = lt.shape[1]
    nb = -(-n_blk // LANE) * LANE
    return pl.pallas_call(
        _route_kernel,
        out_shape=[jax.ShapeDtypeStruct((8, t), jnp.int32), jax.ShapeDtypeStruct((t, LANE), F32),
                   jax.ShapeDtypeStruct((8, nb), jnp.int32)],
        scratch_shapes=[pltpu.VMEM((2, N_EXPERTS, t), F32)],
        compiler_params=pltpu.CompilerParams(vmem_limit_bytes=48 * 2 ** 20),
    )(lt)


def _scatter_kernel(pos_ref, ends_ref, nu_ref, h_ref, xs_ref, zbuf, sem, zsem):
    tm = h_ref.shape[0]
    t = pl.num_programs(0) * tm
    base = pl.program_id(0) * tm
    n_blk = xs_ref.shape[0] // ROW_BLOCK

    def zero_block(blk):
        return pltpu.make_async_copy(zbuf, xs_ref.at[pl.ds(blk * ROW_BLOCK, ROW_BLOCK), :], zsem)

    def for_each_zero_block(fn):
        for e in range(N_EXPERTS):
            first = ends_ref[e - 1] if e else 0

            @pl.when(ends_ref[e] > first)
            def _():
                fn(zero_block(ends_ref[e] - 1))

        def tail(blk, c):
            fn(zero_block(blk))
            return c
        lax.fori_loop(nu_ref[0], n_blk, tail, 0)

    @pl.when(pl.program_id(0) == 0)
    def _():
        zbuf[...] = jnp.zeros_like(zbuf)
        for_each_zero_block(lambda c: c.start())
        for_each_zero_block(lambda c: c.wait())

    def row_copy(r, p):
        return pltpu.make_async_copy(h_ref.at[pl.ds(r, 1), :], xs_ref.at[pl.ds(p, 1), :], sem)

    for r in range(tm):
        for kk in range(2):
            row_copy(r, pos_ref[kk * t + base + r]).start()
    for r in range(2 * tm):
        row_copy(0, 0).wait()


def _scatter_rows(pos_flat, ends, n_used, h2, n_rows, tm):
    t, d = h2.shape
    return pl.pallas_call(
        _scatter_kernel,
        grid_spec=pltpu.PrefetchScalarGridSpec(
            num_scalar_prefetch=3,
            grid=(t // tm,),
            in_specs=[pl.BlockSpec((tm, d), lambda i, *_: (i, 0))],
            out_specs=pl.BlockSpec(memory_space=pl.ANY),
            scratch_shapes=[pltpu.VMEM((ROW_BLOCK, d), h2.dtype), pltpu.SemaphoreType.DMA(()),
                            pltpu.SemaphoreType.DMA(())],
        ),
        out_shape=jax.ShapeDtypeStruct((n_rows, d), h2.dtype),
        compiler_params=_cparams(("arbitrary",), 32),
    )(pos_flat, ends, n_used, h2)


def _expert_kernel(be_ref, nu_ref, x_ref, w1_ref, w3_ref, w2_ref, o_ref):
    b = pl.program_id(0)

    @pl.when(b < nu_ref[0])
    def _():
        xb = x_ref[...].astype(BF16)
        a = jnp.dot(xb, w1_ref[0], preferred_element_type=F32)
        g = jnp.dot(xb, w3_ref[0], preferred_element_type=F32)
        o_ref[...] = jnp.dot((_silu(a) * g).astype(BF16), w2_ref[0], preferred_element_type=F32)

    @pl.when(b >= nu_ref[0])
    def _():
        o_ref[...] = jnp.zeros_like(o_ref)


def _experts(blk_exp, n_used, xs, w1, w3, w2):
    n_rows, d = xs.shape
    n_blk = n_rows // ROW_BLOCK
    last = lambda b, nu: jnp.minimum(b, jnp.maximum(nu[0] - 1, 0))
    rows = lambda b, be, nu: (last(b, nu), 0)
    wsel = lambda b, be, nu: (be[last(b, nu)], 0, 0)
    return pl.pallas_call(
        _expert_kernel,
        grid_spec=pltpu.PrefetchScalarGridSpec(
            num_scalar_prefetch=2,
            grid=(n_blk,),
            in_specs=[pl.BlockSpec((ROW_BLOCK, d), rows),
                      pl.BlockSpec((1, d, D_FF), wsel), pl.BlockSpec((1, d, D_FF), wsel),
                      pl.BlockSpec((1, D_FF, d), wsel)],
            out_specs=pl.BlockSpec((ROW_BLOCK, d), lambda b, be, nu: (b, 0)),
        ),
        out_shape=jax.ShapeDtypeStruct((n_rows, d), F32),
        compiler_params=_cparams(("arbitrary",), 48),
    )(blk_exp, n_used, xs, w1, w3, w2)


def _combine_kernel(pos_ref, x2_ref, gt_ref, nw_ref, yb_ref, o_ref, buf, sem, *, final):
    tm = x2_ref.shape[0]
    n = pl.num_programs(0)
    t = n * tm
    i = pl.program_id(0)

    def row_copy(slot, r, kk, p):
        return pltpu.make_async_copy(yb_ref.at[pl.ds(p, 1), :], buf.at[slot, kk, pl.ds(r, 1), :], sem.at[slot])

    def issue(step, slot):
        for r in range(tm):
            for kk in range(2):
                row_copy(slot, r, kk, pos_ref[kk * t + step * tm + r]).start()

    @pl.when(i == 0)
    def _():
        issue(0, 0)

    @pl.when(i + 1 < n)
    def _():
        issue(i + 1, (i + 1) % 2)

    slot = i % 2
    for r in range(2 * tm):
        row_copy(slot, 0, 0, 0).wait()

    gt = gt_ref[...]
    y = x2_ref[...] + gt[:, 0:1] * buf[slot, 0] + gt[:, 1:2] * buf[slot, 1]
    if final:
        y = y * lax.rsqrt(jnp.mean(y * y, axis=-1, keepdims=True) + EPS) * nw_ref[...]
    o_ref[...] = y


def _combine(pos_flat, x2, gates_t, norm_final, yb, tm, final):
    t, d = x2.shape
    return pl.pallas_call(
        functools.partial(_combine_kernel, final=final),
        grid_spec=pltpu.PrefetchScalarGridSpec(
            num_scalar_prefetch=1,
            grid=(t // tm,),
            in_specs=[pl.BlockSpec((tm, d), lambda i, pos: (i, 0)),
                      pl.BlockSpec((tm, LANE), lambda i, pos: (i, 0)),
                      pl.BlockSpec((1, d), lambda i, pos: (0, 0)),
                      pl.BlockSpec(memory_space=pl.ANY)],
            out_specs=pl.BlockSpec((tm, d), lambda i, pos: (i, 0)),
            scratch_shapes=[pltpu.VMEM((2, 2, tm, d), F32), pltpu.SemaphoreType.DMA((2,))],
        ),
        out_shape=jax.ShapeDtypeStruct((t, d), F32),
        compiler_params=_cparams(("arbitrary",), 32),
    )(pos_flat, x2, gates_t, norm_final.reshape(1, d).astype(F32), yb)


def _layer(x, norm_mix, w_in, conv_a, a_log, dt_bias, a_norm, w_gk2, b_gk, b_norm,
           w_oa, w_ob, w_out, norm_ffn, w_rg, b_rg, w_re, b_re, w1, w3, w2, norm_final, final):
    t, d = x.shape
    tm = min(512, t)

    a_end = 4 * A_QK
    ga_end = a_end + 2 * A_HEADS
    b_end = ga_end + 2 * B_QK + 2 * B_V
    lr_end = b_end + B_GATE_RANK
    assert a_end % LANE == 0 and b_end % LANE == 2 * A_HEADS

    h = _rmsnorm(x, norm_mix, tm)
    tm_big = min(1024, t)
    wt = w_in.T
    def proj_and_cast(col0, n, tn, w):
        if (n // tn) * (t // tm_big) == w.shape[0]:
            return _proj(h, wt, col0, n, tm_big, tn, side=w)
        return _proj(h, wt, col0, n, tm_big, tn), w.astype(BF16)

    z_a, w1b = proj_and_cast(0, a_end, 1024, w1)
    z_b, w2b = proj_and_cast(ga_end, b_end - ga_end, 768, w2)
    z_mix, w3b = proj_and_cast(lr_end, 2 * d, 1024, w3)
    z_ga = _proj(h, wt, a_end, LANE, tm_big, LANE)
    z_lr = _proj(h, wt, b_end - b_end % LANE, LANE, tm_big, LANE)

    oa_g = _mixer_a(z_a, z_ga, conv_a.astype(F32), a_log, dt_bias, a_norm)
    ob_g = _mixer_b(z_b, z_lr, w_gk2, b_gk, b_norm)

    wr = jnp.zeros((d, LANE), F32).at[:, 0:N_GROUPS].set(w_rg.astype(F32))
    wr = wr.at[:, 8:8 + N_EXPERTS].set(w_re.reshape(d, N_EXPERTS).astype(F32))
    wr_hi = wr.astype(BF16)
    wr = jnp.stack([wr_hi, (wr - wr_hi.astype(F32)).astype(BF16)])
    br_t = jnp.zeros((LANE, 1), F32).at[0:N_GROUPS, 0].set(b_rg.astype(F32))
    br_t = br_t.at[8:8 + N_EXPERTS, 0].set(b_re.reshape(N_EXPERTS).astype(F32))
    x2, h2, lt = _merge(oa_g, ob_g, z_mix, x, w_oa.astype(BF16), w_ob.astype(BF16), w_out.astype(BF16),
                        norm_ffn, wr, br_t, min(256, t))

    n_blk = (2 * t + ROW_BLOCK - 1) // ROW_BLOCK + N_EXPERTS
    n_rows = n_blk * ROW_BLOCK
    pos, gates, blk = _route(lt, n_blk)
    pos_flat = pos[0:2].reshape(2 * t)
    blk_exp, n_used, ends = blk[0, :n_blk], blk[1, 0:1], blk[2, :N_EXPERTS]
    xs = _scatter_rows(pos_flat, ends, n_used, h2, n_rows, min(256, t))
    yb = _experts(blk_exp, n_used, xs, w1b, w3b, w2b)
    return _combine(pos_flat, x2, gates, norm_final, yb, min(128, t), final)


def kernel(x, norm_mix, w_in, conv_a, a_log, dt_bias, a_norm, w_gk2, b_gk, b_norm, w_oa, w_ob, w_out,
           norm_ffn, w_rg, b_rg, w_re, b_re, w1, w3, w2, norm_final):
    bsz, seq, d = x.shape
    assert bsz == 1, "one sequence per call"
    depth = norm_mix.shape[0]
    y = x.reshape(seq, d)
    for l in range(depth):
        y = _layer(y, norm_mix[l], w_in[l], conv_a[l], a_log[l], dt_bias[l], a_norm[l], w_gk2[l], b_gk[l],
                   b_norm[l], w_oa[l], w_ob[l], w_out[l], norm_ffn[l], w_rg[l], b_rg[l], w_re[l], b_re[l],
                   w1[l], w3[l], w2[l], norm_final, l == depth - 1)
    return y.reshape(bsz, seq, d)
```

```python
import functools

import jax
import jax.numpy as jnp
from jax import lax
from jax.experimental import pallas as pl
from jax.experimental.pallas import tpu as pltpu

D_MODEL = 2048
CHUNK = 64
EPS = 1e-6
A_HEADS, A_DK, A_DV, A_CONV = 8, 128, 128, 4
A_QK, A_V = A_HEADS * A_DK, A_HEADS * A_DV
B_HEADS, B_DK, B_DV, B_GATE_RANK, B_GATE_NORM = 4, 128, 256, 16, 16.0
B_QK, B_V = B_HEADS * B_DK, B_HEADS * B_DV
N_GROUPS, EXP_PER_GROUP, D_FF = 4, 8, 512
N_EXPERTS = N_GROUPS * EXP_PER_GROUP
ROW_BLOCK = 256
LANE = 128
SUB = 16
TB = 256
A_HG = 8
TA = 128
NEG = -1e30

F32 = jnp.float32
BF16 = jnp.bfloat16
HI = lax.Precision.HIGHEST


def _cparams(sem, vmem_mib):
    return pltpu.CompilerParams(dimension_semantics=sem, vmem_limit_bytes=vmem_mib * 2 ** 20)


def _bdot(a, b):
    return jnp.dot(a.astype(BF16), b.astype(BF16), preferred_element_type=F32)


def _bdot_nt(a, b):
    return lax.dot_general(a.astype(BF16), b.astype(BF16), (((1,), (1,)), ((), ())),
                           preferred_element_type=F32)


def _bdot_tn(a, b):
    return lax.dot_general(a.astype(BF16), b.astype(BF16), (((0,), (0,)), ((), ())),
                           preferred_element_type=F32)


def _fdot(a, b):
    return jnp.dot(a, b, preferred_element_type=F32, precision=HI)


def _sigmoid(x):
    return 1.0 / (1.0 + jnp.exp(-x))


def _silu(x):
    return x * _sigmoid(x)


def _softplus(x):
    return jnp.maximum(x, 0.0) + jnp.log(1.0 + jnp.exp(-jnp.abs(x)))


def _prologue_kernel(x_ref, nw_ref, wga_ref, wlr_ref, alog_ref, dtb_ref, h_ref, zlr_ref, ga_ref, gct_ref,
                     wga_b, wlr_b):
    @pl.when(pl.program_id(0) == 0)
    def _():
        wga_b[...] = wga_ref[...].astype(BF16)
        wlr_b[...] = wlr_ref[...].astype(BF16)

    x = x_ref[...]
    h = (x * lax.rsqrt(jnp.mean(x * x, axis=-1, keepdims=True) + EPS) * nw_ref[...]).astype(BF16)
    h_ref[...] = h
    proj = lambda w: lax.dot_general(h, w, (((1,), (1,)), ((), ())), preferred_element_type=F32)
    zlr_ref[...] = proj(wlr_b[...])
    _gates_a(proj(wga_b[...]), alog_ref[...], dtb_ref[...], ga_ref, gct_ref)


def _prologue(x, norm_mix, wt, a_end, b_end, a_log, dt_bias):
    t, d = x.shape
    assert a_end % LANE == 0 and b_end % LANE == 2 * A_HEADS
    pad = lambda p: jnp.pad(p.astype(F32), (A_HEADS, LANE - 2 * A_HEADS)).reshape(1, LANE)
    const = lambda c: pl.BlockSpec((1, c), lambda i: (0, 0))
    rows = lambda c: pl.BlockSpec((TB, c), lambda i: (i, 0))
    return pl.pallas_call(
        _prologue_kernel,
        grid=(t // TB,),
        in_specs=[rows(d), const(d),
                  pl.BlockSpec((LANE, d), lambda i: (a_end // LANE, 0)),
                  pl.BlockSpec((LANE, d), lambda i: (b_end // LANE, 0)),
                  const(LANE), const(LANE)],
        out_specs=[rows(d), rows(LANE), rows(LANE), pl.BlockSpec((A_HEADS, TB), lambda i: (0, i))],
        out_shape=[jax.ShapeDtypeStruct((t, d), BF16), jax.ShapeDtypeStruct((t, LANE), F32),
                   jax.ShapeDtypeStruct((t, LANE), F32), jax.ShapeDtypeStruct((A_HEADS, t), F32)],
        scratch_shapes=[pltpu.VMEM((LANE, d), BF16), pltpu.VMEM((LANE, d), BF16)],
        compiler_params=_cparams(("arbitrary",), 32),
    )(x, norm_mix.reshape(1, d), wt, wt, pad(a_log), pad(dt_bias))


PREP_ROWS = 256


def _proj_kernel(a_ref, wt_ref, o_ref, wb_ref):
    tn = wt_ref.shape[0]

    @pl.when(pl.program_id(1) == 0)
    def _():
        for r in range(0, tn, PREP_ROWS):
            wb_ref[r:r + PREP_ROWS, :] = wt_ref[r:r + PREP_ROWS, :].astype(BF16)

    o_ref[...] = lax.dot_general(a_ref[...], wb_ref[...], (((1,), (1,)), ((), ())), preferred_element_type=F32)


def _proj(a, wt, col0, n, tm, tn):
    m, k = a.shape
    assert n % tn == 0 and col0 % 8 == 0
    return pl.pallas_call(
        _proj_kernel,
        grid=(n // tn, m // tm),
        in_specs=[pl.BlockSpec((tm, k), lambda j, i: (i, 0)),
                  pl.BlockSpec((pl.Element(tn), pl.Element(k)), lambda j, i: (pl.multiple_of(col0 + j * tn, 8), 0))],
        out_specs=pl.BlockSpec((tm, tn), lambda j, i: (i, j)),
        out_shape=jax.ShapeDtypeStruct((m, n), F32),
        scratch_shapes=[pltpu.VMEM((tn, k), BF16)],
        compiler_params=_cparams(("arbitrary", "arbitrary"), 48),
    )(a, wt)


def _chunk_masks(tb):
    row = lax.broadcasted_iota(jnp.int32, (tb, tb), 0)
    col = lax.broadcasted_iota(jnp.int32, (tb, tb), 1)
    same = (row // CHUNK) == (col // CHUNK)
    return row, col, same, same & (col <= row), same & (col < row)


def _lane_pick(x, idx):
    lane = lax.broadcasted_iota(jnp.int32, x.shape, 1)
    return jnp.sum(jnp.where(lane == idx, x, 0.0), axis=-1, keepdims=True)


def _causal_conv_silu(x_ref, w_ref, halo_ref, cbuf_ref, idx, tb):
    cbuf_ref[idx, 0:8, :] = halo_ref[idx]
    cbuf_ref[idx, 8:8 + tb, :] = x_ref[...]
    halo_ref[idx] = x_ref[tb - 8:tb, :]
    w = w_ref[...]
    acc = w[A_CONV - 1:A_CONV, :] * x_ref[...]
    for j in range(A_CONV - 1):
        off = 8 - (A_CONV - 1) + j
        acc = acc + w[j:j + 1, :] * cbuf_ref[idx, off:off + tb, :]
    return _silu(acc)


def _gates_a(gl, a_log, dt_bias, ga_ref, gct_ref):
    tb = gl.shape[0]
    beta = _sigmoid(gl)
    g = -jnp.exp(a_log) * _softplus(gl + dt_bias)
    _, _, same, causal, _ = _chunk_masks(tb)
    gc = _fdot(jnp.where(causal, 1.0, 0.0), g)
    glast = _fdot(jnp.where(same, 1.0, 0.0), g)
    lane = lax.broadcasted_iota(jnp.int32, (tb, LANE), 1)
    ga_ref[...] = jnp.where(lane < A_HEADS, beta,
                            jnp.where(lane < 2 * A_HEADS, gc, pltpu.roll(glast, A_HEADS, 1)))
    gct_ref[...] = gc.T[A_HEADS:2 * A_HEADS, :]


def _delta_heads(qs, ks, vs, ga, gc_rows, hs, sts, tb):
    n = len(qs)
    rng = range(n)
    _, _, _, causal, strict = _chunk_masks(tb)
    qs = [q * lax.rsqrt(jnp.sum(q * q, axis=-1, keepdims=True) + EPS) * (A_DK ** -0.5) for q in qs]
    ks = [k * lax.rsqrt(jnp.sum(k * k, axis=-1, keepdims=True) + EPS) for k in ks]
    beta = [_lane_pick(ga, h) for h in hs]
    gc = [_lane_pick(ga, h + A_HEADS) for h in hs]
    glast = [_lane_pick(ga, h + 2 * A_HEADS) for h in hs]
    decay = [jnp.exp(jnp.where(causal, gc[i] - gc_rows[i], NEG)) for i in rng]
    kb = [ks[i] * beta[i] for i in rng]

    n_pow = [jnp.where(strict, _bdot_nt(kb[i], ks[i]) * decay[i], 0.0) * -1.0 for i in rng]
    t_mat = list(n_pow)
    lvl = 2
    while lvl < CHUNK:
        n_pow = [_bdot(m, m) for m in n_pow]
        t_mat = [t_mat[i] + n_pow[i] + _bdot(t_mat[i], n_pow[i]) for i in rng]
        lvl *= 2

    egc = [jnp.exp(g) for g in gc]
    rhs = [jnp.concatenate([vs[i] * beta[i], kb[i] * egc[i]], axis=1) for i in rng]
    uw = [rhs[i] + _bdot(t_mat[i], rhs[i]) for i in rng]
    qk = [_bdot_nt(qs[i], ks[i]) * decay[i] for i in rng]
    qkuw = [_bdot(qk[i], uw[i]) for i in rng]
    o_local = [x[:, :A_DV] for x in qkuw]
    q_eff = [qs[i] * egc[i] - qkuw[i][:, A_DV:] for i in rng]
    k_dec = [ks[i] * jnp.exp(glast[i] - gc[i]) for i in rng]
    eg_last = [jnp.exp(g) for g in glast]

    sts = list(sts)
    outs = [[] for _ in rng]
    for c in range(tb // CHUNK):
        lo, hi = c * CHUNK, (c + 1) * CHUNK
        bg = [_bdot_tn(uw[i][lo:hi], k_dec[i][lo:hi]) for i in rng]
        for i in rng:
            outs[i].append(o_local[i][lo:hi] + _bdot_nt(q_eff[i][lo:hi], sts[i]))
        sts = [sts[i] * eg_last[i][lo:lo + 1, :] + bg[i][:A_DV] - _bdot(sts[i], bg[i][A_DV:]) for i in rng]
    return [jnp.concatenate(o, axis=0) for o in outs], sts


def _mixer_a_kernel(xq_ref, xk_ref, xv_ref, z_ref, ga_ref, gct_ref, wq_ref, wk_ref, wv_ref,
                    anorm_ref, o_ref, halo_ref, cbuf_ref, state_ref):
    tb = xq_ref.shape[0]
    hg = xq_ref.shape[1] // A_DK

    @pl.when(pl.program_id(1) == 0)
    def _():
        halo_ref[...] = jnp.zeros_like(halo_ref)
        state_ref[...] = jnp.zeros_like(state_ref)

    q = _causal_conv_silu(xq_ref, wq_ref, halo_ref, cbuf_ref, 0, tb)
    k = _causal_conv_silu(xk_ref, wk_ref, halo_ref, cbuf_ref, 1, tb)
    v = _causal_conv_silu(xv_ref, wv_ref, halo_ref, cbuf_ref, 2, tb)
    ga = ga_ref[...]
    hs = [pl.program_id(0) * hg + j for j in range(hg)]
    sls = [slice(j * A_DK, (j + 1) * A_DK) for j in range(hg)]
    outs, sts = _delta_heads([q[:, s] for s in sls], [k[:, s] for s in sls], [v[:, s] for s in sls], ga,
                             [gct_ref[pl.ds(h, 1), :] for h in hs], hs, [state_ref[j] for j in range(hg)], tb)
    for j in range(hg):
        state_ref[j] = sts[j]
        o = outs[j]
        o = o * lax.rsqrt(jnp.mean(o * o, axis=-1, keepdims=True) + EPS) * anorm_ref[...]
        o_ref[:, sls[j]] = (o * _silu(z_ref[:, sls[j]])).astype(o_ref.dtype)


def _mixer_a(z_a, ga, gct, conv_a, a_norm):
    t = z_a.shape[0]
    ng = A_HEADS // A_HG
    wid = A_HG * A_DK
    blk = lambda off: pl.BlockSpec((TA, wid), lambda g, i: (i, off + g))
    cblk = lambda off: pl.BlockSpec((A_CONV, wid), lambda g, i: (0, off + g))
    return pl.pallas_call(
        _mixer_a_kernel,
        grid=(ng, t // TA),
        in_specs=[blk(0), blk(ng), blk(2 * ng), blk(3 * ng),
                  pl.BlockSpec((TA, LANE), lambda g, i: (i, 0)),
                  pl.BlockSpec((A_HEADS, TA), lambda g, i: (0, i)),
                  cblk(0), cblk(ng), cblk(2 * ng), pl.BlockSpec((1, LANE), lambda g, i: (0, 0))],
        out_specs=pl.BlockSpec((TA, wid), lambda g, i: (i, g)),
        out_shape=jax.ShapeDtypeStruct((t, A_V), BF16),
        scratch_shapes=[pltpu.VMEM((3, 8, wid), F32), pltpu.VMEM((3, 8 + TA, wid), F32),
                        pltpu.VMEM((A_HG, A_DV, A_DK), F32)],
        compiler_params=_cparams(("parallel", "arbitrary"), 48),
    )(z_a, z_a, z_a, z_a, ga, gct, conv_a, conv_a, conv_a,
      a_norm.reshape(1, A_DV).astype(F32))


def _cumsum_rows(mask, x):
    hi = x.astype(BF16)
    r1 = x - hi.astype(F32)
    mid = r1.astype(BF16)
    lo = (r1 - mid.astype(F32)).astype(BF16)
    dot = lambda p: jnp.dot(mask, p, preferred_element_type=F32)
    return dot(hi) + dot(mid) + dot(lo)


def _gla_chunk_scores(qc, kc, bc):
    rid = lax.broadcasted_iota(jnp.int32, (SUB, B_DK), 0)
    rid_lo = lax.broadcasted_iota(jnp.int32, (SUB // 2, B_DK), 0) + SUB // 2
    lane_c = lax.broadcasted_iota(jnp.int32, (SUB, CHUNK), 1)
    crow = lax.broadcasted_iota(jnp.int32, (CHUNK, B_DK), 0)
    half = SUB // 2
    rows = []
    for si in range(CHUNK // SUB):
        r0 = si * SUB
        qb, bb = qc[r0:r0 + SUB], bc[r0:r0 + SUB]
        ys = []
        for j in range(SUB):
            bj = bc[r0 + j:r0 + j + 1]
            if j < half:
                ys.append(qb * jnp.exp(jnp.where(rid >= j, bb - bj, NEG)))
            else:
                ys.append(jnp.zeros((half, B_DK), F32))
                ys.append(qb[half:] * jnp.exp(jnp.where(rid_lo >= j, bb[half:] - bj, NEG)))
        r = _bdot_nt(jnp.concatenate(ys, axis=0), kc)
        blk = jnp.zeros((SUB, CHUNK), F32)
        for j in range(SUB):
            blk = jnp.where(lane_c == r0 + j, r[j * SUB:(j + 1) * SUB], blk)
        if si > 0:
            bref = bc[r0:r0 + 1]
            qt = qb * jnp.exp(bb - bref)
            kt = kc * jnp.exp(jnp.where(crow < r0, bref - bc, NEG))
            blk = blk + _bdot_nt(qt, kt)
        rows.append(blk)
    return jnp.concatenate(rows, axis=0)


def _mixer_b_kernel(q_ref, k_ref, v_ref, zg_ref, tail_ref, wg_ref, bgk_ref, bnorm_ref, o_ref, state_ref):
    tb = q_ref.shape[0]
    nh = q_ref.shape[1] // B_DK

    @pl.when(pl.program_id(0) == 0)
    def _():
        state_ref[...] = jnp.zeros_like(state_ref)

    x = _fdot(tail_ref[...], wg_ref[...]) + bgk_ref[...]
    gk = -_softplus(-x) * (1.0 / B_GATE_NORM)
    _, _, _, causal, _ = _chunk_masks(tb)
    b_all = _cumsum_rows(jnp.where(causal, 1.0, 0.0).astype(BF16), gk)

    sts = [state_ref[h] for h in range(nh)]
    outs = [[] for _ in range(nh)]
    for c in range(tb // CHUNK):
        lo, hi = c * CHUNK, (c + 1) * CHUNK
        for h in range(nh):
            qc = q_ref[lo:hi, h * B_DK:(h + 1) * B_DK] * (B_DK ** -0.5)
            kc = k_ref[lo:hi, h * B_DK:(h + 1) * B_DK]
            vc = v_ref[lo:hi, h * B_DV:(h + 1) * B_DV]
            bc = b_all[lo:hi, h * B_DK:(h + 1) * B_DK]
            a_c = _gla_chunk_scores(qc, kc, bc)
            bl = bc[CHUNK - 1:CHUNK]
            outs[h].append(_bdot(a_c, vc) + _bdot_nt(qc * jnp.exp(bc), sts[h]))
            sts[h] = sts[h] * jnp.exp(bl) + _bdot_tn(vc, kc * jnp.exp(bl - bc))
    for h in range(nh):
        state_ref[h] = sts[h]
        o = jnp.concatenate(outs[h], axis=0)
        o = o * lax.rsqrt(jnp.mean(o * o, axis=-1, keepdims=True) + EPS) * bnorm_ref[...]
        sl = slice(h * B_DV, (h + 1) * B_DV)
        o_ref[:, sl] = (o * _silu(zg_ref[:, sl])).astype(o_ref.dtype)


def _mixer_b(z_b, z_tail, w_gk2, b_gk, b_norm):
    t = z_b.shape[0]
    wg = jnp.zeros((LANE, B_QK), F32).at[2 * A_HEADS:2 * A_HEADS + B_GATE_RANK].set(w_gk2.astype(F32))
    col = lambda w, off: pl.BlockSpec((TB, w), lambda i: (i, off // w))
    full = lambda r, c: pl.BlockSpec((r, c), lambda i: (0, 0))
    return pl.pallas_call(
        _mixer_b_kernel,
        grid=(t // TB,),
        in_specs=[col(B_QK, 0), col(B_QK, B_QK), col(B_V, 2 * B_QK), col(B_V, 2 * B_QK + B_V), col(LANE, 0),
                  full(LANE, B_QK), full(1, B_QK), full(1, B_DV)],
        out_specs=pl.BlockSpec((TB, B_V), lambda i: (i, 0)),
        out_shape=jax.ShapeDtypeStruct((t, B_V), BF16),
        scratch_shapes=[pltpu.VMEM((B_HEADS, B_DV, B_DK), F32)],
        compiler_params=_cparams(("arbitrary",), 48),
    )(z_b, z_b, z_b, z_b, z_tail, wg, b_gk.reshape(1, B_QK).astype(F32),
      b_norm.reshape(1, B_DV).astype(F32))


def _merge_kernel(oa_ref, ob_ref, ma_ref, mb_ref, x_ref, woa_ref, wob_ref, wout_ref, nf_ref,
                  wr_ref, br_ref, x2_ref, h2_ref, lt_ref):
    ya = jnp.dot(oa_ref[...], woa_ref[...], preferred_element_type=F32)
    yb = jnp.dot(ob_ref[...], wob_ref[...], preferred_element_type=F32)
    m = _sigmoid(ma_ref[...]) * ya + _sigmoid(mb_ref[...]) * yb
    x2 = x_ref[...] + jnp.dot(m.astype(BF16), wout_ref[...], preferred_element_type=F32)
    x2_ref[...] = x2
    h2 = x2 * lax.rsqrt(jnp.mean(x2 * x2, axis=-1, keepdims=True) + EPS) * nf_ref[...]
    h2_ref[...] = h2
    h_hi = h2.astype(BF16)
    h_mid = (h2 - h_hi.astype(F32)).astype(BF16)
    dot = lambda a, b: jnp.dot(a, b, preferred_element_type=F32)
    logits = dot(h_hi, wr_ref[0]) + (dot(h_mid, wr_ref[0]) + dot(h_hi, wr_ref[1]))
    lt_ref[...] = logits.T + br_ref[...]


def _merge(oa_g, ob_g, z_mix, x, w_oa, w_ob, w_out, norm_ffn, wr, br_t, tm):
    t, d = x.shape
    row = lambda w, c: pl.BlockSpec((tm, w), lambda i: (i, c))
    full = lambda a: pl.BlockSpec(a.shape, lambda i: (0,) * a.ndim, pipeline_mode=pl.Buffered(1))
    return pl.pallas_call(
        _merge_kernel,
        grid=(t // tm,),
        in_specs=[row(A_V, 0), row(B_V, 0), row(d, 0), row(d, 1), row(d, 0),
                  full(w_oa), full(w_ob), full(w_out), pl.BlockSpec((1, d), lambda i: (0, 0)),
                  full(wr), full(br_t)],
        out_specs=[row(d, 0), row(d, 0), pl.BlockSpec((LANE, tm), lambda i: (0, i))],
        out_shape=[jax.ShapeDtypeStruct((t, d), F32), jax.ShapeDtypeStruct((t, d), F32),
                   jax.ShapeDtypeStruct((LANE, t), F32)],
        compiler_params=_cparams(("parallel",), 56),
    )(oa_g, ob_g, z_mix, z_mix, x, w_oa, w_ob, w_out, norm_ffn.reshape(1, d).astype(F32), wr, br_t)


SEG = 256


def _route_kernel(lt_ref, pos_ref, gate_ref, blk_ref, oh_ref):
    t = lt_ref.shape[1]
    rid8 = lax.broadcasted_iota(jnp.int32, (8, t), 0)
    lg = jnp.where(rid8 < N_GROUPS, lt_ref[0:8, :], -jnp.inf)
    gmax = jnp.max(lg, axis=0, keepdims=True)
    g_idx = jnp.min(jnp.where(lg == gmax, rid8, 8), axis=0, keepdims=True)
    p_top = 1.0 / jnp.sum(jnp.exp(lg - gmax), axis=0, keepdims=True)

    les = jnp.zeros((EXP_PER_GROUP, t), F32)
    for g in range(N_GROUPS):
        les = jnp.where(g_idx == g, lt_ref[8 + g * EXP_PER_GROUP:8 + (g + 1) * EXP_PER_GROUP, :], les)
    m1 = jnp.max(les, axis=0, keepdims=True)
    i1 = jnp.min(jnp.where(les == m1, rid8, 8), axis=0, keepdims=True)
    les2 = jnp.where(rid8 == i1, -jnp.inf, les)
    m2 = jnp.max(les2, axis=0, keepdims=True)
    i2 = jnp.min(jnp.where(les2 == m2, rid8, 8), axis=0, keepdims=True)
    r = jnp.exp(m2 - m1)
    gate_ref[...] = jnp.concatenate([p_top / (1.0 + r), p_top * r / (1.0 + r),
                                     jnp.zeros((LANE - 2, t), F32)], axis=0).T
    e1 = g_idx * EXP_PER_GROUP + i1
    e2 = g_idx * EXP_PER_GROUP + i2

    rid32 = lax.broadcasted_iota(jnp.int32, (N_EXPERTS, t), 0)
    oh_ref[0] = jnp.where(rid32 == e1, 1.0, 0.0)
    oh_ref[1] = jnp.where(rid32 == e2, 1.0, 0.0)

    ui = lax.broadcasted_iota(jnp.int32, (SEG, SEG), 0)
    uj = lax.broadcasted_iota(jnp.int32, (SEG, SEG), 1)
    upper = jnp.where(ui < uj, 1.0, 0.0).astype(BF16)
    carry = jnp.zeros((N_EXPERTS, 1), F32)
    ranks = []
    for kk in range(2):
        segs = []
        for sg in range(t // SEG):
            oh = oh_ref[kk, :, sg * SEG:(sg + 1) * SEG]
            pre = jnp.dot(oh.astype(BF16), upper, preferred_element_type=F32) + carry
            segs.append(jnp.sum(oh * pre, axis=0, keepdims=True))
            carry = carry + jnp.sum(oh, axis=1, keepdims=True)
        ranks.append(jnp.concatenate(segs, axis=1))
    counts = carry
    nblk = jnp.floor((counts + (ROW_BLOCK - 1)) * (1.0 / ROW_BLOCK))
    li = lax.broadcasted_iota(jnp.int32, (N_EXPERTS, N_EXPERTS), 0)
    lj = lax.broadcasted_iota(jnp.int32, (N_EXPERTS, N_EXPERTS), 1)
    nb_b = jnp.broadcast_to(nblk, (N_EXPERTS, LANE))
    start_blk = _fdot(jnp.where(lj < li, 1.0, 0.0), nb_b)[:, 0:1]
    end_blk = start_blk + nblk
    start_row = start_blk * ROW_BLOCK

    pos_ref[...] = jnp.zeros_like(pos_ref)
    for kk in range(2):
        base = jnp.sum(oh_ref[kk] * start_row, axis=0, keepdims=True)
        pos_ref[kk:kk + 1, :] = (base + ranks[kk]).astype(jnp.int32)

    nb = blk_ref.shape[1]
    bid = lax.broadcasted_iota(jnp.int32, (N_EXPERTS, nb), 1).astype(F32)
    be = jnp.sum(jnp.where(end_blk <= bid, 1.0, 0.0), axis=0, keepdims=True)
    blk_ref[...] = jnp.zeros_like(blk_ref)
    blk_ref[0:1, :] = jnp.minimum(be, N_EXPERTS - 1.0).astype(jnp.int32)
    blk_ref[1:2, :] = jnp.broadcast_to(end_blk[N_EXPERTS - 1:N_EXPERTS, :], (1, nb)).astype(jnp.int32)
    ends = jnp.sum(jnp.where(li == lj, end_blk, 0.0), axis=0, keepdims=True)
    blk_ref[2:3, 0:N_EXPERTS] = ends.astype(jnp.int32)


def _route(lt, n_blk):
    t = lt.shape[1]
    nb = -(-n_blk // LANE) * LANE
    return pl.pallas_call(
        _route_kernel,
        out_shape=[jax.ShapeDtypeStruct((8, t), jnp.int32), jax.ShapeDtypeStruct((t, LANE), F32),
                   jax.ShapeDtypeStruct((8, nb), jnp.int32)],
        scratch_shapes=[pltpu.VMEM((2, N_EXPERTS, t), F32)],
        compiler_params=pltpu.CompilerParams(vmem_limit_bytes=48 * 2 ** 20),
    )(lt)


def _scatter_kernel(pos_ref, ends_ref, nu_ref, h_ref, xs_ref, zbuf, sem, zsem):
    tm = h_ref.shape[0]
    t = pl.num_programs(0) * tm
    base = pl.program_id(0) * tm
    n_blk = xs_ref.shape[0] // ROW_BLOCK

    def zero_block(blk):
        return pltpu.make_async_copy(zbuf, xs_ref.at[pl.ds(blk * ROW_BLOCK, ROW_BLOCK), :], zsem)

    def for_each_zero_block(fn):
        for e in range(N_EXPERTS):
            first = ends_ref[e - 1] if e else 0

            @pl.when(ends_ref[e] > first)
            def _():
                fn(zero_block(ends_ref[e] - 1))

        def tail(blk, c):
            fn(zero_block(blk))
            return c
        lax.fori_loop(nu_ref[0], n_blk, tail, 0)

    @pl.when(pl.program_id(0) == 0)
    def _():
        zbuf[...] = jnp.zeros_like(zbuf)
        for_each_zero_block(lambda c: c.start())
        for_each_zero_block(lambda c: c.wait())

    def row_copy(r, p):
        return pltpu.make_async_copy(h_ref.at[pl.ds(r, 1), :], xs_ref.at[pl.ds(p, 1), :], sem)

    for r in range(tm):
        for kk in range(2):
            row_copy(r, pos_ref[kk * t + base + r]).start()
    for r in range(2 * tm):
        row_copy(0, 0).wait()


def _scatter_rows(pos_flat, ends, n_used, h2, n_rows, tm):
    t, d = h2.shape
    return pl.pallas_call(
        _scatter_kernel,
        grid_spec=pltpu.PrefetchScalarGridSpec(
            num_scalar_prefetch=3,
            grid=(t // tm,),
            in_specs=[pl.BlockSpec((tm, d), lambda i, *_: (i, 0))],
            out_specs=pl.BlockSpec(memory_space=pl.ANY),
            scratch_shapes=[pltpu.VMEM((ROW_BLOCK, d), h2.dtype), pltpu.SemaphoreType.DMA(()),
                            pltpu.SemaphoreType.DMA(())],
        ),
        out_shape=jax.ShapeDtypeStruct((n_rows, d), h2.dtype),
        compiler_params=_cparams(("arbitrary",), 32),
    )(pos_flat, ends, n_used, h2)


def _expert_kernel(be_ref, nu_ref, ends_ref, x_ref, w1_hbm, w3_hbm, w2_hbm, o_ref,
                   w1s, w3s, w2s, w1b, w3b, w2b, sem):
    b = pl.program_id(0)
    e = be_ref[b]
    prev = be_ref[jnp.maximum(b - 1, 0)]
    changed = jnp.logical_or(b == 0, e != prev)

    def weight_copies(ex):
        return (pltpu.make_async_copy(w1_hbm.at[ex], w1s, sem.at[0]),
                pltpu.make_async_copy(w3_hbm.at[ex], w3s, sem.at[1]),
                pltpu.make_async_copy(w2_hbm.at[ex], w2s, sem.at[2]))

    @pl.when(b == 0)
    def _():
        for c in weight_copies(e):
            c.start()

    @pl.when(jnp.logical_and(changed, b < nu_ref[0]))
    def _():
        for c in weight_copies(e):
            c.wait()
        w1b[...] = w1s[...].astype(BF16)
        w3b[...] = w3s[...].astype(BF16)
        w2b[...] = w2s[...].astype(BF16)
        nxt = ends_ref[e]

        @pl.when(nxt < nu_ref[0])
        def _():
            for c in weight_copies(be_ref[nxt]):
                c.start()

    @pl.when(b < nu_ref[0])
    def _():
        xb = x_ref[...].astype(BF16)
        a = jnp.dot(xb, w1b[...], preferred_element_type=F32)
        g = jnp.dot(xb, w3b[...], preferred_element_type=F32)
        o_ref[...] = jnp.dot((_silu(a) * g).astype(BF16), w2b[...], preferred_element_type=F32)

    @pl.when(b >= nu_ref[0])
    def _():
        o_ref[...] = jnp.zeros_like(o_ref)


def _experts(blk_exp, n_used, ends, xs, w1, w3, w2):
    n_rows, d = xs.shape
    n_blk = n_rows // ROW_BLOCK
    rows = lambda b, be, nu, en: (jnp.minimum(b, jnp.maximum(nu[0] - 1, 0)), 0)
    hbm = pl.BlockSpec(memory_space=pl.ANY)
    return pl.pallas_call(
        _expert_kernel,
        grid_spec=pltpu.PrefetchScalarGridSpec(
            num_scalar_prefetch=3,
            grid=(n_blk,),
            in_specs=[pl.BlockSpec((ROW_BLOCK, d), rows), hbm, hbm, hbm],
            out_specs=pl.BlockSpec((ROW_BLOCK, d), lambda b, be, nu, en: (b, 0)),
            scratch_shapes=[pltpu.VMEM((d, D_FF), w1.dtype), pltpu.VMEM((d, D_FF), w3.dtype),
                            pltpu.VMEM((D_FF, d), w2.dtype),
                            pltpu.VMEM((d, D_FF), BF16), pltpu.VMEM((d, D_FF), BF16),
                            pltpu.VMEM((D_FF, d), BF16), pltpu.SemaphoreType.DMA((3,))],
        ),
        out_shape=jax.ShapeDtypeStruct((n_rows, d), F32),
        compiler_params=_cparams(("arbitrary",), 56),
    )(blk_exp, n_used, ends, xs, w1, w3, w2)


def _combine_kernel(pos_ref, x2_ref, gt_ref, nw_ref, yb_ref, o_ref, buf, sem, *, final):
    tm = x2_ref.shape[0]
    n = pl.num_programs(0)
    t = n * tm
    i = pl.program_id(0)

    def row_copy(slot, r, kk, p):
        return pltpu.make_async_copy(yb_ref.at[pl.ds(p, 1), :], buf.at[slot, kk, pl.ds(r, 1), :], sem.at[slot])

    def issue(step, slot):
        for r in range(tm):
            for kk in range(2):
                row_copy(slot, r, kk, pos_ref[kk * t + step * tm + r]).start()

    @pl.when(i == 0)
    def _():
        issue(0, 0)

    @pl.when(i + 1 < n)
    def _():
        issue(i + 1, (i + 1) % 2)

    slot = i % 2
    for r in range(2 * tm):
        row_copy(slot, 0, 0, 0).wait()

    gt = gt_ref[...]
    y = x2_ref[...] + gt[:, 0:1] * buf[slot, 0] + gt[:, 1:2] * buf[slot, 1]
    if final:
        y = y * lax.rsqrt(jnp.mean(y * y, axis=-1, keepdims=True) + EPS) * nw_ref[...]
    o_ref[...] = y


def _combine(pos_flat, x2, gates_t, norm_final, yb, tm, final):
    t, d = x2.shape
    return pl.pallas_call(
        functools.partial(_combine_kernel, final=final),
        grid_spec=pltpu.PrefetchScalarGridSpec(
            num_scalar_prefetch=1,
            grid=(t // tm,),
            in_specs=[pl.BlockSpec((tm, d), lambda i, pos: (i, 0)),
                      pl.BlockSpec((tm, LANE), lambda i, pos: (i, 0)),
                      pl.BlockSpec((1, d), lambda i, pos: (0, 0)),
                      pl.BlockSpec(memory_space=pl.ANY)],
            out_specs=pl.BlockSpec((tm, d), lambda i, pos: (i, 0)),
            scratch_shapes=[pltpu.VMEM((2, 2, tm, d), F32), pltpu.SemaphoreType.DMA((2,))],
        ),
        out_shape=jax.ShapeDtypeStruct((t, d), F32),
        compiler_params=_cparams(("arbitrary",), 32),
    )(pos_flat, x2, gates_t, norm_final.reshape(1, d).astype(F32), yb)


def _layer(x, norm_mix, w_in, conv_a, a_log, dt_bias, a_norm, w_gk2, b_gk, b_norm,
           w_oa, w_ob, w_out, norm_ffn, w_rg, b_rg, w_re, b_re, w1, w3, w2, norm_final, final):
    t, d = x.shape

    a_end = 4 * A_QK
    ga_end = a_end + 2 * A_HEADS
    b_end = ga_end + 2 * B_QK + 2 * B_V
    lr_end = b_end + B_GATE_RANK

    tm_big = min(1024, t)
    wt = w_in.T
    h, z_lr, ga, gct = _prologue(x, norm_mix, wt, a_end, b_end, a_log, dt_bias)
    z_a = _proj(h, wt, 0, a_end, tm_big, 1024)
    z_b = _proj(h, wt, ga_end, b_end - ga_end, tm_big, 1024)
    z_mix = _proj(h, wt, lr_end, 2 * d, tm_big, 1024)

    oa_g = _mixer_a(z_a, ga, gct, conv_a.astype(F32), a_norm)
    ob_g = _mixer_b(z_b, z_lr, w_gk2, b_gk, b_norm)

    wr = jnp.zeros((d, LANE), F32).at[:, 0:N_GROUPS].set(w_rg.astype(F32))
    wr = wr.at[:, 8:8 + N_EXPERTS].set(w_re.reshape(d, N_EXPERTS).astype(F32))
    wr_hi = wr.astype(BF16)
    wr = jnp.stack([wr_hi, (wr - wr_hi.astype(F32)).astype(BF16)])
    br_t = jnp.zeros((LANE, 1), F32).at[0:N_GROUPS, 0].set(b_rg.astype(F32))
    br_t = br_t.at[8:8 + N_EXPERTS, 0].set(b_re.reshape(N_EXPERTS).astype(F32))
    x2, h2, lt = _merge(oa_g, ob_g, z_mix, x, w_oa.astype(BF16), w_ob.astype(BF16), w_out.astype(BF16),
                        norm_ffn, wr, br_t, min(256, t))

    n_blk = (2 * t + ROW_BLOCK - 1) // ROW_BLOCK + N_EXPERTS
    n_rows = n_blk * ROW_BLOCK
    pos, gates, blk = _route(lt, n_blk)
    pos_flat = pos[0:2].reshape(2 * t)
    blk_exp, n_used, ends = blk[0, :n_blk], blk[1, 0:1], blk[2, :N_EXPERTS]
    xs = _scatter_rows(pos_flat, ends, n_used, h2, n_rows, min(256, t))
    yb = _experts(blk_exp, n_used, ends, xs, w1, w3, w2)
    return _combine(pos_flat, x2, gates, norm_final, yb, min(256, t), final)


def kernel(x, norm_mix, w_in, conv_a, a_log, dt_bias, a_norm, w_gk2, b_gk, b_norm, w_oa, w_ob, w_out,
           norm_ffn, w_rg, b_rg, w_re, b_re, w1, w3, w2, norm_final):
    bsz, seq, d = x.shape
    assert bsz == 1, "one sequence per call"
    depth = norm_mix.shape[0]
    y = x.reshape(seq, d)
    for l in range(depth):
        y = _layer(y, norm_mix[l], w_in[l], conv_a[l], a_log[l], dt_bias[l], a_norm[l], w_gk2[l], b_gk[l],
                   b_norm[l], w_oa[l], w_ob[l], w_out[l], norm_ffn[l], w_rg[l], b_rg[l], w_re[l], b_re[l],
                   w1[l], w3[l], w2[l], norm_final, l == depth - 1)
    return y.reshape(bsz, seq, d)
```

```python
import functools

import jax
import jax.numpy as jnp
from jax import lax
from jax.experimental import pallas as pl
from jax.experimental.pallas import tpu as pltpu

D_MODEL = 2048
CHUNK = 64
EPS = 1e-6
A_HEADS, A_DK, A_DV, A_CONV = 8, 128, 128, 4
A_QK, A_V = A_HEADS * A_DK, A_HEADS * A_DV
B_HEADS, B_DK, B_DV, B_GATE_RANK, B_GATE_NORM = 4, 128, 256, 16, 16.0
B_QK, B_V = B_HEADS * B_DK, B_HEADS * B_DV
N_GROUPS, EXP_PER_GROUP, D_FF = 4, 8, 512
N_EXPERTS = N_GROUPS * EXP_PER_GROUP
ROW_BLOCK = 256
LANE = 128
SUB = 16
TB = 256
A_HG = 8
TA = 128
NEG = -1e30

F32 = jnp.float32
BF16 = jnp.bfloat16
HI = lax.Precision.HIGHEST


def _cparams(sem, vmem_mib):
    return pltpu.CompilerParams(dimension_semantics=sem, vmem_limit_bytes=vmem_mib * 2 ** 20)


def _bdot(a, b):
    return jnp.dot(a.astype(BF16), b.astype(BF16), preferred_element_type=F32)


def _bdot_nt(a, b):
    return lax.dot_general(a.astype(BF16), b.astype(BF16), (((1,), (1,)), ((), ())),
                           preferred_element_type=F32)


def _bdot_tn(a, b):
    return lax.dot_general(a.astype(BF16), b.astype(BF16), (((0,), (0,)), ((), ())),
                           preferred_element_type=F32)


def _fdot(a, b):
    return jnp.dot(a, b, preferred_element_type=F32, precision=HI)


def _sigmoid(x):
    return 1.0 / (1.0 + jnp.exp(-x))


def _silu(x):
    return x * _sigmoid(x)


def _softplus(x):
    return jnp.maximum(x, 0.0) + jnp.log(1.0 + jnp.exp(-jnp.abs(x)))


def _prologue_kernel(x_ref, nw_ref, wga_ref, wlr_ref, alog_ref, dtb_ref, h_ref, zlr_ref, ga_ref, gct_ref,
                     wga_b, wlr_b):
    @pl.when(pl.program_id(0) == 0)
    def _():
        wga_b[...] = wga_ref[...].astype(BF16)
        wlr_b[...] = wlr_ref[...].astype(BF16)

    x = x_ref[...]
    h = (x * lax.rsqrt(jnp.mean(x * x, axis=-1, keepdims=True) + EPS) * nw_ref[...]).astype(BF16)
    h_ref[...] = h
    proj = lambda w: lax.dot_general(h, w, (((1,), (1,)), ((), ())), preferred_element_type=F32)
    zlr_ref[...] = proj(wlr_b[...])
    _gates_a(proj(wga_b[...]), alog_ref[...], dtb_ref[...], ga_ref, gct_ref)


def _prologue(x, norm_mix, wt, a_end, b_end, a_log, dt_bias):
    t, d = x.shape
    assert a_end % LANE == 0 and b_end % LANE == 2 * A_HEADS
    pad = lambda p: jnp.pad(p.astype(F32), (A_HEADS, LANE - 2 * A_HEADS)).reshape(1, LANE)
    const = lambda c: pl.BlockSpec((1, c), lambda i: (0, 0))
    rows = lambda c: pl.BlockSpec((TB, c), lambda i: (i, 0))
    return pl.pallas_call(
        _prologue_kernel,
        grid=(t // TB,),
        in_specs=[rows(d), const(d),
                  pl.BlockSpec((LANE, d), lambda i: (a_end // LANE, 0)),
                  pl.BlockSpec((LANE, d), lambda i: (b_end // LANE, 0)),
                  const(LANE), const(LANE)],
        out_specs=[rows(d), rows(LANE), rows(LANE), pl.BlockSpec((A_HEADS, TB), lambda i: (0, i))],
        out_shape=[jax.ShapeDtypeStruct((t, d), BF16), jax.ShapeDtypeStruct((t, LANE), F32),
                   jax.ShapeDtypeStruct((t, LANE), F32), jax.ShapeDtypeStruct((A_HEADS, t), F32)],
        scratch_shapes=[pltpu.VMEM((LANE, d), BF16), pltpu.VMEM((LANE, d), BF16)],
        compiler_params=_cparams(("arbitrary",), 32),
    )(x, norm_mix.reshape(1, d), wt, wt, pad(a_log), pad(dt_bias))


PREP_ROWS = 256


def _proj_kernel(a_ref, wt_ref, o_ref, wb_ref):
    tn = wt_ref.shape[0]

    @pl.when(pl.program_id(1) == 0)
    def _():
        for r in range(0, tn, PREP_ROWS):
            wb_ref[r:r + PREP_ROWS, :] = wt_ref[r:r + PREP_ROWS, :].astype(BF16)

    o_ref[...] = lax.dot_general(a_ref[...], wb_ref[...], (((1,), (1,)), ((), ())), preferred_element_type=F32)


def _proj(a, wt, col0, n, tm, tn):
    m, k = a.shape
    assert n % tn == 0 and col0 % 8 == 0
    return pl.pallas_call(
        _proj_kernel,
        grid=(n // tn, m // tm),
        in_specs=[pl.BlockSpec((tm, k), lambda j, i: (i, 0)),
                  pl.BlockSpec((pl.Element(tn), pl.Element(k)), lambda j, i: (pl.multiple_of(col0 + j * tn, 8), 0))],
        out_specs=pl.BlockSpec((tm, tn), lambda j, i: (i, j)),
        out_shape=jax.ShapeDtypeStruct((m, n), F32),
        scratch_shapes=[pltpu.VMEM((tn, k), BF16)],
        compiler_params=_cparams(("arbitrary", "arbitrary"), 48),
    )(a, wt)


def _chunk_masks(tb):
    row = lax.broadcasted_iota(jnp.int32, (tb, tb), 0)
    col = lax.broadcasted_iota(jnp.int32, (tb, tb), 1)
    same = (row // CHUNK) == (col // CHUNK)
    return row, col, same, same & (col <= row), same & (col < row)


def _lane_pick(x, idx):
    lane = lax.broadcasted_iota(jnp.int32, x.shape, 1)
    return jnp.sum(jnp.where(lane == idx, x, 0.0), axis=-1, keepdims=True)


def _causal_conv_silu(x_ref, w_ref, halo_ref, cbuf_ref, idx, tb):
    cbuf_ref[idx, 0:8, :] = halo_ref[idx]
    cbuf_ref[idx, 8:8 + tb, :] = x_ref[...]
    halo_ref[idx] = x_ref[tb - 8:tb, :]
    w = w_ref[...]
    acc = w[A_CONV - 1:A_CONV, :] * x_ref[...]
    for j in range(A_CONV - 1):
        off = 8 - (A_CONV - 1) + j
        acc = acc + w[j:j + 1, :] * cbuf_ref[idx, off:off + tb, :]
    return _silu(acc)


def _gates_a(gl, a_log, dt_bias, ga_ref, gct_ref):
    tb = gl.shape[0]
    beta = _sigmoid(gl)
    g = -jnp.exp(a_log) * _softplus(gl + dt_bias)
    _, _, same, causal, _ = _chunk_masks(tb)
    gc = _cumsum_rows(jnp.where(causal, 1.0, 0.0).astype(BF16), g)
    glast = _cumsum_rows(jnp.where(same, 1.0, 0.0).astype(BF16), g)
    lane = lax.broadcasted_iota(jnp.int32, (tb, LANE), 1)
    ga_ref[...] = jnp.where(lane < A_HEADS, beta,
                            jnp.where(lane < 2 * A_HEADS, gc, pltpu.roll(glast, A_HEADS, 1)))
    gct_ref[...] = gc.T[A_HEADS:2 * A_HEADS, :]


def _delta_heads(qs, ks, vs, ga, gc_rows, hs, sts, tb):
    n = len(qs)
    rng = range(n)
    _, _, _, causal, strict = _chunk_masks(tb)
    qs = [q * lax.rsqrt(jnp.sum(q * q, axis=-1, keepdims=True) + EPS) * (A_DK ** -0.5) for q in qs]
    ks = [k * lax.rsqrt(jnp.sum(k * k, axis=-1, keepdims=True) + EPS) for k in ks]
    beta = [_lane_pick(ga, h) for h in hs]
    gc = [_lane_pick(ga, h + A_HEADS) for h in hs]
    glast = [_lane_pick(ga, h + 2 * A_HEADS) for h in hs]
    decay = [jnp.exp(jnp.where(causal, gc[i] - gc_rows[i], NEG)) for i in rng]
    kb = [ks[i] * beta[i] for i in rng]

    n_pow = [jnp.where(strict, _bdot_nt(kb[i], ks[i]) * decay[i], 0.0) * -1.0 for i in rng]
    t_mat = list(n_pow)
    lvl = 2
    while lvl < CHUNK:
        n_pow = [_bdot(m, m) for m in n_pow]
        t_mat = [t_mat[i] + n_pow[i] + _bdot(t_mat[i], n_pow[i]) for i in rng]
        lvl *= 2

    egc = [jnp.exp(g) for g in gc]
    rhs = [jnp.concatenate([vs[i] * beta[i], kb[i] * egc[i]], axis=1) for i in rng]
    uw = [rhs[i] + _bdot(t_mat[i], rhs[i]) for i in rng]
    qk = [_bdot_nt(qs[i], ks[i]) * decay[i] for i in rng]
    qkuw = [_bdot(qk[i], uw[i]) for i in rng]
    o_local = [x[:, :A_DV] for x in qkuw]
    q_eff = [qs[i] * egc[i] - qkuw[i][:, A_DV:] for i in rng]
    k_dec = [ks[i] * jnp.exp(glast[i] - gc[i]) for i in rng]
    eg_last = [jnp.exp(g) for g in glast]

    sts = list(sts)
    outs = [[] for _ in rng]
    for c in range(tb // CHUNK):
        lo, hi = c * CHUNK, (c + 1) * CHUNK
        bg = [_bdot_tn(uw[i][lo:hi], k_dec[i][lo:hi]) for i in rng]
        for i in rng:
            outs[i].append(o_local[i][lo:hi] + _bdot_nt(q_eff[i][lo:hi], sts[i]))
        sts = [sts[i] * eg_last[i][lo:lo + 1, :] + bg[i][:A_DV] - _bdot(sts[i], bg[i][A_DV:]) for i in rng]
    return [jnp.concatenate(o, axis=0) for o in outs], sts


def _mixer_a_kernel(xq_ref, xk_ref, xv_ref, z_ref, ga_ref, gct_ref, wq_ref, wk_ref, wv_ref,
                    anorm_ref, o_ref, halo_ref, cbuf_ref, state_ref):
    tb = xq_ref.shape[0]
    hg = xq_ref.shape[1] // A_DK

    @pl.when(pl.program_id(1) == 0)
    def _():
        halo_ref[...] = jnp.zeros_like(halo_ref)
        state_ref[...] = jnp.zeros_like(state_ref)

    q = _causal_conv_silu(xq_ref, wq_ref, halo_ref, cbuf_ref, 0, tb)
    k = _causal_conv_silu(xk_ref, wk_ref, halo_ref, cbuf_ref, 1, tb)
    v = _causal_conv_silu(xv_ref, wv_ref, halo_ref, cbuf_ref, 2, tb)
    ga = ga_ref[...]
    hs = [pl.program_id(0) * hg + j for j in range(hg)]
    sls = [slice(j * A_DK, (j + 1) * A_DK) for j in range(hg)]
    outs, sts = _delta_heads([q[:, s] for s in sls], [k[:, s] for s in sls], [v[:, s] for s in sls], ga,
                             [gct_ref[pl.ds(h, 1), :] for h in hs], hs, [state_ref[j] for j in range(hg)], tb)
    for j in range(hg):
        state_ref[j] = sts[j]
        o = outs[j]
        o = o * lax.rsqrt(jnp.mean(o * o, axis=-1, keepdims=True) + EPS) * anorm_ref[...]
        o_ref[:, sls[j]] = (o * _silu(z_ref[:, sls[j]])).astype(o_ref.dtype)


def _mixer_a(z_a, ga, gct, conv_a, a_norm):
    t = z_a.shape[0]
    ng = A_HEADS // A_HG
    wid = A_HG * A_DK
    blk = lambda off: pl.BlockSpec((TA, wid), lambda g, i: (i, off + g))
    cblk = lambda off: pl.BlockSpec((A_CONV, wid), lambda g, i: (0, off + g))
    return pl.pallas_call(
        _mixer_a_kernel,
        grid=(ng, t // TA),
        in_specs=[blk(0), blk(ng), blk(2 * ng), blk(3 * ng),
                  pl.BlockSpec((TA, LANE), lambda g, i: (i, 0)),
                  pl.BlockSpec((A_HEADS, TA), lambda g, i: (0, i)),
                  cblk(0), cblk(ng), cblk(2 * ng), pl.BlockSpec((1, LANE), lambda g, i: (0, 0))],
        out_specs=pl.BlockSpec((TA, wid), lambda g, i: (i, g)),
        out_shape=jax.ShapeDtypeStruct((t, A_V), BF16),
        scratch_shapes=[pltpu.VMEM((3, 8, wid), F32), pltpu.VMEM((3, 8 + TA, wid), F32),
                        pltpu.VMEM((A_HG, A_DV, A_DK), F32)],
        compiler_params=_cparams(("parallel", "arbitrary"), 48),
    )(z_a, z_a, z_a, z_a, ga, gct, conv_a, conv_a, conv_a,
      a_norm.reshape(1, A_DV).astype(F32))


def _cumsum_rows(mask, x):
    hi = x.astype(BF16)
    r1 = x - hi.astype(F32)
    mid = r1.astype(BF16)
    lo = (r1 - mid.astype(F32)).astype(BF16)
    dot = lambda p: jnp.dot(mask, p, preferred_element_type=F32)
    return dot(hi) + dot(mid) + dot(lo)


def _gla_chunk_scores(qc, kc, bc):
    rid = lax.broadcasted_iota(jnp.int32, (SUB, B_DK), 0)
    rid_lo = lax.broadcasted_iota(jnp.int32, (SUB // 2, B_DK), 0) + SUB // 2
    lane_c = lax.broadcasted_iota(jnp.int32, (SUB, CHUNK), 1)
    crow = lax.broadcasted_iota(jnp.int32, (CHUNK, B_DK), 0)
    half = SUB // 2
    rows = []
    for si in range(CHUNK // SUB):
        r0 = si * SUB
        qb, bb = qc[r0:r0 + SUB], bc[r0:r0 + SUB]
        ys = []
        for j in range(SUB):
            bj = bc[r0 + j:r0 + j + 1]
            if j < half:
                ys.append(qb * jnp.exp(jnp.where(rid >= j, bb - bj, NEG)))
            else:
                ys.append(jnp.zeros((half, B_DK), F32))
                ys.append(qb[half:] * jnp.exp(jnp.where(rid_lo >= j, bb[half:] - bj, NEG)))
        r = _bdot_nt(jnp.concatenate(ys, axis=0), kc)
        blk = jnp.zeros((SUB, CHUNK), F32)
        for j in range(SUB):
            blk = jnp.where(lane_c == r0 + j, r[j * SUB:(j + 1) * SUB], blk)
        if si > 0:
            bref = bc[r0:r0 + 1]
            qt = qb * jnp.exp(bb - bref)
            kt = kc * jnp.exp(jnp.where(crow < r0, bref - bc, NEG))
            blk = blk + _bdot_nt(qt, kt)
        rows.append(blk)
    return jnp.concatenate(rows, axis=0)


def _mixer_b_kernel(q_ref, k_ref, v_ref, zg_ref, tail_ref, wg_ref, bgk_ref, bnorm_ref, o_ref, state_ref):
    tb = q_ref.shape[0]
    nh = q_ref.shape[1] // B_DK

    @pl.when(pl.program_id(0) == 0)
    def _():
        state_ref[...] = jnp.zeros_like(state_ref)

    tl = tail_ref[...]
    t_hi = tl.astype(BF16)
    t_mid = (tl - t_hi.astype(F32)).astype(BF16)
    dot = lambda a, b: jnp.dot(a, b, preferred_element_type=F32)
    x = dot(t_hi, wg_ref[0]) + (dot(t_mid, wg_ref[0]) + dot(t_hi, wg_ref[1])) + bgk_ref[...]
    gk = -_softplus(-x) * (1.0 / B_GATE_NORM)
    _, _, _, causal, _ = _chunk_masks(tb)
    b_all = _cumsum_rows(jnp.where(causal, 1.0, 0.0).astype(BF16), gk)

    sts = [state_ref[h] for h in range(nh)]
    for c in range(tb // CHUNK):
        lo, hi = c * CHUNK, (c + 1) * CHUNK
        for h in range(nh):
            qc = q_ref[lo:hi, h * B_DK:(h + 1) * B_DK] * (B_DK ** -0.5)
            kc = k_ref[lo:hi, h * B_DK:(h + 1) * B_DK]
            vc = v_ref[lo:hi, h * B_DV:(h + 1) * B_DV]
            bc = b_all[lo:hi, h * B_DK:(h + 1) * B_DK]
            a_c = _gla_chunk_scores(qc, kc, bc)
            bl = bc[CHUNK - 1:CHUNK]
            o = _bdot(a_c, vc) + _bdot_nt(qc * jnp.exp(bc), sts[h])
            sts[h] = sts[h] * jnp.exp(bl) + _bdot_tn(vc, kc * jnp.exp(bl - bc))
            o = o * lax.rsqrt(jnp.mean(o * o, axis=-1, keepdims=True) + EPS) * bnorm_ref[...]
            sl = slice(h * B_DV, (h + 1) * B_DV)
            o_ref[lo:hi, sl] = (o * _silu(zg_ref[lo:hi, sl])).astype(o_ref.dtype)
    for h in range(nh):
        state_ref[h] = sts[h]


def _mixer_b(z_b, z_tail, w_gk2, b_gk, b_norm):
    t = z_b.shape[0]
    wg = jnp.zeros((LANE, B_QK), F32).at[2 * A_HEADS:2 * A_HEADS + B_GATE_RANK].set(w_gk2.astype(F32))
    wg_hi = wg.astype(BF16)
    wg = jnp.stack([wg_hi, (wg - wg_hi.astype(F32)).astype(BF16)])
    col = lambda w, off: pl.BlockSpec((TB, w), lambda i: (i, off // w))
    full = lambda r, c: pl.BlockSpec((r, c), lambda i: (0, 0))
    return pl.pallas_call(
        _mixer_b_kernel,
        grid=(t // TB,),
        in_specs=[col(B_QK, 0), col(B_QK, B_QK), col(B_V, 2 * B_QK), col(B_V, 2 * B_QK + B_V), col(LANE, 0),
                  pl.BlockSpec((2, LANE, B_QK), lambda i: (0, 0, 0)), full(1, B_QK), full(1, B_DV)],
        out_specs=pl.BlockSpec((TB, B_V), lambda i: (i, 0)),
        out_shape=jax.ShapeDtypeStruct((t, B_V), BF16),
        scratch_shapes=[pltpu.VMEM((B_HEADS, B_DV, B_DK), F32)],
        compiler_params=_cparams(("arbitrary",), 48),
    )(z_b, z_b, z_b, z_b, z_tail, wg, b_gk.reshape(1, B_QK).astype(F32),
      b_norm.reshape(1, B_DV).astype(F32))


def _merge_kernel(oa_ref, ob_ref, ma_ref, mb_ref, x_ref, woa_ref, wob_ref, wout_ref, nf_ref,
                  wr_ref, br_ref, x2_ref, h2_ref, lt_ref):
    ya = jnp.dot(oa_ref[...], woa_ref[...], preferred_element_type=F32)
    yb = jnp.dot(ob_ref[...], wob_ref[...], preferred_element_type=F32)
    m = _sigmoid(ma_ref[...]) * ya + _sigmoid(mb_ref[...]) * yb
    x2 = x_ref[...] + jnp.dot(m.astype(BF16), wout_ref[...], preferred_element_type=F32)
    x2_ref[...] = x2
    h2 = x2 * lax.rsqrt(jnp.mean(x2 * x2, axis=-1, keepdims=True) + EPS) * nf_ref[...]
    h2_ref[...] = h2
    h_hi = h2.astype(BF16)
    h_mid = (h2 - h_hi.astype(F32)).astype(BF16)
    dot = lambda a, b: jnp.dot(a, b, preferred_element_type=F32)
    logits = dot(h_hi, wr_ref[0]) + (dot(h_mid, wr_ref[0]) + dot(h_hi, wr_ref[1]))
    lt_ref[...] = logits.T + br_ref[...]


def _merge(oa_g, ob_g, z_mix, x, w_oa, w_ob, w_out, norm_ffn, wr, br_t, tm):
    t, d = x.shape
    row = lambda w, c: pl.BlockSpec((tm, w), lambda i: (i, c))
    full = lambda a: pl.BlockSpec(a.shape, lambda i: (0,) * a.ndim, pipeline_mode=pl.Buffered(1))
    return pl.pallas_call(
        _merge_kernel,
        grid=(t // tm,),
        in_specs=[row(A_V, 0), row(B_V, 0), row(d, 0), row(d, 1), row(d, 0),
                  full(w_oa), full(w_ob), full(w_out), pl.BlockSpec((1, d), lambda i: (0, 0)),
                  full(wr), full(br_t)],
        out_specs=[row(d, 0), row(d, 0), pl.BlockSpec((LANE, tm), lambda i: (0, i))],
        out_shape=[jax.ShapeDtypeStruct((t, d), F32), jax.ShapeDtypeStruct((t, d), F32),
                   jax.ShapeDtypeStruct((LANE, t), F32)],
        compiler_params=_cparams(("parallel",), 56),
    )(oa_g, ob_g, z_mix, z_mix, x, w_oa, w_ob, w_out, norm_ffn.reshape(1, d).astype(F32), wr, br_t)


SEG = 256


def _route_kernel(lt_ref, pos_ref, gate_ref, blk_ref, oh_ref):
    t = lt_ref.shape[1]
    rid8 = lax.broadcasted_iota(jnp.int32, (8, t), 0)
    lg = jnp.where(rid8 < N_GROUPS, lt_ref[0:8, :], -jnp.inf)
    gmax = jnp.max(lg, axis=0, keepdims=True)
    g_idx = jnp.min(jnp.where(lg == gmax, rid8, 8), axis=0, keepdims=True)
    p_top = 1.0 / jnp.sum(jnp.exp(lg - gmax), axis=0, keepdims=True)

    les = jnp.zeros((EXP_PER_GROUP, t), F32)
    for g in range(N_GROUPS):
        les = jnp.where(g_idx == g, lt_ref[8 + g * EXP_PER_GROUP:8 + (g + 1) * EXP_PER_GROUP, :], les)
    m1 = jnp.max(les, axis=0, keepdims=True)
    i1 = jnp.min(jnp.where(les == m1, rid8, 8), axis=0, keepdims=True)
    les2 = jnp.where(rid8 == i1, -jnp.inf, les)
    m2 = jnp.max(les2, axis=0, keepdims=True)
    i2 = jnp.min(jnp.where(les2 == m2, rid8, 8), axis=0, keepdims=True)
    r = jnp.exp(m2 - m1)
    gate_ref[...] = jnp.concatenate([p_top / (1.0 + r), p_top * r / (1.0 + r),
                                     jnp.zeros((LANE - 2, t), F32)], axis=0).T
    e1 = g_idx * EXP_PER_GROUP + i1
    e2 = g_idx * EXP_PER_GROUP + i2

    rid32 = lax.broadcasted_iota(jnp.int32, (N_EXPERTS, t), 0)
    oh_ref[0] = jnp.where(rid32 == e1, 1.0, 0.0)
    oh_ref[1] = jnp.where(rid32 == e2, 1.0, 0.0)

    ui = lax.broadcasted_iota(jnp.int32, (SEG, SEG), 0)
    uj = lax.broadcasted_iota(jnp.int32, (SEG, SEG), 1)
    upper = jnp.where(ui < uj, 1.0, 0.0).astype(BF16)
    carry = jnp.zeros((N_EXPERTS, 1), F32)
    ranks = []
    for kk in range(2):
        segs = []
        for sg in range(t // SEG):
            oh = oh_ref[kk, :, sg * SEG:(sg + 1) * SEG]
            pre = jnp.dot(oh.astype(BF16), upper, preferred_element_type=F32) + carry
            segs.append(jnp.sum(oh * pre, axis=0, keepdims=True))
            carry = carry + jnp.sum(oh, axis=1, keepdims=True)
        ranks.append(jnp.concatenate(segs, axis=1))
    counts = carry
    nblk = jnp.floor((counts + (ROW_BLOCK - 1)) * (1.0 / ROW_BLOCK))
    li = lax.broadcasted_iota(jnp.int32, (N_EXPERTS, N_EXPERTS), 0)
    lj = lax.broadcasted_iota(jnp.int32, (N_EXPERTS, N_EXPERTS), 1)
    nb_b = jnp.broadcast_to(nblk, (N_EXPERTS, LANE))
    start_blk = _fdot(jnp.where(lj < li, 1.0, 0.0), nb_b)[:, 0:1]
    end_blk = start_blk + nblk
    start_row = start_blk * ROW_BLOCK

    pos_ref[...] = jnp.zeros_like(pos_ref)
    for kk in range(2):
        base = jnp.sum(oh_ref[kk] * start_row, axis=0, keepdims=True)
        pos_ref[kk:kk + 1, :] = (base + ranks[kk]).astype(jnp.int32)

    nb = blk_ref.shape[1]
    bid = lax.broadcasted_iota(jnp.int32, (N_EXPERTS, nb), 1).astype(F32)
    be = jnp.sum(jnp.where(end_blk <= bid, 1.0, 0.0), axis=0, keepdims=True)
    blk_ref[...] = jnp.zeros_like(blk_ref)
    blk_ref[0:1, :] = jnp.minimum(be, N_EXPERTS - 1.0).astype(jnp.int32)
    blk_ref[1:2, :] = jnp.broadcast_to(end_blk[N_EXPERTS - 1:N_EXPERTS, :], (1, nb)).astype(jnp.int32)
    ends = jnp.sum(jnp.where(li == lj, end_blk, 0.0), axis=0, keepdims=True)
    blk_ref[2:3, 0:N_EXPERTS] = ends.astype(jnp.int32)


def _route(lt, n_blk):
    t = lt.shape[1]
    nb = -(-n_blk // LANE) * LANE
    return pl.pallas_call(
        _route_kernel,
        out_shape=[jax.ShapeDtypeStruct((8, t), jnp.int32), jax.ShapeDtypeStruct((t, LANE), F32),
                   jax.ShapeDtypeStruct((8, nb), jnp.int32)],
        scratch_shapes=[pltpu.VMEM((2, N_EXPERTS, t), F32)],
        compiler_params=pltpu.CompilerParams(vmem_limit_bytes=48 * 2 ** 20),
    )(lt)


def _scatter_kernel(pos_ref, ends_ref, nu_ref, h_ref, xs_ref, zbuf, sem, zsem):
    tm = h_ref.shape[0]
    t = pl.num_programs(0) * tm
    base = pl.program_id(0) * tm
    n_blk = xs_ref.shape[0] // ROW_BLOCK

    def zero_block(blk):
        return pltpu.make_async_copy(zbuf, xs_ref.at[pl.ds(blk * ROW_BLOCK, ROW_BLOCK), :], zsem)

    def for_each_zero_block(fn):
        for e in range(N_EXPERTS):
            first = ends_ref[e - 1] if e else 0

            @pl.when(ends_ref[e] > first)
            def _():
                fn(zero_block(ends_ref[e] - 1))

        def tail(blk, c):
            fn(zero_block(blk))
            return c
        lax.fori_loop(nu_ref[0], n_blk, tail, 0)

    @pl.when(pl.program_id(0) == 0)
    def _():
        zbuf[...] = jnp.zeros_like(zbuf)
        for_each_zero_block(lambda c: c.start())
        for_each_zero_block(lambda c: c.wait())

    def row_copy(r, p):
        return pltpu.make_async_copy(h_ref.at[pl.ds(r, 1), :], xs_ref.at[pl.ds(p, 1), :], sem)

    for r in range(tm):
        for kk in range(2):
            row_copy(r, pos_ref[kk * t + base + r]).start()
    for r in range(2 * tm):
        row_copy(0, 0).wait()


def _scatter_rows(pos_flat, ends, n_used, h2, n_rows, tm):
    t, d = h2.shape
    return pl.pallas_call(
        _scatter_kernel,
        grid_spec=pltpu.PrefetchScalarGridSpec(
            num_scalar_prefetch=3,
            grid=(t // tm,),
            in_specs=[pl.BlockSpec((tm, d), lambda i, *_: (i, 0))],
            out_specs=pl.BlockSpec(memory_space=pl.ANY),
            scratch_shapes=[pltpu.VMEM((ROW_BLOCK, d), h2.dtype), pltpu.SemaphoreType.DMA(()),
                            pltpu.SemaphoreType.DMA(())],
        ),
        out_shape=jax.ShapeDtypeStruct((n_rows, d), h2.dtype),
        compiler_params=_cparams(("arbitrary",), 32),
    )(pos_flat, ends, n_used, h2)


def _expert_kernel(be_ref, nu_ref, ends_ref, x_ref, w1_hbm, w3_hbm, w2_hbm, o_ref,
                   w1s, w3s, w2s, w1b, w3b, w2b, sem):
    b = pl.program_id(0)
    e = be_ref[b]
    prev = be_ref[jnp.maximum(b - 1, 0)]
    changed = jnp.logical_or(b == 0, e != prev)

    def weight_copies(ex):
        return (pltpu.make_async_copy(w1_hbm.at[ex], w1s, sem.at[0]),
                pltpu.make_async_copy(w3_hbm.at[ex], w3s, sem.at[1]),
                pltpu.make_async_copy(w2_hbm.at[ex], w2s, sem.at[2]))

    @pl.when(b == 0)
    def _():
        for c in weight_copies(e):
            c.start()

    @pl.when(jnp.logical_and(changed, b < nu_ref[0]))
    def _():
        for c in weight_copies(e):
            c.wait()
        w1b[...] = w1s[...].astype(BF16)
        w3b[...] = w3s[...].astype(BF16)
        w2b[...] = w2s[...].astype(BF16)
        nxt = ends_ref[e]

        @pl.when(nxt < nu_ref[0])
        def _():
            for c in weight_copies(be_ref[nxt]):
                c.start()

    @pl.when(b < nu_ref[0])
    def _():
        xb = x_ref[...].astype(BF16)
        a = jnp.dot(xb, w1b[...], preferred_element_type=F32)
        g = jnp.dot(xb, w3b[...], preferred_element_type=F32)
        o_ref[...] = jnp.dot((_silu(a) * g).astype(BF16), w2b[...], preferred_element_type=F32)

    @pl.when(b >= nu_ref[0])
    def _():
        o_ref[...] = jnp.zeros_like(o_ref)


def _experts(blk_exp, n_used, ends, xs, w1, w3, w2):
    n_rows, d = xs.shape
    n_blk = n_rows // ROW_BLOCK
    rows = lambda b, be, nu, en: (jnp.minimum(b, jnp.maximum(nu[0] - 1, 0)), 0)
    hbm = pl.BlockSpec(memory_space=pl.ANY)
    return pl.pallas_call(
        _expert_kernel,
        grid_spec=pltpu.PrefetchScalarGridSpec(
            num_scalar_prefetch=3,
            grid=(n_blk,),
            in_specs=[pl.BlockSpec((ROW_BLOCK, d), rows), hbm, hbm, hbm],
            out_specs=pl.BlockSpec((ROW_BLOCK, d), lambda b, be, nu, en: (b, 0)),
            scratch_shapes=[pltpu.VMEM((d, D_FF), w1.dtype), pltpu.VMEM((d, D_FF), w3.dtype),
                            pltpu.VMEM((D_FF, d), w2.dtype),
                            pltpu.VMEM((d, D_FF), BF16), pltpu.VMEM((d, D_FF), BF16),
                            pltpu.VMEM((D_FF, d), BF16), pltpu.SemaphoreType.DMA((3,))],
        ),
        out_shape=jax.ShapeDtypeStruct((n_rows, d), F32),
        compiler_params=_cparams(("arbitrary",), 56),
    )(blk_exp, n_used, ends, xs, w1, w3, w2)


def _combine_kernel(pos_ref, x2_ref, gt_ref, nw_ref, yb_ref, o_ref, buf, sem, *, final):
    tm = x2_ref.shape[0]
    n = pl.num_programs(0)
    t = n * tm
    i = pl.program_id(0)

    def row_copy(slot, r, kk, p):
        return pltpu.make_async_copy(yb_ref.at[pl.ds(p, 1), :], buf.at[slot, kk, pl.ds(r, 1), :], sem.at[slot])

    def issue(step, slot):
        for r in range(tm):
            for kk in range(2):
                row_copy(slot, r, kk, pos_ref[kk * t + step * tm + r]).start()

    @pl.when(i == 0)
    def _():
        issue(0, 0)

    @pl.when(i + 1 < n)
    def _():
        issue(i + 1, (i + 1) % 2)

    slot = i % 2
    for r in range(2 * tm):
        row_copy(slot, 0, 0, 0).wait()

    gt = gt_ref[...]
    y = x2_ref[...] + gt[:, 0:1] * buf[slot, 0] + gt[:, 1:2] * buf[slot, 1]
    if final:
        y = y * lax.rsqrt(jnp.mean(y * y, axis=-1, keepdims=True) + EPS) * nw_ref[...]
    o_ref[...] = y


def _combine(pos_flat, x2, gates_t, norm_final, yb, tm, final):
    t, d = x2.shape
    return pl.pallas_call(
        functools.partial(_combine_kernel, final=final),
        grid_spec=pltpu.PrefetchScalarGridSpec(
            num_scalar_prefetch=1,
            grid=(t // tm,),
            in_specs=[pl.BlockSpec((tm, d), lambda i, pos: (i, 0)),
                      pl.BlockSpec((tm, LANE), lambda i, pos: (i, 0)),
                      pl.BlockSpec((1, d), lambda i, pos: (0, 0)),
                      pl.BlockSpec(memory_space=pl.ANY)],
            out_specs=pl.BlockSpec((tm, d), lambda i, pos: (i, 0)),
            scratch_shapes=[pltpu.VMEM((2, 2, tm, d), F32), pltpu.SemaphoreType.DMA((2,))],
        ),
        out_shape=jax.ShapeDtypeStruct((t, d), F32),
        compiler_params=_cparams(("arbitrary",), 32),
    )(pos_flat, x2, gates_t, norm_final.reshape(1, d).astype(F32), yb)


def _layer(x, norm_mix, w_in, conv_a, a_log, dt_bias, a_norm, w_gk2, b_gk, b_norm,
           w_oa, w_ob, w_out, norm_ffn, w_rg, b_rg, w_re, b_re, w1, w3, w2, norm_final, final):
    t, d = x.shape

    a_end = 4 * A_QK
    ga_end = a_end + 2 * A_HEADS
    b_end = ga_end + 2 * B_QK + 2 * B_V
    lr_end = b_end + B_GATE_RANK

    tm_big = min(1024, t)
    wt = w_in.T
    h, z_lr, ga, gct = _prologue(x, norm_mix, wt, a_end, b_end, a_log, dt_bias)
    z_a = _proj(h, wt, 0, a_end, tm_big, 1024)
    z_b = _proj(h, wt, ga_end, b_end - ga_end, tm_big, 1024)
    z_mix = _proj(h, wt, lr_end, 2 * d, tm_big, 1024)

    oa_g = _mixer_a(z_a, ga, gct, conv_a.astype(F32), a_norm)
    ob_g = _mixer_b(z_b, z_lr, w_gk2, b_gk, b_norm)

    wr = jnp.zeros((d, LANE), F32).at[:, 0:N_GROUPS].set(w_rg.astype(F32))
    wr = wr.at[:, 8:8 + N_EXPERTS].set(w_re.reshape(d, N_EXPERTS).astype(F32))
    wr_hi = wr.astype(BF16)
    wr = jnp.stack([wr_hi, (wr - wr_hi.astype(F32)).astype(BF16)])
    br_t = jnp.zeros((LANE, 1), F32).at[0:N_GROUPS, 0].set(b_rg.astype(F32))
    br_t = br_t.at[8:8 + N_EXPERTS, 0].set(b_re.reshape(N_EXPERTS).astype(F32))
    x2, h2, lt = _merge(oa_g, ob_g, z_mix, x, w_oa.astype(BF16), w_ob.astype(BF16), w_out.astype(BF16),
                        norm_ffn, wr, br_t, min(256, t))

    n_blk = (2 * t + ROW_BLOCK - 1) // ROW_BLOCK + N_EXPERTS
    n_rows = n_blk * ROW_BLOCK
    pos, gates, blk = _route(lt, n_blk)
    pos_flat = pos[0:2].reshape(2 * t)
    blk_exp, n_used, ends = blk[0, :n_blk], blk[1, 0:1], blk[2, :N_EXPERTS]
    xs = _scatter_rows(pos_flat, ends, n_used, h2, n_rows, min(512, t))
    yb = _experts(blk_exp, n_used, ends, xs, w1, w3, w2)
    return _combine(pos_flat, x2, gates, norm_final, yb, min(256, t), final)


def kernel(x, norm_mix, w_in, conv_a, a_log, dt_bias, a_norm, w_gk2, b_gk, b_norm, w_oa, w_ob, w_out,
           norm_ffn, w_rg, b_rg, w_re, b_re, w1, w3, w2, norm_final):
    bsz, seq, d = x.shape
    assert bsz == 1, "one sequence per call"
    depth = norm_mix.shape[0]
    y = x.reshape(seq, d)
    for l in range(depth):
        y = _layer(y, norm_mix[l], w_in[l], conv_a[l], a_log[l], dt_bias[l], a_norm[l], w_gk2[l], b_gk[l],
                   b_norm[l], w_oa[l], w_ob[l], w_out[l], norm_ffn[l], w_rg[l], b_rg[l], w_re[l], b_re[l],
                   w1[l], w3[l], w2[l], norm_final, l == depth - 1)
    return y.reshape(bsz, seq, d)
```

```python
import functools

import jax
import jax.numpy as jnp
from jax import lax
from jax.experimental import pallas as pl
from jax.experimental.pallas import tpu as pltpu

D_MODEL = 2048
CHUNK = 64
EPS = 1e-6
A_HEADS, A_DK, A_DV, A_CONV = 8, 128, 128, 4
A_QK, A_V = A_HEADS * A_DK, A_HEADS * A_DV
B_HEADS, B_DK, B_DV, B_GATE_RANK, B_GATE_NORM = 4, 128, 256, 16, 16.0
B_QK, B_V = B_HEADS * B_DK, B_HEADS * B_DV
N_GROUPS, EXP_PER_GROUP, D_FF = 4, 8, 512
N_EXPERTS = N_GROUPS * EXP_PER_GROUP
ROW_BLOCK = 256
LANE = 128
SUB = 16
TB = 256
A_HG = 8
TA = 128
NEG = -1e30

F32 = jnp.float32
BF16 = jnp.bfloat16
HI = lax.Precision.HIGHEST


def _cparams(sem, vmem_mib):
    return pltpu.CompilerParams(dimension_semantics=sem, vmem_limit_bytes=vmem_mib * 2 ** 20)


def _bdot(a, b):
    return jnp.dot(a.astype(BF16), b.astype(BF16), preferred_element_type=F32)


def _bdot_nt(a, b):
    return lax.dot_general(a.astype(BF16), b.astype(BF16), (((1,), (1,)), ((), ())),
                           preferred_element_type=F32)


def _bdot_tn(a, b):
    return lax.dot_general(a.astype(BF16), b.astype(BF16), (((0,), (0,)), ((), ())),
                           preferred_element_type=F32)


def _fdot(a, b):
    return jnp.dot(a, b, preferred_element_type=F32, precision=HI)


def _sigmoid(x):
    return 1.0 / (1.0 + jnp.exp(-x))


def _silu(x):
    return x * _sigmoid(x)


def _softplus(x):
    return jnp.maximum(x, 0.0) + jnp.log(1.0 + jnp.exp(-jnp.abs(x)))


def _prologue_kernel(x_ref, nw_ref, wga_ref, wlr_ref, alog_ref, dtb_ref, h_ref, zlr_ref, ga_ref, gct_ref,
                     wga_b, wlr_b):
    @pl.when(pl.program_id(0) == 0)
    def _():
        wga_b[...] = wga_ref[...].astype(BF16)
        wlr_b[...] = wlr_ref[...].astype(BF16)

    x = x_ref[...]
    h = (x * lax.rsqrt(jnp.mean(x * x, axis=-1, keepdims=True) + EPS) * nw_ref[...]).astype(BF16)
    h_ref[...] = h
    proj = lambda w: lax.dot_general(h, w, (((1,), (1,)), ((), ())), preferred_element_type=F32)
    zlr_ref[...] = proj(wlr_b[...])
    _gates_a(proj(wga_b[...]), alog_ref[...], dtb_ref[...], ga_ref, gct_ref)


def _prologue(x, norm_mix, wt, a_end, b_end, a_log, dt_bias):
    t, d = x.shape
    assert a_end % LANE == 0 and b_end % LANE == 2 * A_HEADS
    pad = lambda p: jnp.pad(p.astype(F32), (A_HEADS, LANE - 2 * A_HEADS)).reshape(1, LANE)
    const = lambda c: pl.BlockSpec((1, c), lambda i: (0, 0))
    rows = lambda c: pl.BlockSpec((TB, c), lambda i: (i, 0))
    return pl.pallas_call(
        _prologue_kernel,
        grid=(t // TB,),
        in_specs=[rows(d), const(d),
                  pl.BlockSpec((LANE, d), lambda i: (a_end // LANE, 0)),
                  pl.BlockSpec((LANE, d), lambda i: (b_end // LANE, 0)),
                  const(LANE), const(LANE)],
        out_specs=[rows(d), rows(LANE), rows(LANE), pl.BlockSpec((A_HEADS, TB), lambda i: (0, i))],
        out_shape=[jax.ShapeDtypeStruct((t, d), BF16), jax.ShapeDtypeStruct((t, LANE), F32),
                   jax.ShapeDtypeStruct((t, LANE), F32), jax.ShapeDtypeStruct((A_HEADS, t), F32)],
        scratch_shapes=[pltpu.VMEM((LANE, d), BF16), pltpu.VMEM((LANE, d), BF16)],
        compiler_params=_cparams(("arbitrary",), 32),
    )(x, norm_mix.reshape(1, d), wt, wt, pad(a_log), pad(dt_bias))


PREP_ROWS = 256


def _proj_kernel(a_ref, wt_ref, o_ref, wb_ref):
    tn = wt_ref.shape[0]

    @pl.when(pl.program_id(1) == 0)
    def _():
        for r in range(0, tn, PREP_ROWS):
            wb_ref[r:r + PREP_ROWS, :] = wt_ref[r:r + PREP_ROWS, :].astype(BF16)

    o_ref[...] = lax.dot_general(a_ref[...], wb_ref[...], (((1,), (1,)), ((), ())), preferred_element_type=F32)


def _proj(a, wt, col0, n, tm, tn):
    m, k = a.shape
    assert n % tn == 0 and col0 % 8 == 0
    return pl.pallas_call(
        _proj_kernel,
        grid=(n // tn, m // tm),
        in_specs=[pl.BlockSpec((tm, k), lambda j, i: (i, 0)),
                  pl.BlockSpec((pl.Element(tn), pl.Element(k)), lambda j, i: (pl.multiple_of(col0 + j * tn, 8), 0))],
        out_specs=pl.BlockSpec((tm, tn), lambda j, i: (i, j)),
        out_shape=jax.ShapeDtypeStruct((m, n), F32),
        scratch_shapes=[pltpu.VMEM((tn, k), BF16)],
        compiler_params=_cparams(("arbitrary", "arbitrary"), 48),
    )(a, wt)


def _chunk_masks(tb):
    row = lax.broadcasted_iota(jnp.int32, (tb, tb), 0)
    col = lax.broadcasted_iota(jnp.int32, (tb, tb), 1)
    same = (row // CHUNK) == (col // CHUNK)
    return row, col, same, same & (col <= row), same & (col < row)


def _lane_pick(x, idx):
    lane = lax.broadcasted_iota(jnp.int32, x.shape, 1)
    return jnp.sum(jnp.where(lane == idx, x, 0.0), axis=-1, keepdims=True)


def _causal_conv_silu(x_ref, w_ref, halo_ref, cbuf_ref, idx, tb):
    cbuf_ref[idx, 0:8, :] = halo_ref[idx]
    cbuf_ref[idx, 8:8 + tb, :] = x_ref[...]
    halo_ref[idx] = x_ref[tb - 8:tb, :]
    w = w_ref[...]
    acc = w[A_CONV - 1:A_CONV, :] * x_ref[...]
    for j in range(A_CONV - 1):
        off = 8 - (A_CONV - 1) + j
        acc = acc + w[j:j + 1, :] * cbuf_ref[idx, off:off + tb, :]
    return _silu(acc)


def _gates_a(gl, a_log, dt_bias, ga_ref, gct_ref):
    tb = gl.shape[0]
    beta = _sigmoid(gl)
    g = -jnp.exp(a_log) * _softplus(gl + dt_bias)
    _, _, same, causal, _ = _chunk_masks(tb)
    gc = _cumsum_rows(jnp.where(causal, 1.0, 0.0).astype(BF16), g)
    glast = _cumsum_rows(jnp.where(same, 1.0, 0.0).astype(BF16), g)
    lane = lax.broadcasted_iota(jnp.int32, (tb, LANE), 1)
    ga_ref[...] = jnp.where(lane < A_HEADS, beta,
                            jnp.where(lane < 2 * A_HEADS, gc, pltpu.roll(glast, A_HEADS, 1)))
    gct_ref[...] = gc.T[A_HEADS:2 * A_HEADS, :]


def _delta_heads(qs, ks, vs, ga, gc_rows, hs, sts, tb):
    n = len(qs)
    rng = range(n)
    _, _, _, causal, strict = _chunk_masks(tb)
    qs = [q * lax.rsqrt(jnp.sum(q * q, axis=-1, keepdims=True) + EPS) * (A_DK ** -0.5) for q in qs]
    ks = [k * lax.rsqrt(jnp.sum(k * k, axis=-1, keepdims=True) + EPS) for k in ks]
    beta = [_lane_pick(ga, h) for h in hs]
    gc = [_lane_pick(ga, h + A_HEADS) for h in hs]
    glast = [_lane_pick(ga, h + 2 * A_HEADS) for h in hs]
    decay = [jnp.exp(jnp.where(causal, gc[i] - gc_rows[i], NEG)) for i in rng]
    kb = [ks[i] * beta[i] for i in rng]

    n_pow = [jnp.where(strict, _bdot_nt(kb[i], ks[i]) * decay[i], 0.0) * -1.0 for i in rng]
    t_mat = list(n_pow)
    lvl = 2
    while lvl < CHUNK:
        n_pow = [_bdot(m, m) for m in n_pow]
        t_mat = [t_mat[i] + n_pow[i] + _bdot(t_mat[i], n_pow[i]) for i in rng]
        lvl *= 2

    egc = [jnp.exp(g) for g in gc]
    rhs = [jnp.concatenate([vs[i] * beta[i], kb[i] * egc[i]], axis=1) for i in rng]
    uw = [rhs[i] + _bdot(t_mat[i], rhs[i]) for i in rng]
    qk = [_bdot_nt(qs[i], ks[i]) * decay[i] for i in rng]
    qkuw = [_bdot(qk[i], uw[i]) for i in rng]
    o_local = [x[:, :A_DV] for x in qkuw]
    q_eff = [qs[i] * egc[i] - qkuw[i][:, A_DV:] for i in rng]
    k_dec = [ks[i] * jnp.exp(glast[i] - gc[i]) for i in rng]
    eg_last = [jnp.exp(g) for g in glast]

    sts = list(sts)
    outs = [[] for _ in rng]
    for c in range(tb // CHUNK):
        lo, hi = c * CHUNK, (c + 1) * CHUNK
        bg = [_bdot_tn(uw[i][lo:hi], k_dec[i][lo:hi]) for i in rng]
        for i in rng:
            outs[i].append(o_local[i][lo:hi] + _bdot_nt(q_eff[i][lo:hi], sts[i]))
        sts = [sts[i] * eg_last[i][lo:lo + 1, :] + bg[i][:A_DV] - _bdot(sts[i], bg[i][A_DV:]) for i in rng]
    return [jnp.concatenate(o, axis=0) for o in outs], sts


def _mixer_a_kernel(xq_ref, xk_ref, xv_ref, z_ref, ga_ref, gct_ref, wq_ref, wk_ref, wv_ref,
                    anorm_ref, o_ref, halo_ref, cbuf_ref, state_ref):
    tb = xq_ref.shape[0]
    hg = xq_ref.shape[1] // A_DK

    @pl.when(pl.program_id(1) == 0)
    def _():
        halo_ref[...] = jnp.zeros_like(halo_ref)
        state_ref[...] = jnp.zeros_like(state_ref)

    q = _causal_conv_silu(xq_ref, wq_ref, halo_ref, cbuf_ref, 0, tb)
    k = _causal_conv_silu(xk_ref, wk_ref, halo_ref, cbuf_ref, 1, tb)
    v = _causal_conv_silu(xv_ref, wv_ref, halo_ref, cbuf_ref, 2, tb)
    ga = ga_ref[...]
    hs = [pl.program_id(0) * hg + j for j in range(hg)]
    sls = [slice(j * A_DK, (j + 1) * A_DK) for j in range(hg)]
    outs, sts = _delta_heads([q[:, s] for s in sls], [k[:, s] for s in sls], [v[:, s] for s in sls], ga,
                             [gct_ref[pl.ds(h, 1), :] for h in hs], hs, [state_ref[j] for j in range(hg)], tb)
    for j in range(hg):
        state_ref[j] = sts[j]
        o = outs[j]
        o = o * lax.rsqrt(jnp.mean(o * o, axis=-1, keepdims=True) + EPS) * anorm_ref[...]
        o_ref[:, sls[j]] = (o * _silu(z_ref[:, sls[j]])).astype(o_ref.dtype)


def _mixer_a(z_a, ga, gct, conv_a, a_norm):
    t = z_a.shape[0]
    ng = A_HEADS // A_HG
    wid = A_HG * A_DK
    blk = lambda off: pl.BlockSpec((TA, wid), lambda g, i: (i, off + g))
    cblk = lambda off: pl.BlockSpec((A_CONV, wid), lambda g, i: (0, off + g))
    return pl.pallas_call(
        _mixer_a_kernel,
        grid=(ng, t // TA),
        in_specs=[blk(0), blk(ng), blk(2 * ng), blk(3 * ng),
                  pl.BlockSpec((TA, LANE), lambda g, i: (i, 0)),
                  pl.BlockSpec((A_HEADS, TA), lambda g, i: (0, i)),
                  cblk(0), cblk(ng), cblk(2 * ng), pl.BlockSpec((1, LANE), lambda g, i: (0, 0))],
        out_specs=pl.BlockSpec((TA, wid), lambda g, i: (i, g)),
        out_shape=jax.ShapeDtypeStruct((t, A_V), BF16),
        scratch_shapes=[pltpu.VMEM((3, 8, wid), F32), pltpu.VMEM((3, 8 + TA, wid), F32),
                        pltpu.VMEM((A_HG, A_DV, A_DK), F32)],
        compiler_params=_cparams(("parallel", "arbitrary"), 48),
    )(z_a, z_a, z_a, z_a, ga, gct, conv_a, conv_a, conv_a,
      a_norm.reshape(1, A_DV).astype(F32))


def _cumsum_rows(mask, x):
    hi = x.astype(BF16)
    r1 = x - hi.astype(F32)
    mid = r1.astype(BF16)
    lo = (r1 - mid.astype(F32)).astype(BF16)
    dot = lambda p: jnp.dot(mask, p, preferred_element_type=F32)
    return dot(hi) + dot(mid) + dot(lo)


def _gla_chunk_scores(qc, kc, bc):
    rid = lax.broadcasted_iota(jnp.int32, (SUB, B_DK), 0)
    rid_lo = lax.broadcasted_iota(jnp.int32, (SUB // 2, B_DK), 0) + SUB // 2
    lane_c = lax.broadcasted_iota(jnp.int32, (SUB, CHUNK), 1)
    crow = lax.broadcasted_iota(jnp.int32, (CHUNK, B_DK), 0)
    half = SUB // 2
    rows = []
    for si in range(CHUNK // SUB):
        r0 = si * SUB
        qb, bb = qc[r0:r0 + SUB], bc[r0:r0 + SUB]
        ys = []
        for j in range(SUB):
            bj = bc[r0 + j:r0 + j + 1]
            if j < half:
                ys.append(qb * jnp.exp(jnp.where(rid >= j, bb - bj, NEG)))
            else:
                ys.append(jnp.zeros((half, B_DK), F32))
                ys.append(qb[half:] * jnp.exp(jnp.where(rid_lo >= j, bb[half:] - bj, NEG)))
        r = _bdot_nt(jnp.concatenate(ys, axis=0), kc)
        blk = jnp.zeros((SUB, CHUNK), F32)
        for j in range(SUB):
            blk = jnp.where(lane_c == r0 + j, r[j * SUB:(j + 1) * SUB], blk)
        if si > 0:
            bref = bc[r0:r0 + 1]
            qt = qb * jnp.exp(bb - bref)
            kt = kc * jnp.exp(jnp.where(crow < r0, bref - bc, NEG))
            blk = blk + _bdot_nt(qt, kt)
        rows.append(blk)
    return jnp.concatenate(rows, axis=0)


def _mixer_b_kernel(q_ref, k_ref, v_ref, zg_ref, tail_ref, wg_ref, bgk_ref, bnorm_ref, o_ref, state_ref):
    tb = q_ref.shape[0]
    nh = q_ref.shape[1] // B_DK

    @pl.when(pl.program_id(0) == 0)
    def _():
        state_ref[...] = jnp.zeros_like(state_ref)

    tl = tail_ref[...]
    t_hi = tl.astype(BF16)
    t_mid = (tl - t_hi.astype(F32)).astype(BF16)
    dot = lambda a, b: jnp.dot(a, b, preferred_element_type=F32)
    x = dot(t_hi, wg_ref[0]) + (dot(t_mid, wg_ref[0]) + dot(t_hi, wg_ref[1])) + bgk_ref[...]
    gk = -_softplus(-x) * (1.0 / B_GATE_NORM)
    _, _, _, causal, _ = _chunk_masks(tb)
    b_all = _cumsum_rows(jnp.where(causal, 1.0, 0.0).astype(BF16), gk)

    sts = [state_ref[h] for h in range(nh)]
    for c in range(tb // CHUNK):
        lo, hi = c * CHUNK, (c + 1) * CHUNK
        for h in range(nh):
            qc = q_ref[lo:hi, h * B_DK:(h + 1) * B_DK] * (B_DK ** -0.5)
            kc = k_ref[lo:hi, h * B_DK:(h + 1) * B_DK]
            vc = v_ref[lo:hi, h * B_DV:(h + 1) * B_DV]
            bc = b_all[lo:hi, h * B_DK:(h + 1) * B_DK]
            a_c = _gla_chunk_scores(qc, kc, bc)
            bl = bc[CHUNK - 1:CHUNK]
            o = _bdot(a_c, vc) + _bdot_nt(qc * jnp.exp(bc), sts[h])
            sts[h] = sts[h] * jnp.exp(bl) + _bdot_tn(vc, kc * jnp.exp(bl - bc))
            o = o * lax.rsqrt(jnp.mean(o * o, axis=-1, keepdims=True) + EPS) * bnorm_ref[...]
            sl = slice(h * B_DV, (h + 1) * B_DV)
            o_ref[lo:hi, sl] = (o * _silu(zg_ref[lo:hi, sl])).astype(o_ref.dtype)
    for h in range(nh):
        state_ref[h] = sts[h]


def _mixer_b(z_b, z_tail, w_gk2, b_gk, b_norm):
    t = z_b.shape[0]
    wg = jnp.zeros((LANE, B_QK), F32).at[2 * A_HEADS:2 * A_HEADS + B_GATE_RANK].set(w_gk2.astype(F32))
    wg_hi = wg.astype(BF16)
    wg = jnp.stack([wg_hi, (wg - wg_hi.astype(F32)).astype(BF16)])
    col = lambda w, off: pl.BlockSpec((TB, w), lambda i: (i, off // w))
    full = lambda r, c: pl.BlockSpec((r, c), lambda i: (0, 0))
    return pl.pallas_call(
        _mixer_b_kernel,
        grid=(t // TB,),
        in_specs=[col(B_QK, 0), col(B_QK, B_QK), col(B_V, 2 * B_QK), col(B_V, 2 * B_QK + B_V), col(LANE, 0),
                  pl.BlockSpec((2, LANE, B_QK), lambda i: (0, 0, 0)), full(1, B_QK), full(1, B_DV)],
        out_specs=pl.BlockSpec((TB, B_V), lambda i: (i, 0)),
        out_shape=jax.ShapeDtypeStruct((t, B_V), BF16),
        scratch_shapes=[pltpu.VMEM((B_HEADS, B_DV, B_DK), F32)],
        compiler_params=_cparams(("arbitrary",), 48),
    )(z_b, z_b, z_b, z_b, z_tail, wg, b_gk.reshape(1, B_QK).astype(F32),
      b_norm.reshape(1, B_DV).astype(F32))


def _merge_kernel(oa_ref, ob_ref, ma_ref, mb_ref, x_ref, woa_ref, wob_ref, wout_ref, nf_ref,
                  wr_ref, br_ref, x2_ref, h2_ref, lt_ref):
    ya = jnp.dot(oa_ref[...], woa_ref[...], preferred_element_type=F32)
    yb = jnp.dot(ob_ref[...], wob_ref[...], preferred_element_type=F32)
    m = _sigmoid(ma_ref[...]) * ya + _sigmoid(mb_ref[...]) * yb
    x2 = x_ref[...] + jnp.dot(m.astype(BF16), wout_ref[...], preferred_element_type=F32)
    x2_ref[...] = x2
    h2 = x2 * lax.rsqrt(jnp.mean(x2 * x2, axis=-1, keepdims=True) + EPS) * nf_ref[...]
    h2_ref[...] = h2
    h_hi = h2.astype(BF16)
    h_mid = (h2 - h_hi.astype(F32)).astype(BF16)
    dot = lambda a, b: jnp.dot(a, b, preferred_element_type=F32)
    logits = dot(h_hi, wr_ref[0]) + (dot(h_mid, wr_ref[0]) + dot(h_hi, wr_ref[1]))
    lt_ref[...] = logits.T + br_ref[...]


def _merge(oa_g, ob_g, z_mix, x, w_oa, w_ob, w_out, norm_ffn, wr, br_t, tm):
    t, d = x.shape
    row = lambda w, c: pl.BlockSpec((tm, w), lambda i: (i, c))
    full = lambda a: pl.BlockSpec(a.shape, lambda i: (0,) * a.ndim, pipeline_mode=pl.Buffered(1))
    return pl.pallas_call(
        _merge_kernel,
        grid=(t // tm,),
        in_specs=[row(A_V, 0), row(B_V, 0), row(d, 0), row(d, 1), row(d, 0),
                  full(w_oa), full(w_ob), full(w_out), pl.BlockSpec((1, d), lambda i: (0, 0)),
                  full(wr), full(br_t)],
        out_specs=[row(d, 0), row(d, 0), pl.BlockSpec((LANE, tm), lambda i: (0, i))],
        out_shape=[jax.ShapeDtypeStruct((t, d), F32), jax.ShapeDtypeStruct((t, d), F32),
                   jax.ShapeDtypeStruct((LANE, t), F32)],
        compiler_params=_cparams(("parallel",), 56),
    )(oa_g, ob_g, z_mix, z_mix, x, w_oa, w_ob, w_out, norm_ffn.reshape(1, d).astype(F32), wr, br_t)


SEG = 256


def _route_kernel(lt_ref, pos_ref, gate_ref, blk_ref, oh_ref):
    t = lt_ref.shape[1]
    rid8 = lax.broadcasted_iota(jnp.int32, (8, t), 0)
    lg = jnp.where(rid8 < N_GROUPS, lt_ref[0:8, :], -jnp.inf)
    gmax = jnp.max(lg, axis=0, keepdims=True)
    g_idx = jnp.min(jnp.where(lg == gmax, rid8, 8), axis=0, keepdims=True)
    p_top = 1.0 / jnp.sum(jnp.exp(lg - gmax), axis=0, keepdims=True)

    les = jnp.zeros((EXP_PER_GROUP, t), F32)
    for g in range(N_GROUPS):
        les = jnp.where(g_idx == g, lt_ref[8 + g * EXP_PER_GROUP:8 + (g + 1) * EXP_PER_GROUP, :], les)
    m1 = jnp.max(les, axis=0, keepdims=True)
    i1 = jnp.min(jnp.where(les == m1, rid8, 8), axis=0, keepdims=True)
    les2 = jnp.where(rid8 == i1, -jnp.inf, les)
    m2 = jnp.max(les2, axis=0, keepdims=True)
    i2 = jnp.min(jnp.where(les2 == m2, rid8, 8), axis=0, keepdims=True)
    r = jnp.exp(m2 - m1)
    gate_ref[...] = jnp.concatenate([p_top / (1.0 + r), p_top * r / (1.0 + r),
                                     jnp.zeros((LANE - 2, t), F32)], axis=0).T
    e1 = g_idx * EXP_PER_GROUP + i1
    e2 = g_idx * EXP_PER_GROUP + i2

    rid32 = lax.broadcasted_iota(jnp.int32, (N_EXPERTS, t), 0)
    oh_ref[0] = jnp.where(rid32 == e1, 1.0, 0.0)
    oh_ref[1] = jnp.where(rid32 == e2, 1.0, 0.0)

    ui = lax.broadcasted_iota(jnp.int32, (SEG, SEG), 0)
    uj = lax.broadcasted_iota(jnp.int32, (SEG, SEG), 1)
    upper = jnp.where(ui < uj, 1.0, 0.0).astype(BF16)
    carry = jnp.zeros((N_EXPERTS, 1), F32)
    ranks = []
    for kk in range(2):
        segs = []
        for sg in range(t // SEG):
            oh = oh_ref[kk, :, sg * SEG:(sg + 1) * SEG]
            pre = jnp.dot(oh.astype(BF16), upper, preferred_element_type=F32) + carry
            segs.append(jnp.sum(oh * pre, axis=0, keepdims=True))
            carry = carry + jnp.sum(oh, axis=1, keepdims=True)
        ranks.append(jnp.concatenate(segs, axis=1))
    counts = carry
    nblk = jnp.floor((counts + (ROW_BLOCK - 1)) * (1.0 / ROW_BLOCK))
    li = lax.broadcasted_iota(jnp.int32, (N_EXPERTS, N_EXPERTS), 0)
    lj = lax.broadcasted_iota(jnp.int32, (N_EXPERTS, N_EXPERTS), 1)
    nb_b = jnp.broadcast_to(nblk, (N_EXPERTS, LANE))
    start_blk = _fdot(jnp.where(lj < li, 1.0, 0.0), nb_b)[:, 0:1]
    end_blk = start_blk + nblk
    start_row = start_blk * ROW_BLOCK

    pos_ref[...] = jnp.zeros_like(pos_ref)
    for kk in range(2):
        base = jnp.sum(oh_ref[kk] * start_row, axis=0, keepdims=True)
        pos_ref[kk:kk + 1, :] = (base + ranks[kk]).astype(jnp.int32)

    nb = blk_ref.shape[1]
    bid = lax.broadcasted_iota(jnp.int32, (N_EXPERTS, nb), 1).astype(F32)
    be = jnp.sum(jnp.where(end_blk <= bid, 1.0, 0.0), axis=0, keepdims=True)
    blk_ref[...] = jnp.zeros_like(blk_ref)
    blk_ref[0:1, :] = jnp.minimum(be, N_EXPERTS - 1.0).astype(jnp.int32)
    blk_ref[1:2, :] = jnp.broadcast_to(end_blk[N_EXPERTS - 1:N_EXPERTS, :], (1, nb)).astype(jnp.int32)
    ends = jnp.sum(jnp.where(li == lj, end_blk, 0.0), axis=0, keepdims=True)
    blk_ref[2:3, 0:N_EXPERTS] = ends.astype(jnp.int32)


def _route(lt, n_blk):
    t = lt.shape[1]
    nb = -(-n_blk // LANE) * LANE
    return pl.pallas_call(
        _route_kernel,
        out_shape=[jax.ShapeDtypeStruct((8, t), jnp.int32), jax.ShapeDtypeStruct((t, LANE), F32),
                   jax.ShapeDtypeStruct((8, nb), jnp.int32)],
        scratch_shapes=[pltpu.VMEM((2, N_EXPERTS, t), F32)],
        compiler_params=pltpu.CompilerParams(vmem_limit_bytes=48 * 2 ** 20),
    )(lt)


def _scatter_kernel(pos_ref, ends_ref, nu_ref, h_ref, xs_ref, zbuf, sem, zsem):
    tm = h_ref.shape[0]
    t = pl.num_programs(0) * tm
    base = pl.program_id(0) * tm
    n_blk = xs_ref.shape[0] // ROW_BLOCK

    def zero_block(blk):
        return pltpu.make_async_copy(zbuf, xs_ref.at[pl.ds(blk * ROW_BLOCK, ROW_BLOCK), :], zsem)

    def for_each_zero_block(fn):
        for e in range(N_EXPERTS):
            first = ends_ref[e - 1] if e else 0

            @pl.when(ends_ref[e] > first)
            def _():
                fn(zero_block(ends_ref[e] - 1))

        def tail(blk, c):
            fn(zero_block(blk))
            return c
        lax.fori_loop(nu_ref[0], n_blk, tail, 0)

    @pl.when(pl.program_id(0) == 0)
    def _():
        zbuf[...] = jnp.zeros_like(zbuf)
        for_each_zero_block(lambda c: c.start())
        for_each_zero_block(lambda c: c.wait())

    def row_copy(r, p):
        return pltpu.make_async_copy(h_ref.at[pl.ds(r, 1), :], xs_ref.at[pl.ds(p, 1), :], sem)

    for r in range(tm):
        for kk in range(2):
            row_copy(r, pos_ref[kk * t + base + r]).start()
    for r in range(2 * tm):
        row_copy(0, 0).wait()


def _scatter_rows(pos_flat, ends, n_used, h2, n_rows, tm):
    t, d = h2.shape
    return pl.pallas_call(
        _scatter_kernel,
        grid_spec=pltpu.PrefetchScalarGridSpec(
            num_scalar_prefetch=3,
            grid=(t // tm,),
            in_specs=[pl.BlockSpec((tm, d), lambda i, *_: (i, 0))],
            out_specs=pl.BlockSpec(memory_space=pl.ANY),
            scratch_shapes=[pltpu.VMEM((ROW_BLOCK, d), h2.dtype), pltpu.SemaphoreType.DMA(()),
                            pltpu.SemaphoreType.DMA(())],
        ),
        out_shape=jax.ShapeDtypeStruct((n_rows, d), h2.dtype),
        compiler_params=_cparams(("arbitrary",), 32),
    )(pos_flat, ends, n_used, h2)


GROUPS_PER_STEP = 2


def _expert_kernel(be_ref, nu_ref, ends_ref, x_ref, w1_hbm, w3_hbm, w2_hbm, o_ref,
                   w1s, w3s, w2s, w1b, w3b, w2b, sem):
    nu = nu_ref[0]

    def weight_copies(ex):
        return (pltpu.make_async_copy(w1_hbm.at[ex], w1s, sem.at[0]),
                pltpu.make_async_copy(w3_hbm.at[ex], w3s, sem.at[1]),
                pltpu.make_async_copy(w2_hbm.at[ex], w2s, sem.at[2]))

    def row_group(b, rows):
        e = be_ref[b]
        changed = jnp.logical_or(b == 0, e != be_ref[jnp.maximum(b - 1, 0)])

        @pl.when(jnp.logical_and(b == 0, nu > 0))
        def _():
            for c in weight_copies(e):
                c.start()

        @pl.when(jnp.logical_and(changed, b < nu))
        def _():
            for c in weight_copies(e):
                c.wait()
            w1b[...] = w1s[...].astype(BF16)
            w3b[...] = w3s[...].astype(BF16)
            w2b[...] = w2s[...].astype(BF16)
            nxt = ends_ref[e]

            @pl.when(nxt < nu)
            def _():
                for c in weight_copies(be_ref[nxt]):
                    c.start()

        @pl.when(b < nu)
        def _():
            xb = x_ref[rows, :].astype(BF16)
            a = jnp.dot(xb, w1b[...], preferred_element_type=F32)
            g = jnp.dot(xb, w3b[...], preferred_element_type=F32)
            o_ref[rows, :] = jnp.dot((_silu(a) * g).astype(BF16), w2b[...], preferred_element_type=F32)

        @pl.when(b >= nu)
        def _():
            o_ref[rows, :] = jnp.zeros((ROW_BLOCK, o_ref.shape[1]), o_ref.dtype)

    for s in range(GROUPS_PER_STEP):
        row_group(GROUPS_PER_STEP * pl.program_id(0) + s, slice(s * ROW_BLOCK, (s + 1) * ROW_BLOCK))


def _experts(blk_exp, n_used, ends, xs, w1, w3, w2):
    n_rows, d = xs.shape
    step_rows = GROUPS_PER_STEP * ROW_BLOCK
    assert n_rows % step_rows == 0
    rows = lambda g, be, nu, en: (jnp.minimum(g, jnp.maximum(nu[0] - 1, 0) // GROUPS_PER_STEP), 0)
    hbm = pl.BlockSpec(memory_space=pl.ANY)
    return pl.pallas_call(
        _expert_kernel,
        grid_spec=pltpu.PrefetchScalarGridSpec(
            num_scalar_prefetch=3,
            grid=(n_rows // step_rows,),
            in_specs=[pl.BlockSpec((step_rows, d), rows), hbm, hbm, hbm],
            out_specs=pl.BlockSpec((step_rows, d), lambda g, be, nu, en: (g, 0)),
            scratch_shapes=[pltpu.VMEM((d, D_FF), w1.dtype), pltpu.VMEM((d, D_FF), w3.dtype),
                            pltpu.VMEM((D_FF, d), w2.dtype),
                            pltpu.VMEM((d, D_FF), BF16), pltpu.VMEM((d, D_FF), BF16),
                            pltpu.VMEM((D_FF, d), BF16), pltpu.SemaphoreType.DMA((3,))],
        ),
        out_shape=jax.ShapeDtypeStruct((n_rows, d), F32),
        compiler_params=_cparams(("arbitrary",), 56),
    )(blk_exp, n_used, ends, xs, w1, w3, w2)


def _combine_kernel(pos_ref, x2_ref, gt_ref, nw_ref, yb_ref, o_ref, buf, sem, *, final):
    tm = x2_ref.shape[0]
    n = pl.num_programs(0)
    t = n * tm
    i = pl.program_id(0)

    def row_copy(slot, r, kk, p):
        return pltpu.make_async_copy(yb_ref.at[pl.ds(p, 1), :], buf.at[slot, kk, pl.ds(r, 1), :], sem.at[slot])

    def issue(step, slot):
        for r in range(tm):
            for kk in range(2):
                row_copy(slot, r, kk, pos_ref[kk * t + step * tm + r]).start()

    @pl.when(i == 0)
    def _():
        issue(0, 0)

    @pl.when(i + 1 < n)
    def _():
        issue(i + 1, (i + 1) % 2)

    slot = i % 2
    for r in range(2 * tm):
        row_copy(slot, 0, 0, 0).wait()

    gt = gt_ref[...]
    y = x2_ref[...] + gt[:, 0:1] * buf[slot, 0] + gt[:, 1:2] * buf[slot, 1]
    if final:
        y = y * lax.rsqrt(jnp.mean(y * y, axis=-1, keepdims=True) + EPS) * nw_ref[...]
    o_ref[...] = y


def _combine(pos_flat, x2, gates_t, norm_final, yb, tm, final):
    t, d = x2.shape
    return pl.pallas_call(
        functools.partial(_combine_kernel, final=final),
        grid_spec=pltpu.PrefetchScalarGridSpec(
            num_scalar_prefetch=1,
            grid=(t // tm,),
            in_specs=[pl.BlockSpec((tm, d), lambda i, pos: (i, 0)),
                      pl.BlockSpec((tm, LANE), lambda i, pos: (i, 0)),
                      pl.BlockSpec((1, d), lambda i, pos: (0, 0)),
                      pl.BlockSpec(memory_space=pl.ANY)],
            out_specs=pl.BlockSpec((tm, d), lambda i, pos: (i, 0)),
            scratch_shapes=[pltpu.VMEM((2, 2, tm, d), F32), pltpu.SemaphoreType.DMA((2,))],
        ),
        out_shape=jax.ShapeDtypeStruct((t, d), F32),
        compiler_params=_cparams(("arbitrary",), 32),
    )(pos_flat, x2, gates_t, norm_final.reshape(1, d).astype(F32), yb)


def _layer(x, norm_mix, w_in, conv_a, a_log, dt_bias, a_norm, w_gk2, b_gk, b_norm,
           w_oa, w_ob, w_out, norm_ffn, w_rg, b_rg, w_re, b_re, w1, w3, w2, norm_final, final):
    t, d = x.shape

    a_end = 4 * A_QK
    ga_end = a_end + 2 * A_HEADS
    b_end = ga_end + 2 * B_QK + 2 * B_V
    lr_end = b_end + B_GATE_RANK

    tm_big = min(1024, t)
    wt = w_in.T
    h, z_lr, ga, gct = _prologue(x, norm_mix, wt, a_end, b_end, a_log, dt_bias)
    z_a = _proj(h, wt, 0, a_end, tm_big, 1024)
    z_b = _proj(h, wt, ga_end, b_end - ga_end, tm_big, 1024)
    z_mix = _proj(h, wt, lr_end, 2 * d, tm_big, 1024)

    oa_g = _mixer_a(z_a, ga, gct, conv_a.astype(F32), a_norm)
    ob_g = _mixer_b(z_b, z_lr, w_gk2, b_gk, b_norm)

    wr = jnp.zeros((d, LANE), F32).at[:, 0:N_GROUPS].set(w_rg.astype(F32))
    wr = wr.at[:, 8:8 + N_EXPERTS].set(w_re.reshape(d, N_EXPERTS).astype(F32))
    wr_hi = wr.astype(BF16)
    wr = jnp.stack([wr_hi, (wr - wr_hi.astype(F32)).astype(BF16)])
    br_t = jnp.zeros((LANE, 1), F32).at[0:N_GROUPS, 0].set(b_rg.astype(F32))
    br_t = br_t.at[8:8 + N_EXPERTS, 0].set(b_re.reshape(N_EXPERTS).astype(F32))
    x2, h2, lt = _merge(oa_g, ob_g, z_mix, x, w_oa.astype(BF16), w_ob.astype(BF16), w_out.astype(BF16),
                        norm_ffn, wr, br_t, min(256, t))

    n_blk = (2 * t + ROW_BLOCK - 1) // ROW_BLOCK + N_EXPERTS
    n_blk += -n_blk % GROUPS_PER_STEP
    n_rows = n_blk * ROW_BLOCK
    pos, gates, blk = _route(lt, n_blk)
    pos_flat = pos[0:2].reshape(2 * t)
    blk_exp, n_used, ends = blk[0, :n_blk], blk[1, 0:1], blk[2, :N_EXPERTS]
    xs = _scatter_rows(pos_flat, ends, n_used, h2, n_rows, min(512, t))
    yb = _experts(blk_exp, n_used, ends, xs, w1, w3, w2)
    return _combine(pos_flat, x2, gates, norm_final, yb, min(256, t), final)


def kernel(x, norm_mix, w_in, conv_a, a_log, dt_bias, a_norm, w_gk2, b_gk, b_norm, w_oa, w_ob, w_out,
           norm_ffn, w_rg, b_rg, w_re, b_re, w1, w3, w2, norm_final):
    bsz, seq, d = x.shape
    assert bsz == 1, "one sequence per call"
    depth = norm_mix.shape[0]
    y = x.reshape(seq, d)
    for l in range(depth):
        y = _layer(y, norm_mix[l], w_in[l], conv_a[l], a_log[l], dt_bias[l], a_norm[l], w_gk2[l], b_gk[l],
                   b_norm[l], w_oa[l], w_ob[l], w_out[l], norm_ffn[l], w_rg[l], b_rg[l], w_re[l], b_re[l],
                   w1[l], w3[l], w2[l], norm_final, l == depth - 1)
    return y.reshape(bsz, seq, d)
```

```python
import functools

import jax
import jax.numpy as jnp
from jax import lax
from jax.experimental import pallas as pl
from jax.experimental.pallas import tpu as pltpu

D_MODEL = 2048
CHUNK = 64
EPS = 1e-6
A_HEADS, A_DK, A_DV, A_CONV = 8, 128, 128, 4
A_QK, A_V = A_HEADS * A_DK, A_HEADS * A_DV
B_HEADS, B_DK, B_DV, B_GATE_RANK, B_GATE_NORM = 4, 128, 256, 16, 16.0
B_QK, B_V = B_HEADS * B_DK, B_HEADS * B_DV
N_GROUPS, EXP_PER_GROUP, D_FF = 4, 8, 512
N_EXPERTS = N_GROUPS * EXP_PER_GROUP
ROW_BLOCK = 256
LANE = 128
SUB = 16
TB = 256
A_HG = 8
TA = 128
NEG = -1e30

F32 = jnp.float32
BF16 = jnp.bfloat16
HI = lax.Precision.HIGHEST


def _cparams(sem, vmem_mib):
    return pltpu.CompilerParams(dimension_semantics=sem, vmem_limit_bytes=vmem_mib * 2 ** 20)


def _bdot(a, b):
    return jnp.dot(a.astype(BF16), b.astype(BF16), preferred_element_type=F32)


def _bdot_nt(a, b):
    return lax.dot_general(a.astype(BF16), b.astype(BF16), (((1,), (1,)), ((), ())),
                           preferred_element_type=F32)


def _bdot_tn(a, b):
    return lax.dot_general(a.astype(BF16), b.astype(BF16), (((0,), (0,)), ((), ())),
                           preferred_element_type=F32)


def _fdot(a, b):
    return jnp.dot(a, b, preferred_element_type=F32, precision=HI)


def _sigmoid(x):
    return 1.0 / (1.0 + jnp.exp(-x))


def _silu(x):
    return x * _sigmoid(x)


def _softplus(x):
    return jnp.maximum(x, 0.0) + jnp.log(1.0 + jnp.exp(-jnp.abs(x)))


def _prologue_kernel(x_ref, nw_ref, wga_ref, wlr_ref, alog_ref, dtb_ref, h_ref, zlr_ref, ga_ref, gct_ref,
                     wga_b, wlr_b):
    @pl.when(pl.program_id(0) == 0)
    def _():
        wga_b[...] = wga_ref[...].astype(BF16)
        wlr_b[...] = wlr_ref[...].astype(BF16)

    x = x_ref[...]
    h = (x * lax.rsqrt(jnp.mean(x * x, axis=-1, keepdims=True) + EPS) * nw_ref[...]).astype(BF16)
    h_ref[...] = h
    proj = lambda w: lax.dot_general(h, w, (((1,), (1,)), ((), ())), preferred_element_type=F32)
    zlr_ref[...] = proj(wlr_b[...])
    _gates_a(proj(wga_b[...]), alog_ref[...], dtb_ref[...], ga_ref, gct_ref)


def _prologue(x, norm_mix, wt, a_end, b_end, a_log, dt_bias):
    t, d = x.shape
    assert a_end % LANE == 0 and b_end % LANE == 2 * A_HEADS
    pad = lambda p: jnp.pad(p.astype(F32), (A_HEADS, LANE - 2 * A_HEADS)).reshape(1, LANE)
    const = lambda c: pl.BlockSpec((1, c), lambda i: (0, 0))
    rows = lambda c: pl.BlockSpec((TB, c), lambda i: (i, 0))
    return pl.pallas_call(
        _prologue_kernel,
        grid=(t // TB,),
        in_specs=[rows(d), const(d),
                  pl.BlockSpec((LANE, d), lambda i: (a_end // LANE, 0)),
                  pl.BlockSpec((LANE, d), lambda i: (b_end // LANE, 0)),
                  const(LANE), const(LANE)],
        out_specs=[rows(d), rows(LANE), rows(LANE), pl.BlockSpec((A_HEADS, TB), lambda i: (0, i))],
        out_shape=[jax.ShapeDtypeStruct((t, d), BF16), jax.ShapeDtypeStruct((t, LANE), F32),
                   jax.ShapeDtypeStruct((t, LANE), F32), jax.ShapeDtypeStruct((A_HEADS, t), F32)],
        scratch_shapes=[pltpu.VMEM((LANE, d), BF16), pltpu.VMEM((LANE, d), BF16)],
        compiler_params=_cparams(("arbitrary",), 32),
    )(x, norm_mix.reshape(1, d), wt, wt, pad(a_log), pad(dt_bias))


PREP_ROWS = 256


def _proj_kernel(a_ref, wt_ref, o_ref, wb_ref):
    tn = wt_ref.shape[0]

    @pl.when(pl.program_id(1) == 0)
    def _():
        for r in range(0, tn, PREP_ROWS):
            wb_ref[r:r + PREP_ROWS, :] = wt_ref[r:r + PREP_ROWS, :].astype(BF16)

    o_ref[...] = lax.dot_general(a_ref[...], wb_ref[...], (((1,), (1,)), ((), ())), preferred_element_type=F32)


def _proj(a, wt, col0, n, tm, tn):
    m, k = a.shape
    assert n % tn == 0 and col0 % 8 == 0
    return pl.pallas_call(
        _proj_kernel,
        grid=(n // tn, m // tm),
        in_specs=[pl.BlockSpec((tm, k), lambda j, i: (i, 0)),
                  pl.BlockSpec((pl.Element(tn), pl.Element(k)), lambda j, i: (pl.multiple_of(col0 + j * tn, 8), 0))],
        out_specs=pl.BlockSpec((tm, tn), lambda j, i: (i, j)),
        out_shape=jax.ShapeDtypeStruct((m, n), F32),
        scratch_shapes=[pltpu.VMEM((tn, k), BF16)],
        compiler_params=_cparams(("arbitrary", "arbitrary"), 48),
    )(a, wt)


def _chunk_masks(tb):
    row = lax.broadcasted_iota(jnp.int32, (tb, tb), 0)
    col = lax.broadcasted_iota(jnp.int32, (tb, tb), 1)
    same = (row // CHUNK) == (col // CHUNK)
    return row, col, same, same & (col <= row), same & (col < row)


def _lane_pick(x, idx):
    lane = lax.broadcasted_iota(jnp.int32, x.shape, 1)
    return jnp.sum(jnp.where(lane == idx, x, 0.0), axis=-1, keepdims=True)


def _causal_conv_silu(x_ref, w_ref, halo_ref, cbuf_ref, idx, tb):
    cbuf_ref[idx, 0:8, :] = halo_ref[idx]
    cbuf_ref[idx, 8:8 + tb, :] = x_ref[...]
    halo_ref[idx] = x_ref[tb - 8:tb, :]
    w = w_ref[...]
    acc = w[A_CONV - 1:A_CONV, :] * x_ref[...]
    for j in range(A_CONV - 1):
        off = 8 - (A_CONV - 1) + j
        acc = acc + w[j:j + 1, :] * cbuf_ref[idx, off:off + tb, :]
    return _silu(acc)


def _gates_a(gl, a_log, dt_bias, ga_ref, gct_ref):
    tb = gl.shape[0]
    beta = _sigmoid(gl)
    g = -jnp.exp(a_log) * _softplus(gl + dt_bias)
    _, _, same, causal, _ = _chunk_masks(tb)
    gc = _cumsum_rows(jnp.where(causal, 1.0, 0.0).astype(BF16), g)
    glast = _cumsum_rows(jnp.where(same, 1.0, 0.0).astype(BF16), g)
    lane = lax.broadcasted_iota(jnp.int32, (tb, LANE), 1)
    ga_ref[...] = jnp.where(lane < A_HEADS, beta,
                            jnp.where(lane < 2 * A_HEADS, gc, pltpu.roll(glast, A_HEADS, 1)))
    gct_ref[...] = gc.T[A_HEADS:2 * A_HEADS, :]


def _delta_heads(qs, ks, vs, ga, gc_rows, hs, sts, tb):
    n = len(qs)
    rng = range(n)
    _, _, _, causal, strict = _chunk_masks(tb)
    qs = [q * lax.rsqrt(jnp.sum(q * q, axis=-1, keepdims=True) + EPS) * (A_DK ** -0.5) for q in qs]
    ks = [k * lax.rsqrt(jnp.sum(k * k, axis=-1, keepdims=True) + EPS) for k in ks]
    beta = [_lane_pick(ga, h) for h in hs]
    gc = [_lane_pick(ga, h + A_HEADS) for h in hs]
    glast = [_lane_pick(ga, h + 2 * A_HEADS) for h in hs]
    decay = [jnp.exp(jnp.where(causal, gc[i] - gc_rows[i], NEG)) for i in rng]
    kb = [ks[i] * beta[i] for i in rng]

    n_pow = [jnp.where(strict, _bdot_nt(kb[i], ks[i]) * decay[i], 0.0) * -1.0 for i in rng]
    t_mat = list(n_pow)
    lvl = 2
    while lvl < CHUNK:
        n_pow = [_bdot(m, m) for m in n_pow]
        t_mat = [t_mat[i] + n_pow[i] + _bdot(t_mat[i], n_pow[i]) for i in rng]
        lvl *= 2

    egc = [jnp.exp(g) for g in gc]
    rhs = [jnp.concatenate([vs[i] * beta[i], kb[i] * egc[i]], axis=1) for i in rng]
    uw = [rhs[i] + _bdot(t_mat[i], rhs[i]) for i in rng]
    qk = [_bdot_nt(qs[i], ks[i]) * decay[i] for i in rng]
    qkuw = [_bdot(qk[i], uw[i]) for i in rng]
    o_local = [x[:, :A_DV] for x in qkuw]
    q_eff = [qs[i] * egc[i] - qkuw[i][:, A_DV:] for i in rng]
    k_dec = [ks[i] * jnp.exp(glast[i] - gc[i]) for i in rng]
    eg_last = [jnp.exp(g) for g in glast]

    sts = list(sts)
    outs = [[] for _ in rng]
    for c in range(tb // CHUNK):
        lo, hi = c * CHUNK, (c + 1) * CHUNK
        bg = [_bdot_tn(uw[i][lo:hi], k_dec[i][lo:hi]) for i in rng]
        for i in rng:
            outs[i].append(o_local[i][lo:hi] + _bdot_nt(q_eff[i][lo:hi], sts[i]))
        sts = [sts[i] * eg_last[i][lo:lo + 1, :] + bg[i][:A_DV] - _bdot(sts[i], bg[i][A_DV:]) for i in rng]
    return [jnp.concatenate(o, axis=0) for o in outs], sts


def _mixer_a_kernel(xq_ref, xk_ref, xv_ref, z_ref, ga_ref, gct_ref, wq_ref, wk_ref, wv_ref,
                    anorm_ref, o_ref, halo_ref, cbuf_ref, state_ref):
    tb = xq_ref.shape[0]
    hg = xq_ref.shape[1] // A_DK

    @pl.when(pl.program_id(1) == 0)
    def _():
        halo_ref[...] = jnp.zeros_like(halo_ref)
        state_ref[...] = jnp.zeros_like(state_ref)

    q = _causal_conv_silu(xq_ref, wq_ref, halo_ref, cbuf_ref, 0, tb)
    k = _causal_conv_silu(xk_ref, wk_ref, halo_ref, cbuf_ref, 1, tb)
    v = _causal_conv_silu(xv_ref, wv_ref, halo_ref, cbuf_ref, 2, tb)
    ga = ga_ref[...]
    hs = [pl.program_id(0) * hg + j for j in range(hg)]
    sls = [slice(j * A_DK, (j + 1) * A_DK) for j in range(hg)]
    outs, sts = _delta_heads([q[:, s] for s in sls], [k[:, s] for s in sls], [v[:, s] for s in sls], ga,
                             [gct_ref[pl.ds(h, 1), :] for h in hs], hs, [state_ref[j] for j in range(hg)], tb)
    for j in range(hg):
        state_ref[j] = sts[j]
        o = outs[j]
        o = o * lax.rsqrt(jnp.mean(o * o, axis=-1, keepdims=True) + EPS) * anorm_ref[...]
        o_ref[:, sls[j]] = (o * _silu(z_ref[:, sls[j]])).astype(o_ref.dtype)


def _mixer_a(z_a, ga, gct, conv_a, a_norm):
    t = z_a.shape[0]
    ng = A_HEADS // A_HG
    wid = A_HG * A_DK
    blk = lambda off: pl.BlockSpec((TA, wid), lambda g, i: (i, off + g))
    cblk = lambda off: pl.BlockSpec((A_CONV, wid), lambda g, i: (0, off + g))
    return pl.pallas_call(
        _mixer_a_kernel,
        grid=(ng, t // TA),
        in_specs=[blk(0), blk(ng), blk(2 * ng), blk(3 * ng),
                  pl.BlockSpec((TA, LANE), lambda g, i: (i, 0)),
                  pl.BlockSpec((A_HEADS, TA), lambda g, i: (0, i)),
                  cblk(0), cblk(ng), cblk(2 * ng), pl.BlockSpec((1, LANE), lambda g, i: (0, 0))],
        out_specs=pl.BlockSpec((TA, wid), lambda g, i: (i, g)),
        out_shape=jax.ShapeDtypeStruct((t, A_V), BF16),
        scratch_shapes=[pltpu.VMEM((3, 8, wid), F32), pltpu.VMEM((3, 8 + TA, wid), F32),
                        pltpu.VMEM((A_HG, A_DV, A_DK), F32)],
        compiler_params=_cparams(("parallel", "arbitrary"), 48),
    )(z_a, z_a, z_a, z_a, ga, gct, conv_a, conv_a, conv_a,
      a_norm.reshape(1, A_DV).astype(F32))


def _cumsum_rows(mask, x):
    hi = x.astype(BF16)
    r1 = x - hi.astype(F32)
    mid = r1.astype(BF16)
    lo = (r1 - mid.astype(F32)).astype(BF16)
    dot = lambda p: jnp.dot(mask, p, preferred_element_type=F32)
    return dot(hi) + dot(mid) + dot(lo)


def _gla_chunk_scores(qc, kc, bc):
    rid = lax.broadcasted_iota(jnp.int32, (SUB, B_DK), 0)
    rid_lo = lax.broadcasted_iota(jnp.int32, (SUB // 2, B_DK), 0) + SUB // 2
    lane_c = lax.broadcasted_iota(jnp.int32, (SUB, CHUNK), 1)
    crow = lax.broadcasted_iota(jnp.int32, (CHUNK, B_DK), 0)
    half = SUB // 2
    rows = []
    for si in range(CHUNK // SUB):
        r0 = si * SUB
        qb, bb = qc[r0:r0 + SUB], bc[r0:r0 + SUB]
        ys = []
        for j in range(SUB):
            bj = bc[r0 + j:r0 + j + 1]
            if j < half:
                ys.append(qb * jnp.exp(jnp.where(rid >= j, bb - bj, NEG)))
            else:
                ys.append(jnp.zeros((half, B_DK), F32))
                ys.append(qb[half:] * jnp.exp(jnp.where(rid_lo >= j, bb[half:] - bj, NEG)))
        r = _bdot_nt(jnp.concatenate(ys, axis=0), kc)
        blk = jnp.zeros((SUB, CHUNK), F32)
        for j in range(SUB):
            blk = jnp.where(lane_c == r0 + j, r[j * SUB:(j + 1) * SUB], blk)
        if si > 0:
            bref = bc[r0:r0 + 1]
            qt = qb * jnp.exp(bb - bref)
            kt = kc * jnp.exp(jnp.where(crow < r0, bref - bc, NEG))
            blk = blk + _bdot_nt(qt, kt)
        rows.append(blk)
    return jnp.concatenate(rows, axis=0)


def _mixer_b_kernel(q_ref, k_ref, v_ref, zg_ref, tail_ref, wg_ref, bgk_ref, bnorm_ref, o_ref, state_ref):
    tb = q_ref.shape[0]
    nh = q_ref.shape[1] // B_DK

    @pl.when(pl.program_id(0) == 0)
    def _():
        state_ref[...] = jnp.zeros_like(state_ref)

    tl = tail_ref[...]
    t_hi = tl.astype(BF16)
    t_mid = (tl - t_hi.astype(F32)).astype(BF16)
    dot = lambda a, b: jnp.dot(a, b, preferred_element_type=F32)
    x = dot(t_hi, wg_ref[0]) + (dot(t_mid, wg_ref[0]) + dot(t_hi, wg_ref[1])) + bgk_ref[...]
    gk = -_softplus(-x) * (1.0 / B_GATE_NORM)
    _, _, _, causal, _ = _chunk_masks(tb)
    b_all = _cumsum_rows(jnp.where(causal, 1.0, 0.0).astype(BF16), gk)

    sts = [state_ref[h] for h in range(nh)]
    for c in range(tb // CHUNK):
        lo, hi = c * CHUNK, (c + 1) * CHUNK
        for h in range(nh):
            qc = q_ref[lo:hi, h * B_DK:(h + 1) * B_DK] * (B_DK ** -0.5)
            kc = k_ref[lo:hi, h * B_DK:(h + 1) * B_DK]
            vc = v_ref[lo:hi, h * B_DV:(h + 1) * B_DV]
            bc = b_all[lo:hi, h * B_DK:(h + 1) * B_DK]
            a_c = _gla_chunk_scores(qc, kc, bc)
            bl = bc[CHUNK - 1:CHUNK]
            o = _bdot(a_c, vc) + _bdot_nt(qc * jnp.exp(bc), sts[h])
            sts[h] = sts[h] * jnp.exp(bl) + _bdot_tn(vc, kc * jnp.exp(bl - bc))
            o = o * lax.rsqrt(jnp.mean(o * o, axis=-1, keepdims=True) + EPS) * bnorm_ref[...]
            sl = slice(h * B_DV, (h + 1) * B_DV)
            o_ref[lo:hi, sl] = (o * _silu(zg_ref[lo:hi, sl])).astype(o_ref.dtype)
    for h in range(nh):
        state_ref[h] = sts[h]


def _mixer_b(z_b, z_tail, w_gk2, b_gk, b_norm):
    t = z_b.shape[0]
    wg = jnp.zeros((LANE, B_QK), F32).at[2 * A_HEADS:2 * A_HEADS + B_GATE_RANK].set(w_gk2.astype(F32))
    wg_hi = wg.astype(BF16)
    wg = jnp.stack([wg_hi, (wg - wg_hi.astype(F32)).astype(BF16)])
    col = lambda w, off: pl.BlockSpec((TB, w), lambda i: (i, off // w))
    full = lambda r, c: pl.BlockSpec((r, c), lambda i: (0, 0))
    return pl.pallas_call(
        _mixer_b_kernel,
        grid=(t // TB,),
        in_specs=[col(B_QK, 0), col(B_QK, B_QK), col(B_V, 2 * B_QK), col(B_V, 2 * B_QK + B_V), col(LANE, 0),
                  pl.BlockSpec((2, LANE, B_QK), lambda i: (0, 0, 0)), full(1, B_QK), full(1, B_DV)],
        out_specs=pl.BlockSpec((TB, B_V), lambda i: (i, 0)),
        out_shape=jax.ShapeDtypeStruct((t, B_V), BF16),
        scratch_shapes=[pltpu.VMEM((B_HEADS, B_DV, B_DK), F32)],
        compiler_params=_cparams(("arbitrary",), 48),
    )(z_b, z_b, z_b, z_b, z_tail, wg, b_gk.reshape(1, B_QK).astype(F32),
      b_norm.reshape(1, B_DV).astype(F32))


def _merge_kernel(oa_ref, ob_ref, ma_ref, mb_ref, x_ref, woa_ref, wob_ref, wout_ref, nf_ref,
                  wr_ref, br_ref, x2_ref, h2_ref, lt_ref):
    ya = jnp.dot(oa_ref[...], woa_ref[...], preferred_element_type=F32)
    yb = jnp.dot(ob_ref[...], wob_ref[...], preferred_element_type=F32)
    m = _sigmoid(ma_ref[...]) * ya + _sigmoid(mb_ref[...]) * yb
    x2 = x_ref[...] + jnp.dot(m.astype(BF16), wout_ref[...], preferred_element_type=F32)
    x2_ref[...] = x2
    h2 = x2 * lax.rsqrt(jnp.mean(x2 * x2, axis=-1, keepdims=True) + EPS) * nf_ref[...]
    h2_ref[...] = h2
    h_hi = h2.astype(BF16)
    h_mid = (h2 - h_hi.astype(F32)).astype(BF16)
    dot = lambda a, b: jnp.dot(a, b, preferred_element_type=F32)
    logits = dot(h_hi, wr_ref[0]) + (dot(h_mid, wr_ref[0]) + dot(h_hi, wr_ref[1]))
    lt_ref[...] = logits.T + br_ref[...]


def _merge(oa_g, ob_g, z_mix, x, w_oa, w_ob, w_out, norm_ffn, wr, br_t, tm):
    t, d = x.shape
    row = lambda w, c: pl.BlockSpec((tm, w), lambda i: (i, c))
    full = lambda a: pl.BlockSpec(a.shape, lambda i: (0,) * a.ndim, pipeline_mode=pl.Buffered(1))
    return pl.pallas_call(
        _merge_kernel,
        grid=(t // tm,),
        in_specs=[row(A_V, 0), row(B_V, 0), row(d, 0), row(d, 1), row(d, 0),
                  full(w_oa), full(w_ob), full(w_out), pl.BlockSpec((1, d), lambda i: (0, 0)),
                  full(wr), full(br_t)],
        out_specs=[row(d, 0), row(d, 0), pl.BlockSpec((LANE, tm), lambda i: (0, i))],
        out_shape=[jax.ShapeDtypeStruct((t, d), F32), jax.ShapeDtypeStruct((t, d), F32),
                   jax.ShapeDtypeStruct((LANE, t), F32)],
        compiler_params=_cparams(("parallel",), 56),
    )(oa_g, ob_g, z_mix, z_mix, x, w_oa, w_ob, w_out, norm_ffn.reshape(1, d).astype(F32), wr, br_t)


SEG = 256


def _route_kernel(lt_ref, pos_ref, gate_ref, blk_ref, oh_ref):
    t = lt_ref.shape[1]
    rid8 = lax.broadcasted_iota(jnp.int32, (8, t), 0)
    lg = jnp.where(rid8 < N_GROUPS, lt_ref[0:8, :], -jnp.inf)
    gmax = jnp.max(lg, axis=0, keepdims=True)
    g_idx = jnp.min(jnp.where(lg == gmax, rid8, 8), axis=0, keepdims=True)
    p_top = 1.0 / jnp.sum(jnp.exp(lg - gmax), axis=0, keepdims=True)

    les = jnp.zeros((EXP_PER_GROUP, t), F32)
    for g in range(N_GROUPS):
        les = jnp.where(g_idx == g, lt_ref[8 + g * EXP_PER_GROUP:8 + (g + 1) * EXP_PER_GROUP, :], les)
    m1 = jnp.max(les, axis=0, keepdims=True)
    i1 = jnp.min(jnp.where(les == m1, rid8, 8), axis=0, keepdims=True)
    les2 = jnp.where(rid8 == i1, -jnp.inf, les)
    m2 = jnp.max(les2, axis=0, keepdims=True)
    i2 = jnp.min(jnp.where(les2 == m2, rid8, 8), axis=0, keepdims=True)
    r = jnp.exp(m2 - m1)
    gate_ref[...] = jnp.concatenate([p_top / (1.0 + r), p_top * r / (1.0 + r),
                                     jnp.zeros((LANE - 2, t), F32)], axis=0).T
    e1 = g_idx * EXP_PER_GROUP + i1
    e2 = g_idx * EXP_PER_GROUP + i2

    rid32 = lax.broadcasted_iota(jnp.int32, (N_EXPERTS, t), 0)
    oh_ref[0] = jnp.where(rid32 == e1, 1.0, 0.0)
    oh_ref[1] = jnp.where(rid32 == e2, 1.0, 0.0)

    ui = lax.broadcasted_iota(jnp.int32, (SEG, SEG), 0)
    uj = lax.broadcasted_iota(jnp.int32, (SEG, SEG), 1)
    upper = jnp.where(ui < uj, 1.0, 0.0).astype(BF16)
    carry = jnp.zeros((N_EXPERTS, 1), F32)
    ranks = []
    for kk in range(2):
        segs = []
        for sg in range(t // SEG):
            oh = oh_ref[kk, :, sg * SEG:(sg + 1) * SEG]
            pre = jnp.dot(oh.astype(BF16), upper, preferred_element_type=F32) + carry
            segs.append(jnp.sum(oh * pre, axis=0, keepdims=True))
            carry = carry + jnp.sum(oh, axis=1, keepdims=True)
        ranks.append(jnp.concatenate(segs, axis=1))
    counts = carry
    nblk = jnp.floor((counts + (ROW_BLOCK - 1)) * (1.0 / ROW_BLOCK))
    li = lax.broadcasted_iota(jnp.int32, (N_EXPERTS, N_EXPERTS), 0)
    lj = lax.broadcasted_iota(jnp.int32, (N_EXPERTS, N_EXPERTS), 1)
    nb_b = jnp.broadcast_to(nblk, (N_EXPERTS, LANE))
    start_blk = _fdot(jnp.where(lj < li, 1.0, 0.0), nb_b)[:, 0:1]
    end_blk = start_blk + nblk
    start_row = start_blk * ROW_BLOCK

    pos_ref[...] = jnp.zeros_like(pos_ref)
    for kk in range(2):
        base = jnp.sum(oh_ref[kk] * start_row, axis=0, keepdims=True)
        pos_ref[kk:kk + 1, :] = (base + ranks[kk]).astype(jnp.int32)

    nb = blk_ref.shape[1]
    bid = lax.broadcasted_iota(jnp.int32, (N_EXPERTS, nb), 1).astype(F32)
    be = jnp.sum(jnp.where(end_blk <= bid, 1.0, 0.0), axis=0, keepdims=True)
    blk_ref[...] = jnp.zeros_like(blk_ref)
    blk_ref[0:1, :] = jnp.minimum(be, N_EXPERTS - 1.0).astype(jnp.int32)
    blk_ref[1:2, :] = jnp.broadcast_to(end_blk[N_EXPERTS - 1:N_EXPERTS, :], (1, nb)).astype(jnp.int32)
    ends = jnp.sum(jnp.where(li == lj, end_blk, 0.0), axis=0, keepdims=True)
    blk_ref[2:3, 0:N_EXPERTS] = ends.astype(jnp.int32)


def _route(lt, n_blk):
    t = lt.shape[1]
    nb = -(-n_blk // LANE) * LANE
    return pl.pallas_call(
        _route_kernel,
        out_shape=[jax.ShapeDtypeStruct((8, t), jnp.int32), jax.ShapeDtypeStruct((t, LANE), F32),
                   jax.ShapeDtypeStruct((8, nb), jnp.int32)],
        scratch_shapes=[pltpu.VMEM((2, N_EXPERTS, t), F32)],
        compiler_params=pltpu.CompilerParams(vmem_limit_bytes=48 * 2 ** 20),
    )(lt)


def _scatter_kernel(pos_ref, ends_ref, nu_ref, h_ref, xs_ref, zbuf, sem, zsem):
    tm = h_ref.shape[0]
    t = pl.num_programs(0) * tm
    base = pl.program_id(0) * tm
    n_blk = xs_ref.shape[0] // ROW_BLOCK

    def zero_block(blk):
        return pltpu.make_async_copy(zbuf, xs_ref.at[pl.ds(blk * ROW_BLOCK, ROW_BLOCK), :], zsem)

    def for_each_zero_block(fn):
        for e in range(N_EXPERTS):
            first = ends_ref[e - 1] if e else 0

            @pl.when(ends_ref[e] > first)
            def _():
                fn(zero_block(ends_ref[e] - 1))

        def tail(blk, c):
            fn(zero_block(blk))
            return c
        lax.fori_loop(nu_ref[0], n_blk, tail, 0)

    @pl.when(pl.program_id(0) == 0)
    def _():
        zbuf[...] = jnp.zeros_like(zbuf)
        for_each_zero_block(lambda c: c.start())
        for_each_zero_block(lambda c: c.wait())

    def row_copy(r, p):
        return pltpu.make_async_copy(h_ref.at[pl.ds(r, 1), :], xs_ref.at[pl.ds(p, 1), :], sem)

    for r in range(tm):
        for kk in range(2):
            row_copy(r, pos_ref[kk * t + base + r]).start()
    for r in range(2 * tm):
        row_copy(0, 0).wait()


def _scatter_rows(pos_flat, ends, n_used, h2, n_rows, tm):
    t, d = h2.shape
    return pl.pallas_call(
        _scatter_kernel,
        grid_spec=pltpu.PrefetchScalarGridSpec(
            num_scalar_prefetch=3,
            grid=(t // tm,),
            in_specs=[pl.BlockSpec((tm, d), lambda i, *_: (i, 0))],
            out_specs=pl.BlockSpec(memory_space=pl.ANY),
            scratch_shapes=[pltpu.VMEM((ROW_BLOCK, d), h2.dtype), pltpu.SemaphoreType.DMA(()),
                            pltpu.SemaphoreType.DMA(())],
        ),
        out_shape=jax.ShapeDtypeStruct((n_rows, d), h2.dtype),
        compiler_params=_cparams(("arbitrary",), 32),
    )(pos_flat, ends, n_used, h2)


GROUPS_PER_STEP = 4


def _expert_kernel(be_ref, nu_ref, ends_ref, x_ref, w1_hbm, w3_hbm, w2_hbm, o_ref,
                   w1s, w3s, w2s, w1b, w3b, w2b, sem):
    nu = nu_ref[0]

    def weight_copies(ex):
        return (pltpu.make_async_copy(w1_hbm.at[ex], w1s, sem.at[0]),
                pltpu.make_async_copy(w3_hbm.at[ex], w3s, sem.at[1]),
                pltpu.make_async_copy(w2_hbm.at[ex], w2s, sem.at[2]))

    def row_group(b, rows):
        e = be_ref[b]
        changed = jnp.logical_or(b == 0, e != be_ref[jnp.maximum(b - 1, 0)])

        @pl.when(jnp.logical_and(b == 0, nu > 0))
        def _():
            for c in weight_copies(e):
                c.start()

        @pl.when(jnp.logical_and(changed, b < nu))
        def _():
            for c in weight_copies(e):
                c.wait()
            w1b[...] = w1s[...].astype(BF16)
            w3b[...] = w3s[...].astype(BF16)
            w2b[...] = w2s[...].astype(BF16)
            nxt = ends_ref[e]

            @pl.when(nxt < nu)
            def _():
                for c in weight_copies(be_ref[nxt]):
                    c.start()

        @pl.when(b < nu)
        def _():
            xb = x_ref[rows, :].astype(BF16)
            a = jnp.dot(xb, w1b[...], preferred_element_type=F32)
            g = jnp.dot(xb, w3b[...], preferred_element_type=F32)
            o_ref[rows, :] = jnp.dot((_silu(a) * g).astype(BF16), w2b[...], preferred_element_type=F32)

        @pl.when(b >= nu)
        def _():
            o_ref[rows, :] = jnp.zeros((ROW_BLOCK, o_ref.shape[1]), o_ref.dtype)

    for s in range(GROUPS_PER_STEP):
        row_group(GROUPS_PER_STEP * pl.program_id(0) + s, slice(s * ROW_BLOCK, (s + 1) * ROW_BLOCK))


def _experts(blk_exp, n_used, ends, xs, w1, w3, w2):
    n_rows, d = xs.shape
    step_rows = GROUPS_PER_STEP * ROW_BLOCK
    assert n_rows % step_rows == 0
    rows = lambda g, be, nu, en: (jnp.minimum(g, jnp.maximum(nu[0] - 1, 0) // GROUPS_PER_STEP), 0)
    hbm = pl.BlockSpec(memory_space=pl.ANY)
    return pl.pallas_call(
        _expert_kernel,
        grid_spec=pltpu.PrefetchScalarGridSpec(
            num_scalar_prefetch=3,
            grid=(n_rows // step_rows,),
            in_specs=[pl.BlockSpec((step_rows, d), rows), hbm, hbm, hbm],
            out_specs=pl.BlockSpec((step_rows, d), lambda g, be, nu, en: (g, 0)),
            scratch_shapes=[pltpu.VMEM((d, D_FF), w1.dtype), pltpu.VMEM((d, D_FF), w3.dtype),
                            pltpu.VMEM((D_FF, d), w2.dtype),
                            pltpu.VMEM((d, D_FF), BF16), pltpu.VMEM((d, D_FF), BF16),
                            pltpu.VMEM((D_FF, d), BF16), pltpu.SemaphoreType.DMA((3,))],
        ),
        out_shape=jax.ShapeDtypeStruct((n_rows, d), F32),
        compiler_params=_cparams(("arbitrary",), 56),
    )(blk_exp, n_used, ends, xs, w1, w3, w2)


def _combine_kernel(pos_ref, x2_ref, gt_ref, nw_ref, yb_ref, o_ref, buf, sem, *, final):
    tm = x2_ref.shape[0]
    n = pl.num_programs(0)
    t = n * tm
    i = pl.program_id(0)

    def row_copy(slot, r, kk, p):
        return pltpu.make_async_copy(yb_ref.at[pl.ds(p, 1), :], buf.at[slot, kk, pl.ds(r, 1), :], sem.at[slot])

    def issue(step, slot):
        for r in range(tm):
            for kk in range(2):
                row_copy(slot, r, kk, pos_ref[kk * t + step * tm + r]).start()

    @pl.when(i == 0)
    def _():
        issue(0, 0)

    @pl.when(i + 1 < n)
    def _():
        issue(i + 1, (i + 1) % 2)

    slot = i % 2
    for r in range(2 * tm):
        row_copy(slot, 0, 0, 0).wait()

    gt = gt_ref[...]
    y = x2_ref[...] + gt[:, 0:1] * buf[slot, 0] + gt[:, 1:2] * buf[slot, 1]
    if final:
        y = y * lax.rsqrt(jnp.mean(y * y, axis=-1, keepdims=True) + EPS) * nw_ref[...]
    o_ref[...] = y


def _combine(pos_flat, x2, gates_t, norm_final, yb, tm, final):
    t, d = x2.shape
    return pl.pallas_call(
        functools.partial(_combine_kernel, final=final),
        grid_spec=pltpu.PrefetchScalarGridSpec(
            num_scalar_prefetch=1,
            grid=(t // tm,),
            in_specs=[pl.BlockSpec((tm, d), lambda i, pos: (i, 0)),
                      pl.BlockSpec((tm, LANE), lambda i, pos: (i, 0)),
                      pl.BlockSpec((1, d), lambda i, pos: (0, 0)),
                      pl.BlockSpec(memory_space=pl.ANY)],
            out_specs=pl.BlockSpec((tm, d), lambda i, pos: (i, 0)),
            scratch_shapes=[pltpu.VMEM((2, 2, tm, d), F32), pltpu.SemaphoreType.DMA((2,))],
        ),
        out_shape=jax.ShapeDtypeStruct((t, d), F32),
        compiler_params=_cparams(("arbitrary",), 32),
    )(pos_flat, x2, gates_t, norm_final.reshape(1, d).astype(F32), yb)


def _layer(x, norm_mix, w_in, conv_a, a_log, dt_bias, a_norm, w_gk2, b_gk, b_norm,
           w_oa, w_ob, w_out, norm_ffn, w_rg, b_rg, w_re, b_re, w1, w3, w2, norm_final, final):
    t, d = x.shape

    a_end = 4 * A_QK
    ga_end = a_end + 2 * A_HEADS
    b_end = ga_end + 2 * B_QK + 2 * B_V
    lr_end = b_end + B_GATE_RANK

    tm_big = min(1024, t)
    wt = w_in.T
    h, z_lr, ga, gct = _prologue(x, norm_mix, wt, a_end, b_end, a_log, dt_bias)
    z_a = _proj(h, wt, 0, a_end, tm_big, 1024)
    z_b = _proj(h, wt, ga_end, b_end - ga_end, tm_big, 1024)
    z_mix = _proj(h, wt, lr_end, 2 * d, tm_big, 1024)

    oa_g = _mixer_a(z_a, ga, gct, conv_a.astype(F32), a_norm)
    ob_g = _mixer_b(z_b, z_lr, w_gk2, b_gk, b_norm)

    wr = jnp.zeros((d, LANE), F32).at[:, 0:N_GROUPS].set(w_rg.astype(F32))
    wr = wr.at[:, 8:8 + N_EXPERTS].set(w_re.reshape(d, N_EXPERTS).astype(F32))
    wr_hi = wr.astype(BF16)
    wr = jnp.stack([wr_hi, (wr - wr_hi.astype(F32)).astype(BF16)])
    br_t = jnp.zeros((LANE, 1), F32).at[0:N_GROUPS, 0].set(b_rg.astype(F32))
    br_t = br_t.at[8:8 + N_EXPERTS, 0].set(b_re.reshape(N_EXPERTS).astype(F32))
    x2, h2, lt = _merge(oa_g, ob_g, z_mix, x, w_oa.astype(BF16), w_ob.astype(BF16), w_out.astype(BF16),
                        norm_ffn, wr, br_t, min(256, t))

    n_blk = (2 * t + ROW_BLOCK - 1) // ROW_BLOCK + N_EXPERTS
    n_blk += -n_blk % GROUPS_PER_STEP
    n_rows = n_blk * ROW_BLOCK
    pos, gates, blk = _route(lt, n_blk)
    pos_flat = pos[0:2].reshape(2 * t)
    blk_exp, n_used, ends = blk[0, :n_blk], blk[1, 0:1], blk[2, :N_EXPERTS]
    xs = _scatter_rows(pos_flat, ends, n_used, h2, n_rows, min(512, t))
    yb = _experts(blk_exp, n_used, ends, xs, w1, w3, w2)
    return _combine(pos_flat, x2, gates, norm_final, yb, min(256, t), final)


def kernel(x, norm_mix, w_in, conv_a, a_log, dt_bias, a_norm, w_gk2, b_gk, b_norm, w_oa, w_ob, w_out,
           norm_ffn, w_rg, b_rg, w_re, b_re, w1, w3, w2, norm_final):
    bsz, seq, d = x.shape
    assert bsz == 1, "one sequence per call"
    depth = norm_mix.shape[0]
    y = x.reshape(seq, d)
    for l in range(depth):
        y = _layer(y, norm_mix[l], w_in[l], conv_a[l], a_log[l], dt_bias[l], a_norm[l], w_gk2[l], b_gk[l],
                   b_norm[l], w_oa[l], w_ob[l], w_out[l], norm_ffn[l], w_rg[l], b_rg[l], w_re[l], b_re[l],
                   w1[l], w3[l], w2[l], norm_final, l == depth - 1)
    return y.reshape(bsz, seq, d)
```

```python
import functools

import jax
import jax.numpy as jnp
from jax import lax
from jax.experimental import pallas as pl
from jax.experimental.pallas import tpu as pltpu

D_MODEL = 2048
CHUNK = 64
EPS = 1e-6
A_HEADS, A_DK, A_DV, A_CONV = 8, 128, 128, 4
A_QK, A_V = A_HEADS * A_DK, A_HEADS * A_DV
B_HEADS, B_DK, B_DV, B_GATE_RANK, B_GATE_NORM = 4, 128, 256, 16, 16.0
B_QK, B_V = B_HEADS * B_DK, B_HEADS * B_DV
N_GROUPS, EXP_PER_GROUP, D_FF = 4, 8, 512
N_EXPERTS = N_GROUPS * EXP_PER_GROUP
ROW_BLOCK = 256
LANE = 128
SUB = 16
TB = 256
A_HG = 8
TA = 128
NEG = -1e30

F32 = jnp.float32
BF16 = jnp.bfloat16
HI = lax.Precision.HIGHEST


def _cparams(sem, vmem_mib):
    return pltpu.CompilerParams(dimension_semantics=sem, vmem_limit_bytes=vmem_mib * 2 ** 20)


def _bdot(a, b):
    return jnp.dot(a.astype(BF16), b.astype(BF16), preferred_element_type=F32)


def _bdot_nt(a, b):
    return lax.dot_general(a.astype(BF16), b.astype(BF16), (((1,), (1,)), ((), ())),
                           preferred_element_type=F32)


def _bdot_tn(a, b):
    return lax.dot_general(a.astype(BF16), b.astype(BF16), (((0,), (0,)), ((), ())),
                           preferred_element_type=F32)


def _fdot(a, b):
    return jnp.dot(a, b, preferred_element_type=F32, precision=HI)


def _sigmoid(x):
    return 1.0 / (1.0 + jnp.exp(-x))


def _silu(x):
    return x * _sigmoid(x)


def _softplus(x):
    return jnp.maximum(x, 0.0) + jnp.log(1.0 + jnp.exp(-jnp.abs(x)))


def _prologue_kernel(x_ref, nw_ref, wga_ref, wlr_ref, alog_ref, dtb_ref, h_ref, zlr_ref, ga_ref, gct_ref,
                     wga_b, wlr_b):
    @pl.when(pl.program_id(0) == 0)
    def _():
        wga_b[...] = wga_ref[...].astype(BF16)
        wlr_b[...] = wlr_ref[...].astype(BF16)

    x = x_ref[...]
    h = (x * lax.rsqrt(jnp.mean(x * x, axis=-1, keepdims=True) + EPS) * nw_ref[...]).astype(BF16)
    h_ref[...] = h
    proj = lambda w: lax.dot_general(h, w, (((1,), (1,)), ((), ())), preferred_element_type=F32)
    zlr_ref[...] = proj(wlr_b[...])
    _gates_a(proj(wga_b[...]), alog_ref[...], dtb_ref[...], ga_ref, gct_ref)


def _prologue(x, norm_mix, wt, a_end, b_end, a_log, dt_bias):
    t, d = x.shape
    assert a_end % LANE == 0 and b_end % LANE == 2 * A_HEADS
    pad = lambda p: jnp.pad(p.astype(F32), (A_HEADS, LANE - 2 * A_HEADS)).reshape(1, LANE)
    const = lambda c: pl.BlockSpec((1, c), lambda i: (0, 0))
    rows = lambda c: pl.BlockSpec((TB, c), lambda i: (i, 0))
    return pl.pallas_call(
        _prologue_kernel,
        grid=(t // TB,),
        in_specs=[rows(d), const(d),
                  pl.BlockSpec((LANE, d), lambda i: (a_end // LANE, 0)),
                  pl.BlockSpec((LANE, d), lambda i: (b_end // LANE, 0)),
                  const(LANE), const(LANE)],
        out_specs=[rows(d), rows(LANE), rows(LANE), pl.BlockSpec((A_HEADS, TB), lambda i: (0, i))],
        out_shape=[jax.ShapeDtypeStruct((t, d), BF16), jax.ShapeDtypeStruct((t, LANE), F32),
                   jax.ShapeDtypeStruct((t, LANE), F32), jax.ShapeDtypeStruct((A_HEADS, t), F32)],
        scratch_shapes=[pltpu.VMEM((LANE, d), BF16), pltpu.VMEM((LANE, d), BF16)],
        compiler_params=_cparams(("arbitrary",), 32),
    )(x, norm_mix.reshape(1, d), wt, wt, pad(a_log), pad(dt_bias))


PREP_ROWS = 256


def _proj_kernel(a_ref, wt_ref, o_ref, wb_ref):
    tn = wt_ref.shape[0]

    @pl.when(pl.program_id(1) == 0)
    def _():
        for r in range(0, tn, PREP_ROWS):
            wb_ref[r:r + PREP_ROWS, :] = wt_ref[r:r + PREP_ROWS, :].astype(BF16)

    o_ref[...] = lax.dot_general(a_ref[...], wb_ref[...], (((1,), (1,)), ((), ())), preferred_element_type=F32)


def _proj(a, wt, col0, n, tm, tn):
    m, k = a.shape
    assert n % tn == 0 and col0 % 8 == 0
    return pl.pallas_call(
        _proj_kernel,
        grid=(n // tn, m // tm),
        in_specs=[pl.BlockSpec((tm, k), lambda j, i: (i, 0)),
                  pl.BlockSpec((pl.Element(tn), pl.Element(k)), lambda j, i: (pl.multiple_of(col0 + j * tn, 8), 0))],
        out_specs=pl.BlockSpec((tm, tn), lambda j, i: (i, j)),
        out_shape=jax.ShapeDtypeStruct((m, n), F32),
        scratch_shapes=[pltpu.VMEM((tn, k), BF16)],
        compiler_params=_cparams(("arbitrary", "arbitrary"), 48),
    )(a, wt)


def _chunk_masks(tb):
    row = lax.broadcasted_iota(jnp.int32, (tb, tb), 0)
    col = lax.broadcasted_iota(jnp.int32, (tb, tb), 1)
    same = (row // CHUNK) == (col // CHUNK)
    return row, col, same, same & (col <= row), same & (col < row)


def _lane_pick(x, idx):
    lane = lax.broadcasted_iota(jnp.int32, x.shape, 1)
    return jnp.sum(jnp.where(lane == idx, x, 0.0), axis=-1, keepdims=True)


def _causal_conv_silu(x_ref, w_ref, halo_ref, cbuf_ref, idx, tb):
    cbuf_ref[idx, 0:8, :] = halo_ref[idx]
    cbuf_ref[idx, 8:8 + tb, :] = x_ref[...]
    halo_ref[idx] = x_ref[tb - 8:tb, :]
    w = w_ref[...]
    acc = w[A_CONV - 1:A_CONV, :] * x_ref[...]
    for j in range(A_CONV - 1):
        off = 8 - (A_CONV - 1) + j
        acc = acc + w[j:j + 1, :] * cbuf_ref[idx, off:off + tb, :]
    return _silu(acc)


def _gates_a(gl, a_log, dt_bias, ga_ref, gct_ref):
    tb = gl.shape[0]
    beta = _sigmoid(gl)
    g = -jnp.exp(a_log) * _softplus(gl + dt_bias)
    _, _, same, causal, _ = _chunk_masks(tb)
    gc = _cumsum_rows(jnp.where(causal, 1.0, 0.0).astype(BF16), g)
    glast = _cumsum_rows(jnp.where(same, 1.0, 0.0).astype(BF16), g)
    lane = lax.broadcasted_iota(jnp.int32, (tb, LANE), 1)
    ga_ref[...] = jnp.where(lane < A_HEADS, beta,
                            jnp.where(lane < 2 * A_HEADS, gc, pltpu.roll(glast, A_HEADS, 1)))
    gct_ref[...] = gc.T[A_HEADS:2 * A_HEADS, :]


def _delta_heads(qs, ks, vs, ga, gc_rows, hs, sts, tb):
    n = len(qs)
    rng = range(n)
    _, _, _, causal, strict = _chunk_masks(tb)
    qs = [q * lax.rsqrt(jnp.sum(q * q, axis=-1, keepdims=True) + EPS) * (A_DK ** -0.5) for q in qs]
    ks = [k * lax.rsqrt(jnp.sum(k * k, axis=-1, keepdims=True) + EPS) for k in ks]
    beta = [_lane_pick(ga, h) for h in hs]
    gc = [_lane_pick(ga, h + A_HEADS) for h in hs]
    glast = [_lane_pick(ga, h + 2 * A_HEADS) for h in hs]
    decay = [jnp.exp(jnp.where(causal, gc[i] - gc_rows[i], NEG)) for i in rng]
    kb = [ks[i] * beta[i] for i in rng]

    n_pow = [jnp.where(strict, _bdot_nt(kb[i], ks[i]) * decay[i], 0.0) * -1.0 for i in rng]
    t_mat = list(n_pow)
    lvl = 2
    while lvl < CHUNK:
        n_pow = [_bdot(m, m) for m in n_pow]
        t_mat = [t_mat[i] + n_pow[i] + _bdot(t_mat[i], n_pow[i]) for i in rng]
        lvl *= 2

    egc = [jnp.exp(g) for g in gc]
    rhs = [jnp.concatenate([vs[i] * beta[i], kb[i] * egc[i]], axis=1) for i in rng]
    uw = [rhs[i] + _bdot(t_mat[i], rhs[i]) for i in rng]
    qk = [_bdot_nt(qs[i], ks[i]) * decay[i] for i in rng]
    qkuw = [_bdot(qk[i], uw[i]) for i in rng]
    o_local = [x[:, :A_DV] for x in qkuw]
    q_eff = [qs[i] * egc[i] - qkuw[i][:, A_DV:] for i in rng]
    k_dec = [ks[i] * jnp.exp(glast[i] - gc[i]) for i in rng]
    eg_last = [jnp.exp(g) for g in glast]

    sts = list(sts)
    outs = [[] for _ in rng]
    for c in range(tb // CHUNK):
        lo, hi = c * CHUNK, (c + 1) * CHUNK
        bg = [_bdot_tn(uw[i][lo:hi], k_dec[i][lo:hi]) for i in rng]
        for i in rng:
            outs[i].append(o_local[i][lo:hi] + _bdot_nt(q_eff[i][lo:hi], sts[i]))
        sts = [sts[i] * eg_last[i][lo:lo + 1, :] + bg[i][:A_DV] - _bdot(sts[i], bg[i][A_DV:]) for i in rng]
    return [jnp.concatenate(o, axis=0) for o in outs], sts


def _mixer_a_kernel(xq_ref, xk_ref, xv_ref, z_ref, ga_ref, gct_ref, wq_ref, wk_ref, wv_ref,
                    anorm_ref, o_ref, halo_ref, cbuf_ref, state_ref):
    tb = xq_ref.shape[0]
    hg = xq_ref.shape[1] // A_DK

    @pl.when(pl.program_id(1) == 0)
    def _():
        halo_ref[...] = jnp.zeros_like(halo_ref)
        state_ref[...] = jnp.zeros_like(state_ref)

    q = _causal_conv_silu(xq_ref, wq_ref, halo_ref, cbuf_ref, 0, tb)
    k = _causal_conv_silu(xk_ref, wk_ref, halo_ref, cbuf_ref, 1, tb)
    v = _causal_conv_silu(xv_ref, wv_ref, halo_ref, cbuf_ref, 2, tb)
    hs = [pl.program_id(0) * hg + j for j in range(hg)]
    sls = [slice(j * A_DK, (j + 1) * A_DK) for j in range(hg)]
    sts = [state_ref[j] for j in range(hg)]
    gc_rows = [gct_ref[pl.ds(h, 1), :] for h in hs]
    for r0 in range(0, tb, TA):
        rs = slice(r0, r0 + TA)
        outs, sts = _delta_heads([q[rs, s] for s in sls], [k[rs, s] for s in sls], [v[rs, s] for s in sls],
                                 ga_ref[rs, :], [g[:, rs] for g in gc_rows], hs, sts, TA)
        for j in range(hg):
            o = outs[j]
            o = o * lax.rsqrt(jnp.mean(o * o, axis=-1, keepdims=True) + EPS) * anorm_ref[...]
            o_ref[rs, sls[j]] = (o * _silu(z_ref[rs, sls[j]])).astype(o_ref.dtype)
    for j in range(hg):
        state_ref[j] = sts[j]


def _mixer_a(z_a, ga, gct, conv_a, a_norm):
    t = z_a.shape[0]
    ng = A_HEADS // A_HG
    wid = A_HG * A_DK
    blk = lambda off: pl.BlockSpec((TB, wid), lambda g, i: (i, off + g))
    cblk = lambda off: pl.BlockSpec((A_CONV, wid), lambda g, i: (0, off + g))
    return pl.pallas_call(
        _mixer_a_kernel,
        grid=(ng, t // TB),
        in_specs=[blk(0), blk(ng), blk(2 * ng), blk(3 * ng),
                  pl.BlockSpec((TB, LANE), lambda g, i: (i, 0)),
                  pl.BlockSpec((A_HEADS, TB), lambda g, i: (0, i)),
                  cblk(0), cblk(ng), cblk(2 * ng), pl.BlockSpec((1, LANE), lambda g, i: (0, 0))],
        out_specs=pl.BlockSpec((TB, wid), lambda g, i: (i, g)),
        out_shape=jax.ShapeDtypeStruct((t, A_V), BF16),
        scratch_shapes=[pltpu.VMEM((3, 8, wid), F32), pltpu.VMEM((3, 8 + TB, wid), F32),
                        pltpu.VMEM((A_HG, A_DV, A_DK), F32)],
        compiler_params=_cparams(("parallel", "arbitrary"), 48),
    )(z_a, z_a, z_a, z_a, ga, gct, conv_a, conv_a, conv_a,
      a_norm.reshape(1, A_DV).astype(F32))


def _cumsum_rows(mask, x):
    hi = x.astype(BF16)
    r1 = x - hi.astype(F32)
    mid = r1.astype(BF16)
    lo = (r1 - mid.astype(F32)).astype(BF16)
    dot = lambda p: jnp.dot(mask, p, preferred_element_type=F32)
    return dot(hi) + dot(mid) + dot(lo)


def _gla_chunk_scores(qc, kc, bc):
    rid = lax.broadcasted_iota(jnp.int32, (SUB, B_DK), 0)
    rid_lo = lax.broadcasted_iota(jnp.int32, (SUB // 2, B_DK), 0) + SUB // 2
    lane_c = lax.broadcasted_iota(jnp.int32, (SUB, CHUNK), 1)
    crow = lax.broadcasted_iota(jnp.int32, (CHUNK, B_DK), 0)
    half = SUB // 2
    rows = []
    for si in range(CHUNK // SUB):
        r0 = si * SUB
        qb, bb = qc[r0:r0 + SUB], bc[r0:r0 + SUB]
        ys = []
        for j in range(SUB):
            bj = bc[r0 + j:r0 + j + 1]
            if j < half:
                ys.append(qb * jnp.exp(jnp.where(rid >= j, bb - bj, NEG)))
            else:
                ys.append(jnp.zeros((half, B_DK), F32))
                ys.append(qb[half:] * jnp.exp(jnp.where(rid_lo >= j, bb[half:] - bj, NEG)))
        r = _bdot_nt(jnp.concatenate(ys, axis=0), kc)
        blk = jnp.zeros((SUB, CHUNK), F32)
        for j in range(SUB):
            blk = jnp.where(lane_c == r0 + j, r[j * SUB:(j + 1) * SUB], blk)
        if si > 0:
            bref = bc[r0:r0 + 1]
            qt = qb * jnp.exp(bb - bref)
            kt = kc * jnp.exp(jnp.where(crow < r0, bref - bc, NEG))
            blk = blk + _bdot_nt(qt, kt)
        rows.append(blk)
    return jnp.concatenate(rows, axis=0)


def _mixer_b_kernel(q_ref, k_ref, v_ref, zg_ref, tail_ref, wg_ref, bgk_ref, bnorm_ref, o_ref, state_ref):
    tb = q_ref.shape[0]
    nh = q_ref.shape[1] // B_DK

    @pl.when(pl.program_id(0) == 0)
    def _():
        state_ref[...] = jnp.zeros_like(state_ref)

    tl = tail_ref[...]
    t_hi = tl.astype(BF16)
    t_mid = (tl - t_hi.astype(F32)).astype(BF16)
    dot = lambda a, b: jnp.dot(a, b, preferred_element_type=F32)
    x = dot(t_hi, wg_ref[0]) + (dot(t_mid, wg_ref[0]) + dot(t_hi, wg_ref[1])) + bgk_ref[...]
    gk = -_softplus(-x) * (1.0 / B_GATE_NORM)
    _, _, _, causal, _ = _chunk_masks(tb)
    b_all = _cumsum_rows(jnp.where(causal, 1.0, 0.0).astype(BF16), gk)

    sts = [state_ref[h] for h in range(nh)]
    for c in range(tb // CHUNK):
        lo, hi = c * CHUNK, (c + 1) * CHUNK
        for h in range(nh):
            qc = q_ref[lo:hi, h * B_DK:(h + 1) * B_DK] * (B_DK ** -0.5)
            kc = k_ref[lo:hi, h * B_DK:(h + 1) * B_DK]
            vc = v_ref[lo:hi, h * B_DV:(h + 1) * B_DV]
            bc = b_all[lo:hi, h * B_DK:(h + 1) * B_DK]
            a_c = _gla_chunk_scores(qc, kc, bc)
            bl = bc[CHUNK - 1:CHUNK]
            o = _bdot(a_c, vc) + _bdot_nt(qc * jnp.exp(bc), sts[h])
            sts[h] = sts[h] * jnp.exp(bl) + _bdot_tn(vc, kc * jnp.exp(bl - bc))
            o = o * lax.rsqrt(jnp.mean(o * o, axis=-1, keepdims=True) + EPS) * bnorm_ref[...]
            sl = slice(h * B_DV, (h + 1) * B_DV)
            o_ref[lo:hi, sl] = (o * _silu(zg_ref[lo:hi, sl])).astype(o_ref.dtype)
    for h in range(nh):
        state_ref[h] = sts[h]


def _mixer_b(z_b, z_tail, w_gk2, b_gk, b_norm):
    t = z_b.shape[0]
    wg = jnp.zeros((LANE, B_QK), F32).at[2 * A_HEADS:2 * A_HEADS + B_GATE_RANK].set(w_gk2.astype(F32))
    wg_hi = wg.astype(BF16)
    wg = jnp.stack([wg_hi, (wg - wg_hi.astype(F32)).astype(BF16)])
    col = lambda w, off: pl.BlockSpec((TB, w), lambda i: (i, off // w))
    full = lambda r, c: pl.BlockSpec((r, c), lambda i: (0, 0))
    return pl.pallas_call(
        _mixer_b_kernel,
        grid=(t // TB,),
        in_specs=[col(B_QK, 0), col(B_QK, B_QK), col(B_V, 2 * B_QK), col(B_V, 2 * B_QK + B_V), col(LANE, 0),
                  pl.BlockSpec((2, LANE, B_QK), lambda i: (0, 0, 0)), full(1, B_QK), full(1, B_DV)],
        out_specs=pl.BlockSpec((TB, B_V), lambda i: (i, 0)),
        out_shape=jax.ShapeDtypeStruct((t, B_V), BF16),
        scratch_shapes=[pltpu.VMEM((B_HEADS, B_DV, B_DK), F32)],
        compiler_params=_cparams(("arbitrary",), 48),
    )(z_b, z_b, z_b, z_b, z_tail, wg, b_gk.reshape(1, B_QK).astype(F32),
      b_norm.reshape(1, B_DV).astype(F32))


def _merge_kernel(oa_ref, ob_ref, ma_ref, mb_ref, x_ref, woa_ref, wob_ref, wout_ref, nf_ref,
                  wr_ref, br_ref, x2_ref, h2_ref, lt_ref):
    ya = jnp.dot(oa_ref[...], woa_ref[...], preferred_element_type=F32)
    yb = jnp.dot(ob_ref[...], wob_ref[...], preferred_element_type=F32)
    m = _sigmoid(ma_ref[...]) * ya + _sigmoid(mb_ref[...]) * yb
    x2 = x_ref[...] + jnp.dot(m.astype(BF16), wout_ref[...], preferred_element_type=F32)
    x2_ref[...] = x2
    h2 = x2 * lax.rsqrt(jnp.mean(x2 * x2, axis=-1, keepdims=True) + EPS) * nf_ref[...]
    h2_ref[...] = h2
    h_hi = h2.astype(BF16)
    h_mid = (h2 - h_hi.astype(F32)).astype(BF16)
    dot = lambda a, b: jnp.dot(a, b, preferred_element_type=F32)
    logits = dot(h_hi, wr_ref[0]) + (dot(h_mid, wr_ref[0]) + dot(h_hi, wr_ref[1]))
    lt_ref[...] = logits.T + br_ref[...]


def _merge(oa_g, ob_g, z_mix, x, w_oa, w_ob, w_out, norm_ffn, wr, br_t, tm):
    t, d = x.shape
    row = lambda w, c: pl.BlockSpec((tm, w), lambda i: (i, c))
    full = lambda a: pl.BlockSpec(a.shape, lambda i: (0,) * a.ndim, pipeline_mode=pl.Buffered(1))
    return pl.pallas_call(
        _merge_kernel,
        grid=(t // tm,),
        in_specs=[row(A_V, 0), row(B_V, 0), row(d, 0), row(d, 1), row(d, 0),
                  full(w_oa), full(w_ob), full(w_out), pl.BlockSpec((1, d), lambda i: (0, 0)),
                  full(wr), full(br_t)],
        out_specs=[row(d, 0), row(d, 0), pl.BlockSpec((LANE, tm), lambda i: (0, i))],
        out_shape=[jax.ShapeDtypeStruct((t, d), F32), jax.ShapeDtypeStruct((t, d), F32),
                   jax.ShapeDtypeStruct((LANE, t), F32)],
        compiler_params=_cparams(("parallel",), 56),
    )(oa_g, ob_g, z_mix, z_mix, x, w_oa, w_ob, w_out, norm_ffn.reshape(1, d).astype(F32), wr, br_t)


SEG = 256


def _route_kernel(lt_ref, pos_ref, gate_ref, blk_ref, oh_ref):
    t = lt_ref.shape[1]
    rid8 = lax.broadcasted_iota(jnp.int32, (8, t), 0)
    lg = jnp.where(rid8 < N_GROUPS, lt_ref[0:8, :], -jnp.inf)
    gmax = jnp.max(lg, axis=0, keepdims=True)
    g_idx = jnp.min(jnp.where(lg == gmax, rid8, 8), axis=0, keepdims=True)
    p_top = 1.0 / jnp.sum(jnp.exp(lg - gmax), axis=0, keepdims=True)

    les = jnp.zeros((EXP_PER_GROUP, t), F32)
    for g in range(N_GROUPS):
        les = jnp.where(g_idx == g, lt_ref[8 + g * EXP_PER_GROUP:8 + (g + 1) * EXP_PER_GROUP, :], les)
    m1 = jnp.max(les, axis=0, keepdims=True)
    i1 = jnp.min(jnp.where(les == m1, rid8, 8), axis=0, keepdims=True)
    les2 = jnp.where(rid8 == i1, -jnp.inf, les)
    m2 = jnp.max(les2, axis=0, keepdims=True)
    i2 = jnp.min(jnp.where(les2 == m2, rid8, 8), axis=0, keepdims=True)
    r = jnp.exp(m2 - m1)
    gate_ref[...] = jnp.concatenate([p_top / (1.0 + r), p_top * r / (1.0 + r),
                                     jnp.zeros((LANE - 2, t), F32)], axis=0).T
    e1 = g_idx * EXP_PER_GROUP + i1
    e2 = g_idx * EXP_PER_GROUP + i2

    rid32 = lax.broadcasted_iota(jnp.int32, (N_EXPERTS, t), 0)
    oh_ref[0] = jnp.where(rid32 == e1, 1.0, 0.0)
    oh_ref[1] = jnp.where(rid32 == e2, 1.0, 0.0)

    ui = lax.broadcasted_iota(jnp.int32, (SEG, SEG), 0)
    uj = lax.broadcasted_iota(jnp.int32, (SEG, SEG), 1)
    upper = jnp.where(ui < uj, 1.0, 0.0).astype(BF16)
    carry = jnp.zeros((N_EXPERTS, 1), F32)
    ranks = []
    for kk in range(2):
        segs = []
        for sg in range(t // SEG):
            oh = oh_ref[kk, :, sg * SEG:(sg + 1) * SEG]
            pre = jnp.dot(oh.astype(BF16), upper, preferred_element_type=F32) + carry
            segs.append(jnp.sum(oh * pre, axis=0, keepdims=True))
            carry = carry + jnp.sum(oh, axis=1, keepdims=True)
        ranks.append(jnp.concatenate(segs, axis=1))
    counts = carry
    nblk = jnp.floor((counts + (ROW_BLOCK - 1)) * (1.0 / ROW_BLOCK))
    li = lax.broadcasted_iota(jnp.int32, (N_EXPERTS, N_EXPERTS), 0)
    lj = lax.broadcasted_iota(jnp.int32, (N_EXPERTS, N_EXPERTS), 1)
    nb_b = jnp.broadcast_to(nblk, (N_EXPERTS, LANE))
    start_blk = _fdot(jnp.where(lj < li, 1.0, 0.0), nb_b)[:, 0:1]
    end_blk = start_blk + nblk
    start_row = start_blk * ROW_BLOCK

    pos_ref[...] = jnp.zeros_like(pos_ref)
    for kk in range(2):
        base = jnp.sum(oh_ref[kk] * start_row, axis=0, keepdims=True)
        pos_ref[kk:kk + 1, :] = (base + ranks[kk]).astype(jnp.int32)

    nb = blk_ref.shape[1]
    bid = lax.broadcasted_iota(jnp.int32, (N_EXPERTS, nb), 1).astype(F32)
    be = jnp.sum(jnp.where(end_blk <= bid, 1.0, 0.0), axis=0, keepdims=True)
    blk_ref[...] = jnp.zeros_like(blk_ref)
    blk_ref[0:1, :] = jnp.minimum(be, N_EXPERTS - 1.0).astype(jnp.int32)
    blk_ref[1:2, :] = jnp.broadcast_to(end_blk[N_EXPERTS - 1:N_EXPERTS, :], (1, nb)).astype(jnp.int32)
    ends = jnp.sum(jnp.where(li == lj, end_blk, 0.0), axis=0, keepdims=True)
    blk_ref[2:3, 0:N_EXPERTS] = ends.astype(jnp.int32)


def _route(lt, n_blk):
    t = lt.shape[1]
    nb = -(-n_blk // LANE) * LANE
    return pl.pallas_call(
        _route_kernel,
        out_shape=[jax.ShapeDtypeStruct((8, t), jnp.int32), jax.ShapeDtypeStruct((t, LANE), F32),
                   jax.ShapeDtypeStruct((8, nb), jnp.int32)],
        scratch_shapes=[pltpu.VMEM((2, N_EXPERTS, t), F32)],
        compiler_params=pltpu.CompilerParams(vmem_limit_bytes=48 * 2 ** 20),
    )(lt)


def _scatter_kernel(pos_ref, ends_ref, nu_ref, h_ref, xs_ref, zbuf, sem, zsem):
    tm = h_ref.shape[0]
    t = pl.num_programs(0) * tm
    base = pl.program_id(0) * tm
    n_blk = xs_ref.shape[0] // ROW_BLOCK

    def zero_block(blk):
        return pltpu.make_async_copy(zbuf, xs_ref.at[pl.ds(blk * ROW_BLOCK, ROW_BLOCK), :], zsem)

    def for_each_zero_block(fn):
        for e in range(N_EXPERTS):
            first = ends_ref[e - 1] if e else 0

            @pl.when(ends_ref[e] > first)
            def _():
                fn(zero_block(ends_ref[e] - 1))

        def tail(blk, c):
            fn(zero_block(blk))
            return c
        lax.fori_loop(nu_ref[0], n_blk, tail, 0)

    @pl.when(pl.program_id(0) == 0)
    def _():
        zbuf[...] = jnp.zeros_like(zbuf)
        for_each_zero_block(lambda c: c.start())
        for_each_zero_block(lambda c: c.wait())

    def row_copy(r, p):
        return pltpu.make_async_copy(h_ref.at[pl.ds(r, 1), :], xs_ref.at[pl.ds(p, 1), :], sem)

    for r in range(tm):
        for kk in range(2):
            row_copy(r, pos_ref[kk * t + base + r]).start()
    for r in range(2 * tm):
        row_copy(0, 0).wait()


def _scatter_rows(pos_flat, ends, n_used, h2, n_rows, tm):
    t, d = h2.shape
    return pl.pallas_call(
        _scatter_kernel,
        grid_spec=pltpu.PrefetchScalarGridSpec(
            num_scalar_prefetch=3,
            grid=(t // tm,),
            in_specs=[pl.BlockSpec((tm, d), lambda i, *_: (i, 0))],
            out_specs=pl.BlockSpec(memory_space=pl.ANY),
            scratch_shapes=[pltpu.VMEM((ROW_BLOCK, d), h2.dtype), pltpu.SemaphoreType.DMA(()),
                            pltpu.SemaphoreType.DMA(())],
        ),
        out_shape=jax.ShapeDtypeStruct((n_rows, d), h2.dtype),
        compiler_params=_cparams(("arbitrary",), 32),
    )(pos_flat, ends, n_used, h2)


GROUPS_PER_STEP = 2


def _expert_kernel(be_ref, nu_ref, ends_ref, x_ref, w1_hbm, w3_hbm, w2_hbm, o_ref,
                   w1s, w3s, w2s, w1b, w3b, w2b, sem):
    nu = nu_ref[0]

    def weight_copies(ex):
        return (pltpu.make_async_copy(w1_hbm.at[ex], w1s, sem.at[0]),
                pltpu.make_async_copy(w3_hbm.at[ex], w3s, sem.at[1]),
                pltpu.make_async_copy(w2_hbm.at[ex], w2s, sem.at[2]))

    def row_group(b, rows):
        e = be_ref[b]
        changed = jnp.logical_or(b == 0, e != be_ref[jnp.maximum(b - 1, 0)])

        @pl.when(jnp.logical_and(b == 0, nu > 0))
        def _():
            for c in weight_copies(e):
                c.start()

        @pl.when(jnp.logical_and(changed, b < nu))
        def _():
            for c in weight_copies(e):
                c.wait()
            w1b[...] = w1s[...].astype(BF16)
            w3b[...] = w3s[...].astype(BF16)
            w2b[...] = w2s[...].astype(BF16)
            nxt = ends_ref[e]

            @pl.when(nxt < nu)
            def _():
                for c in weight_copies(be_ref[nxt]):
                    c.start()

        @pl.when(b < nu)
        def _():
            xb = x_ref[rows, :].astype(BF16)
            a = jnp.dot(xb, w1b[...], preferred_element_type=F32)
            g = jnp.dot(xb, w3b[...], preferred_element_type=F32)
            o_ref[rows, :] = jnp.dot((_silu(a) * g).astype(BF16), w2b[...], preferred_element_type=F32)

        @pl.when(b >= nu)
        def _():
            o_ref[rows, :] = jnp.zeros((ROW_BLOCK, o_ref.shape[1]), o_ref.dtype)

    for s in range(GROUPS_PER_STEP):
        row_group(GROUPS_PER_STEP * pl.program_id(0) + s, slice(s * ROW_BLOCK, (s + 1) * ROW_BLOCK))


def _experts(blk_exp, n_used, ends, xs, w1, w3, w2):
    n_rows, d = xs.shape
    step_rows = GROUPS_PER_STEP * ROW_BLOCK
    assert n_rows % step_rows == 0
    rows = lambda g, be, nu, en: (jnp.minimum(g, jnp.maximum(nu[0] - 1, 0) // GROUPS_PER_STEP), 0)
    hbm = pl.BlockSpec(memory_space=pl.ANY)
    return pl.pallas_call(
        _expert_kernel,
        grid_spec=pltpu.PrefetchScalarGridSpec(
            num_scalar_prefetch=3,
            grid=(n_rows // step_rows,),
            in_specs=[pl.BlockSpec((step_rows, d), rows), hbm, hbm, hbm],
            out_specs=pl.BlockSpec((step_rows, d), lambda g, be, nu, en: (g, 0)),
            scratch_shapes=[pltpu.VMEM((d, D_FF), w1.dtype), pltpu.VMEM((d, D_FF), w3.dtype),
                            pltpu.VMEM((D_FF, d), w2.dtype),
                            pltpu.VMEM((d, D_FF), BF16), pltpu.VMEM((d, D_FF), BF16),
                            pltpu.VMEM((D_FF, d), BF16), pltpu.SemaphoreType.DMA((3,))],
        ),
        out_shape=jax.ShapeDtypeStruct((n_rows, d), F32),
        compiler_params=_cparams(("arbitrary",), 56),
    )(blk_exp, n_used, ends, xs, w1, w3, w2)


def _combine_kernel(pos_ref, x2_ref, gt_ref, nw_ref, yb_ref, o_ref, buf, sem, *, final):
    tm = x2_ref.shape[0]
    n = pl.num_programs(0)
    t = n * tm
    i = pl.program_id(0)

    def row_copy(slot, r, kk, p):
        return pltpu.make_async_copy(yb_ref.at[pl.ds(p, 1), :], buf.at[slot, kk, pl.ds(r, 1), :], sem.at[slot])

    def issue(step, slot):
        for r in range(tm):
            for kk in range(2):
                row_copy(slot, r, kk, pos_ref[kk * t + step * tm + r]).start()

    @pl.when(i == 0)
    def _():
        issue(0, 0)

    @pl.when(i + 1 < n)
    def _():
        issue(i + 1, (i + 1) % 2)

    slot = i % 2
    for r in range(2 * tm):
        row_copy(slot, 0, 0, 0).wait()

    gt = gt_ref[...]
    y = x2_ref[...] + gt[:, 0:1] * buf[slot, 0] + gt[:, 1:2] * buf[slot, 1]
    if final:
        y = y * lax.rsqrt(jnp.mean(y * y, axis=-1, keepdims=True) + EPS) * nw_ref[...]
    o_ref[...] = y


def _combine(pos_flat, x2, gates_t, norm_final, yb, tm, final):
    t, d = x2.shape
    return pl.pallas_call(
        functools.partial(_combine_kernel, final=final),
        grid_spec=pltpu.PrefetchScalarGridSpec(
            num_scalar_prefetch=1,
            grid=(t // tm,),
            in_specs=[pl.BlockSpec((tm, d), lambda i, pos: (i, 0)),
                      pl.BlockSpec((tm, LANE), lambda i, pos: (i, 0)),
                      pl.BlockSpec((1, d), lambda i, pos: (0, 0)),
                      pl.BlockSpec(memory_space=pl.ANY)],
            out_specs=pl.BlockSpec((tm, d), lambda i, pos: (i, 0)),
            scratch_shapes=[pltpu.VMEM((2, 2, tm, d), F32), pltpu.SemaphoreType.DMA((2,))],
        ),
        out_shape=jax.ShapeDtypeStruct((t, d), F32),
        compiler_params=_cparams(("arbitrary",), 32),
    )(pos_flat, x2, gates_t, norm_final.reshape(1, d).astype(F32), yb)


def _layer(x, norm_mix, w_in, conv_a, a_log, dt_bias, a_norm, w_gk2, b_gk, b_norm,
           w_oa, w_ob, w_out, norm_ffn, w_rg, b_rg, w_re, b_re, w1, w3, w2, norm_final, final):
    t, d = x.shape

    a_end = 4 * A_QK
    ga_end = a_end + 2 * A_HEADS
    b_end = ga_end + 2 * B_QK + 2 * B_V
    lr_end = b_end + B_GATE_RANK

    tm_big = min(1024, t)
    wt = w_in.T
    h, z_lr, ga, gct = _prologue(x, norm_mix, wt, a_end, b_end, a_log, dt_bias)
    z_a = _proj(h, wt, 0, a_end, tm_big, 1024)
    z_b = _proj(h, wt, ga_end, b_end - ga_end, tm_big, 1024)
    z_mix = _proj(h, wt, lr_end, 2 * d, tm_big, 1024)

    oa_g = _mixer_a(z_a, ga, gct, conv_a.astype(F32), a_norm)
    ob_g = _mixer_b(z_b, z_lr, w_gk2, b_gk, b_norm)

    wr = jnp.zeros((d, LANE), F32).at[:, 0:N_GROUPS].set(w_rg.astype(F32))
    wr = wr.at[:, 8:8 + N_EXPERTS].set(w_re.reshape(d, N_EXPERTS).astype(F32))
    wr_hi = wr.astype(BF16)
    wr = jnp.stack([wr_hi, (wr - wr_hi.astype(F32)).astype(BF16)])
    br_t = jnp.zeros((LANE, 1), F32).at[0:N_GROUPS, 0].set(b_rg.astype(F32))
    br_t = br_t.at[8:8 + N_EXPERTS, 0].set(b_re.reshape(N_EXPERTS).astype(F32))
    x2, h2, lt = _merge(oa_g, ob_g, z_mix, x, w_oa.astype(BF16), w_ob.astype(BF16), w_out.astype(BF16),
                        norm_ffn, wr, br_t, min(256, t))

    n_blk = (2 * t + ROW_BLOCK - 1) // ROW_BLOCK + N_EXPERTS
    n_blk += -n_blk % GROUPS_PER_STEP
    n_rows = n_blk * ROW_BLOCK
    pos, gates, blk = _route(lt, n_blk)
    pos_flat = pos[0:2].reshape(2 * t)
    blk_exp, n_used, ends = blk[0, :n_blk], blk[1, 0:1], blk[2, :N_EXPERTS]
    xs = _scatter_rows(pos_flat, ends, n_used, h2, n_rows, min(512, t))
    yb = _experts(blk_exp, n_used, ends, xs, w1, w3, w2)
    return _combine(pos_flat, x2, gates, norm_final, yb, min(256, t), final)


def kernel(x, norm_mix, w_in, conv_a, a_log, dt_bias, a_norm, w_gk2, b_gk, b_norm, w_oa, w_ob, w_out,
           norm_ffn, w_rg, b_rg, w_re, b_re, w1, w3, w2, norm_final):
    bsz, seq, d = x.shape
    assert bsz == 1, "one sequence per call"
    depth = norm_mix.shape[0]
    y = x.reshape(seq, d)
    for l in range(depth):
        y = _layer(y, norm_mix[l], w_in[l], conv_a[l], a_log[l], dt_bias[l], a_norm[l], w_gk2[l], b_gk[l],
                   b_norm[l], w_oa[l], w_ob[l], w_out[l], norm_ffn[l], w_rg[l], b_rg[l], w_re[l], b_re[l],
                   w1[l], w3[l], w2[l], norm_final, l == depth - 1)
    return y.reshape(bsz, seq, d)
```

```python
import functools

import jax
import jax.numpy as jnp
from jax import lax
from jax.experimental import pallas as pl
from jax.experimental.pallas import tpu as pltpu

D_MODEL = 2048
CHUNK = 64
EPS = 1e-6
A_HEADS, A_DK, A_DV, A_CONV = 8, 128, 128, 4
A_QK, A_V = A_HEADS * A_DK, A_HEADS * A_DV
B_HEADS, B_DK, B_DV, B_GATE_RANK, B_GATE_NORM = 4, 128, 256, 16, 16.0
B_QK, B_V = B_HEADS * B_DK, B_HEADS * B_DV
N_GROUPS, EXP_PER_GROUP, D_FF = 4, 8, 512
N_EXPERTS = N_GROUPS * EXP_PER_GROUP
ROW_BLOCK = 256
LANE = 128
SUB = 16
TB = 256
A_HG = 8
TA = 128
NEG = -1e30

F32 = jnp.float32
BF16 = jnp.bfloat16
HI = lax.Precision.HIGHEST


def _cparams(sem, vmem_mib):
    return pltpu.CompilerParams(dimension_semantics=sem, vmem_limit_bytes=vmem_mib * 2 ** 20)


def _bdot(a, b):
    return jnp.dot(a.astype(BF16), b.astype(BF16), preferred_element_type=F32)


def _bdot_nt(a, b):
    return lax.dot_general(a.astype(BF16), b.astype(BF16), (((1,), (1,)), ((), ())),
                           preferred_element_type=F32)


def _bdot_tn(a, b):
    return lax.dot_general(a.astype(BF16), b.astype(BF16), (((0,), (0,)), ((), ())),
                           preferred_element_type=F32)


def _fdot(a, b):
    return jnp.dot(a, b, preferred_element_type=F32, precision=HI)


def _sigmoid(x):
    return 1.0 / (1.0 + jnp.exp(-x))


def _silu(x):
    return x * _sigmoid(x)


def _softplus(x):
    return jnp.maximum(x, 0.0) + jnp.log(1.0 + jnp.exp(-jnp.abs(x)))


def _prologue_kernel(x_ref, nw_ref, wga_ref, wlr_ref, alog_ref, dtb_ref, h_ref, zlr_ref, ga_ref, gct_ref,
                     wga_b, wlr_b):
    @pl.when(pl.program_id(0) == 0)
    def _():
        wga_b[...] = wga_ref[...].astype(BF16)
        wlr_b[...] = wlr_ref[...].astype(BF16)

    x = x_ref[...]
    h = (x * lax.rsqrt(jnp.mean(x * x, axis=-1, keepdims=True) + EPS) * nw_ref[...]).astype(BF16)
    h_ref[...] = h
    proj = lambda w: lax.dot_general(h, w, (((1,), (1,)), ((), ())), preferred_element_type=F32)
    zlr_ref[...] = proj(wlr_b[...])
    _gates_a(proj(wga_b[...]), alog_ref[...], dtb_ref[...], ga_ref, gct_ref)


def _prologue(x, norm_mix, wt, a_end, b_end, a_log, dt_bias):
    t, d = x.shape
    assert a_end % LANE == 0 and b_end % LANE == 2 * A_HEADS
    pad = lambda p: jnp.pad(p.astype(F32), (A_HEADS, LANE - 2 * A_HEADS)).reshape(1, LANE)
    const = lambda c: pl.BlockSpec((1, c), lambda i: (0, 0))
    rows = lambda c: pl.BlockSpec((TB, c), lambda i: (i, 0))
    return pl.pallas_call(
        _prologue_kernel,
        grid=(t // TB,),
        in_specs=[rows(d), const(d),
                  pl.BlockSpec((LANE, d), lambda i: (a_end // LANE, 0)),
                  pl.BlockSpec((LANE, d), lambda i: (b_end // LANE, 0)),
                  const(LANE), const(LANE)],
        out_specs=[rows(d), rows(LANE), rows(LANE), pl.BlockSpec((A_HEADS, TB), lambda i: (0, i))],
        out_shape=[jax.ShapeDtypeStruct((t, d), BF16), jax.ShapeDtypeStruct((t, LANE), F32),
                   jax.ShapeDtypeStruct((t, LANE), F32), jax.ShapeDtypeStruct((A_HEADS, t), F32)],
        scratch_shapes=[pltpu.VMEM((LANE, d), BF16), pltpu.VMEM((LANE, d), BF16)],
        compiler_params=_cparams(("arbitrary",), 32),
    )(x, norm_mix.reshape(1, d), wt, wt, pad(a_log), pad(dt_bias))


PREP_ROWS = 256


def _proj_kernel(row0_ref, a_ref, wt_hbm, o_ref, ws_ref, wb_ref, sem):
    j, i = pl.program_id(0), pl.program_id(1)
    tn = ws_ref.shape[0]

    def fetch(w):
        return pltpu.make_async_copy(wt_hbm.at[pl.ds(pl.multiple_of(row0_ref[w], 8), tn), :], ws_ref, sem)

    @pl.when(jnp.logical_and(j == 0, i == 0))
    def _():
        fetch(0).start()

    @pl.when(i == 0)
    def _():
        fetch(j).wait()
        for r in range(0, tn, PREP_ROWS):
            wb_ref[r:r + PREP_ROWS, :] = ws_ref[r:r + PREP_ROWS, :].astype(BF16)

        @pl.when(j + 1 < pl.num_programs(0))
        def _():
            fetch(j + 1).start()

    o_ref[...] = lax.dot_general(a_ref[...], wb_ref[...], (((1,), (1,)), ((), ())), preferred_element_type=F32)


def _proj(a, wt, row0s, tm, tn):
    m, k = a.shape
    assert all(r % 8 == 0 for r in row0s)
    nt = len(row0s)
    return pl.pallas_call(
        _proj_kernel,
        grid_spec=pltpu.PrefetchScalarGridSpec(
            num_scalar_prefetch=1,
            grid=(nt, m // tm),
            in_specs=[pl.BlockSpec((tm, k), lambda j, i, r: (i, 0)), pl.BlockSpec(memory_space=pl.ANY)],
            out_specs=pl.BlockSpec((tm, tn), lambda j, i, r: (i, j)),
            scratch_shapes=[pltpu.VMEM((tn, k), wt.dtype), pltpu.VMEM((tn, k), BF16), pltpu.SemaphoreType.DMA(())],
        ),
        out_shape=jax.ShapeDtypeStruct((m, nt * tn), F32),
        compiler_params=_cparams(("arbitrary", "arbitrary"), 48),
    )(jnp.asarray(row0s, jnp.int32), a, wt)


def _chunk_masks(tb):
    row = lax.broadcasted_iota(jnp.int32, (tb, tb), 0)
    col = lax.broadcasted_iota(jnp.int32, (tb, tb), 1)
    same = (row // CHUNK) == (col // CHUNK)
    return row, col, same, same & (col <= row), same & (col < row)


def _lane_pick(x, idx):
    lane = lax.broadcasted_iota(jnp.int32, x.shape, 1)
    return jnp.sum(jnp.where(lane == idx, x, 0.0), axis=-1, keepdims=True)


def _causal_conv_silu(x_ref, w_ref, halo_ref, cbuf_ref, idx, tb):
    cbuf_ref[idx, 0:8, :] = halo_ref[idx]
    cbuf_ref[idx, 8:8 + tb, :] = x_ref[...]
    halo_ref[idx] = x_ref[tb - 8:tb, :]
    w = w_ref[...]
    acc = w[A_CONV - 1:A_CONV, :] * x_ref[...]
    for j in range(A_CONV - 1):
        off = 8 - (A_CONV - 1) + j
        acc = acc + w[j:j + 1, :] * cbuf_ref[idx, off:off + tb, :]
    return _silu(acc)


def _gates_a(gl, a_log, dt_bias, ga_ref, gct_ref):
    tb = gl.shape[0]
    beta = _sigmoid(gl)
    g = -jnp.exp(a_log) * _softplus(gl + dt_bias)
    _, _, same, causal, _ = _chunk_masks(tb)
    gc = _cumsum_rows(jnp.where(causal, 1.0, 0.0).astype(BF16), g)
    glast = _cumsum_rows(jnp.where(same, 1.0, 0.0).astype(BF16), g)
    lane = lax.broadcasted_iota(jnp.int32, (tb, LANE), 1)
    ga_ref[...] = jnp.where(lane < A_HEADS, beta,
                            jnp.where(lane < 2 * A_HEADS, gc, pltpu.roll(glast, A_HEADS, 1)))
    gct_ref[...] = gc.T[A_HEADS:2 * A_HEADS, :]


def _delta_heads(qs, ks, vs, ga, gc_rows, hs, sts, tb):
    n = len(qs)
    rng = range(n)
    _, _, _, causal, strict = _chunk_masks(tb)
    qs = [q * lax.rsqrt(jnp.sum(q * q, axis=-1, keepdims=True) + EPS) * (A_DK ** -0.5) for q in qs]
    ks = [k * lax.rsqrt(jnp.sum(k * k, axis=-1, keepdims=True) + EPS) for k in ks]
    beta = [_lane_pick(ga, h) for h in hs]
    gc = [_lane_pick(ga, h + A_HEADS) for h in hs]
    glast = [_lane_pick(ga, h + 2 * A_HEADS) for h in hs]
    decay = [jnp.exp(jnp.where(causal, gc[i] - gc_rows[i], NEG)) for i in rng]
    kb = [ks[i] * beta[i] for i in rng]

    n_pow = [jnp.where(strict, _bdot_nt(kb[i], ks[i]) * decay[i], 0.0) * -1.0 for i in rng]
    t_mat = list(n_pow)
    lvl = 2
    while lvl < CHUNK:
        n_pow = [_bdot(m, m) for m in n_pow]
        t_mat = [t_mat[i] + n_pow[i] + _bdot(t_mat[i], n_pow[i]) for i in rng]
        lvl *= 2

    egc = [jnp.exp(g) for g in gc]
    rhs = [jnp.concatenate([vs[i] * beta[i], kb[i] * egc[i]], axis=1) for i in rng]
    uw = [rhs[i] + _bdot(t_mat[i], rhs[i]) for i in rng]
    qk = [_bdot_nt(qs[i], ks[i]) * decay[i] for i in rng]
    qkuw = [_bdot(qk[i], uw[i]) for i in rng]
    o_local = [x[:, :A_DV] for x in qkuw]
    q_eff = [qs[i] * egc[i] - qkuw[i][:, A_DV:] for i in rng]
    k_dec = [ks[i] * jnp.exp(glast[i] - gc[i]) for i in rng]
    eg_last = [jnp.exp(g) for g in glast]

    sts = list(sts)
    outs = [[] for _ in rng]
    for c in range(tb // CHUNK):
        lo, hi = c * CHUNK, (c + 1) * CHUNK
        bg = [_bdot_tn(uw[i][lo:hi], k_dec[i][lo:hi]) for i in rng]
        for i in rng:
            outs[i].append(o_local[i][lo:hi] + _bdot_nt(q_eff[i][lo:hi], sts[i]))
        sts = [sts[i] * eg_last[i][lo:lo + 1, :] + bg[i][:A_DV] - _bdot(sts[i], bg[i][A_DV:]) for i in rng]
    return [jnp.concatenate(o, axis=0) for o in outs], sts


def _mixer_a_kernel(xq_ref, xk_ref, xv_ref, z_ref, ga_ref, gct_ref, wq_ref, wk_ref, wv_ref,
                    anorm_ref, o_ref, halo_ref, cbuf_ref, state_ref):
    tb = xq_ref.shape[0]
    hg = xq_ref.shape[1] // A_DK

    @pl.when(pl.program_id(1) == 0)
    def _():
        halo_ref[...] = jnp.zeros_like(halo_ref)
        state_ref[...] = jnp.zeros_like(state_ref)

    q = _causal_conv_silu(xq_ref, wq_ref, halo_ref, cbuf_ref, 0, tb)
    k = _causal_conv_silu(xk_ref, wk_ref, halo_ref, cbuf_ref, 1, tb)
    v = _causal_conv_silu(xv_ref, wv_ref, halo_ref, cbuf_ref, 2, tb)
    hs = [pl.program_id(0) * hg + j for j in range(hg)]
    sls = [slice(j * A_DK, (j + 1) * A_DK) for j in range(hg)]
    sts = [state_ref[j] for j in range(hg)]
    gc_rows = [gct_ref[pl.ds(h, 1), :] for h in hs]
    for r0 in range(0, tb, TA):
        rs = slice(r0, r0 + TA)
        outs, sts = _delta_heads([q[rs, s] for s in sls], [k[rs, s] for s in sls], [v[rs, s] for s in sls],
                                 ga_ref[rs, :], [g[:, rs] for g in gc_rows], hs, sts, TA)
        for j in range(hg):
            o = outs[j]
            o = o * lax.rsqrt(jnp.mean(o * o, axis=-1, keepdims=True) + EPS) * anorm_ref[...]
            o_ref[rs, sls[j]] = (o * _silu(z_ref[rs, sls[j]])).astype(o_ref.dtype)
    for j in range(hg):
        state_ref[j] = sts[j]


def _mixer_a(z_a, ga, gct, conv_a, a_norm):
    t = z_a.shape[0]
    ng = A_HEADS // A_HG
    wid = A_HG * A_DK
    blk = lambda off: pl.BlockSpec((TB, wid), lambda g, i: (i, off + g))
    cblk = lambda off: pl.BlockSpec((A_CONV, wid), lambda g, i: (0, off + g))
    return pl.pallas_call(
        _mixer_a_kernel,
        grid=(ng, t // TB),
        in_specs=[blk(0), blk(ng), blk(2 * ng), blk(3 * ng),
                  pl.BlockSpec((TB, LANE), lambda g, i: (i, 0)),
                  pl.BlockSpec((A_HEADS, TB), lambda g, i: (0, i)),
                  cblk(0), cblk(ng), cblk(2 * ng), pl.BlockSpec((1, LANE), lambda g, i: (0, 0))],
        out_specs=pl.BlockSpec((TB, wid), lambda g, i: (i, g)),
        out_shape=jax.ShapeDtypeStruct((t, A_V), BF16),
        scratch_shapes=[pltpu.VMEM((3, 8, wid), F32), pltpu.VMEM((3, 8 + TB, wid), F32),
                        pltpu.VMEM((A_HG, A_DV, A_DK), F32)],
        compiler_params=_cparams(("parallel", "arbitrary"), 48),
    )(z_a, z_a, z_a, z_a, ga, gct, conv_a, conv_a, conv_a,
      a_norm.reshape(1, A_DV).astype(F32))


def _cumsum_rows(mask, x):
    hi = x.astype(BF16)
    r1 = x - hi.astype(F32)
    mid = r1.astype(BF16)
    lo = (r1 - mid.astype(F32)).astype(BF16)
    dot = lambda p: jnp.dot(mask, p, preferred_element_type=F32)
    return dot(hi) + dot(mid) + dot(lo)


def _gla_chunk_scores(qc, kc, bc):
    rid = lax.broadcasted_iota(jnp.int32, (SUB, B_DK), 0)
    rid_lo = lax.broadcasted_iota(jnp.int32, (SUB // 2, B_DK), 0) + SUB // 2
    lane_c = lax.broadcasted_iota(jnp.int32, (SUB, CHUNK), 1)
    crow = lax.broadcasted_iota(jnp.int32, (CHUNK, B_DK), 0)
    half = SUB // 2
    rows = []
    for si in range(CHUNK // SUB):
        r0 = si * SUB
        qb, bb = qc[r0:r0 + SUB], bc[r0:r0 + SUB]
        ys = []
        for j in range(SUB):
            bj = bc[r0 + j:r0 + j + 1]
            if j < half:
                ys.append(qb * jnp.exp(jnp.where(rid >= j, bb - bj, NEG)))
            else:
                ys.append(jnp.zeros((half, B_DK), F32))
                ys.append(qb[half:] * jnp.exp(jnp.where(rid_lo >= j, bb[half:] - bj, NEG)))
        r = _bdot_nt(jnp.concatenate(ys, axis=0), kc)
        blk = jnp.zeros((SUB, CHUNK), F32)
        for j in range(SUB):
            blk = jnp.where(lane_c == r0 + j, r[j * SUB:(j + 1) * SUB], blk)
        if si > 0:
            bref = bc[r0:r0 + 1]
            qt = qb * jnp.exp(bb - bref)
            kt = kc * jnp.exp(jnp.where(crow < r0, bref - bc, NEG))
            blk = blk + _bdot_nt(qt, kt)
        rows.append(blk)
    return jnp.concatenate(rows, axis=0)


def _mixer_b_kernel(q_ref, k_ref, v_ref, zg_ref, tail_ref, wg_ref, bgk_ref, bnorm_ref, o_ref, state_ref):
    tb = q_ref.shape[0]
    nh = q_ref.shape[1] // B_DK

    @pl.when(pl.program_id(0) == 0)
    def _():
        state_ref[...] = jnp.zeros_like(state_ref)

    tl = tail_ref[...]
    t_hi = tl.astype(BF16)
    t_mid = (tl - t_hi.astype(F32)).astype(BF16)
    dot = lambda a, b: jnp.dot(a, b, preferred_element_type=F32)
    x = dot(t_hi, wg_ref[0]) + (dot(t_mid, wg_ref[0]) + dot(t_hi, wg_ref[1])) + bgk_ref[...]
    gk = -_softplus(-x) * (1.0 / B_GATE_NORM)
    _, _, _, causal, _ = _chunk_masks(tb)
    b_all = _cumsum_rows(jnp.where(causal, 1.0, 0.0).astype(BF16), gk)

    sts = [state_ref[h] for h in range(nh)]
    for c in range(tb // CHUNK):
        lo, hi = c * CHUNK, (c + 1) * CHUNK
        for h in range(nh):
            qc = q_ref[lo:hi, h * B_DK:(h + 1) * B_DK] * (B_DK ** -0.5)
            kc = k_ref[lo:hi, h * B_DK:(h + 1) * B_DK]
            vc = v_ref[lo:hi, h * B_DV:(h + 1) * B_DV]
            bc = b_all[lo:hi, h * B_DK:(h + 1) * B_DK]
            a_c = _gla_chunk_scores(qc, kc, bc)
            bl = bc[CHUNK - 1:CHUNK]
            o = _bdot(a_c, vc) + _bdot_nt(qc * jnp.exp(bc), sts[h])
            sts[h] = sts[h] * jnp.exp(bl) + _bdot_tn(vc, kc * jnp.exp(bl - bc))
            o = o * lax.rsqrt(jnp.mean(o * o, axis=-1, keepdims=True) + EPS) * bnorm_ref[...]
            sl = slice(h * B_DV, (h + 1) * B_DV)
            o_ref[lo:hi, sl] = (o * _silu(zg_ref[lo:hi, sl])).astype(o_ref.dtype)
    for h in range(nh):
        state_ref[h] = sts[h]


def _mixer_b(z_b, c0, z_tail, w_gk2, b_gk, b_norm):
    t = z_b.shape[0]
    assert c0 % B_V == 0
    wg = jnp.zeros((LANE, B_QK), F32).at[2 * A_HEADS:2 * A_HEADS + B_GATE_RANK].set(w_gk2.astype(F32))
    wg_hi = wg.astype(BF16)
    wg = jnp.stack([wg_hi, (wg - wg_hi.astype(F32)).astype(BF16)])
    col = lambda w, off: pl.BlockSpec((TB, w), lambda i: (i, off // w))
    full = lambda r, c: pl.BlockSpec((r, c), lambda i: (0, 0))
    return pl.pallas_call(
        _mixer_b_kernel,
        grid=(t // TB,),
        in_specs=[col(B_QK, c0), col(B_QK, c0 + B_QK), col(B_V, c0 + 2 * B_QK), col(B_V, c0 + 2 * B_QK + B_V),
                  col(LANE, 0),
                  pl.BlockSpec((2, LANE, B_QK), lambda i: (0, 0, 0)), full(1, B_QK), full(1, B_DV)],
        out_specs=pl.BlockSpec((TB, B_V), lambda i: (i, 0)),
        out_shape=jax.ShapeDtypeStruct((t, B_V), BF16),
        scratch_shapes=[pltpu.VMEM((B_HEADS, B_DV, B_DK), F32)],
        compiler_params=_cparams(("arbitrary",), 48),
    )(z_b, z_b, z_b, z_b, z_tail, wg, b_gk.reshape(1, B_QK).astype(F32),
      b_norm.reshape(1, B_DV).astype(F32))


def _merge_kernel(oa_ref, ob_ref, ma_ref, mb_ref, x_ref, woa_ref, wob_ref, wout_ref, nf_ref,
                  wr_ref, br_ref, x2_ref, h2_ref, lt_ref):
    ya = jnp.dot(oa_ref[...], woa_ref[...], preferred_element_type=F32)
    yb = jnp.dot(ob_ref[...], wob_ref[...], preferred_element_type=F32)
    m = _sigmoid(ma_ref[...]) * ya + _sigmoid(mb_ref[...]) * yb
    x2 = x_ref[...] + jnp.dot(m.astype(BF16), wout_ref[...], preferred_element_type=F32)
    x2_ref[...] = x2
    h2 = x2 * lax.rsqrt(jnp.mean(x2 * x2, axis=-1, keepdims=True) + EPS) * nf_ref[...]
    h2_ref[...] = h2
    h_hi = h2.astype(BF16)
    h_mid = (h2 - h_hi.astype(F32)).astype(BF16)
    dot = lambda a, b: jnp.dot(a, b, preferred_element_type=F32)
    logits = dot(h_hi, wr_ref[0]) + (dot(h_mid, wr_ref[0]) + dot(h_hi, wr_ref[1]))
    lt_ref[...] = logits.T + br_ref[...]


def _merge(oa_g, ob_g, z_mix, c0, x, w_oa, w_ob, w_out, norm_ffn, wr, br_t, tm):
    t, d = x.shape
    assert c0 % d == 0
    row = lambda w, c: pl.BlockSpec((tm, w), lambda i: (i, c))
    full = lambda a: pl.BlockSpec(a.shape, lambda i: (0,) * a.ndim, pipeline_mode=pl.Buffered(1))
    return pl.pallas_call(
        _merge_kernel,
        grid=(t // tm,),
        in_specs=[row(A_V, 0), row(B_V, 0), row(d, c0 // d), row(d, c0 // d + 1), row(d, 0),
                  full(w_oa), full(w_ob), full(w_out), pl.BlockSpec((1, d), lambda i: (0, 0)),
                  full(wr), full(br_t)],
        out_specs=[row(d, 0), row(d, 0), pl.BlockSpec((LANE, tm), lambda i: (0, i))],
        out_shape=[jax.ShapeDtypeStruct((t, d), F32), jax.ShapeDtypeStruct((t, d), F32),
                   jax.ShapeDtypeStruct((LANE, t), F32)],
        compiler_params=_cparams(("parallel",), 56),
    )(oa_g, ob_g, z_mix, z_mix, x, w_oa, w_ob, w_out, norm_ffn.reshape(1, d).astype(F32), wr, br_t)


SEG = 256


def _route_kernel(lt_ref, pos_ref, gate_ref, blk_ref, oh_ref):
    t = lt_ref.shape[1]
    rid8 = lax.broadcasted_iota(jnp.int32, (8, t), 0)
    lg = jnp.where(rid8 < N_GROUPS, lt_ref[0:8, :], -jnp.inf)
    gmax = jnp.max(lg, axis=0, keepdims=True)
    g_idx = jnp.min(jnp.where(lg == gmax, rid8, 8), axis=0, keepdims=True)
    p_top = 1.0 / jnp.sum(jnp.exp(lg - gmax), axis=0, keepdims=True)

    les = jnp.zeros((EXP_PER_GROUP, t), F32)
    for g in range(N_GROUPS):
        les = jnp.where(g_idx == g, lt_ref[8 + g * EXP_PER_GROUP:8 + (g + 1) * EXP_PER_GROUP, :], les)
    m1 = jnp.max(les, axis=0, keepdims=True)
    i1 = jnp.min(jnp.where(les == m1, rid8, 8), axis=0, keepdims=True)
    les2 = jnp.where(rid8 == i1, -jnp.inf, les)
    m2 = jnp.max(les2, axis=0, keepdims=True)
    i2 = jnp.min(jnp.where(les2 == m2, rid8, 8), axis=0, keepdims=True)
    r = jnp.exp(m2 - m1)
    gate_ref[...] = jnp.concatenate([p_top / (1.0 + r), p_top * r / (1.0 + r),
                                     jnp.zeros((LANE - 2, t), F32)], axis=0).T
    e1 = g_idx * EXP_PER_GROUP + i1
    e2 = g_idx * EXP_PER_GROUP + i2

    rid32 = lax.broadcasted_iota(jnp.int32, (N_EXPERTS, t), 0)
    oh_ref[0] = jnp.where(rid32 == e1, 1.0, 0.0)
    oh_ref[1] = jnp.where(rid32 == e2, 1.0, 0.0)

    ui = lax.broadcasted_iota(jnp.int32, (SEG, SEG), 0)
    uj = lax.broadcasted_iota(jnp.int32, (SEG, SEG), 1)
    upper = jnp.where(ui < uj, 1.0, 0.0).astype(BF16)
    carry = jnp.zeros((N_EXPERTS, 1), F32)
    ranks = []
    for kk in range(2):
        segs = []
        for sg in range(t // SEG):
            oh = oh_ref[kk, :, sg * SEG:(sg + 1) * SEG]
            pre = jnp.dot(oh.astype(BF16), upper, preferred_element_type=F32) + carry
            segs.append(jnp.sum(oh * pre, axis=0, keepdims=True))
            carry = carry + jnp.sum(oh, axis=1, keepdims=True)
        ranks.append(jnp.concatenate(segs, axis=1))
    counts = carry
    nblk = jnp.floor((counts + (ROW_BLOCK - 1)) * (1.0 / ROW_BLOCK))
    li = lax.broadcasted_iota(jnp.int32, (N_EXPERTS, N_EXPERTS), 0)
    lj = lax.broadcasted_iota(jnp.int32, (N_EXPERTS, N_EXPERTS), 1)
    nb_b = jnp.broadcast_to(nblk, (N_EXPERTS, LANE))
    start_blk = _fdot(jnp.where(lj < li, 1.0, 0.0), nb_b)[:, 0:1]
    end_blk = start_blk + nblk
    start_row = start_blk * ROW_BLOCK

    pos_ref[...] = jnp.zeros_like(pos_ref)
    for kk in range(2):
        base = jnp.sum(oh_ref[kk] * start_row, axis=0, keepdims=True)
        pos_ref[kk:kk + 1, :] = (base + ranks[kk]).astype(jnp.int32)

    nb = blk_ref.shape[1]
    bid = lax.broadcasted_iota(jnp.int32, (N_EXPERTS, nb), 1).astype(F32)
    be = jnp.sum(jnp.where(end_blk <= bid, 1.0, 0.0), axis=0, keepdims=True)
    blk_ref[...] = jnp.zeros_like(blk_ref)
    blk_ref[0:1, :] = jnp.minimum(be, N_EXPERTS - 1.0).astype(jnp.int32)
    blk_ref[1:2, :] = jnp.broadcast_to(end_blk[N_EXPERTS - 1:N_EXPERTS, :], (1, nb)).astype(jnp.int32)
    ends = jnp.sum(jnp.where(li == lj, end_blk, 0.0), axis=0, keepdims=True)
    blk_ref[2:3, 0:N_EXPERTS] = ends.astype(jnp.int32)


def _route(lt, n_blk):
    t = lt.shape[1]
    nb = -(-n_blk // LANE) * LANE
    return pl.pallas_call(
        _route_kernel,
        out_shape=[jax.ShapeDtypeStruct((8, t), jnp.int32), jax.ShapeDtypeStruct((t, LANE), F32),
                   jax.ShapeDtypeStruct((8, nb), jnp.int32)],
        scratch_shapes=[pltpu.VMEM((2, N_EXPERTS, t), F32)],
        compiler_params=pltpu.CompilerParams(vmem_limit_bytes=48 * 2 ** 20),
    )(lt)


def _scatter_kernel(pos_ref, ends_ref, nu_ref, h_ref, xs_ref, zbuf, sem, zsem):
    tm = h_ref.shape[0]
    t = pl.num_programs(0) * tm
    base = pl.program_id(0) * tm
    n_blk = xs_ref.shape[0] // ROW_BLOCK

    def zero_block(blk):
        return pltpu.make_async_copy(zbuf, xs_ref.at[pl.ds(blk * ROW_BLOCK, ROW_BLOCK), :], zsem)

    def for_each_zero_block(fn):
        for e in range(N_EXPERTS):
            first = ends_ref[e - 1] if e else 0

            @pl.when(ends_ref[e] > first)
            def _():
                fn(zero_block(ends_ref[e] - 1))

        def tail(blk, c):
            fn(zero_block(blk))
            return c
        lax.fori_loop(nu_ref[0], n_blk, tail, 0)

    @pl.when(pl.program_id(0) == 0)
    def _():
        zbuf[...] = jnp.zeros_like(zbuf)
        for_each_zero_block(lambda c: c.start())
        for_each_zero_block(lambda c: c.wait())

    def row_copy(r, p):
        return pltpu.make_async_copy(h_ref.at[pl.ds(r, 1), :], xs_ref.at[pl.ds(p, 1), :], sem)

    for r in range(tm):
        for kk in range(2):
            row_copy(r, pos_ref[kk * t + base + r]).start()
    for r in range(2 * tm):
        row_copy(0, 0).wait()


def _scatter_rows(pos_flat, ends, n_used, h2, n_rows, tm):
    t, d = h2.shape
    return pl.pallas_call(
        _scatter_kernel,
        grid_spec=pltpu.PrefetchScalarGridSpec(
            num_scalar_prefetch=3,
            grid=(t // tm,),
            in_specs=[pl.BlockSpec((tm, d), lambda i, *_: (i, 0))],
            out_specs=pl.BlockSpec(memory_space=pl.ANY),
            scratch_shapes=[pltpu.VMEM((ROW_BLOCK, d), h2.dtype), pltpu.SemaphoreType.DMA(()),
                            pltpu.SemaphoreType.DMA(())],
        ),
        out_shape=jax.ShapeDtypeStruct((n_rows, d), h2.dtype),
        compiler_params=_cparams(("arbitrary",), 32),
    )(pos_flat, ends, n_used, h2)


GROUPS_PER_STEP = 2


def _expert_kernel(be_ref, nu_ref, ends_ref, x_ref, w1_hbm, w3_hbm, w2_hbm, o_ref,
                   w1s, w3s, w2s, w1b, w3b, w2b, sem):
    nu = nu_ref[0]

    def weight_copies(ex):
        return (pltpu.make_async_copy(w1_hbm.at[ex], w1s, sem.at[0]),
                pltpu.make_async_copy(w3_hbm.at[ex], w3s, sem.at[1]),
                pltpu.make_async_copy(w2_hbm.at[ex], w2s, sem.at[2]))

    def row_group(b, rows):
        e = be_ref[b]
        changed = jnp.logical_or(b == 0, e != be_ref[jnp.maximum(b - 1, 0)])

        @pl.when(jnp.logical_and(b == 0, nu > 0))
        def _():
            for c in weight_copies(e):
                c.start()

        @pl.when(jnp.logical_and(changed, b < nu))
        def _():
            for c in weight_copies(e):
                c.wait()
            w1b[...] = w1s[...].astype(BF16)
            w3b[...] = w3s[...].astype(BF16)
            w2b[...] = w2s[...].astype(BF16)
            nxt = ends_ref[e]

            @pl.when(nxt < nu)
            def _():
                for c in weight_copies(be_ref[nxt]):
                    c.start()

        @pl.when(b < nu)
        def _():
            xb = x_ref[rows, :].astype(BF16)
            a = jnp.dot(xb, w1b[...], preferred_element_type=F32)
            g = jnp.dot(xb, w3b[...], preferred_element_type=F32)
            o_ref[rows, :] = jnp.dot((_silu(a) * g).astype(BF16), w2b[...], preferred_element_type=F32)

        @pl.when(b >= nu)
        def _():
            o_ref[rows, :] = jnp.zeros((ROW_BLOCK, o_ref.shape[1]), o_ref.dtype)

    for s in range(GROUPS_PER_STEP):
        row_group(GROUPS_PER_STEP * pl.program_id(0) + s, slice(s * ROW_BLOCK, (s + 1) * ROW_BLOCK))


def _experts(blk_exp, n_used, ends, xs, w1, w3, w2):
    n_rows, d = xs.shape
    step_rows = GROUPS_PER_STEP * ROW_BLOCK
    assert n_rows % step_rows == 0
    rows = lambda g, be, nu, en: (jnp.minimum(g, jnp.maximum(nu[0] - 1, 0) // GROUPS_PER_STEP), 0)
    hbm = pl.BlockSpec(memory_space=pl.ANY)
    return pl.pallas_call(
        _expert_kernel,
        grid_spec=pltpu.PrefetchScalarGridSpec(
            num_scalar_prefetch=3,
            grid=(n_rows // step_rows,),
            in_specs=[pl.BlockSpec((step_rows, d), rows), hbm, hbm, hbm],
            out_specs=pl.BlockSpec((step_rows, d), lambda g, be, nu, en: (g, 0)),
            scratch_shapes=[pltpu.VMEM((d, D_FF), w1.dtype), pltpu.VMEM((d, D_FF), w3.dtype),
                            pltpu.VMEM((D_FF, d), w2.dtype),
                            pltpu.VMEM((d, D_FF), BF16), pltpu.VMEM((d, D_FF), BF16),
                            pltpu.VMEM((D_FF, d), BF16), pltpu.SemaphoreType.DMA((3,))],
        ),
        out_shape=jax.ShapeDtypeStruct((n_rows, d), F32),
        compiler_params=_cparams(("arbitrary",), 56),
    )(blk_exp, n_used, ends, xs, w1, w3, w2)


def _combine_kernel(pos_ref, x2_ref, gt_ref, nw_ref, yb_ref, o_ref, buf, sem, *, final):
    tm = x2_ref.shape[0]
    n = pl.num_programs(0)
    t = n * tm
    i = pl.program_id(0)

    def row_copy(slot, r, kk, p):
        return pltpu.make_async_copy(yb_ref.at[pl.ds(p, 1), :], buf.at[slot, kk, pl.ds(r, 1), :], sem.at[slot])

    def issue(step, slot):
        for r in range(tm):
            for kk in range(2):
                row_copy(slot, r, kk, pos_ref[kk * t + step * tm + r]).start()

    @pl.when(i == 0)
    def _():
        issue(0, 0)

    @pl.when(i + 1 < n)
    def _():
        issue(i + 1, (i + 1) % 2)

    slot = i % 2
    for r in range(2 * tm):
        row_copy(slot, 0, 0, 0).wait()

    gt = gt_ref[...]
    y = x2_ref[...] + gt[:, 0:1] * buf[slot, 0] + gt[:, 1:2] * buf[slot, 1]
    if final:
        y = y * lax.rsqrt(jnp.mean(y * y, axis=-1, keepdims=True) + EPS) * nw_ref[...]
    o_ref[...] = y


def _combine(pos_flat, x2, gates_t, norm_final, yb, tm, final):
    t, d = x2.shape
    return pl.pallas_call(
        functools.partial(_combine_kernel, final=final),
        grid_spec=pltpu.PrefetchScalarGridSpec(
            num_scalar_prefetch=1,
            grid=(t // tm,),
            in_specs=[pl.BlockSpec((tm, d), lambda i, pos: (i, 0)),
                      pl.BlockSpec((tm, LANE), lambda i, pos: (i, 0)),
                      pl.BlockSpec((1, d), lambda i, pos: (0, 0)),
                      pl.BlockSpec(memory_space=pl.ANY)],
            out_specs=pl.BlockSpec((tm, d), lambda i, pos: (i, 0)),
            scratch_shapes=[pltpu.VMEM((2, 2, tm, d), F32), pltpu.SemaphoreType.DMA((2,))],
        ),
        out_shape=jax.ShapeDtypeStruct((t, d), F32),
        compiler_params=_cparams(("arbitrary",), 32),
    )(pos_flat, x2, gates_t, norm_final.reshape(1, d).astype(F32), yb)


def _layer(x, norm_mix, w_in, conv_a, a_log, dt_bias, a_norm, w_gk2, b_gk, b_norm,
           w_oa, w_ob, w_out, norm_ffn, w_rg, b_rg, w_re, b_re, w1, w3, w2, norm_final, final):
    t, d = x.shape

    a_end = 4 * A_QK
    ga_end = a_end + 2 * A_HEADS
    b_end = ga_end + 2 * B_QK + 2 * B_V
    lr_end = b_end + B_GATE_RANK

    tm_big = min(1024, t)
    wt = w_in.T
    h, z_lr, ga, gct = _prologue(x, norm_mix, wt, a_end, b_end, a_log, dt_bias)
    tn = 1024
    wins = (range(0, a_end, tn), range(lr_end, lr_end + 2 * d, tn), range(ga_end, b_end, tn))
    z = _proj(h, wt, tuple(r for w in wins for r in w), tm_big, tn)
    col_mix, col_b = a_end, a_end + 2 * d

    oa_g = _mixer_a(z, ga, gct, conv_a.astype(F32), a_norm)
    ob_g = _mixer_b(z, col_b, z_lr, w_gk2, b_gk, b_norm)

    wr = jnp.zeros((d, LANE), F32).at[:, 0:N_GROUPS].set(w_rg.astype(F32))
    wr = wr.at[:, 8:8 + N_EXPERTS].set(w_re.reshape(d, N_EXPERTS).astype(F32))
    wr_hi = wr.astype(BF16)
    wr = jnp.stack([wr_hi, (wr - wr_hi.astype(F32)).astype(BF16)])
    br_t = jnp.zeros((LANE, 1), F32).at[0:N_GROUPS, 0].set(b_rg.astype(F32))
    br_t = br_t.at[8:8 + N_EXPERTS, 0].set(b_re.reshape(N_EXPERTS).astype(F32))
    x2, h2, lt = _merge(oa_g, ob_g, z, col_mix, x, w_oa.astype(BF16), w_ob.astype(BF16), w_out.astype(BF16),
                        norm_ffn, wr, br_t, min(256, t))

    n_blk = (2 * t + ROW_BLOCK - 1) // ROW_BLOCK + N_EXPERTS
    n_blk += -n_blk % GROUPS_PER_STEP
    n_rows = n_blk * ROW_BLOCK
    pos, gates, blk = _route(lt, n_blk)
    pos_flat = pos[0:2].reshape(2 * t)
    blk_exp, n_used, ends = blk[0, :n_blk], blk[1, 0:1], blk[2, :N_EXPERTS]
    xs = _scatter_rows(pos_flat, ends, n_used, h2, n_rows, min(512, t))
    yb = _experts(blk_exp, n_used, ends, xs, w1, w3, w2)
    return _combine(pos_flat, x2, gates, norm_final, yb, min(256, t), final)


def kernel(x, norm_mix, w_in, conv_a, a_log, dt_bias, a_norm, w_gk2, b_gk, b_norm, w_oa, w_ob, w_out,
           norm_ffn, w_rg, b_rg, w_re, b_re, w1, w3, w2, norm_final):
    bsz, seq, d = x.shape
    assert bsz == 1, "one sequence per call"
    depth = norm_mix.shape[0]
    y = x.reshape(seq, d)
    for l in range(depth):
        y = _layer(y, norm_mix[l], w_in[l], conv_a[l], a_log[l], dt_bias[l], a_norm[l], w_gk2[l], b_gk[l],
                   b_norm[l], w_oa[l], w_ob[l], w_out[l], norm_ffn[l], w_rg[l], b_rg[l], w_re[l], b_re[l],
                   w1[l], w3[l], w2[l], norm_final, l == depth - 1)
    return y.reshape(bsz, seq, d)
```
